```python
import math
import jax
import jax.numpy as jnp
from jax import lax
import numpy as np

D_MODEL = 1024
BATCH = 8
SEQ = 2048
DEPTH = 1
DEC_BATCH = 16
DEC_SEQ = 2048
PAST_LEN = 128

D_CONV = D_MODEL
CONV_WIDTH = 31
SSD_D_INNER = 2 * D_MODEL
SSD_HEAD_DIM = 64
SSD_HEADS = SSD_D_INNER // SSD_HEAD_DIM
SSD_GROUPS = 8
SSD_STATE = 128
SSD_CONV_WIDTH = 5
SSD_CHUNK = 128
SSD_XBC = SSD_D_INNER + 2 * SSD_GROUPS * SSD_STATE
IN_SPLITS = (2 * D_CONV,
             2 * D_CONV + SSD_D_INNER,
             2 * D_CONV + SSD_D_INNER + SSD_XBC,
             2 * D_CONV + SSD_D_INNER + SSD_XBC + SSD_HEADS,
             2 * D_CONV + SSD_D_INNER + SSD_XBC + 2 * SSD_HEADS)
N_IN = 2 * D_CONV + SSD_D_INNER + SSD_XBC + 2 * SSD_HEADS + 2 * D_MODEL
MEM_TOKENS = 256
XATTN_HEADS = 4
XATTN_HEAD_DIM = D_MODEL // XATTN_HEADS
N_GROUPS = 8
EXPERTS_PER_GROUP = 8
N_EXPERTS = N_GROUPS * EXPERTS_PER_GROUP
TOP_K_EXPERT = 2
EXPERT_FF = 512
MOE_BLOCK = 128
RMS_EPS = 1e-6
LN_EPS = 1e-5

kernel_name = 'hybrid_conformer_ssd_hmoe_encoder'


def _rms_norm(x, w):
    xf = x.astype(jnp.float32)
    y = xf * lax.rsqrt(jnp.mean(xf * xf, axis=-1, keepdims=True) + RMS_EPS)
    return (y * w.astype(jnp.float32)).astype(x.dtype)


def _layer_norm(x, w, b):
    xf = x.astype(jnp.float32)
    mu = jnp.mean(xf, axis=-1, keepdims=True)
    xc = xf - mu
    y = xc * lax.rsqrt(jnp.mean(xc * xc, axis=-1, keepdims=True) + LN_EPS)
    return (y * w.astype(jnp.float32) + b.astype(jnp.float32)).astype(x.dtype)


def _dw_conv(x, w, b):
    pad = w.shape[0] // 2
    y = lax.conv_general_dilated(x, w[:, None, :].astype(x.dtype), (1,), [(pad, pad)],
                                 dimension_numbers=('NWC', 'WIO', 'NWC'),
                                 feature_group_count=x.shape[-1])
    return y + b.astype(x.dtype)


def _ssd_chunked(x, dt, a_head, bm, cm):
    b, s = x.shape[0], x.shape[1]
    nc = s // SSD_CHUNK
    r = SSD_HEADS // SSD_GROUPS
    xdt = (x * dt[..., None]).reshape(b, nc, SSD_CHUNK, SSD_GROUPS, r, SSD_HEAD_DIM)
    a = (dt * a_head).reshape(b, nc, SSD_CHUNK, SSD_GROUPS, r).transpose(0, 1, 3, 4, 2)
    bc = bm.reshape(b, nc, SSD_CHUNK, SSD_GROUPS, SSD_STATE)
    cc = cm.reshape(b, nc, SSD_CHUNK, SSD_GROUPS, SSD_STATE)
    a_cs = jnp.cumsum(a, axis=-1)
    lower = jnp.tril(jnp.ones((SSD_CHUNK, SSD_CHUNK), bool))
    decay = jnp.exp(jnp.where(lower, a_cs[..., :, None] - a_cs[..., None, :], -jnp.inf))
    cb = jnp.einsum('bclgn,bcsgn->bcgls', cc, bc)
    y_diag = jnp.einsum('bcgrls,bcsgrp->bclgrp', cb[:, :, :, None] * decay, xdt)
    decay_states = jnp.exp(a_cs[..., -1:] - a_cs)
    states = jnp.einsum('bclgn,bcgrl,bclgrp->bcgrpn', bc, decay_states, xdt)
    a_last = a_cs[..., -1]
    chunk_cs = jnp.cumsum(a_last, axis=1)
    excl = chunk_cs - a_last
    strict = jnp.tril(jnp.ones((nc, nc), bool), -1)[None, :, :, None, None]
    decay_chunk = jnp.exp(jnp.where(strict, excl[:, :, None] - chunk_cs[:, None, :], -jnp.inf))
    states_in = jnp.einsum('bzcgr,bcgrpn->bzgrpn', decay_chunk, states)
    y_off = jnp.einsum('bclgn,bcgrpn,bcgrl->bclgrp', cc, states_in, jnp.exp(a_cs))
    return (y_diag + y_off).reshape(b, s, SSD_HEADS, SSD_HEAD_DIM)


def _gated_group_rms_norm(y, z, w):
    yz = y.astype(jnp.float32) * jax.nn.silu(z.astype(jnp.float32))
    g = yz.reshape(yz.shape[:-1] + (SSD_GROUPS, SSD_D_INNER // SSD_GROUPS))
    g = g * lax.rsqrt(jnp.mean(g * g, axis=-1, keepdims=True) + RMS_EPS)
    return g.reshape(yz.shape) * w.astype(jnp.float32)


def _parallel_mixer(h, w_in, conv_dw_w, conv_dw_b, conv_ln_w, conv_ln_b, conv_pw_out,
                    ssd_conv_w, ssd_conv_b, ssd_dt_bias_f, ssd_dt_bias_b, ssd_a_log_f, ssd_a_log_b,
                    ssd_d, ssd_norm_w, ssd_out, w_out):
    bsz, s, _ = h.shape
    f32 = jnp.float32
    proj = h @ w_in
    u_glu, z, xbc, dt_f, dt_b, gates = jnp.split(proj, IN_SPLITS, axis=-1)
    u_val, u_gate = jnp.split(u_glu, 2, axis=-1)
    u = u_val * jax.nn.sigmoid(u_gate)
    u = _dw_conv(u, conv_dw_w, conv_dw_b)
    u = jax.nn.silu(_layer_norm(u, conv_ln_w, conv_ln_b))
    branch_a = u @ conv_pw_out
    xbc = jax.nn.silu(_dw_conv(xbc, ssd_conv_w, ssd_conv_b)).astype(f32)
    xs, bm, cm = jnp.split(xbc, [SSD_D_INNER, SSD_D_INNER + SSD_GROUPS * SSD_STATE], axis=-1)
    xs = xs.reshape(bsz, s, SSD_HEADS, SSD_HEAD_DIM)
    bm = bm.reshape(bsz, s, SSD_GROUPS, SSD_STATE)
    cm = cm.reshape(bsz, s, SSD_GROUPS, SSD_STATE)
    dtf = jax.nn.softplus(dt_f.astype(f32) + ssd_dt_bias_f.astype(f32))
    dtb = jax.nn.softplus(dt_b.astype(f32) + ssd_dt_bias_b.astype(f32))
    a_f = -jnp.exp(ssd_a_log_f.astype(f32))
    a_b = -jnp.exp(ssd_a_log_b.astype(f32))
    y_fwd = _ssd_chunked(xs, dtf, a_f, bm, cm)
    y_bwd = jnp.flip(_ssd_chunked(jnp.flip(xs, 1), jnp.flip(dtb, 1), a_b,
                                  jnp.flip(bm, 1), jnp.flip(cm, 1)), 1)
    y = y_fwd + y_bwd + xs * ssd_d.astype(f32)[:, None]
    y = _gated_group_rms_norm(y.reshape(bsz, s, SSD_D_INNER), z, ssd_norm_w).astype(h.dtype)
    branch_b = y @ ssd_out
    g_a, g_b = jnp.split(gates, 2, axis=-1)
    merged = jax.nn.sigmoid(g_a) * branch_a + jax.nn.sigmoid(g_b) * branch_b
    return merged @ w_out


def _cross_attn(h, mem_n, wq, wkv, wo):
    b, s, _ = h.shape
    m = mem_n.shape[1]
    q = (h @ wq).reshape(b, s, XATTN_HEADS, XATTN_HEAD_DIM)
    k, v = jnp.split(mem_n @ wkv, 2, axis=-1)
    k = k.reshape(b, m, XATTN_HEADS, XATTN_HEAD_DIM)
    v = v.reshape(b, m, XATTN_HEADS, XATTN_HEAD_DIM)
    scores = jnp.einsum('bshd,bmhd->bhsm', q, k).astype(jnp.float32) * (1.0 / math.sqrt(XATTN_HEAD_DIM))
    p = jax.nn.softmax(scores, axis=-1).astype(v.dtype)
    o = jnp.einsum('bhsm,bmhd->bshd', p, v).reshape(b, s, D_MODEL)
    return o @ wo


def _hier_moe(h, router_group_w, router_group_b, router_expert_w, router_expert_b,
              expert_w_gate, expert_w_up, expert_w_down):
    bsz, s, d = h.shape
    t = bsz * s
    xt = h.reshape(t, d)
    g_logits = (xt @ router_group_w).astype(jnp.float32) + router_group_b.astype(jnp.float32)
    g_prob = jax.nn.softmax(g_logits, axis=-1)
    g_idx = jnp.argmax(g_logits, axis=-1)
    g_w = jnp.take_along_axis(g_prob, g_idx[:, None], axis=-1)
    e_logits = ((xt @ router_expert_w).astype(jnp.float32) + router_expert_b.astype(jnp.float32)
                ).reshape(t, N_GROUPS, EXPERTS_PER_GROUP)
    e_sel = jnp.take_along_axis(e_logits, g_idx[:, None, None], axis=1)[:, 0]
    e_prob = jax.nn.softmax(e_sel, axis=-1)
    top_v, top_i = lax.top_k(e_prob, TOP_K_EXPERT)
    top_v = top_v / jnp.sum(top_v, axis=-1, keepdims=True)
    gate = g_w * top_v
    expert = g_idx[:, None].astype(jnp.int32) * EXPERTS_PER_GROUP + top_i.astype(jnp.int32)
    n_pairs = t * TOP_K_EXPERT
    e_flat = expert.reshape(-1)
    tok_flat = jnp.repeat(jnp.arange(t, dtype=jnp.int32), TOP_K_EXPERT)
    gate_flat = gate.reshape(-1)
    order = jnp.argsort(e_flat)
    e_sorted = e_flat[order]
    counts = jnp.bincount(e_flat, length=N_EXPERTS)
    starts = jnp.cumsum(counts) - counts
    padded = ((counts + MOE_BLOCK - 1) // MOE_BLOCK) * MOE_BLOCK
    pad_ends = jnp.cumsum(padded)
    pad_starts = pad_ends - padded
    dest = pad_starts[e_sorted] + jnp.arange(n_pairs, dtype=jnp.int32) - starts[e_sorted]
    cap = ((n_pairs + MOE_BLOCK - 1) // MOE_BLOCK) * MOE_BLOCK + N_EXPERTS * MOE_BLOCK
    n_blocks = cap // MOE_BLOCK
    buf_tok = jnp.full((cap,), t, jnp.int32).at[dest].set(tok_flat[order])
    buf_gate = jnp.zeros((cap,), jnp.float32).at[dest].set(gate_flat[order])
    blk_exp = jnp.clip(jnp.searchsorted(pad_ends, jnp.arange(n_blocks, dtype=jnp.int32) * MOE_BLOCK,
                                        side='right'), 0, N_EXPERTS - 1)
    x_pad = jnp.concatenate([xt, jnp.zeros((1, d), xt.dtype)], axis=0)

    def expert_block(args):
        tok, e = args
        xb = x_pad[tok]
        hid = jax.nn.silu(xb @ expert_w_gate[e]) * (xb @ expert_w_up[e])
        return hid @ expert_w_down[e]

    out = lax.map(expert_block, (buf_tok.reshape(n_blocks, MOE_BLOCK), blk_exp))
    out = out.reshape(cap, d) * buf_gate[:, None].astype(out.dtype)
    y = jnp.zeros((t + 1, d), out.dtype).at[buf_tok].add(out)[:t]
    return y.reshape(bsz, s, d)


def _encoder(x, mem, p):
    for l in range(DEPTH):
        h = _rms_norm(x, p['norm_mix_w'][l])
        x = x + _parallel_mixer(h, p['w_in'][l], p['conv_dw_w'][l], p['conv_dw_b'][l],
                                p['conv_ln_w'][l], p['conv_ln_b'][l], p['conv_pw_out'][l],
                                p['ssd_conv_w'][l], p['ssd_conv_b'][l], p['ssd_dt_bias_f'][l],
                                p['ssd_dt_bias_b'][l], p['ssd_a_log_f'][l], p['ssd_a_log_b'][l],
                                p['ssd_d'][l], p['ssd_norm_w'][l], p['ssd_out'][l], p['w_out'][l])
        h = _rms_norm(x, p['norm_xattn_w'][l])
        mem_n = _rms_norm(mem, p['norm_mem_w'][l])
        x = x + _cross_attn(h, mem_n, p['xattn_wq'][l], p['xattn_wkv'][l], p['xattn_wo'][l])
        h = _rms_norm(x, p['norm_ffn_w'][l])
        x = x + _hier_moe(h, p['router_group_w'][l], p['router_group_b'][l],
                          p['router_expert_w'][l], p['router_expert_b'][l],
                          p['expert_w_gate'][l], p['expert_w_up'][l], p['expert_w_down'][l])
    return _rms_norm(x, p['norm_final_w'])


def setup_inputs(seed: int = 0) -> dict:
    key = jax.random.key(seed)
    k = jax.random.split(key, 36)
    f32 = jnp.float32
    L = DEPTH

    def nrm(i, shape, scale):
        return jax.random.normal(k[i], shape, f32) * scale

    def gain(i, shape):
        return 1.0 + nrm(i, shape, 0.02)

    def dt_bias(i):
        dt = jnp.exp(jax.random.uniform(k[i], (L, SSD_HEADS), f32,
                                        minval=math.log(1e-3), maxval=math.log(1e-1)))
        return dt + jnp.log(-jnp.expm1(-dt))

    def a_log(i):
        return jnp.log(jax.random.uniform(k[i], (L, SSD_HEADS), f32, minval=1.0, maxval=16.0))

    return {
        'x_prompt': nrm(0, (BATCH, SEQ, D_MODEL), 1.0),
        'x_sample': nrm(1, (DEC_BATCH, DEC_SEQ, D_MODEL), 1.0),
        'mem_prompt': nrm(2, (BATCH, MEM_TOKENS, D_MODEL), 1.0),
        'mem_sample': nrm(3, (DEC_BATCH, MEM_TOKENS, D_MODEL), 1.0),
        'norm_mix_w': gain(4, (L, D_MODEL)),
        'w_in': nrm(5, (L, D_MODEL, N_IN), D_MODEL ** -0.5),
        'conv_dw_w': nrm(6, (L, CONV_WIDTH, D_CONV), CONV_WIDTH ** -0.5),
        'conv_dw_b': nrm(7, (L, D_CONV), 0.02),
        'conv_ln_w': gain(8, (L, D_CONV)),
        'conv_ln_b': nrm(9, (L, D_CONV), 0.02),
        'conv_pw_out': nrm(10, (L, D_CONV, D_MODEL), D_CONV ** -0.5),
        'ssd_conv_w': nrm(11, (L, SSD_CONV_WIDTH, SSD_XBC), SSD_CONV_WIDTH ** -0.5),
        'ssd_conv_b': nrm(12, (L, SSD_XBC), 0.02),
        'ssd_dt_bias_f': dt_bias(13),
        'ssd_dt_bias_b': dt_bias(14),
        'ssd_a_log_f': a_log(15),
        'ssd_a_log_b': a_log(16),
        'ssd_d': gain(17, (L, SSD_HEADS)),
        'ssd_norm_w': gain(18, (L, SSD_D_INNER)),
        'ssd_out': nrm(19, (L, SSD_D_INNER, D_MODEL), SSD_D_INNER ** -0.5),
        'w_out': nrm(20, (L, D_MODEL, D_MODEL), D_MODEL ** -0.5),
        'norm_xattn_w': gain(21, (L, D_MODEL)),
        'norm_mem_w': gain(22, (L, D_MODEL)),
        'xattn_wq': nrm(23, (L, D_MODEL, D_MODEL), D_MODEL ** -0.5),
        'xattn_wkv': nrm(24, (L, D_MODEL, 2 * D_MODEL), D_MODEL ** -0.5),
        'xattn_wo': nrm(25, (L, D_MODEL, D_MODEL), D_MODEL ** -0.5),
        'norm_ffn_w': gain(26, (L, D_MODEL)),
        'router_group_w': nrm(27, (L, D_MODEL, N_GROUPS), D_MODEL ** -0.5),
        'router_group_b': nrm(28, (L, N_GROUPS), 0.01),
        'router_expert_w': nrm(29, (L, D_MODEL, N_EXPERTS), D_MODEL ** -0.5),
        'router_expert_b': nrm(30, (L, N_EXPERTS), 0.01),
        'expert_w_gate': nrm(31, (L, N_EXPERTS, D_MODEL, EXPERT_FF), D_MODEL ** -0.5),
        'expert_w_up': nrm(32, (L, N_EXPERTS, D_MODEL, EXPERT_FF), D_MODEL ** -0.5),
        'expert_w_down': nrm(33, (L, N_EXPERTS, EXPERT_FF, D_MODEL), EXPERT_FF ** -0.5),
        'norm_final_w': gain(34, (D_MODEL,)),
    }


def reference(x_prompt, x_sample, mem_prompt, mem_sample, norm_mix_w, w_in, conv_dw_w, conv_dw_b,
              conv_ln_w, conv_ln_b, conv_pw_out, ssd_conv_w, ssd_conv_b, ssd_dt_bias_f, ssd_dt_bias_b,
              ssd_a_log_f, ssd_a_log_b, ssd_d, ssd_norm_w, ssd_out, w_out, norm_xattn_w, norm_mem_w,
              xattn_wq, xattn_wkv, xattn_wo, norm_ffn_w, router_group_w, router_group_b,
              router_expert_w, router_expert_b, expert_w_gate, expert_w_up, expert_w_down,
              norm_final_w):
    params = {
        'norm_mix_w': norm_mix_w, 'w_in': w_in, 'conv_dw_w': conv_dw_w, 'conv_dw_b': conv_dw_b,
        'conv_ln_w': conv_ln_w, 'conv_ln_b': conv_ln_b, 'conv_pw_out': conv_pw_out,
        'ssd_conv_w': ssd_conv_w, 'ssd_conv_b': ssd_conv_b, 'ssd_dt_bias_f': ssd_dt_bias_f,
        'ssd_dt_bias_b': ssd_dt_bias_b, 'ssd_a_log_f': ssd_a_log_f, 'ssd_a_log_b': ssd_a_log_b,
        'ssd_d': ssd_d, 'ssd_norm_w': ssd_norm_w, 'ssd_out': ssd_out, 'w_out': w_out,
        'norm_xattn_w': norm_xattn_w, 'norm_mem_w': norm_mem_w, 'xattn_wq': xattn_wq,
        'xattn_wkv': xattn_wkv, 'xattn_wo': xattn_wo, 'norm_ffn_w': norm_ffn_w,
        'router_group_w': router_group_w, 'router_group_b': router_group_b,
        'router_expert_w': router_expert_w, 'router_expert_b': router_expert_b,
        'expert_w_gate': expert_w_gate, 'expert_w_up': expert_w_up, 'expert_w_down': expert_w_down,
        'norm_final_w': norm_final_w,
    }
    y_prompt = _encoder(x_prompt, mem_prompt, params)
    y_sample = _encoder(x_sample, mem_sample, params)
    return (y_prompt, y_sample)
```

```python
import functools
import math

import jax
import jax.numpy as jnp
from jax import lax
from jax.experimental import pallas as pl
from jax.experimental.pallas import tpu as pltpu

F32 = jnp.float32
BF16 = jnp.bfloat16

D_MODEL = 1024
CONV_WIDTH = 31
SSD_D_INNER = 2048
SSD_HEAD_DIM = 64
SSD_HEADS = 32
SSD_GROUPS = 8
SSD_STATE = 128
SSD_CONV_WIDTH = 5
SSD_CHUNK = 128
GROUP_CH = SSD_D_INNER // SSD_GROUPS
HEADS_PER_GROUP = SSD_HEADS // SSD_GROUPS
XATTN_HEADS = 4
XATTN_HEAD_DIM = D_MODEL // XATTN_HEADS
N_GROUPS = 8
EXPERTS_PER_GROUP = 8
N_EXPERTS = 64
EXPERT_FF = 512
MOE_BLOCK = 128
RMS_EPS = 1e-6
LN_EPS = 1e-5

LANES = 128
HALO = 16
VMEM_LIMIT = 56 * 1024 * 1024


def _cparams(sem):
    return pltpu.CompilerParams(dimension_semantics=sem, vmem_limit_bytes=VMEM_LIMIT)


def _rms(x, w):
    ms = jnp.mean(x * x, axis=-1, keepdims=True)
    return x * lax.rsqrt(ms + RMS_EPS) * w


def _sigmoid(x):
    return 1.0 / (1.0 + jnp.exp(-x))


def _silu(x):
    return x * _sigmoid(x)


def _softplus(x):
    return jnp.maximum(x, 0.0) + jnp.log1p(jnp.exp(-jnp.abs(x)))


CONV_STRIDE = 4
CONV_ROWS = 8 * CONV_STRIDE
CONV_UNROLL = 2


def _depthwise_conv(src, dst, slab, w_ref, b_ref, n_rows, width, first_row, epilogue):
    lanes = pl.ds(slab * LANES, LANES)
    taps = [jnp.broadcast_to(w_ref[k:k + 1, lanes], (8, LANES)) for k in range(width)]
    bias = jnp.broadcast_to(b_ref[:, lanes], (8, LANES))
    step = CONV_ROWS * CONV_UNROLL

    def body(r, carry):
        base = r * step
        for t0 in range(CONV_STRIDE * CONV_UNROLL):
            row = base + (t0 // CONV_STRIDE) * CONV_ROWS + t0 % CONV_STRIDE
            acc = bias
            for k in range(width):
                acc = acc + src[slab, pl.ds(row + first_row - width // 2 + k, 8, stride=CONV_STRIDE), :] * taps[k]
            dst[slab, pl.ds(row, 8, stride=CONV_STRIDE), :] = epilogue(acc)
        return carry

    lax.fori_loop(0, n_rows // step, body, 0)


def _glu_kernel(x_ref, nw_ref, wv_ref, wg_ref, o_ref, h_scr):
    @pl.when(pl.program_id(1) == 0)
    def _():
        h_scr[...] = _rms(x_ref[...], nw_ref[...]).astype(BF16)

    h = h_scr[...]
    v = jnp.dot(h, wv_ref[...], preferred_element_type=F32)
    g = jnp.dot(h, wg_ref[...], preferred_element_type=F32)
    o_ref[...] = (v * _sigmoid(g)).astype(o_ref.dtype)


def _glu_proj(x, nw, wv, wg, tm, tn):
    t, d = x.shape
    n = wv.shape[1]
    return pl.pallas_call(
        _glu_kernel,
        grid=(t // tm, n // tn),
        in_specs=[
            pl.BlockSpec((tm, d), lambda i, j: (i, 0)),
            pl.BlockSpec((1, d), lambda i, j: (0, 0)),
            pl.BlockSpec((d, tn), lambda i, j: (0, j)),
            pl.BlockSpec((d, tn), lambda i, j: (0, j)),
        ],
        out_specs=pl.BlockSpec((tm, tn), lambda i, j: (i, j)),
        out_shape=jax.ShapeDtypeStruct((t, n), BF16),
        scratch_shapes=[pltpu.VMEM((tm, d), BF16)],
        compiler_params=_cparams(("parallel", "arbitrary")),
        name="glu_proj",
    )(x, nw, wv, wg)


def _norm_proj_kernel(epi, x_ref, nw_ref, w_ref, o_ref, h_scr):
    @pl.when(pl.program_id(1) == 0)
    def _():
        h_scr[...] = _rms(x_ref[...], nw_ref[...]).astype(BF16)

    acc = jnp.dot(h_scr[...], w_ref[...], preferred_element_type=F32)
    o_ref[...] = epi(acc).astype(o_ref.dtype)


def _norm_proj(x, nw, w, epi, out_dtype, tm, tn, name):
    t, d = x.shape
    n = w.shape[1]
    return pl.pallas_call(
        functools.partial(_norm_proj_kernel, epi),
        grid=(t // tm, n // tn),
        in_specs=[
            pl.BlockSpec((tm, d), lambda i, j: (i, 0)),
            pl.BlockSpec((1, d), lambda i, j: (0, 0)),
            pl.BlockSpec((d, tn), lambda i, j: (0, j)),
        ],
        out_specs=pl.BlockSpec((tm, tn), lambda i, j: (i, j)),
        out_shape=jax.ShapeDtypeStruct((t, n), out_dtype),
        scratch_shapes=[pltpu.VMEM((tm, d), BF16)],
        compiler_params=_cparams(("parallel", "arbitrary")),
        name=name,
    )(x, nw, w)


def _gated_proj_kernel(y_ref, w_ref, g_ref, o_ref):
    acc = jnp.dot(y_ref[...], w_ref[...], preferred_element_type=F32)
    o_ref[...] = (acc * g_ref[...].astype(F32)).astype(o_ref.dtype)


def _gated_proj(y, w, g, g_col_block, tm):
    t, k = y.shape
    n = w.shape[1]
    return pl.pallas_call(
        _gated_proj_kernel,
        grid=(t // tm,),
        in_specs=[
            pl.BlockSpec((tm, k), lambda i: (i, 0)),
            pl.BlockSpec((k, n), lambda i: (0, 0)),
            pl.BlockSpec((tm, n), lambda i: (i, g_col_block)),
        ],
        out_specs=pl.BlockSpec((tm, n), lambda i: (i, 0)),
        out_shape=jax.ShapeDtypeStruct((t, n), BF16),
        compiler_params=_cparams(("parallel",)),
        name="ssd_out_proj",
    )(y, w, g)


def _dtprep_kernel(raw_ref, bias_ref, alog_ref, dt_ref, acs_ref):
    rows = raw_ref.shape[0]
    a_head = -jnp.exp(alog_ref[...])
    row = lax.broadcasted_iota(jnp.int32, (SSD_CHUNK, LANES), 0)
    lane = lax.broadcasted_iota(jnp.int32, (SSD_CHUNK, LANES), 1)
    is_bwd = (lane % (2 * HEADS_PER_GROUP)) >= HEADS_PER_GROUP
    for c in range(rows // SSD_CHUNK):
        sl = pl.ds(c * SSD_CHUNK, SSD_CHUNK)
        dt = _softplus(raw_ref[sl, :] + bias_ref[...])
        dt_ref[sl, :] = dt
        a = dt * a_head
        fwd = a
        bwd = a
        sh = 1
        while sh < SSD_CHUNK:
            fwd = fwd + jnp.where(row >= sh, pltpu.roll(fwd, sh, 0), 0.0)
            bwd = bwd + jnp.where(row < SSD_CHUNK - sh, pltpu.roll(bwd, SSD_CHUNK - sh, 0), 0.0)
            sh *= 2
        acs_ref[sl, :] = jnp.where(is_bwd, bwd, fwd)


def _dtprep(raw, bias, alog, tm):
    t = raw.shape[0]
    return pl.pallas_call(
        _dtprep_kernel,
        grid=(t // tm,),
        in_specs=[
            pl.BlockSpec((tm, LANES), lambda i: (i, 0)),
            pl.BlockSpec((1, LANES), lambda i: (0, 0)),
            pl.BlockSpec((1, LANES), lambda i: (0, 0)),
        ],
        out_specs=[
            pl.BlockSpec((tm, LANES), lambda i: (i, 0)),
            pl.BlockSpec((tm, LANES), lambda i: (i, 0)),
        ],
        out_shape=[jax.ShapeDtypeStruct((t, LANES), F32)] * 2,
        compiler_params=_cparams(("parallel",)),
        name="dt_prep",
    )(raw, bias, alog)


SSD_PAD = 8
XBC_W = GROUP_CH + 2 * SSD_STATE


def _expand_heads(cols):
    n = cols.shape[0]
    head = lax.broadcasted_iota(jnp.int32, (n, GROUP_CH), 1) // SSD_HEAD_DIM
    out = jnp.broadcast_to(cols[:, 3:4], (n, GROUP_CH))
    for h in (2, 1, 0):
        out = jnp.where(head == h, cols[:, h:h + 1], out)
    return out


def _ssd_kernel(xs_ref, b_ref, c_ref, sz_ref, cw_ref, cb_ref, dthm_ref, acshm_ref, acstm_ref, aend_ref,
                dskip_ref, nw_ref, o_ref, raw_scr, act_scr, y_scr, st_scr):
    s = xs_ref.shape[1]
    nc = s // SSD_CHUNK
    L = SSD_CHUNK

    zpad = jnp.zeros((SSD_PAD, LANES), F32)
    for j in range(XBC_W // LANES):
        raw_scr[j, pl.ds(0, SSD_PAD), :] = zpad
        raw_scr[j, pl.ds(SSD_PAD + s, SSD_PAD), :] = zpad
    raw_scr[0, pl.ds(SSD_PAD, s), :] = xs_ref[0, :, :LANES].astype(F32)
    raw_scr[1, pl.ds(SSD_PAD, s), :] = xs_ref[0, :, LANES:].astype(F32)
    raw_scr[2, pl.ds(SSD_PAD, s), :] = b_ref[0].astype(F32)
    raw_scr[3, pl.ds(SSD_PAD, s), :] = c_ref[0].astype(F32)
    for j in range(XBC_W // LANES):
        _depthwise_conv(raw_scr, act_scr, j, cw_ref.at[0], cb_ref.at[0], s, SSD_CONV_WIDTH, SSD_PAD, _silu)

    li = lax.broadcasted_iota(jnp.int32, (L, L), 0)
    si = lax.broadcasted_iota(jnp.int32, (L, L), 1)
    head_of_ch = lax.broadcasted_iota(jnp.int32, (L, GROUP_CH), 1) // SSD_HEAD_DIM

    def chunk(c, direction):
        r0 = pl.multiple_of(c * L, L)
        x = jnp.concatenate([act_scr[0, pl.ds(r0, L), :], act_scr[1, pl.ds(r0, L), :]], axis=1)
        xb = x.astype(BF16)
        bk = act_scr[2, pl.ds(r0, L), :].astype(BF16)
        ck = act_scr[3, pl.ds(r0, L), :].astype(BF16)
        acs_tm = acstm_ref[0, 0, pl.ds(r0, L), :]
        acs_hm = acshm_ref[0, 0, c]
        dt_hm = dthm_ref[0, 0, c]
        h0 = direction * HEADS_PER_GROUP
        keep = (li >= si) if direction == 0 else (si >= li)
        edge = L - 1 if direction == 0 else 0

        cb = lax.dot_general(ck, bk, (((1,), (1,)), ((), ())), preferred_element_type=F32)
        ms = []
        xbd = []
        for h in range(HEADS_PER_GROUP):
            col = acs_tm[:, h0 + h:h0 + h + 1]
            rowv = acs_hm[h0 + h:h0 + h + 1, :]
            decay = jnp.exp(jnp.where(keep, col - rowv, -jnp.inf))
            ms.append((cb * decay * dt_hm[h0 + h:h0 + h + 1, :]).astype(BF16))
            xbd.append(jnp.where(head_of_ch == h, xb, jnp.zeros_like(xb)))
        y = jnp.dot(jnp.concatenate(ms, axis=1), jnp.concatenate(xbd, axis=0), preferred_element_type=F32)

        acs_dir = acs_tm[:, h0:h0 + HEADS_PER_GROUP]
        state = st_scr[direction]
        y = y + jnp.dot(ck, state.astype(BF16), preferred_element_type=F32) * _expand_heads(jnp.exp(acs_dir))

        a_end = acs_tm[edge:edge + 1, h0:h0 + HEADS_PER_GROUP]
        dt_tm = acstm_ref[0, 1, pl.ds(r0, L), :][:, h0:h0 + HEADS_PER_GROUP]
        w_state = _expand_heads(jnp.exp(a_end - acs_dir) * dt_tm)
        upd = lax.dot_general(bk, (x * w_state).astype(BF16), (((0,), (0,)), ((), ())),
                              preferred_element_type=F32)
        st_scr[direction] = state * jnp.exp(aend_ref[0, 0, c, direction:direction + 1, :]) + upd
        return r0, x, y

    st_scr[...] = jnp.zeros_like(st_scr)

    def fwd_body(c, carry):
        r0, _, y = chunk(c, 0)
        y_scr[pl.ds(r0, L), :] = y
        return carry

    lax.fori_loop(0, nc, fwd_body, 0)

    def bwd_body(i, carry):
        c = nc - 1 - i
        r0, x, y = chunk(c, 1)
        y = y + y_scr[pl.ds(r0, L), :] + x * dskip_ref[0]
        yz = y * sz_ref[0, pl.ds(r0, L), :].astype(F32)
        ms = jnp.mean(yz * yz, axis=-1, keepdims=True)
        o_ref[0, pl.ds(r0, L), :] = (yz * lax.rsqrt(ms + RMS_EPS) * nw_ref[0]).astype(o_ref.dtype)
        return carry

    lax.fori_loop(0, nc, bwd_body, 0)


def _ssd(xbc, sz, cw, cb, dt_hm, acs_hm, tm2, aend, dskip, nw):
    nb, s, _ = xbc.shape
    nc = s // SSD_CHUNK
    return pl.pallas_call(
        _ssd_kernel,
        grid=(nb, SSD_GROUPS),
        in_specs=[
            pl.BlockSpec((1, s, GROUP_CH), lambda b, g: (b, 0, g)),
            pl.BlockSpec((1, s, SSD_STATE), lambda b, g: (b, 0, SSD_D_INNER // SSD_STATE + g)),
            pl.BlockSpec((1, s, SSD_STATE), lambda b, g: (b, 0, SSD_D_INNER // SSD_STATE + SSD_GROUPS + g)),
            pl.BlockSpec((1, s, GROUP_CH), lambda b, g: (b, 0, g)),
            pl.BlockSpec((1, 8, XBC_W), lambda b, g: (g, 0, 0)),
            pl.BlockSpec((1, 1, XBC_W), lambda b, g: (g, 0, 0)),
            pl.BlockSpec((1, 1, nc, 8, SSD_CHUNK), lambda b, g: (b, g, 0, 0, 0)),
            pl.BlockSpec((1, 1, nc, 8, SSD_CHUNK), lambda b, g: (b, g, 0, 0, 0)),
            pl.BlockSpec((1, 2, s, 8), lambda b, g: (b * SSD_GROUPS + g, 0, 0, 0)),
            pl.BlockSpec((1, 1, nc, 2, GROUP_CH), lambda b, g: (b, g, 0, 0, 0)),
            pl.BlockSpec((1, 1, GROUP_CH), lambda b, g: (g, 0, 0)),
            pl.BlockSpec((1, 1, GROUP_CH), lambda b, g: (g, 0, 0)),
        ],
        out_specs=pl.BlockSpec((1, s, GROUP_CH), lambda b, g: (b, 0, g)),
        out_shape=jax.ShapeDtypeStruct((nb, s, SSD_D_INNER), BF16),
        scratch_shapes=[
            pltpu.VMEM((XBC_W // LANES, s + 2 * SSD_PAD, LANES), F32),
            pltpu.VMEM((XBC_W // LANES, s, LANES), F32),
            pltpu.VMEM((s, GROUP_CH), F32),
            pltpu.VMEM((2, SSD_STATE, GROUP_CH), F32),
        ],
        compiler_params=_cparams(("parallel", "parallel")),
        name="ssd_scan",
    )(xbc, xbc, xbc, sz, cw, cb, dt_hm, acs_hm, tm2, aend, dskip, nw)


D_SLABS = D_MODEL // LANES


def _conformer_kernel(u_ref, up_ref, un_ref, mb_ref, g_ref, x_ref, dww_ref, dwb_ref, lnw_ref, lnb_ref,
                      pw_ref, wo_ref, o_ref, u_scr, c_scr):
    i = pl.program_id(1)
    n = pl.num_programs(1)
    tq = u_ref.shape[1]
    for j in range(D_SLABS):
        cs = slice(j * LANES, (j + 1) * LANES)
        u_scr[j, pl.ds(0, HALO), :] = jnp.where(i > 0, up_ref[0, :, cs].astype(F32), 0.0)
        u_scr[j, pl.ds(HALO, tq), :] = u_ref[0, :, cs].astype(F32)
        u_scr[j, pl.ds(HALO + tq, HALO), :] = jnp.where(i < n - 1, un_ref[0, :, cs].astype(F32), 0.0)
    for j in range(D_SLABS):
        _depthwise_conv(u_scr, c_scr, j, dww_ref, dwb_ref, tq, CONV_WIDTH, HALO, lambda a: a)

    cv = jnp.concatenate([c_scr[j] for j in range(D_SLABS)], axis=1)
    mu = jnp.mean(cv, axis=-1, keepdims=True)
    xc = cv - mu
    var = jnp.mean(xc * xc, axis=-1, keepdims=True)
    y = xc * lax.rsqrt(var + LN_EPS) * lnw_ref[...] + lnb_ref[...]
    a = jnp.dot(_silu(y).astype(BF16), pw_ref[...], preferred_element_type=F32)
    merged = g_ref[0].astype(F32) * a + mb_ref[0].astype(F32)
    o_ref[0] = x_ref[0] + jnp.dot(merged.astype(BF16), wo_ref[...], preferred_element_type=F32)


def _conformer(u, mb, gates, x, dww, dwb, lnw, lnb, pw, wo, tq):
    nb, s, d = x.shape
    hb = tq // HALO
    nh = s // HALO
    wspec = pl.BlockSpec((d, d), lambda b, i: (0, 0))
    vspec = pl.BlockSpec((1, d), lambda b, i: (0, 0))
    return pl.pallas_call(
        _conformer_kernel,
        grid=(nb, s // tq),
        in_specs=[
            pl.BlockSpec((1, tq, d), lambda b, i: (b, i, 0)),
            pl.BlockSpec((1, HALO, d), lambda b, i: (b, jnp.maximum(i * hb - 1, 0), 0)),
            pl.BlockSpec((1, HALO, d), lambda b, i: (b, jnp.minimum((i + 1) * hb, nh - 1), 0)),
            pl.BlockSpec((1, tq, d), lambda b, i: (b, i, 0)),
            pl.BlockSpec((1, tq, d), lambda b, i: (b, i, 0)),
            pl.BlockSpec((1, tq, d), lambda b, i: (b, i, 0)),
            pl.BlockSpec((32, d), lambda b, i: (0, 0)),
            vspec, vspec, vspec, wspec, wspec,
        ],
        out_specs=pl.BlockSpec((1, tq, d), lambda b, i: (b, i, 0)),
        out_shape=jax.ShapeDtypeStruct((nb, s, d), F32),
        scratch_shapes=[pltpu.VMEM((D_SLABS, tq + 2 * HALO, LANES), F32), pltpu.VMEM((D_SLABS, tq, LANES), F32)],
        compiler_params=_cparams(("parallel", "parallel")),
        name="conformer_merge",
    )(u, u, u, mb, gates, x, dww, dwb, lnw, lnb, pw, wo)


def _attn_kernel(x_ref, k_ref, v_ref, nw_ref, wq_ref, wo_ref, nf_ref, wr_ref, br_ref, x2_ref, h3_ref, lg_ref):
    x = x_ref[0]
    h = _rms(x, nw_ref[...]).astype(BF16)
    q = jnp.dot(h, wq_ref[...], preferred_element_type=F32)
    scale = 1.0 / math.sqrt(XATTN_HEAD_DIM)
    outs = []
    for hd in range(XATTN_HEADS):
        cs = slice(hd * XATTN_HEAD_DIM, (hd + 1) * XATTN_HEAD_DIM)
        sc = lax.dot_general(q[:, cs].astype(BF16), k_ref[0, :, cs], (((1,), (1,)), ((), ())),
                             preferred_element_type=F32) * scale
        m = jnp.max(sc, axis=-1, keepdims=True)
        e = jnp.exp(sc - m)
        p = e / jnp.sum(e, axis=-1, keepdims=True)
        outs.append(jnp.dot(p.astype(BF16), v_ref[0, :, cs], preferred_element_type=F32))
    o = jnp.concatenate(outs, axis=1).astype(BF16)
    x2 = x + jnp.dot(o, wo_ref[...], preferred_element_type=F32)
    x2_ref[0] = x2
    h3 = _rms(x2, nf_ref[...])
    h3_ref[0] = h3
    lg_ref[0] = jnp.dot(h3.astype(BF16), wr_ref[...], preferred_element_type=F32) + br_ref[...]


def _attention(x, kv, nw, wq, wo, nf, wr, br, tq):
    nb, s, d = x.shape
    m = kv.shape[1]
    wspec = pl.BlockSpec((d, d), lambda b, i: (0, 0))
    vspec = pl.BlockSpec((1, d), lambda b, i: (0, 0))
    return pl.pallas_call(
        _attn_kernel,
        grid=(nb, s // tq),
        in_specs=[
            pl.BlockSpec((1, tq, d), lambda b, i: (b, i, 0)),
            pl.BlockSpec((1, m, d), lambda b, i: (b, 0, 0)),
            pl.BlockSpec((1, m, d), lambda b, i: (b, 0, 1)),
            vspec, wspec, wspec, vspec,
            pl.BlockSpec((d, LANES), lambda b, i: (0, 0)),
            pl.BlockSpec((1, LANES), lambda b, i: (0, 0)),
        ],
        out_specs=[
            pl.BlockSpec((1, tq, d), lambda b, i: (b, i, 0)),
            pl.BlockSpec((1, tq, d), lambda b, i: (b, i, 0)),
            pl.BlockSpec((1, tq, LANES), lambda b, i: (b, i, 0)),
        ],
        out_shape=[
            jax.ShapeDtypeStruct((nb, s, d), F32),
            jax.ShapeDtypeStruct((nb, s, d), F32),
            jax.ShapeDtypeStruct((nb, s, LANES), F32),
        ],
        compiler_params=_cparams(("parallel", "parallel")),
        name="cross_attn_router",
    )(x, kv, kv, nw, wq, wo, nf, wr, br)


def _route_kernel(lg_ref, eid_ref, gate_ref):
    lg = lg_ref[...]
    shape = lg.shape
    lane = lax.broadcasted_iota(jnp.int32, shape, 1)
    lane_f = lane.astype(F32)
    big = float(LANES)
    neg = -jnp.inf

    is_g = lane < N_GROUPS
    gl = jnp.where(is_g, lg, neg)
    gmax = jnp.max(gl, axis=-1, keepdims=True)
    g_idx = jnp.min(jnp.where(gl == gmax, lane_f, big), axis=-1, keepdims=True)
    g_sum = jnp.sum(jnp.where(is_g, jnp.exp(gl - gmax), 0.0), axis=-1, keepdims=True)
    g_w = 1.0 / g_sum

    e_grp = lax.shift_right_arithmetic(lane - N_GROUPS, 3).astype(F32)
    is_e = (lane >= N_GROUPS) & (lane < N_GROUPS + N_EXPERTS) & (e_grp == g_idx)
    el = jnp.where(is_e, lg, neg)
    emax = jnp.max(el, axis=-1, keepdims=True)
    ee = jnp.where(is_e, jnp.exp(el - emax), 0.0)
    ep = ee / jnp.sum(ee, axis=-1, keepdims=True)
    ep = jnp.where(is_e, ep, -1.0)
    v1 = jnp.max(ep, axis=-1, keepdims=True)
    i1 = jnp.min(jnp.where(ep == v1, lane_f, big), axis=-1, keepdims=True)
    ep2 = jnp.where(lane_f == i1, -1.0, ep)
    v2 = jnp.max(ep2, axis=-1, keepdims=True)
    i2 = jnp.min(jnp.where(ep2 == v2, lane_f, big), axis=-1, keepdims=True)
    tot = v1 + v2
    gate1 = g_w * (v1 / tot)
    gate2 = g_w * (v2 / tot)
    e1 = (i1 - N_GROUPS).astype(jnp.int32)
    e2 = (i2 - N_GROUPS).astype(jnp.int32)
    eid_ref[...] = jnp.where(lane == 0, e1, jnp.where(lane == 1, e2, 0))
    gate_ref[...] = jnp.where(lane == 0, gate1, jnp.where(lane == 1, gate2, 0.0))


def _route(lg, tm):
    t = lg.shape[0]
    spec = pl.BlockSpec((tm, LANES), lambda i: (i, 0))
    return pl.pallas_call(
        _route_kernel,
        grid=(t // tm,),
        in_specs=[spec],
        out_specs=[spec, spec],
        out_shape=[jax.ShapeDtypeStruct((t, LANES), jnp.int32), jax.ShapeDtypeStruct((t, LANES), F32)],
        compiler_params=_cparams(("parallel",)),
        name="moe_route",
    )(lg)


def _row_copy(src_hbm, dst_vmem, sem, row, slot):
    return pltpu.make_async_copy(src_hbm.at[pl.ds(row, 1)], dst_vmem.at[pl.ds(slot, 1)], sem)


def _expert_kernel(be_ref, na_ref, tok_ref, h_hbm, wg_ref, wu_ref, wd_ref, o_ref, x_scr, sem):
    j = pl.program_id(0)

    @pl.when(j < na_ref[0])
    def _():
        def start(r, carry):
            _row_copy(h_hbm, x_scr, sem, tok_ref[0, 0, r], r).start()
            return carry

        lax.fori_loop(0, MOE_BLOCK, start, 0)

        def wait(r, carry):
            _row_copy(h_hbm, x_scr, sem, 0, r).wait()
            return carry

        lax.fori_loop(0, MOE_BLOCK, wait, 0)
        x = x_scr[...].astype(BF16)
        g = jnp.dot(x, wg_ref[0], preferred_element_type=F32)
        u = jnp.dot(x, wu_ref[0], preferred_element_type=F32)
        hid = (_silu(g) * u).astype(BF16)
        o_ref[...] = jnp.dot(hid, wd_ref[0], preferred_element_type=F32)

    @pl.when(j >= na_ref[0])
    def _():
        o_ref[...] = jnp.zeros_like(o_ref)


def _experts(blk_exp, n_active, buf_tok, h3, wg, wu, wd):
    n_blocks = blk_exp.shape[0]
    d = h3.shape[1]
    ff = wg.shape[2]
    grid_spec = pltpu.PrefetchScalarGridSpec(
        num_scalar_prefetch=2,
        grid=(n_blocks,),
        in_specs=[
            pl.BlockSpec((1, 1, MOE_BLOCK), lambda j, be, na: (j, 0, 0), memory_space=pltpu.SMEM),
            pl.BlockSpec(memory_space=pl.ANY),
            pl.BlockSpec((1, d, ff), lambda j, be, na: (be[j], 0, 0)),
            pl.BlockSpec((1, d, ff), lambda j, be, na: (be[j], 0, 0)),
            pl.BlockSpec((1, ff, d), lambda j, be, na: (be[j], 0, 0)),
        ],
        out_specs=pl.BlockSpec((MOE_BLOCK, d), lambda j, be, na: (j, 0)),
        scratch_shapes=[pltpu.VMEM((MOE_BLOCK, d), F32), pltpu.SemaphoreType.DMA(())],
    )
    return pl.pallas_call(
        _expert_kernel,
        grid_spec=grid_spec,
        out_shape=jax.ShapeDtypeStruct((n_blocks * MOE_BLOCK, d), F32),
        compiler_params=_cparams(("arbitrary",)),
        name="moe_experts",
    )(blk_exp, n_active, buf_tok.reshape(n_blocks, 1, MOE_BLOCK), h3, wg, wu, wd)


def _combine_kernel(pos_ref, x_ref, gate_ref, nw_ref, ys_hbm, o_ref, y_scr, sem):
    tm = x_ref.shape[0]

    def start(r, carry):
        _row_copy(ys_hbm, y_scr, sem, pos_ref[0, 0, r], r).start()
        return carry

    lax.fori_loop(0, 2 * tm, start, 0)

    def wait(r, carry):
        _row_copy(ys_hbm, y_scr, sem, 0, r).wait()
        return carry

    lax.fori_loop(0, 2 * tm, wait, 0)
    gate = gate_ref[...]
    y = x_ref[...] + (gate[:, 0:1] * y_scr[pl.ds(0, tm), :] + gate[:, 1:2] * y_scr[pl.ds(tm, tm), :])
    o_ref[...] = _rms(y, nw_ref[...])


def _combine(pos, x2, gates, nw, ys, tm):
    t, d = x2.shape
    return pl.pallas_call(
        _combine_kernel,
        grid=(t // tm,),
        in_specs=[
            pl.BlockSpec((1, 1, 2 * tm), lambda i: (i, 0, 0), memory_space=pltpu.SMEM),
            pl.BlockSpec((tm, d), lambda i: (i, 0)),
            pl.BlockSpec((tm, LANES), lambda i: (i, 0)),
            pl.BlockSpec((1, d), lambda i: (0, 0)),
            pl.BlockSpec(memory_space=pl.ANY),
        ],
        out_specs=pl.BlockSpec((tm, d), lambda i: (i, 0)),
        out_shape=jax.ShapeDtypeStruct((t, d), F32),
        scratch_shapes=[pltpu.VMEM((2 * tm, d), F32), pltpu.SemaphoreType.DMA(())],
        compiler_params=_cparams(("arbitrary",)),
        name="moe_combine",
    )(pos, x2, gates, nw, ys)


def _routing_plan(expert):
    t = expert.shape[0]
    n_pairs = 2 * t
    e_flat = expert.reshape(-1)
    order = jnp.argsort(e_flat)
    e_sorted = e_flat[order]
    counts = jnp.bincount(e_flat, length=N_EXPERTS)
    starts = jnp.cumsum(counts) - counts
    padded = ((counts + MOE_BLOCK - 1) // MOE_BLOCK) * MOE_BLOCK
    pad_ends = jnp.cumsum(padded)
    pad_starts = pad_ends - padded
    dest = (pad_starts[e_sorted] + jnp.arange(n_pairs, dtype=jnp.int32) - starts[e_sorted]).astype(jnp.int32)
    cap = ((n_pairs + MOE_BLOCK - 1) // MOE_BLOCK) * MOE_BLOCK + N_EXPERTS * MOE_BLOCK
    n_blocks = cap // MOE_BLOCK
    tok_sorted = (order // 2).astype(jnp.int32)
    buf_tok = jnp.zeros((cap,), jnp.int32).at[dest].set(tok_sorted)
    pos = jnp.zeros((n_pairs,), jnp.int32).at[order].set(dest).reshape(t, 2)
    blk_exp = jnp.clip(jnp.searchsorted(pad_ends, jnp.arange(n_blocks, dtype=jnp.int32) * MOE_BLOCK,
                                        side='right'), 0, N_EXPERTS - 1).astype(jnp.int32)
    n_active = (pad_ends[-1] // MOE_BLOCK).astype(jnp.int32).reshape(1)
    return buf_tok, pos, blk_exp, n_active


def _pick(n, pref):
    t = pref
    while n % t:
        t //= 2
    return t


def _encoder(x, mem, p):
    nb, s, d = x.shape
    t = nb * s
    nc = s // SSD_CHUNK
    tm = _pick(t, 1024)
    tq = _pick(s, 512)
    xf = x.reshape(t, d)

    perm = jnp.arange(SSD_HEADS).reshape(SSD_GROUPS, HEADS_PER_GROUP)
    perm = jnp.concatenate([perm, perm + SSD_HEADS], axis=1).reshape(-1)

    w_in = p['w_in']
    o_z = 2 * D_MODEL
    o_x = o_z + SSD_D_INNER
    o_dt = o_x + SSD_D_INNER + 2 * SSD_GROUPS * SSD_STATE
    o_g = o_dt + 2 * SSD_HEADS
    w_val = w_in[:, :D_MODEL].astype(BF16)
    w_gate = w_in[:, D_MODEL:o_z].astype(BF16)
    w_z = w_in[:, o_z:o_x].astype(BF16)
    w_xbc = w_in[:, o_x:o_dt].astype(BF16)
    w_dt = jnp.pad(w_in[:, o_dt:o_g][:, perm], ((0, 0), (0, LANES - 2 * SSD_HEADS))).astype(BF16)
    w_g = w_in[:, o_g:].astype(BF16)
    nmix = p['norm_mix_w'].reshape(1, d)

    u = _glu_proj(xf, nmix, w_val, w_gate, tm, 512)
    sz = _norm_proj(xf, nmix, w_z, _silu, BF16, tm, 512, "z_proj")
    xbc = _norm_proj(xf, nmix, w_xbc, lambda a: a, BF16, tm, 512, "xbc_proj")
    dt_raw = _norm_proj(xf, nmix, w_dt, lambda a: a, F32, tm, LANES, "dt_proj")
    gates = _norm_proj(xf, nmix, w_g, _sigmoid, BF16, tm, 512, "gate_proj")

    pad = LANES - 2 * SSD_HEADS
    dt_bias = jnp.pad(jnp.concatenate([p['ssd_dt_bias_f'], p['ssd_dt_bias_b']])[perm], (0, pad)).reshape(1, LANES)
    a_log = jnp.pad(jnp.concatenate([p['ssd_a_log_f'], p['ssd_a_log_b']])[perm], (0, pad)).reshape(1, LANES)
    dtv, acs = _dtprep(dt_raw, dt_bias, a_log, tm)

    def per_group_tm(a):
        return a[:, :2 * SSD_HEADS].reshape(nb, s, SSD_GROUPS, 8).transpose(0, 2, 1, 3)

    def per_group_hm(a):
        return a[:, :2 * SSD_HEADS].reshape(nb, nc, SSD_CHUNK, SSD_GROUPS, 8).transpose(0, 3, 1, 4, 2)

    acs_g = per_group_tm(acs)
    tm2 = jnp.stack([acs_g, per_group_tm(dtv)], axis=2).reshape(nb * SSD_GROUPS, 2, s, 8)
    acs_hm = per_group_hm(acs)
    dt_hm = per_group_hm(dtv)
    acs_c = acs_g.reshape(nb, SSD_GROUPS, nc, SSD_CHUNK, 8)
    aend = jnp.stack([acs_c[:, :, :, SSD_CHUNK - 1, :HEADS_PER_GROUP], acs_c[:, :, :, 0, HEADS_PER_GROUP:]], axis=3)
    aend = jnp.repeat(aend, SSD_HEAD_DIM, axis=-1)

    cwx = p['ssd_conv_w']
    cbx = p['ssd_conv_b']

    def per_group_conv(a):
        xs_ = a[:, :SSD_D_INNER].reshape(-1, SSD_GROUPS, GROUP_CH)
        b_ = a[:, SSD_D_INNER:SSD_D_INNER + SSD_GROUPS * SSD_STATE].reshape(-1, SSD_GROUPS, SSD_STATE)
        c_ = a[:, SSD_D_INNER + SSD_GROUPS * SSD_STATE:].reshape(-1, SSD_GROUPS, SSD_STATE)
        return jnp.concatenate([xs_, b_, c_], axis=-1).transpose(1, 0, 2)

    cw = jnp.pad(per_group_conv(cwx), ((0, 0), (0, 8 - SSD_CONV_WIDTH), (0, 0)))
    cb = per_group_conv(cbx.reshape(1, -1))
    dskip = jnp.repeat(p['ssd_d'], SSD_HEAD_DIM).reshape(SSD_GROUPS, 1, GROUP_CH)
    nw_ssd = p['ssd_norm_w'].reshape(SSD_GROUPS, 1, GROUP_CH)

    yn = _ssd(xbc.reshape(nb, s, -1), sz.reshape(nb, s, -1), cw, cb, dt_hm, acs_hm, tm2, aend, dskip, nw_ssd)
    mb = _gated_proj(yn.reshape(t, SSD_D_INNER), p['ssd_out'].astype(BF16), gates, 1, tm)

    dww = jnp.pad(p['conv_dw_w'], ((0, 32 - CONV_WIDTH), (0, 0)))
    x1 = _conformer(u.reshape(nb, s, d), mb.reshape(nb, s, d), gates.reshape(nb, s, 2 * d), x, dww,
                    p['conv_dw_b'].reshape(1, d), p['conv_ln_w'].reshape(1, d), p['conv_ln_b'].reshape(1, d),
                    p['conv_pw_out'].astype(BF16), p['w_out'].astype(BF16), tq)

    m = mem.shape[1]
    kv = _norm_proj(mem.reshape(nb * m, d), p['norm_mem_w'].reshape(1, d), p['xattn_wkv'].astype(BF16),
                    lambda a: a, BF16, _pick(nb * m, 512), 1024, "kv_proj")
    n_r = N_GROUPS + N_EXPERTS
    wr = jnp.pad(jnp.concatenate([p['router_group_w'], p['router_expert_w']], axis=1),
                 ((0, 0), (0, LANES - n_r))).astype(BF16)
    br = jnp.pad(jnp.concatenate([p['router_group_b'], p['router_expert_b']]), (0, LANES - n_r)).reshape(1, LANES)
    x2, h3, logits = _attention(x1, kv.reshape(nb, m, 2 * d), p['norm_xattn_w'].reshape(1, d),
                                p['xattn_wq'].astype(BF16), p['xattn_wo'].astype(BF16),
                                p['norm_ffn_w'].reshape(1, d), wr, br, tq)

    eid, gate = _route(logits.reshape(t, LANES), tm)
    buf_tok, pos, blk_exp, n_active = _routing_plan(eid[:, :2])
    ys = _experts(blk_exp, n_active, buf_tok, h3.reshape(t, d), p['expert_w_gate'].astype(BF16),
                  p['expert_w_up'].astype(BF16), p['expert_w_down'].astype(BF16))
    tc = _pick(t, 256)
    pos_blk = pos.reshape(t // tc, tc, 2).transpose(0, 2, 1).reshape(t // tc, 1, 2 * tc)
    out = _combine(pos_blk, x2.reshape(t, d), gate, p['norm_final_w'].reshape(1, d), ys, tc)
    return out.reshape(nb, s, d)


def kernel(x_prompt, x_sample, mem_prompt, mem_sample, norm_mix_w, w_in, conv_dw_w, conv_dw_b, conv_ln_w, conv_ln_b, conv_pw_out, ssd_conv_w, ssd_conv_b, ssd_dt_bias_f, ssd_dt_bias_b, ssd_a_log_f, ssd_a_log_b, ssd_d, ssd_norm_w, ssd_out, w_out, norm_xattn_w, norm_mem_w, xattn_wq, xattn_wkv, xattn_wo, norm_ffn_w, router_group_w, router_group_b, router_expert_w, router_expert_b, expert_w_gate, expert_w_up, expert_w_down, norm_final_w):
    p = {
        'norm_mix_w': norm_mix_w[0], 'w_in': w_in[0], 'conv_dw_w': conv_dw_w[0], 'conv_dw_b': conv_dw_b[0],
        'conv_ln_w': conv_ln_w[0], 'conv_ln_b': conv_ln_b[0], 'conv_pw_out': conv_pw_out[0],
        'ssd_conv_w': ssd_conv_w[0], 'ssd_conv_b': ssd_conv_b[0], 'ssd_dt_bias_f': ssd_dt_bias_f[0],
        'ssd_dt_bias_b': ssd_dt_bias_b[0], 'ssd_a_log_f': ssd_a_log_f[0], 'ssd_a_log_b': ssd_a_log_b[0],
        'ssd_d': ssd_d[0], 'ssd_norm_w': ssd_norm_w[0], 'ssd_out': ssd_out[0], 'w_out': w_out[0],
        'norm_xattn_w': norm_xattn_w[0], 'norm_mem_w': norm_mem_w[0], 'xattn_wq': xattn_wq[0],
        'xattn_wkv': xattn_wkv[0], 'xattn_wo': xattn_wo[0], 'norm_ffn_w': norm_ffn_w[0],
        'router_group_w': router_group_w[0], 'router_group_b': router_group_b[0],
        'router_expert_w': router_expert_w[0], 'router_expert_b': router_expert_b[0],
        'expert_w_gate': expert_w_gate[0], 'expert_w_up': expert_w_up[0], 'expert_w_down': expert_w_down[0],
        'norm_final_w': norm_final_w,
    }
    n_prompt = x_prompt.shape[0]
    x = jnp.concatenate([x_prompt, x_sample], axis=0)
    mem = jnp.concatenate([mem_prompt, mem_sample], axis=0)
    y = _encoder(x, mem, p)
    return (y[:n_prompt], y[n_prompt:])
```

```python
import functools
import math

import jax
import jax.numpy as jnp
from jax import lax
from jax.experimental import pallas as pl
from jax.experimental.pallas import tpu as pltpu

F32 = jnp.float32
BF16 = jnp.bfloat16

D_MODEL = 1024
CONV_WIDTH = 31
SSD_D_INNER = 2048
SSD_HEAD_DIM = 64
SSD_HEADS = 32
SSD_GROUPS = 8
SSD_STATE = 128
SSD_CONV_WIDTH = 5
SSD_CHUNK = 128
GROUP_CH = SSD_D_INNER // SSD_GROUPS
HEADS_PER_GROUP = SSD_HEADS // SSD_GROUPS
XATTN_HEADS = 4
XATTN_HEAD_DIM = D_MODEL // XATTN_HEADS
N_GROUPS = 8
EXPERTS_PER_GROUP = 8
N_EXPERTS = 64
EXPERT_FF = 512
MOE_BLOCK = 128
RMS_EPS = 1e-6
LN_EPS = 1e-5

LANES = 128
HALO = 16
VMEM_LIMIT = 56 * 1024 * 1024


def _cparams(sem):
    return pltpu.CompilerParams(dimension_semantics=sem, vmem_limit_bytes=VMEM_LIMIT)


def _rms(x, w):
    ms = jnp.mean(x * x, axis=-1, keepdims=True)
    return x * lax.rsqrt(ms + RMS_EPS) * w


def _sigmoid(x):
    return 1.0 / (1.0 + jnp.exp(-x))


def _silu(x):
    return x * _sigmoid(x)


def _softplus(x):
    return jnp.maximum(x, 0.0) + jnp.log1p(jnp.exp(-jnp.abs(x)))


CONV_STRIDE = 4
CONV_ROWS = 8 * CONV_STRIDE
CONV_UNROLL = 2


def _depthwise_conv(src, dst, slab, w_ref, b_ref, n_rows, width, first_row, epilogue):
    lanes = pl.ds(slab * LANES, LANES)
    taps = [jnp.broadcast_to(w_ref[k:k + 1, lanes], (8, LANES)) for k in range(width)]
    bias = jnp.broadcast_to(b_ref[:, lanes], (8, LANES))
    step = CONV_ROWS * CONV_UNROLL

    def body(r, carry):
        base = r * step
        for t0 in range(CONV_STRIDE * CONV_UNROLL):
            row = base + (t0 // CONV_STRIDE) * CONV_ROWS + t0 % CONV_STRIDE
            acc = bias
            for k in range(width):
                acc = acc + src[slab, pl.ds(row + first_row - width // 2 + k, 8, stride=CONV_STRIDE), :] * taps[k]
            dst[slab, pl.ds(row, 8, stride=CONV_STRIDE), :] = epilogue(acc)
        return carry

    lax.fori_loop(0, n_rows // step, body, 0)


def _glu_kernel(x_ref, nw_ref, wv_ref, wg_ref, o_ref, h_scr):
    @pl.when(pl.program_id(1) == 0)
    def _():
        h_scr[...] = _rms(x_ref[...], nw_ref[...]).astype(BF16)

    h = h_scr[...]
    v = jnp.dot(h, wv_ref[...], preferred_element_type=F32)
    g = jnp.dot(h, wg_ref[...], preferred_element_type=F32)
    o_ref[...] = (v * _sigmoid(g)).astype(o_ref.dtype)


def _glu_proj(x, nw, wv, wg, tm, tn):
    t, d = x.shape
    n = wv.shape[1]
    return pl.pallas_call(
        _glu_kernel,
        grid=(t // tm, n // tn),
        in_specs=[
            pl.BlockSpec((tm, d), lambda i, j: (i, 0)),
            pl.BlockSpec((1, d), lambda i, j: (0, 0)),
            pl.BlockSpec((d, tn), lambda i, j: (0, j)),
            pl.BlockSpec((d, tn), lambda i, j: (0, j)),
        ],
        out_specs=pl.BlockSpec((tm, tn), lambda i, j: (i, j)),
        out_shape=jax.ShapeDtypeStruct((t, n), BF16),
        scratch_shapes=[pltpu.VMEM((tm, d), BF16)],
        compiler_params=_cparams(("parallel", "arbitrary")),
        name="glu_proj",
    )(x, nw, wv, wg)


def _norm_proj_kernel(epi, x_ref, nw_ref, w_ref, o_ref, h_scr):
    @pl.when(pl.program_id(1) == 0)
    def _():
        h_scr[...] = _rms(x_ref[...], nw_ref[...]).astype(BF16)

    acc = jnp.dot(h_scr[...], w_ref[...], preferred_element_type=F32)
    o_ref[...] = epi(acc).astype(o_ref.dtype)


def _norm_proj(x, nw, w, epi, out_dtype, tm, tn, name):
    t, d = x.shape
    n = w.shape[1]
    return pl.pallas_call(
        functools.partial(_norm_proj_kernel, epi),
        grid=(t // tm, n // tn),
        in_specs=[
            pl.BlockSpec((tm, d), lambda i, j: (i, 0)),
            pl.BlockSpec((1, d), lambda i, j: (0, 0)),
            pl.BlockSpec((d, tn), lambda i, j: (0, j)),
        ],
        out_specs=pl.BlockSpec((tm, tn), lambda i, j: (i, j)),
        out_shape=jax.ShapeDtypeStruct((t, n), out_dtype),
        scratch_shapes=[pltpu.VMEM((tm, d), BF16)],
        compiler_params=_cparams(("parallel", "arbitrary")),
        name=name,
    )(x, nw, w)


def _gated_proj_kernel(y_ref, w_ref, g_ref, o_ref):
    acc = jnp.dot(y_ref[...], w_ref[...], preferred_element_type=F32)
    o_ref[...] = (acc * g_ref[...].astype(F32)).astype(o_ref.dtype)


def _gated_proj(y, w, g, g_col_block, tm):
    t, k = y.shape
    n = w.shape[1]
    return pl.pallas_call(
        _gated_proj_kernel,
        grid=(t // tm,),
        in_specs=[
            pl.BlockSpec((tm, k), lambda i: (i, 0)),
            pl.BlockSpec((k, n), lambda i: (0, 0)),
            pl.BlockSpec((tm, n), lambda i: (i, g_col_block)),
        ],
        out_specs=pl.BlockSpec((tm, n), lambda i: (i, 0)),
        out_shape=jax.ShapeDtypeStruct((t, n), BF16),
        compiler_params=_cparams(("parallel",)),
        name="ssd_out_proj",
    )(y, w, g)


def _dtprep_kernel(raw_ref, bias_ref, alog_ref, dt_ref, acs_ref):
    rows = raw_ref.shape[0]
    a_head = -jnp.exp(alog_ref[...])
    row = lax.broadcasted_iota(jnp.int32, (SSD_CHUNK, LANES), 0)
    lane = lax.broadcasted_iota(jnp.int32, (SSD_CHUNK, LANES), 1)
    is_bwd = (lane % (2 * HEADS_PER_GROUP)) >= HEADS_PER_GROUP
    for c in range(rows // SSD_CHUNK):
        sl = pl.ds(c * SSD_CHUNK, SSD_CHUNK)
        dt = _softplus(raw_ref[sl, :] + bias_ref[...])
        dt_ref[sl, :] = dt
        a = dt * a_head
        fwd = a
        bwd = a
        sh = 1
        while sh < SSD_CHUNK:
            fwd = fwd + jnp.where(row >= sh, pltpu.roll(fwd, sh, 0), 0.0)
            bwd = bwd + jnp.where(row < SSD_CHUNK - sh, pltpu.roll(bwd, SSD_CHUNK - sh, 0), 0.0)
            sh *= 2
        acs_ref[sl, :] = jnp.where(is_bwd, bwd, fwd)


def _dtprep(raw, bias, alog, tm):
    t = raw.shape[0]
    return pl.pallas_call(
        _dtprep_kernel,
        grid=(t // tm,),
        in_specs=[
            pl.BlockSpec((tm, LANES), lambda i: (i, 0)),
            pl.BlockSpec((1, LANES), lambda i: (0, 0)),
            pl.BlockSpec((1, LANES), lambda i: (0, 0)),
        ],
        out_specs=[
            pl.BlockSpec((tm, LANES), lambda i: (i, 0)),
            pl.BlockSpec((tm, LANES), lambda i: (i, 0)),
        ],
        out_shape=[jax.ShapeDtypeStruct((t, LANES), F32)] * 2,
        compiler_params=_cparams(("parallel",)),
        name="dt_prep",
    )(raw, bias, alog)


SSD_PAD = 8
XBC_W = GROUP_CH + 2 * SSD_STATE


def _expand_heads(cols):
    n = cols.shape[0]
    head = lax.broadcasted_iota(jnp.int32, (n, GROUP_CH), 1) // SSD_HEAD_DIM
    out = jnp.broadcast_to(cols[:, 3:4], (n, GROUP_CH))
    for h in (2, 1, 0):
        out = jnp.where(head == h, cols[:, h:h + 1], out)
    return out


def _ssd_kernel(xs_ref, b_ref, c_ref, sz_ref, cw_ref, cb_ref, dthm_ref, acshm_ref, acstm_ref, aend_ref,
                dskip_ref, nw_ref, o_ref, raw_scr, act_scr, y_scr, st_scr):
    s = xs_ref.shape[1]
    nc = s // SSD_CHUNK
    L = SSD_CHUNK

    zpad = jnp.zeros((SSD_PAD, LANES), F32)
    for j in range(XBC_W // LANES):
        raw_scr[j, pl.ds(0, SSD_PAD), :] = zpad
        raw_scr[j, pl.ds(SSD_PAD + s, SSD_PAD), :] = zpad
    raw_scr[0, pl.ds(SSD_PAD, s), :] = xs_ref[0, :, :LANES].astype(F32)
    raw_scr[1, pl.ds(SSD_PAD, s), :] = xs_ref[0, :, LANES:].astype(F32)
    raw_scr[2, pl.ds(SSD_PAD, s), :] = b_ref[0].astype(F32)
    raw_scr[3, pl.ds(SSD_PAD, s), :] = c_ref[0].astype(F32)
    for j in range(XBC_W // LANES):
        _depthwise_conv(raw_scr, act_scr, j, cw_ref.at[0], cb_ref.at[0], s, SSD_CONV_WIDTH, SSD_PAD, _silu)

    li = lax.broadcasted_iota(jnp.int32, (L, L), 0)
    si = lax.broadcasted_iota(jnp.int32, (L, L), 1)
    head_of_ch = lax.broadcasted_iota(jnp.int32, (L, GROUP_CH), 1) // SSD_HEAD_DIM

    def chunk(c, direction):
        r0 = pl.multiple_of(c * L, L)
        x = jnp.concatenate([act_scr[0, pl.ds(r0, L), :], act_scr[1, pl.ds(r0, L), :]], axis=1)
        xb = x.astype(BF16)
        bk = act_scr[2, pl.ds(r0, L), :].astype(BF16)
        ck = act_scr[3, pl.ds(r0, L), :].astype(BF16)
        acs_tm = acstm_ref[0, 0, pl.ds(r0, L), :]
        acs_hm = acshm_ref[0, 0, c]
        dt_hm = dthm_ref[0, 0, c]
        h0 = direction * HEADS_PER_GROUP
        keep = (li >= si) if direction == 0 else (si >= li)
        edge = L - 1 if direction == 0 else 0

        cb = lax.dot_general(ck, bk, (((1,), (1,)), ((), ())), preferred_element_type=F32)
        ms = []
        xbd = []
        for h in range(HEADS_PER_GROUP):
            col = acs_tm[:, h0 + h:h0 + h + 1]
            rowv = acs_hm[h0 + h:h0 + h + 1, :]
            decay = jnp.exp(jnp.where(keep, col - rowv, -jnp.inf))
            ms.append((cb * decay * dt_hm[h0 + h:h0 + h + 1, :]).astype(BF16))
            xbd.append(jnp.where(head_of_ch == h, xb, jnp.zeros_like(xb)))
        y = jnp.dot(jnp.concatenate(ms, axis=1), jnp.concatenate(xbd, axis=0), preferred_element_type=F32)

        acs_dir = acs_tm[:, h0:h0 + HEADS_PER_GROUP]
        state = st_scr[direction]
        y = y + jnp.dot(ck, state.astype(BF16), preferred_element_type=F32) * _expand_heads(jnp.exp(acs_dir))

        a_end = acs_tm[edge:edge + 1, h0:h0 + HEADS_PER_GROUP]
        dt_tm = acstm_ref[0, 1, pl.ds(r0, L), :][:, h0:h0 + HEADS_PER_GROUP]
        w_state = _expand_heads(jnp.exp(a_end - acs_dir) * dt_tm)
        upd = lax.dot_general(bk, (x * w_state).astype(BF16), (((0,), (0,)), ((), ())),
                              preferred_element_type=F32)
        st_scr[direction] = state * jnp.exp(aend_ref[0, 0, c, direction:direction + 1, :]) + upd
        return r0, x, y

    st_scr[...] = jnp.zeros_like(st_scr)

    def fwd_body(c, carry):
        r0, _, y = chunk(c, 0)
        y_scr[pl.ds(r0, L), :] = y
        return carry

    lax.fori_loop(0, nc, fwd_body, 0)

    def bwd_body(i, carry):
        c = nc - 1 - i
        r0, x, y = chunk(c, 1)
        y = y + y_scr[pl.ds(r0, L), :] + x * dskip_ref[0]
        yz = y * sz_ref[0, pl.ds(r0, L), :].astype(F32)
        ms = jnp.mean(yz * yz, axis=-1, keepdims=True)
        o_ref[0, pl.ds(r0, L), :] = (yz * lax.rsqrt(ms + RMS_EPS) * nw_ref[0]).astype(o_ref.dtype)
        return carry

    lax.fori_loop(0, nc, bwd_body, 0)


def _ssd(xbc, sz, cw, cb, dt_hm, acs_hm, tm2, aend, dskip, nw):
    nb, s, _ = xbc.shape
    nc = s // SSD_CHUNK
    return pl.pallas_call(
        _ssd_kernel,
        grid=(nb, SSD_GROUPS),
        in_specs=[
            pl.BlockSpec((1, s, GROUP_CH), lambda b, g: (b, 0, g)),
            pl.BlockSpec((1, s, SSD_STATE), lambda b, g: (b, 0, SSD_D_INNER // SSD_STATE + g)),
            pl.BlockSpec((1, s, SSD_STATE), lambda b, g: (b, 0, SSD_D_INNER // SSD_STATE + SSD_GROUPS + g)),
            pl.BlockSpec((1, s, GROUP_CH), lambda b, g: (b, 0, g)),
            pl.BlockSpec((1, 8, XBC_W), lambda b, g: (g, 0, 0)),
            pl.BlockSpec((1, 1, XBC_W), lambda b, g: (g, 0, 0)),
            pl.BlockSpec((1, 1, nc, 8, SSD_CHUNK), lambda b, g: (b, g, 0, 0, 0)),
            pl.BlockSpec((1, 1, nc, 8, SSD_CHUNK), lambda b, g: (b, g, 0, 0, 0)),
            pl.BlockSpec((1, 2, s, 8), lambda b, g: (b * SSD_GROUPS + g, 0, 0, 0)),
            pl.BlockSpec((1, 1, nc, 2, GROUP_CH), lambda b, g: (b, g, 0, 0, 0)),
            pl.BlockSpec((1, 1, GROUP_CH), lambda b, g: (g, 0, 0)),
            pl.BlockSpec((1, 1, GROUP_CH), lambda b, g: (g, 0, 0)),
        ],
        out_specs=pl.BlockSpec((1, s, GROUP_CH), lambda b, g: (b, 0, g)),
        out_shape=jax.ShapeDtypeStruct((nb, s, SSD_D_INNER), BF16),
        scratch_shapes=[
            pltpu.VMEM((XBC_W // LANES, s + 2 * SSD_PAD, LANES), F32),
            pltpu.VMEM((XBC_W // LANES, s, LANES), F32),
            pltpu.VMEM((s, GROUP_CH), F32),
            pltpu.VMEM((2, SSD_STATE, GROUP_CH), F32),
        ],
        compiler_params=_cparams(("parallel", "parallel")),
        name="ssd_scan",
    )(xbc, xbc, xbc, sz, cw, cb, dt_hm, acs_hm, tm2, aend, dskip, nw)


D_SLABS = D_MODEL // LANES


def _conformer_kernel(u_ref, up_ref, un_ref, mb_ref, g_ref, x_ref, dww_ref, dwb_ref, lnw_ref, lnb_ref,
                      pw_ref, wo_ref, o_ref, u_scr, c_scr):
    i = pl.program_id(1)
    n = pl.num_programs(1)
    tq = u_ref.shape[1]
    for j in range(D_SLABS):
        cs = slice(j * LANES, (j + 1) * LANES)
        u_scr[j, pl.ds(0, HALO), :] = jnp.where(i > 0, up_ref[0, :, cs].astype(F32), 0.0)
        u_scr[j, pl.ds(HALO, tq), :] = u_ref[0, :, cs].astype(F32)
        u_scr[j, pl.ds(HALO + tq, HALO), :] = jnp.where(i < n - 1, un_ref[0, :, cs].astype(F32), 0.0)
    for j in range(D_SLABS):
        _depthwise_conv(u_scr, c_scr, j, dww_ref, dwb_ref, tq, CONV_WIDTH, HALO, lambda a: a)

    cv = jnp.concatenate([c_scr[j] for j in range(D_SLABS)], axis=1)
    mu = jnp.mean(cv, axis=-1, keepdims=True)
    xc = cv - mu
    var = jnp.mean(xc * xc, axis=-1, keepdims=True)
    y = xc * lax.rsqrt(var + LN_EPS) * lnw_ref[...] + lnb_ref[...]
    a = jnp.dot(_silu(y).astype(BF16), pw_ref[...], preferred_element_type=F32)
    merged = g_ref[0].astype(F32) * a + mb_ref[0].astype(F32)
    o_ref[0] = x_ref[0] + jnp.dot(merged.astype(BF16), wo_ref[...], preferred_element_type=F32)


def _conformer(u, mb, gates, x, dww, dwb, lnw, lnb, pw, wo, tq):
    nb, s, d = x.shape
    hb = tq // HALO
    nh = s // HALO
    wspec = pl.BlockSpec((d, d), lambda b, i: (0, 0))
    vspec = pl.BlockSpec((1, d), lambda b, i: (0, 0))
    return pl.pallas_call(
        _conformer_kernel,
        grid=(nb, s // tq),
        in_specs=[
            pl.BlockSpec((1, tq, d), lambda b, i: (b, i, 0)),
            pl.BlockSpec((1, HALO, d), lambda b, i: (b, jnp.maximum(i * hb - 1, 0), 0)),
            pl.BlockSpec((1, HALO, d), lambda b, i: (b, jnp.minimum((i + 1) * hb, nh - 1), 0)),
            pl.BlockSpec((1, tq, d), lambda b, i: (b, i, 0)),
            pl.BlockSpec((1, tq, d), lambda b, i: (b, i, 0)),
            pl.BlockSpec((1, tq, d), lambda b, i: (b, i, 0)),
            pl.BlockSpec((32, d), lambda b, i: (0, 0)),
            vspec, vspec, vspec, wspec, wspec,
        ],
        out_specs=pl.BlockSpec((1, tq, d), lambda b, i: (b, i, 0)),
        out_shape=jax.ShapeDtypeStruct((nb, s, d), F32),
        scratch_shapes=[pltpu.VMEM((D_SLABS, tq + 2 * HALO, LANES), F32), pltpu.VMEM((D_SLABS, tq, LANES), F32)],
        compiler_params=_cparams(("parallel", "parallel")),
        name="conformer_merge",
    )(u, u, u, mb, gates, x, dww, dwb, lnw, lnb, pw, wo)


def _attn_kernel(x_ref, k_ref, v_ref, nw_ref, wq_ref, wo_ref, nf_ref, wr_ref, br_ref, x2_ref, h3_ref, lg_ref):
    x = x_ref[0]
    h = _rms(x, nw_ref[...]).astype(BF16)
    q = jnp.dot(h, wq_ref[...], preferred_element_type=F32)
    scale = 1.0 / math.sqrt(XATTN_HEAD_DIM)
    outs = []
    for hd in range(XATTN_HEADS):
        cs = slice(hd * XATTN_HEAD_DIM, (hd + 1) * XATTN_HEAD_DIM)
        sc = lax.dot_general(q[:, cs].astype(BF16), k_ref[0, :, cs], (((1,), (1,)), ((), ())),
                             preferred_element_type=F32) * scale
        m = jnp.max(sc, axis=-1, keepdims=True)
        e = jnp.exp(sc - m)
        p = e / jnp.sum(e, axis=-1, keepdims=True)
        outs.append(jnp.dot(p.astype(BF16), v_ref[0, :, cs], preferred_element_type=F32))
    o = jnp.concatenate(outs, axis=1).astype(BF16)
    x2 = x + jnp.dot(o, wo_ref[...], preferred_element_type=F32)
    x2_ref[0] = x2
    h3 = _rms(x2, nf_ref[...])
    h3_ref[0] = h3
    lg_ref[0] = jnp.dot(h3.astype(BF16), wr_ref[...], preferred_element_type=F32) + br_ref[...]


def _attention(x, kv, nw, wq, wo, nf, wr, br, tq):
    nb, s, d = x.shape
    m = kv.shape[1]
    wspec = pl.BlockSpec((d, d), lambda b, i: (0, 0))
    vspec = pl.BlockSpec((1, d), lambda b, i: (0, 0))
    return pl.pallas_call(
        _attn_kernel,
        grid=(nb, s // tq),
        in_specs=[
            pl.BlockSpec((1, tq, d), lambda b, i: (b, i, 0)),
            pl.BlockSpec((1, m, d), lambda b, i: (b, 0, 0)),
            pl.BlockSpec((1, m, d), lambda b, i: (b, 0, 1)),
            vspec, wspec, wspec, vspec,
            pl.BlockSpec((d, LANES), lambda b, i: (0, 0)),
            pl.BlockSpec((1, LANES), lambda b, i: (0, 0)),
        ],
        out_specs=[
            pl.BlockSpec((1, tq, d), lambda b, i: (b, i, 0)),
            pl.BlockSpec((1, tq, d), lambda b, i: (b, i, 0)),
            pl.BlockSpec((1, tq, LANES), lambda b, i: (b, i, 0)),
        ],
        out_shape=[
            jax.ShapeDtypeStruct((nb, s, d), F32),
            jax.ShapeDtypeStruct((nb, s, d), F32),
            jax.ShapeDtypeStruct((nb, s, LANES), F32),
        ],
        compiler_params=_cparams(("parallel", "parallel")),
        name="cross_attn_router",
    )(x, kv, kv, nw, wq, wo, nf, wr, br)


MOE_ROWS = 256


def _route_math(lg, lane):
    lane_f = lane.astype(F32)
    big = float(LANES)
    neg = -jnp.inf

    is_g = lane < N_GROUPS
    gl = jnp.where(is_g, lg, neg)
    gmax = jnp.max(gl, axis=-1, keepdims=True)
    g_idx = jnp.min(jnp.where(gl == gmax, lane_f, big), axis=-1, keepdims=True)
    g_sum = jnp.sum(jnp.where(is_g, jnp.exp(gl - gmax), 0.0), axis=-1, keepdims=True)
    g_w = 1.0 / g_sum

    e_grp = lax.shift_right_arithmetic(lane - N_GROUPS, 3).astype(F32)
    is_e = (lane >= N_GROUPS) & (lane < N_GROUPS + N_EXPERTS) & (e_grp == g_idx)
    el = jnp.where(is_e, lg, neg)
    emax = jnp.max(el, axis=-1, keepdims=True)
    ee = jnp.where(is_e, jnp.exp(el - emax), 0.0)
    ep = ee / jnp.sum(ee, axis=-1, keepdims=True)
    ep = jnp.where(is_e, ep, -1.0)
    v1 = jnp.max(ep, axis=-1, keepdims=True)
    i1 = jnp.min(jnp.where(ep == v1, lane_f, big), axis=-1, keepdims=True)
    ep2 = jnp.where(lane_f == i1, -1.0, ep)
    v2 = jnp.max(ep2, axis=-1, keepdims=True)
    i2 = jnp.min(jnp.where(ep2 == v2, lane_f, big), axis=-1, keepdims=True)
    tot = v1 + v2
    gate1 = g_w * (v1 / tot)
    gate2 = g_w * (v2 / tot)
    return i1 - N_GROUPS, i2 - N_GROUPS, gate1, gate2


def _plan_kernel(lg_ref, gate_ref, pos_ref, cnt_ref, carry_scr, base_scr):
    p = pl.program_id(0)
    i = pl.program_id(1)
    tm = lg_ref.shape[0]
    lane = lax.broadcasted_iota(jnp.int32, (tm, LANES), 1)
    lane_f = lane.astype(F32)
    e1, e2, gate1, gate2 = _route_math(lg_ref[...], lane)
    oh1 = lane_f == e1
    oh2 = lane_f == e2
    m = jnp.where(oh1 | oh2, 1.0, 0.0)
    colsum = jnp.sum(m, axis=0, keepdims=True)

    @pl.when((p == 0) & (i == 0))
    def _():
        carry_scr[...] = jnp.zeros_like(carry_scr)

    @pl.when(p == 0)
    def _():
        carry_scr[...] += colsum

    @pl.when((p == 1) & (i == 0))
    def _():
        counts = carry_scr[...]
        cnt_ref[...] = counts
        shift = MOE_ROWS.bit_length() - 1
        blocks = lax.shift_right_logical(counts.astype(jnp.int32) + (MOE_ROWS - 1), shift)
        padded = (blocks * MOE_ROWS).astype(F32)
        lane8 = lax.broadcasted_iota(jnp.int32, padded.shape, 1)
        inc = padded
        sh = 1
        while sh < LANES:
            inc = inc + jnp.where(lane8 >= sh, pltpu.roll(inc, sh, 1), 0.0)
            sh *= 2
        base_scr[...] = inc - padded
        carry_scr[...] = jnp.zeros_like(carry_scr)

    @pl.when(p == 1)
    def _():
        r = lax.broadcasted_iota(jnp.int32, (tm, tm), 0)
        c = lax.broadcasted_iota(jnp.int32, (tm, tm), 1)
        earlier = jnp.where(r > c, 1.0, 0.0).astype(BF16)
        tot = (jnp.dot(earlier, m.astype(BF16), preferred_element_type=F32)
               + base_scr[0:1, :] + carry_scr[0:1, :])
        pos1 = jnp.sum(jnp.where(oh1, tot, 0.0), axis=1, keepdims=True)
        pos2 = jnp.sum(jnp.where(oh2, tot, 0.0), axis=1, keepdims=True)
        pos_ref[...] = jnp.where(lane == 0, pos1, jnp.where(lane == 1, pos2, 0.0)).astype(jnp.int32)
        gate_ref[...] = jnp.where(lane == 0, gate1, jnp.where(lane == 1, gate2, 0.0))
        carry_scr[...] += colsum


def _plan(lg, tm):
    t = lg.shape[0]
    out_spec = pl.BlockSpec((tm, LANES), lambda p, i: (i * p, 0))
    return pl.pallas_call(
        _plan_kernel,
        grid=(2, t // tm),
        in_specs=[pl.BlockSpec((tm, LANES), lambda p, i: (i, 0))],
        out_specs=[out_spec, out_spec, pl.BlockSpec((8, LANES), lambda p, i: (0, 0))],
        out_shape=[jax.ShapeDtypeStruct((t, LANES), F32), jax.ShapeDtypeStruct((t, LANES), jnp.int32),
                   jax.ShapeDtypeStruct((8, LANES), F32)],
        scratch_shapes=[pltpu.VMEM((8, LANES), F32), pltpu.VMEM((8, LANES), F32)],
        compiler_params=_cparams(("arbitrary", "arbitrary")),
        name="moe_plan",
    )(lg)


def _dispatch_kernel(pos_ref, h_hbm, xs_in_hbm, xs_hbm, sem):
    del xs_in_hbm
    i = pl.program_id(0)
    td = pos_ref.shape[2] // 2

    def body(r, carry):
        src = h_hbm.at[pl.ds(i * td + r, 1)]
        pltpu.make_async_copy(src, xs_hbm.at[pl.ds(pos_ref[0, 0, r], 1)], sem).start()
        pltpu.make_async_copy(src, xs_hbm.at[pl.ds(pos_ref[0, 0, td + r], 1)], sem).start()
        return carry

    lax.fori_loop(0, td, body, 0, unroll=8)
    pltpu.make_async_copy(h_hbm.at[pl.ds(0, 2 * td)], xs_hbm.at[pl.ds(0, 2 * td)], sem).wait()


def _dispatch(pos_blk, h3, xs_zero):
    n_steps = pos_blk.shape[0]
    return pl.pallas_call(
        _dispatch_kernel,
        grid=(n_steps,),
        in_specs=[
            pl.BlockSpec((1, 1, pos_blk.shape[2]), lambda i: (i, 0, 0), memory_space=pltpu.SMEM),
            pl.BlockSpec(memory_space=pl.ANY),
            pl.BlockSpec(memory_space=pl.ANY),
        ],
        out_specs=pl.BlockSpec(memory_space=pl.ANY),
        out_shape=jax.ShapeDtypeStruct(xs_zero.shape, xs_zero.dtype),
        scratch_shapes=[pltpu.SemaphoreType.DMA(())],
        input_output_aliases={2: 0},
        compiler_params=_cparams(("arbitrary",)),
        name="moe_dispatch",
    )(pos_blk, h3, xs_zero)


def _expert_kernel(be_ref, na_ref, x_ref, wg_ref, wu_ref, wd_ref, o_ref, wg_scr, wu_scr, wd_scr):
    j = pl.program_id(0)
    prev = be_ref[jnp.maximum(j - 1, 0)]

    @pl.when((j == 0) | (be_ref[j] != prev))
    def _():
        wg_scr[...] = wg_ref[0].astype(BF16)
        wu_scr[...] = wu_ref[0].astype(BF16)
        wd_scr[...] = wd_ref[0].astype(BF16)

    @pl.when(j < na_ref[0])
    def _():
        x = x_ref[...].astype(BF16)
        g = jnp.dot(x, wg_scr[...], preferred_element_type=F32)
        u = jnp.dot(x, wu_scr[...], preferred_element_type=F32)
        hid = (_silu(g) * u).astype(BF16)
        o_ref[...] = jnp.dot(hid, wd_scr[...], preferred_element_type=F32)

    @pl.when(j >= na_ref[0])
    def _():
        o_ref[...] = jnp.zeros_like(o_ref)


def _experts(blk_exp, n_active, xs, wg, wu, wd):
    n_blocks = blk_exp.shape[0]
    d = xs.shape[1]
    ff = wg.shape[2]
    grid_spec = pltpu.PrefetchScalarGridSpec(
        num_scalar_prefetch=2,
        grid=(n_blocks,),
        in_specs=[
            pl.BlockSpec((MOE_ROWS, d), lambda j, be, na: (j, 0)),
            pl.BlockSpec((1, d, ff), lambda j, be, na: (be[j], 0, 0)),
            pl.BlockSpec((1, d, ff), lambda j, be, na: (be[j], 0, 0)),
            pl.BlockSpec((1, ff, d), lambda j, be, na: (be[j], 0, 0)),
        ],
        out_specs=pl.BlockSpec((MOE_ROWS, d), lambda j, be, na: (j, 0)),
        scratch_shapes=[pltpu.VMEM((d, ff), BF16), pltpu.VMEM((d, ff), BF16), pltpu.VMEM((ff, d), BF16)],
    )
    return pl.pallas_call(
        _expert_kernel,
        grid_spec=grid_spec,
        out_shape=jax.ShapeDtypeStruct((n_blocks * MOE_ROWS, d), F32),
        compiler_params=_cparams(("arbitrary",)),
        name="moe_experts",
    )(blk_exp, n_active, xs, wg, wu, wd)


def _combine_kernel(pos_ref, x_ref, gate_ref, nw_ref, ys_hbm, o_ref, y_scr, sem):
    tm = x_ref.shape[0]

    def body(r, carry):
        pltpu.make_async_copy(ys_hbm.at[pl.ds(pos_ref[0, 0, r], 1)], y_scr.at[pl.ds(r, 1)], sem).start()
        return carry

    lax.fori_loop(0, 2 * tm, body, 0, unroll=8)
    pltpu.make_async_copy(ys_hbm.at[pl.ds(0, 2 * tm)], y_scr, sem).wait()
    gate = gate_ref[...]
    y = x_ref[...] + (gate[:, 0:1] * y_scr[pl.ds(0, tm), :] + gate[:, 1:2] * y_scr[pl.ds(tm, tm), :])
    o_ref[...] = _rms(y, nw_ref[...])


def _combine(pos_blk, x2, gates, nw, ys, tm):
    t, d = x2.shape
    return pl.pallas_call(
        _combine_kernel,
        grid=(t // tm,),
        in_specs=[
            pl.BlockSpec((1, 1, 2 * tm), lambda i: (i, 0, 0), memory_space=pltpu.SMEM),
            pl.BlockSpec((tm, d), lambda i: (i, 0)),
            pl.BlockSpec((tm, LANES), lambda i: (i, 0)),
            pl.BlockSpec((1, d), lambda i: (0, 0)),
            pl.BlockSpec(memory_space=pl.ANY),
        ],
        out_specs=pl.BlockSpec((tm, d), lambda i: (i, 0)),
        out_shape=jax.ShapeDtypeStruct((t, d), F32),
        scratch_shapes=[pltpu.VMEM((2 * tm, d), F32), pltpu.SemaphoreType.DMA(())],
        compiler_params=_cparams(("arbitrary",)),
        name="moe_combine",
    )(pos_blk, x2, gates, nw, ys)


def _block_experts(counts, n_blocks):
    per_expert = (counts + MOE_ROWS - 1) // MOE_ROWS
    ends = jnp.cumsum(per_expert)
    blk_exp = jnp.clip(jnp.searchsorted(ends, jnp.arange(n_blocks, dtype=jnp.int32), side='right'),
                       0, N_EXPERTS - 1).astype(jnp.int32)
    return blk_exp, ends[-1:].astype(jnp.int32)


def _pick(n, pref):
    t = pref
    while n % t:
        t //= 2
    return t


def _encoder(x, mem, p):
    nb, s, d = x.shape
    t = nb * s
    nc = s // SSD_CHUNK
    tm = _pick(t, 1024)
    tq = _pick(s, 512)
    xf = x.reshape(t, d)

    perm = jnp.arange(SSD_HEADS).reshape(SSD_GROUPS, HEADS_PER_GROUP)
    perm = jnp.concatenate([perm, perm + SSD_HEADS], axis=1).reshape(-1)

    w_in = p['w_in']
    o_z = 2 * D_MODEL
    o_x = o_z + SSD_D_INNER
    o_dt = o_x + SSD_D_INNER + 2 * SSD_GROUPS * SSD_STATE
    o_g = o_dt + 2 * SSD_HEADS
    w_val = w_in[:, :D_MODEL].astype(BF16)
    w_gate = w_in[:, D_MODEL:o_z].astype(BF16)
    w_z = w_in[:, o_z:o_x].astype(BF16)
    w_xbc = w_in[:, o_x:o_dt].astype(BF16)
    w_dt = jnp.pad(w_in[:, o_dt:o_g][:, perm], ((0, 0), (0, LANES - 2 * SSD_HEADS))).astype(BF16)
    w_g = w_in[:, o_g:].astype(BF16)
    nmix = p['norm_mix_w'].reshape(1, d)

    u = _glu_proj(xf, nmix, w_val, w_gate, tm, 512)
    sz = _norm_proj(xf, nmix, w_z, _silu, BF16, tm, 512, "z_proj")
    xbc = _norm_proj(xf, nmix, w_xbc, lambda a: a, BF16, tm, 512, "xbc_proj")
    dt_raw = _norm_proj(xf, nmix, w_dt, lambda a: a, F32, tm, LANES, "dt_proj")
    gates = _norm_proj(xf, nmix, w_g, _sigmoid, BF16, tm, 512, "gate_proj")

    pad = LANES - 2 * SSD_HEADS
    dt_bias = jnp.pad(jnp.concatenate([p['ssd_dt_bias_f'], p['ssd_dt_bias_b']])[perm], (0, pad)).reshape(1, LANES)
    a_log = jnp.pad(jnp.concatenate([p['ssd_a_log_f'], p['ssd_a_log_b']])[perm], (0, pad)).reshape(1, LANES)
    dtv, acs = _dtprep(dt_raw, dt_bias, a_log, tm)

    def per_group_tm(a):
        return a[:, :2 * SSD_HEADS].reshape(nb, s, SSD_GROUPS, 8).transpose(0, 2, 1, 3)

    def per_group_hm(a):
        return a[:, :2 * SSD_HEADS].reshape(nb, nc, SSD_CHUNK, SSD_GROUPS, 8).transpose(0, 3, 1, 4, 2)

    acs_g = per_group_tm(acs)
    tm2 = jnp.stack([acs_g, per_group_tm(dtv)], axis=2).reshape(nb * SSD_GROUPS, 2, s, 8)
    acs_hm = per_group_hm(acs)
    dt_hm = per_group_hm(dtv)
    acs_c = acs_g.reshape(nb, SSD_GROUPS, nc, SSD_CHUNK, 8)
    aend = jnp.stack([acs_c[:, :, :, SSD_CHUNK - 1, :HEADS_PER_GROUP], acs_c[:, :, :, 0, HEADS_PER_GROUP:]], axis=3)
    aend = jnp.repeat(aend, SSD_HEAD_DIM, axis=-1)

    cwx = p['ssd_conv_w']
    cbx = p['ssd_conv_b']

    def per_group_conv(a):
        xs_ = a[:, :SSD_D_INNER].reshape(-1, SSD_GROUPS, GROUP_CH)
        b_ = a[:, SSD_D_INNER:SSD_D_INNER + SSD_GROUPS * SSD_STATE].reshape(-1, SSD_GROUPS, SSD_STATE)
        c_ = a[:, SSD_D_INNER + SSD_GROUPS * SSD_STATE:].reshape(-1, SSD_GROUPS, SSD_STATE)
        return jnp.concatenate([xs_, b_, c_], axis=-1).transpose(1, 0, 2)

    cw = jnp.pad(per_group_conv(cwx), ((0, 0), (0, 8 - SSD_CONV_WIDTH), (0, 0)))
    cb = per_group_conv(cbx.reshape(1, -1))
    dskip = jnp.repeat(p['ssd_d'], SSD_HEAD_DIM).reshape(SSD_GROUPS, 1, GROUP_CH)
    nw_ssd = p['ssd_norm_w'].reshape(SSD_GROUPS, 1, GROUP_CH)

    yn = _ssd(xbc.reshape(nb, s, -1), sz.reshape(nb, s, -1), cw, cb, dt_hm, acs_hm, tm2, aend, dskip, nw_ssd)
    mb = _gated_proj(yn.reshape(t, SSD_D_INNER), p['ssd_out'].astype(BF16), gates, 1, tm)

    dww = jnp.pad(p['conv_dw_w'], ((0, 32 - CONV_WIDTH), (0, 0)))
    x1 = _conformer(u.reshape(nb, s, d), mb.reshape(nb, s, d), gates.reshape(nb, s, 2 * d), x, dww,
                    p['conv_dw_b'].reshape(1, d), p['conv_ln_w'].reshape(1, d), p['conv_ln_b'].reshape(1, d),
                    p['conv_pw_out'].astype(BF16), p['w_out'].astype(BF16), tq)

    m = mem.shape[1]
    kv = _norm_proj(mem.reshape(nb * m, d), p['norm_mem_w'].reshape(1, d), p['xattn_wkv'].astype(BF16),
                    lambda a: a, BF16, _pick(nb * m, 512), 1024, "kv_proj")
    n_r = N_GROUPS + N_EXPERTS
    wr = jnp.pad(jnp.concatenate([p['router_group_w'], p['router_expert_w']], axis=1),
                 ((0, 0), (0, LANES - n_r))).astype(BF16)
    br = jnp.pad(jnp.concatenate([p['router_group_b'], p['router_expert_b']]), (0, LANES - n_r)).reshape(1, LANES)
    x2, h3, logits = _attention(x1, kv.reshape(nb, m, 2 * d), p['norm_xattn_w'].reshape(1, d),
                                p['xattn_wq'].astype(BF16), p['xattn_wo'].astype(BF16),
                                p['norm_ffn_w'].reshape(1, d), wr, br, tq)

    tc = _pick(t, 512)
    gate, pos, cnt = _plan(logits.reshape(t, LANES), tc)
    n_blocks = (2 * t) // MOE_ROWS + N_EXPERTS
    blk_exp, n_active = _block_experts(cnt[0, :N_EXPERTS].astype(jnp.int32), n_blocks)
    pos_blk = pos[:, :2].reshape(t // tc, tc, 2).transpose(0, 2, 1).reshape(t // tc, 1, 2 * tc)
    xs = _dispatch(pos_blk, h3.reshape(t, d), jnp.zeros((n_blocks * MOE_ROWS, d), F32))
    ys = _experts(blk_exp, n_active, xs, p['expert_w_gate'], p['expert_w_up'], p['expert_w_down'])
    out = _combine(pos_blk, x2.reshape(t, d), gate, p['norm_final_w'].reshape(1, d), ys, tc)
    return out.reshape(nb, s, d)


def kernel(x_prompt, x_sample, mem_prompt, mem_sample, norm_mix_w, w_in, conv_dw_w, conv_dw_b, conv_ln_w, conv_ln_b, conv_pw_out, ssd_conv_w, ssd_conv_b, ssd_dt_bias_f, ssd_dt_bias_b, ssd_a_log_f, ssd_a_log_b, ssd_d, ssd_norm_w, ssd_out, w_out, norm_xattn_w, norm_mem_w, xattn_wq, xattn_wkv, xattn_wo, norm_ffn_w, router_group_w, router_group_b, router_expert_w, router_expert_b, expert_w_gate, expert_w_up, expert_w_down, norm_final_w):
    p = {
        'norm_mix_w': norm_mix_w[0], 'w_in': w_in[0], 'conv_dw_w': conv_dw_w[0], 'conv_dw_b': conv_dw_b[0],
        'conv_ln_w': conv_ln_w[0], 'conv_ln_b': conv_ln_b[0], 'conv_pw_out': conv_pw_out[0],
        'ssd_conv_w': ssd_conv_w[0], 'ssd_conv_b': ssd_conv_b[0], 'ssd_dt_bias_f': ssd_dt_bias_f[0],
        'ssd_dt_bias_b': ssd_dt_bias_b[0], 'ssd_a_log_f': ssd_a_log_f[0], 'ssd_a_log_b': ssd_a_log_b[0],
        'ssd_d': ssd_d[0], 'ssd_norm_w': ssd_norm_w[0], 'ssd_out': ssd_out[0], 'w_out': w_out[0],
        'norm_xattn_w': norm_xattn_w[0], 'norm_mem_w': norm_mem_w[0], 'xattn_wq': xattn_wq[0],
        'xattn_wkv': xattn_wkv[0], 'xattn_wo': xattn_wo[0], 'norm_ffn_w': norm_ffn_w[0],
        'router_group_w': router_group_w[0], 'router_group_b': router_group_b[0],
        'router_expert_w': router_expert_w[0], 'router_expert_b': router_expert_b[0],
        'expert_w_gate': expert_w_gate[0], 'expert_w_up': expert_w_up[0], 'expert_w_down': expert_w_down[0],
        'norm_final_w': norm_final_w,
    }
    n_prompt = x_prompt.shape[0]
    x = jnp.concatenate([x_prompt, x_sample], axis=0)
    mem = jnp.concatenate([mem_prompt, mem_sample], axis=0)
    y = _encoder(x, mem, p)
    return (y[:n_prompt], y[n_prompt:])
```

```python
import functools
import math

import jax
import jax.numpy as jnp
from jax import lax
from jax.experimental import pallas as pl
from jax.experimental.pallas import tpu as pltpu

F32 = jnp.float32
BF16 = jnp.bfloat16

D_MODEL = 1024
CONV_WIDTH = 31
SSD_D_INNER = 2048
SSD_HEAD_DIM = 64
SSD_HEADS = 32
SSD_GROUPS = 8
SSD_STATE = 128
SSD_CONV_WIDTH = 5
SSD_CHUNK = 128
GROUP_CH = SSD_D_INNER // SSD_GROUPS
HEADS_PER_GROUP = SSD_HEADS // SSD_GROUPS
XATTN_HEADS = 4
XATTN_HEAD_DIM = D_MODEL // XATTN_HEADS
N_GROUPS = 8
EXPERTS_PER_GROUP = 8
N_EXPERTS = 64
EXPERT_FF = 512
MOE_BLOCK = 128
RMS_EPS = 1e-6
LN_EPS = 1e-5

LANES = 128
HALO = 16
VMEM_LIMIT = 56 * 1024 * 1024


def _cparams(sem):
    return pltpu.CompilerParams(dimension_semantics=sem, vmem_limit_bytes=VMEM_LIMIT)


def _rms(x, w):
    ms = jnp.mean(x * x, axis=-1, keepdims=True)
    return x * lax.rsqrt(ms + RMS_EPS) * w


def _sigmoid(x):
    return 1.0 / (1.0 + jnp.exp(-x))


def _silu(x):
    return x * _sigmoid(x)


def _softplus(x):
    return jnp.maximum(x, 0.0) + jnp.log1p(jnp.exp(-jnp.abs(x)))


CONV_STRIDE = 4
CONV_ROWS = 8 * CONV_STRIDE
CONV_UNROLL = 2


def _depthwise_conv(src, dst, slab, w_ref, b_ref, n_rows, width, first_row, epilogue):
    lanes = pl.ds(slab * LANES, LANES)
    taps = [jnp.broadcast_to(w_ref[k:k + 1, lanes], (8, LANES)) for k in range(width)]
    bias = jnp.broadcast_to(b_ref[:, lanes], (8, LANES))
    step = CONV_ROWS * CONV_UNROLL

    def body(r, carry):
        base = r * step
        for t0 in range(CONV_STRIDE * CONV_UNROLL):
            row = base + (t0 // CONV_STRIDE) * CONV_ROWS + t0 % CONV_STRIDE
            acc = bias
            for k in range(width):
                acc = acc + src[slab, pl.ds(row + first_row - width // 2 + k, 8, stride=CONV_STRIDE), :] * taps[k]
            dst[slab, pl.ds(row, 8, stride=CONV_STRIDE), :] = epilogue(acc)
        return carry

    lax.fori_loop(0, n_rows // step, body, 0)


def _glu_kernel(x_ref, nw_ref, wv_ref, wg_ref, o_ref, h_scr):
    @pl.when(pl.program_id(1) == 0)
    def _():
        h_scr[...] = _rms(x_ref[...], nw_ref[...]).astype(BF16)

    h = h_scr[...]
    v = jnp.dot(h, wv_ref[...], preferred_element_type=F32)
    g = jnp.dot(h, wg_ref[...], preferred_element_type=F32)
    o_ref[...] = (v * _sigmoid(g)).astype(o_ref.dtype)


def _glu_proj(x, nw, wv, wg, tm, tn):
    t, d = x.shape
    n = wv.shape[1]
    return pl.pallas_call(
        _glu_kernel,
        grid=(t // tm, n // tn),
        in_specs=[
            pl.BlockSpec((tm, d), lambda i, j: (i, 0)),
            pl.BlockSpec((1, d), lambda i, j: (0, 0)),
            pl.BlockSpec((d, tn), lambda i, j: (0, j)),
            pl.BlockSpec((d, tn), lambda i, j: (0, j)),
        ],
        out_specs=pl.BlockSpec((tm, tn), lambda i, j: (i, j)),
        out_shape=jax.ShapeDtypeStruct((t, n), BF16),
        scratch_shapes=[pltpu.VMEM((tm, d), BF16)],
        compiler_params=_cparams(("parallel", "arbitrary")),
        name="glu_proj",
    )(x, nw, wv, wg)


def _norm_proj_kernel(epi, x_ref, nw_ref, w_ref, o_ref, h_scr):
    @pl.when(pl.program_id(1) == 0)
    def _():
        h_scr[...] = _rms(x_ref[...], nw_ref[...]).astype(BF16)

    acc = jnp.dot(h_scr[...], w_ref[...], preferred_element_type=F32)
    o_ref[...] = epi(acc).astype(o_ref.dtype)


def _norm_proj(x, nw, w, epi, out_dtype, tm, tn, name):
    t, d = x.shape
    n = w.shape[1]
    return pl.pallas_call(
        functools.partial(_norm_proj_kernel, epi),
        grid=(t // tm, n // tn),
        in_specs=[
            pl.BlockSpec((tm, d), lambda i, j: (i, 0)),
            pl.BlockSpec((1, d), lambda i, j: (0, 0)),
            pl.BlockSpec((d, tn), lambda i, j: (0, j)),
        ],
        out_specs=pl.BlockSpec((tm, tn), lambda i, j: (i, j)),
        out_shape=jax.ShapeDtypeStruct((t, n), out_dtype),
        scratch_shapes=[pltpu.VMEM((tm, d), BF16)],
        compiler_params=_cparams(("parallel", "arbitrary")),
        name=name,
    )(x, nw, w)


def _gated_proj_kernel(y_ref, w_ref, g_ref, o_ref):
    acc = jnp.dot(y_ref[...], w_ref[...], preferred_element_type=F32)
    o_ref[...] = (acc * g_ref[...].astype(F32)).astype(o_ref.dtype)


def _gated_proj(y, w, g, g_col_block, tm):
    t, k = y.shape
    n = w.shape[1]
    return pl.pallas_call(
        _gated_proj_kernel,
        grid=(t // tm,),
        in_specs=[
            pl.BlockSpec((tm, k), lambda i: (i, 0)),
            pl.BlockSpec((k, n), lambda i: (0, 0)),
            pl.BlockSpec((tm, n), lambda i: (i, g_col_block)),
        ],
        out_specs=pl.BlockSpec((tm, n), lambda i: (i, 0)),
        out_shape=jax.ShapeDtypeStruct((t, n), BF16),
        compiler_params=_cparams(("parallel",)),
        name="ssd_out_proj",
    )(y, w, g)


def _dtprep_kernel(raw_ref, bias_ref, alog_ref, dt_ref, acs_ref):
    rows = raw_ref.shape[0]
    a_head = -jnp.exp(alog_ref[...])
    row = lax.broadcasted_iota(jnp.int32, (SSD_CHUNK, LANES), 0)
    lane = lax.broadcasted_iota(jnp.int32, (SSD_CHUNK, LANES), 1)
    is_bwd = (lane % (2 * HEADS_PER_GROUP)) >= HEADS_PER_GROUP
    for c in range(rows // SSD_CHUNK):
        sl = pl.ds(c * SSD_CHUNK, SSD_CHUNK)
        dt = _softplus(raw_ref[sl, :] + bias_ref[...])
        dt_ref[sl, :] = dt
        a = dt * a_head
        fwd = a
        bwd = a
        sh = 1
        while sh < SSD_CHUNK:
            fwd = fwd + jnp.where(row >= sh, pltpu.roll(fwd, sh, 0), 0.0)
            bwd = bwd + jnp.where(row < SSD_CHUNK - sh, pltpu.roll(bwd, SSD_CHUNK - sh, 0), 0.0)
            sh *= 2
        acs_ref[sl, :] = jnp.where(is_bwd, bwd, fwd)


def _dtprep(raw, bias, alog, tm):
    t = raw.shape[0]
    return pl.pallas_call(
        _dtprep_kernel,
        grid=(t // tm,),
        in_specs=[
            pl.BlockSpec((tm, LANES), lambda i: (i, 0)),
            pl.BlockSpec((1, LANES), lambda i: (0, 0)),
            pl.BlockSpec((1, LANES), lambda i: (0, 0)),
        ],
        out_specs=[
            pl.BlockSpec((tm, LANES), lambda i: (i, 0)),
            pl.BlockSpec((tm, LANES), lambda i: (i, 0)),
        ],
        out_shape=[jax.ShapeDtypeStruct((t, LANES), F32)] * 2,
        compiler_params=_cparams(("parallel",)),
        name="dt_prep",
    )(raw, bias, alog)


SSD_PAD = 8
XBC_W = GROUP_CH + 2 * SSD_STATE


def _expand_heads(cols):
    n = cols.shape[0]
    head = lax.broadcasted_iota(jnp.int32, (n, GROUP_CH), 1) // SSD_HEAD_DIM
    out = jnp.broadcast_to(cols[:, 3:4], (n, GROUP_CH))
    for h in (2, 1, 0):
        out = jnp.where(head == h, cols[:, h:h + 1], out)
    return out


def _ssd_kernel(xs_ref, b_ref, c_ref, sz_ref, cw_ref, cb_ref, dthm_ref, acshm_ref, acstm_ref, aend_ref,
                dskip_ref, nw_ref, o_ref, raw_scr, act_scr, y_scr, st_scr):
    s = xs_ref.shape[1]
    nc = s // SSD_CHUNK
    L = SSD_CHUNK

    zpad = jnp.zeros((SSD_PAD, LANES), F32)
    for j in range(XBC_W // LANES):
        raw_scr[j, pl.ds(0, SSD_PAD), :] = zpad
        raw_scr[j, pl.ds(SSD_PAD + s, SSD_PAD), :] = zpad
    raw_scr[0, pl.ds(SSD_PAD, s), :] = xs_ref[0, :, :LANES].astype(F32)
    raw_scr[1, pl.ds(SSD_PAD, s), :] = xs_ref[0, :, LANES:].astype(F32)
    raw_scr[2, pl.ds(SSD_PAD, s), :] = b_ref[0].astype(F32)
    raw_scr[3, pl.ds(SSD_PAD, s), :] = c_ref[0].astype(F32)
    for j in range(XBC_W // LANES):
        _depthwise_conv(raw_scr, act_scr, j, cw_ref.at[0], cb_ref.at[0], s, SSD_CONV_WIDTH, SSD_PAD, _silu)

    li = lax.broadcasted_iota(jnp.int32, (L, L), 0)
    si = lax.broadcasted_iota(jnp.int32, (L, L), 1)
    head_of_ch = lax.broadcasted_iota(jnp.int32, (L, GROUP_CH), 1) // SSD_HEAD_DIM

    def chunk(c, direction):
        r0 = pl.multiple_of(c * L, L)
        x = jnp.concatenate([act_scr[0, pl.ds(r0, L), :], act_scr[1, pl.ds(r0, L), :]], axis=1)
        xb = x.astype(BF16)
        bk = act_scr[2, pl.ds(r0, L), :].astype(BF16)
        ck = act_scr[3, pl.ds(r0, L), :].astype(BF16)
        acs_tm = acstm_ref[0, 0, pl.ds(r0, L), :]
        acs_hm = acshm_ref[0, 0, c]
        dt_hm = dthm_ref[0, 0, c]
        h0 = direction * HEADS_PER_GROUP
        keep = (li >= si) if direction == 0 else (si >= li)
        edge = L - 1 if direction == 0 else 0

        cb = lax.dot_general(ck, bk, (((1,), (1,)), ((), ())), preferred_element_type=F32)
        ms = []
        xbd = []
        for h in range(HEADS_PER_GROUP):
            col = acs_tm[:, h0 + h:h0 + h + 1]
            rowv = acs_hm[h0 + h:h0 + h + 1, :]
            decay = jnp.exp(jnp.where(keep, col - rowv, -jnp.inf))
            ms.append((cb * decay * dt_hm[h0 + h:h0 + h + 1, :]).astype(BF16))
            xbd.append(jnp.where(head_of_ch == h, xb, jnp.zeros_like(xb)))
        y = jnp.dot(jnp.concatenate(ms, axis=1), jnp.concatenate(xbd, axis=0), preferred_element_type=F32)

        acs_dir = acs_tm[:, h0:h0 + HEADS_PER_GROUP]
        state = st_scr[direction]
        y = y + jnp.dot(ck, state.astype(BF16), preferred_element_type=F32) * _expand_heads(jnp.exp(acs_dir))

        a_end = acs_tm[edge:edge + 1, h0:h0 + HEADS_PER_GROUP]
        dt_tm = acstm_ref[0, 1, pl.ds(r0, L), :][:, h0:h0 + HEADS_PER_GROUP]
        w_state = _expand_heads(jnp.exp(a_end - acs_dir) * dt_tm)
        upd = lax.dot_general(bk, (x * w_state).astype(BF16), (((0,), (0,)), ((), ())),
                              preferred_element_type=F32)
        st_scr[direction] = state * jnp.exp(aend_ref[0, 0, c, direction:direction + 1, :]) + upd
        return r0, x, y

    st_scr[...] = jnp.zeros_like(st_scr)

    def fwd_body(c, carry):
        r0, _, y = chunk(c, 0)
        y_scr[pl.ds(r0, L), :] = y
        return carry

    lax.fori_loop(0, nc, fwd_body, 0)

    def bwd_body(i, carry):
        c = nc - 1 - i
        r0, x, y = chunk(c, 1)
        y = y + y_scr[pl.ds(r0, L), :] + x * dskip_ref[0]
        yz = y * sz_ref[0, pl.ds(r0, L), :].astype(F32)
        ms = jnp.mean(yz * yz, axis=-1, keepdims=True)
        o_ref[0, pl.ds(r0, L), :] = (yz * lax.rsqrt(ms + RMS_EPS) * nw_ref[0]).astype(o_ref.dtype)
        return carry

    lax.fori_loop(0, nc, bwd_body, 0)


def _ssd(xbc, sz, cw, cb, dt_hm, acs_hm, tm2, aend, dskip, nw):
    nb, s, _ = xbc.shape
    nc = s // SSD_CHUNK
    return pl.pallas_call(
        _ssd_kernel,
        grid=(nb, SSD_GROUPS),
        in_specs=[
            pl.BlockSpec((1, s, GROUP_CH), lambda b, g: (b, 0, g)),
            pl.BlockSpec((1, s, SSD_STATE), lambda b, g: (b, 0, SSD_D_INNER // SSD_STATE + g)),
            pl.BlockSpec((1, s, SSD_STATE), lambda b, g: (b, 0, SSD_D_INNER // SSD_STATE + SSD_GROUPS + g)),
            pl.BlockSpec((1, s, GROUP_CH), lambda b, g: (b, 0, g)),
            pl.BlockSpec((1, 8, XBC_W), lambda b, g: (g, 0, 0)),
            pl.BlockSpec((1, 1, XBC_W), lambda b, g: (g, 0, 0)),
            pl.BlockSpec((1, 1, nc, 8, SSD_CHUNK), lambda b, g: (b, g, 0, 0, 0)),
            pl.BlockSpec((1, 1, nc, 8, SSD_CHUNK), lambda b, g: (b, g, 0, 0, 0)),
            pl.BlockSpec((1, 2, s, 8), lambda b, g: (b * SSD_GROUPS + g, 0, 0, 0)),
            pl.BlockSpec((1, 1, nc, 2, GROUP_CH), lambda b, g: (b, g, 0, 0, 0)),
            pl.BlockSpec((1, 1, GROUP_CH), lambda b, g: (g, 0, 0)),
            pl.BlockSpec((1, 1, GROUP_CH), lambda b, g: (g, 0, 0)),
        ],
        out_specs=pl.BlockSpec((1, s, GROUP_CH), lambda b, g: (b, 0, g)),
        out_shape=jax.ShapeDtypeStruct((nb, s, SSD_D_INNER), BF16),
        scratch_shapes=[
            pltpu.VMEM((XBC_W // LANES, s + 2 * SSD_PAD, LANES), F32),
            pltpu.VMEM((XBC_W // LANES, s, LANES), F32),
            pltpu.VMEM((s, GROUP_CH), F32),
            pltpu.VMEM((2, SSD_STATE, GROUP_CH), F32),
        ],
        compiler_params=_cparams(("parallel", "parallel")),
        name="ssd_scan",
    )(xbc, xbc, xbc, sz, cw, cb, dt_hm, acs_hm, tm2, aend, dskip, nw)


D_SLABS = D_MODEL // LANES


def _conformer_kernel(u_ref, up_ref, un_ref, mb_ref, g_ref, x_ref, dww_ref, dwb_ref, lnw_ref, lnb_ref,
                      pw_ref, wo_ref, o_ref, u_scr, c_scr):
    i = pl.program_id(1)
    n = pl.num_programs(1)
    tq = u_ref.shape[1]
    for j in range(D_SLABS):
        cs = slice(j * LANES, (j + 1) * LANES)
        u_scr[j, pl.ds(0, HALO), :] = jnp.where(i > 0, up_ref[0, :, cs].astype(F32), 0.0)
        u_scr[j, pl.ds(HALO, tq), :] = u_ref[0, :, cs].astype(F32)
        u_scr[j, pl.ds(HALO + tq, HALO), :] = jnp.where(i < n - 1, un_ref[0, :, cs].astype(F32), 0.0)
    for j in range(D_SLABS):
        _depthwise_conv(u_scr, c_scr, j, dww_ref, dwb_ref, tq, CONV_WIDTH, HALO, lambda a: a)

    cv = jnp.concatenate([c_scr[j] for j in range(D_SLABS)], axis=1)
    mu = jnp.mean(cv, axis=-1, keepdims=True)
    xc = cv - mu
    var = jnp.mean(xc * xc, axis=-1, keepdims=True)
    y = xc * lax.rsqrt(var + LN_EPS) * lnw_ref[...] + lnb_ref[...]
    a = jnp.dot(_silu(y).astype(BF16), pw_ref[...], preferred_element_type=F32)
    merged = g_ref[0].astype(F32) * a + mb_ref[0].astype(F32)
    o_ref[0] = x_ref[0] + jnp.dot(merged.astype(BF16), wo_ref[...], preferred_element_type=F32)


def _conformer(u, mb, gates, x, dww, dwb, lnw, lnb, pw, wo, tq):
    nb, s, d = x.shape
    hb = tq // HALO
    nh = s // HALO
    wspec = pl.BlockSpec((d, d), lambda b, i: (0, 0))
    vspec = pl.BlockSpec((1, d), lambda b, i: (0, 0))
    return pl.pallas_call(
        _conformer_kernel,
        grid=(nb, s // tq),
        in_specs=[
            pl.BlockSpec((1, tq, d), lambda b, i: (b, i, 0)),
            pl.BlockSpec((1, HALO, d), lambda b, i: (b, jnp.maximum(i * hb - 1, 0), 0)),
            pl.BlockSpec((1, HALO, d), lambda b, i: (b, jnp.minimum((i + 1) * hb, nh - 1), 0)),
            pl.BlockSpec((1, tq, d), lambda b, i: (b, i, 0)),
            pl.BlockSpec((1, tq, d), lambda b, i: (b, i, 0)),
            pl.BlockSpec((1, tq, d), lambda b, i: (b, i, 0)),
            pl.BlockSpec((32, d), lambda b, i: (0, 0)),
            vspec, vspec, vspec, wspec, wspec,
        ],
        out_specs=pl.BlockSpec((1, tq, d), lambda b, i: (b, i, 0)),
        out_shape=jax.ShapeDtypeStruct((nb, s, d), F32),
        scratch_shapes=[pltpu.VMEM((D_SLABS, tq + 2 * HALO, LANES), F32), pltpu.VMEM((D_SLABS, tq, LANES), F32)],
        compiler_params=_cparams(("parallel", "parallel")),
        name="conformer_merge",
    )(u, u, u, mb, gates, x, dww, dwb, lnw, lnb, pw, wo)


def _attn_kernel(x_ref, k_ref, v_ref, nw_ref, wq_ref, wo_ref, nf_ref, wr_ref, br_ref, x2_ref, h3_ref, lg_ref):
    x = x_ref[0]
    h = _rms(x, nw_ref[...]).astype(BF16)
    q = jnp.dot(h, wq_ref[...], preferred_element_type=F32)
    scale = 1.0 / math.sqrt(XATTN_HEAD_DIM)
    outs = []
    for hd in range(XATTN_HEADS):
        cs = slice(hd * XATTN_HEAD_DIM, (hd + 1) * XATTN_HEAD_DIM)
        sc = lax.dot_general(q[:, cs].astype(BF16), k_ref[0, :, cs], (((1,), (1,)), ((), ())),
                             preferred_element_type=F32) * scale
        m = jnp.max(sc, axis=-1, keepdims=True)
        e = jnp.exp(sc - m)
        p = e / jnp.sum(e, axis=-1, keepdims=True)
        outs.append(jnp.dot(p.astype(BF16), v_ref[0, :, cs], preferred_element_type=F32))
    o = jnp.concatenate(outs, axis=1).astype(BF16)
    x2 = x + jnp.dot(o, wo_ref[...], preferred_element_type=F32)
    x2_ref[0] = x2
    h3 = _rms(x2, nf_ref[...])
    h3_ref[0] = h3
    lg_ref[0] = jnp.dot(h3.astype(BF16), wr_ref[...], preferred_element_type=F32) + br_ref[...]


def _attention(x, kv, nw, wq, wo, nf, wr, br, tq):
    nb, s, d = x.shape
    m = kv.shape[1]
    wspec = pl.BlockSpec((d, d), lambda b, i: (0, 0))
    vspec = pl.BlockSpec((1, d), lambda b, i: (0, 0))
    return pl.pallas_call(
        _attn_kernel,
        grid=(nb, s // tq),
        in_specs=[
            pl.BlockSpec((1, tq, d), lambda b, i: (b, i, 0)),
            pl.BlockSpec((1, m, d), lambda b, i: (b, 0, 0)),
            pl.BlockSpec((1, m, d), lambda b, i: (b, 0, 1)),
            vspec, wspec, wspec, vspec,
            pl.BlockSpec((d, LANES), lambda b, i: (0, 0)),
            pl.BlockSpec((1, LANES), lambda b, i: (0, 0)),
        ],
        out_specs=[
            pl.BlockSpec((1, tq, d), lambda b, i: (b, i, 0)),
            pl.BlockSpec((1, tq, d), lambda b, i: (b, i, 0)),
            pl.BlockSpec((1, tq, LANES), lambda b, i: (b, i, 0)),
        ],
        out_shape=[
            jax.ShapeDtypeStruct((nb, s, d), F32),
            jax.ShapeDtypeStruct((nb, s, d), F32),
            jax.ShapeDtypeStruct((nb, s, LANES), F32),
        ],
        compiler_params=_cparams(("parallel", "parallel")),
        name="cross_attn_router",
    )(x, kv, kv, nw, wq, wo, nf, wr, br)


MOE_ROWS = 256


def _route_math(lg, lane):
    lane_f = lane.astype(F32)
    big = float(LANES)
    neg = -jnp.inf

    is_g = lane < N_GROUPS
    gl = jnp.where(is_g, lg, neg)
    gmax = jnp.max(gl, axis=-1, keepdims=True)
    g_idx = jnp.min(jnp.where(gl == gmax, lane_f, big), axis=-1, keepdims=True)
    g_sum = jnp.sum(jnp.where(is_g, jnp.exp(gl - gmax), 0.0), axis=-1, keepdims=True)
    g_w = 1.0 / g_sum

    e_grp = lax.shift_right_arithmetic(lane - N_GROUPS, 3).astype(F32)
    is_e = (lane >= N_GROUPS) & (lane < N_GROUPS + N_EXPERTS) & (e_grp == g_idx)
    el = jnp.where(is_e, lg, neg)
    emax = jnp.max(el, axis=-1, keepdims=True)
    ee = jnp.where(is_e, jnp.exp(el - emax), 0.0)
    ep = ee / jnp.sum(ee, axis=-1, keepdims=True)
    ep = jnp.where(is_e, ep, -1.0)
    v1 = jnp.max(ep, axis=-1, keepdims=True)
    i1 = jnp.min(jnp.where(ep == v1, lane_f, big), axis=-1, keepdims=True)
    ep2 = jnp.where(lane_f == i1, -1.0, ep)
    v2 = jnp.max(ep2, axis=-1, keepdims=True)
    i2 = jnp.min(jnp.where(ep2 == v2, lane_f, big), axis=-1, keepdims=True)
    tot = v1 + v2
    gate1 = g_w * (v1 / tot)
    gate2 = g_w * (v2 / tot)
    return i1 - N_GROUPS, i2 - N_GROUPS, gate1, gate2


def _plan_kernel(lg_ref, gate_ref, pos_ref, cnt_ref, carry_scr, base_scr):
    p = pl.program_id(0)
    i = pl.program_id(1)
    tm = lg_ref.shape[0]
    lane = lax.broadcasted_iota(jnp.int32, (tm, LANES), 1)
    lane_f = lane.astype(F32)
    e1, e2, gate1, gate2 = _route_math(lg_ref[...], lane)
    oh1 = lane_f == e1
    oh2 = lane_f == e2
    m = jnp.where(oh1 | oh2, 1.0, 0.0)
    colsum = jnp.sum(m, axis=0, keepdims=True)

    @pl.when((p == 0) & (i == 0))
    def _():
        carry_scr[...] = jnp.zeros_like(carry_scr)

    @pl.when(p == 0)
    def _():
        carry_scr[...] += colsum

    @pl.when((p == 1) & (i == 0))
    def _():
        counts = carry_scr[...]
        cnt_ref[...] = counts
        shift = MOE_ROWS.bit_length() - 1
        blocks = lax.shift_right_logical(counts.astype(jnp.int32) + (MOE_ROWS - 1), shift)
        padded = (blocks * MOE_ROWS).astype(F32)
        lane8 = lax.broadcasted_iota(jnp.int32, padded.shape, 1)
        inc = padded
        sh = 1
        while sh < LANES:
            inc = inc + jnp.where(lane8 >= sh, pltpu.roll(inc, sh, 1), 0.0)
            sh *= 2
        base_scr[...] = inc - padded
        carry_scr[...] = jnp.zeros_like(carry_scr)

    @pl.when(p == 1)
    def _():
        r = lax.broadcasted_iota(jnp.int32, (tm, tm), 0)
        c = lax.broadcasted_iota(jnp.int32, (tm, tm), 1)
        earlier = jnp.where(r > c, 1.0, 0.0).astype(BF16)
        tot = (jnp.dot(earlier, m.astype(BF16), preferred_element_type=F32)
               + base_scr[0:1, :] + carry_scr[0:1, :])
        pos1 = jnp.sum(jnp.where(oh1, tot, 0.0), axis=1, keepdims=True)
        pos2 = jnp.sum(jnp.where(oh2, tot, 0.0), axis=1, keepdims=True)
        pos_ref[...] = jnp.where(lane == 0, pos1, jnp.where(lane == 1, pos2, 0.0)).astype(jnp.int32)
        gate_ref[...] = jnp.where(lane == 0, gate1, jnp.where(lane == 1, gate2, 0.0))
        carry_scr[...] += colsum


def _plan(lg, tm):
    t = lg.shape[0]
    out_spec = pl.BlockSpec((tm, LANES), lambda p, i: (i * p, 0))
    return pl.pallas_call(
        _plan_kernel,
        grid=(2, t // tm),
        in_specs=[pl.BlockSpec((tm, LANES), lambda p, i: (i, 0))],
        out_specs=[out_spec, out_spec, pl.BlockSpec((8, LANES), lambda p, i: (0, 0))],
        out_shape=[jax.ShapeDtypeStruct((t, LANES), F32), jax.ShapeDtypeStruct((t, LANES), jnp.int32),
                   jax.ShapeDtypeStruct((8, LANES), F32)],
        scratch_shapes=[pltpu.VMEM((8, LANES), F32), pltpu.VMEM((8, LANES), F32)],
        compiler_params=_cparams(("arbitrary", "arbitrary")),
        name="moe_plan",
    )(lg)


def _dispatch_kernel(pos_ref, h_ref, xs_in_hbm, xs_hbm, sem):
    del xs_in_hbm
    td = h_ref.shape[0]

    def body(r, carry):
        src = h_ref.at[pl.ds(r, 1)]
        pltpu.make_async_copy(src, xs_hbm.at[pl.ds(pos_ref[0, 0, r], 1)], sem).start()
        pltpu.make_async_copy(src, xs_hbm.at[pl.ds(pos_ref[0, 0, td + r], 1)], sem).start()
        return carry

    lax.fori_loop(0, td, body, 0, unroll=8)
    for _ in range(2):
        pltpu.make_async_copy(h_ref, xs_hbm.at[pl.ds(0, td)], sem).wait()


def _dispatch(pos_blk, h3, xs_zero):
    n_steps = pos_blk.shape[0]
    td = pos_blk.shape[2] // 2
    return pl.pallas_call(
        _dispatch_kernel,
        grid=(n_steps,),
        in_specs=[
            pl.BlockSpec((1, 1, 2 * td), lambda i: (i, 0, 0), memory_space=pltpu.SMEM),
            pl.BlockSpec((td, h3.shape[1]), lambda i: (i, 0)),
            pl.BlockSpec(memory_space=pl.ANY),
        ],
        out_specs=pl.BlockSpec(memory_space=pl.ANY),
        out_shape=jax.ShapeDtypeStruct(xs_zero.shape, xs_zero.dtype),
        scratch_shapes=[pltpu.SemaphoreType.DMA(())],
        input_output_aliases={2: 0},
        compiler_params=_cparams(("arbitrary",)),
        name="moe_dispatch",
    )(pos_blk, h3, xs_zero)


def _expert_kernel(be_ref, na_ref, x_ref, wg_ref, wu_ref, wd_ref, o_ref, wg_scr, wu_scr, wd_scr):
    j = pl.program_id(0)
    prev = be_ref[jnp.maximum(j - 1, 0)]

    @pl.when((j == 0) | (be_ref[j] != prev))
    def _():
        wg_scr[...] = wg_ref[0].astype(BF16)
        wu_scr[...] = wu_ref[0].astype(BF16)
        wd_scr[...] = wd_ref[0].astype(BF16)

    @pl.when(j < na_ref[0])
    def _():
        x = x_ref[...].astype(BF16)
        g = jnp.dot(x, wg_scr[...], preferred_element_type=F32)
        u = jnp.dot(x, wu_scr[...], preferred_element_type=F32)
        hid = (_silu(g) * u).astype(BF16)
        o_ref[...] = jnp.dot(hid, wd_scr[...], preferred_element_type=F32)

    @pl.when(j >= na_ref[0])
    def _():
        o_ref[...] = jnp.zeros_like(o_ref)


def _experts(blk_exp, n_active, xs, wg, wu, wd):
    n_blocks = blk_exp.shape[0]
    d = xs.shape[1]
    ff = wg.shape[2]
    grid_spec = pltpu.PrefetchScalarGridSpec(
        num_scalar_prefetch=2,
        grid=(n_blocks,),
        in_specs=[
            pl.BlockSpec((MOE_ROWS, d), lambda j, be, na: (j, 0)),
            pl.BlockSpec((1, d, ff), lambda j, be, na: (be[j], 0, 0)),
            pl.BlockSpec((1, d, ff), lambda j, be, na: (be[j], 0, 0)),
            pl.BlockSpec((1, ff, d), lambda j, be, na: (be[j], 0, 0)),
        ],
        out_specs=pl.BlockSpec((MOE_ROWS, d), lambda j, be, na: (j, 0)),
        scratch_shapes=[pltpu.VMEM((d, ff), BF16), pltpu.VMEM((d, ff), BF16), pltpu.VMEM((ff, d), BF16)],
    )
    return pl.pallas_call(
        _expert_kernel,
        grid_spec=grid_spec,
        out_shape=jax.ShapeDtypeStruct((n_blocks * MOE_ROWS, d), F32),
        compiler_params=_cparams(("arbitrary",)),
        name="moe_experts",
    )(blk_exp, n_active, xs, wg, wu, wd)


def _combine_kernel(pos_ref, x_ref, gate_ref, nw_ref, ys_hbm, o_ref, y_scr, sem):
    tm = x_ref.shape[0]

    def body(r, carry):
        pltpu.make_async_copy(ys_hbm.at[pl.ds(pos_ref[0, 0, r], 1)], y_scr.at[pl.ds(r, 1)], sem).start()
        return carry

    lax.fori_loop(0, 2 * tm, body, 0, unroll=8)
    pltpu.make_async_copy(ys_hbm.at[pl.ds(0, 2 * tm)], y_scr, sem).wait()
    gate = gate_ref[...]
    y = x_ref[...] + (gate[:, 0:1] * y_scr[pl.ds(0, tm), :] + gate[:, 1:2] * y_scr[pl.ds(tm, tm), :])
    o_ref[...] = _rms(y, nw_ref[...])


def _combine(pos_blk, x2, gates, nw, ys, tm):
    t, d = x2.shape
    return pl.pallas_call(
        _combine_kernel,
        grid=(t // tm,),
        in_specs=[
            pl.BlockSpec((1, 1, 2 * tm), lambda i: (i, 0, 0), memory_space=pltpu.SMEM),
            pl.BlockSpec((tm, d), lambda i: (i, 0)),
            pl.BlockSpec((tm, LANES), lambda i: (i, 0)),
            pl.BlockSpec((1, d), lambda i: (0, 0)),
            pl.BlockSpec(memory_space=pl.ANY),
        ],
        out_specs=pl.BlockSpec((tm, d), lambda i: (i, 0)),
        out_shape=jax.ShapeDtypeStruct((t, d), F32),
        scratch_shapes=[pltpu.VMEM((2 * tm, d), F32), pltpu.SemaphoreType.DMA(())],
        compiler_params=_cparams(("arbitrary",)),
        name="moe_combine",
    )(pos_blk, x2, gates, nw, ys)


def _block_experts(counts, n_blocks):
    per_expert = (counts + MOE_ROWS - 1) // MOE_ROWS
    ends = jnp.cumsum(per_expert)
    blk = jnp.arange(n_blocks, dtype=jnp.int32)
    blk_exp = jnp.minimum(jnp.sum(blk[:, None] >= ends[None, :], axis=1), N_EXPERTS - 1).astype(jnp.int32)
    return blk_exp, ends[-1:].astype(jnp.int32)


def _pick(n, pref):
    t = pref
    while n % t:
        t //= 2
    return t


def _encoder(x, mem, p):
    nb, s, d = x.shape
    t = nb * s
    nc = s // SSD_CHUNK
    tm = _pick(t, 1024)
    tq = _pick(s, 512)
    xf = x.reshape(t, d)

    perm = jnp.arange(SSD_HEADS).reshape(SSD_GROUPS, HEADS_PER_GROUP)
    perm = jnp.concatenate([perm, perm + SSD_HEADS], axis=1).reshape(-1)

    w_in = p['w_in']
    o_z = 2 * D_MODEL
    o_x = o_z + SSD_D_INNER
    o_dt = o_x + SSD_D_INNER + 2 * SSD_GROUPS * SSD_STATE
    o_g = o_dt + 2 * SSD_HEADS
    w_val = w_in[:, :D_MODEL].astype(BF16)
    w_gate = w_in[:, D_MODEL:o_z].astype(BF16)
    w_z = w_in[:, o_z:o_x].astype(BF16)
    w_xbc = w_in[:, o_x:o_dt].astype(BF16)
    w_dt = jnp.pad(w_in[:, o_dt:o_g][:, perm], ((0, 0), (0, LANES - 2 * SSD_HEADS))).astype(BF16)
    w_g = w_in[:, o_g:].astype(BF16)
    nmix = p['norm_mix_w'].reshape(1, d)

    u = _glu_proj(xf, nmix, w_val, w_gate, tm, 512)
    sz = _norm_proj(xf, nmix, w_z, _silu, BF16, tm, 512, "z_proj")
    xbc = _norm_proj(xf, nmix, w_xbc, lambda a: a, BF16, tm, 512, "xbc_proj")
    dt_raw = _norm_proj(xf, nmix, w_dt, lambda a: a, F32, tm, LANES, "dt_proj")
    gates = _norm_proj(xf, nmix, w_g, _sigmoid, BF16, tm, 512, "gate_proj")

    pad = LANES - 2 * SSD_HEADS
    dt_bias = jnp.pad(jnp.concatenate([p['ssd_dt_bias_f'], p['ssd_dt_bias_b']])[perm], (0, pad)).reshape(1, LANES)
    a_log = jnp.pad(jnp.concatenate([p['ssd_a_log_f'], p['ssd_a_log_b']])[perm], (0, pad)).reshape(1, LANES)
    dtv, acs = _dtprep(dt_raw, dt_bias, a_log, tm)

    def per_group_tm(a):
        return a[:, :2 * SSD_HEADS].reshape(nb, s, SSD_GROUPS, 8).transpose(0, 2, 1, 3)

    def per_group_hm(a):
        return a[:, :2 * SSD_HEADS].reshape(nb, nc, SSD_CHUNK, SSD_GROUPS, 8).transpose(0, 3, 1, 4, 2)

    acs_g = per_group_tm(acs)
    tm2 = jnp.stack([acs_g, per_group_tm(dtv)], axis=2).reshape(nb * SSD_GROUPS, 2, s, 8)
    acs_hm = per_group_hm(acs)
    dt_hm = per_group_hm(dtv)
    acs_c = acs_g.reshape(nb, SSD_GROUPS, nc, SSD_CHUNK, 8)
    aend = jnp.stack([acs_c[:, :, :, SSD_CHUNK - 1, :HEADS_PER_GROUP], acs_c[:, :, :, 0, HEADS_PER_GROUP:]], axis=3)
    aend = jnp.repeat(aend, SSD_HEAD_DIM, axis=-1)

    cwx = p['ssd_conv_w']
    cbx = p['ssd_conv_b']

    def per_group_conv(a):
        xs_ = a[:, :SSD_D_INNER].reshape(-1, SSD_GROUPS, GROUP_CH)
        b_ = a[:, SSD_D_INNER:SSD_D_INNER + SSD_GROUPS * SSD_STATE].reshape(-1, SSD_GROUPS, SSD_STATE)
        c_ = a[:, SSD_D_INNER + SSD_GROUPS * SSD_STATE:].reshape(-1, SSD_GROUPS, SSD_STATE)
        return jnp.concatenate([xs_, b_, c_], axis=-1).transpose(1, 0, 2)

    cw = jnp.pad(per_group_conv(cwx), ((0, 0), (0, 8 - SSD_CONV_WIDTH), (0, 0)))
    cb = per_group_conv(cbx.reshape(1, -1))
    dskip = jnp.repeat(p['ssd_d'], SSD_HEAD_DIM).reshape(SSD_GROUPS, 1, GROUP_CH)
    nw_ssd = p['ssd_norm_w'].reshape(SSD_GROUPS, 1, GROUP_CH)

    yn = _ssd(xbc.reshape(nb, s, -1), sz.reshape(nb, s, -1), cw, cb, dt_hm, acs_hm, tm2, aend, dskip, nw_ssd)
    mb = _gated_proj(yn.reshape(t, SSD_D_INNER), p['ssd_out'].astype(BF16), gates, 1, tm)

    dww = jnp.pad(p['conv_dw_w'], ((0, 32 - CONV_WIDTH), (0, 0)))
    x1 = _conformer(u.reshape(nb, s, d), mb.reshape(nb, s, d), gates.reshape(nb, s, 2 * d), x, dww,
                    p['conv_dw_b'].reshape(1, d), p['conv_ln_w'].reshape(1, d), p['conv_ln_b'].reshape(1, d),
                    p['conv_pw_out'].astype(BF16), p['w_out'].astype(BF16), tq)

    m = mem.shape[1]
    kv = _norm_proj(mem.reshape(nb * m, d), p['norm_mem_w'].reshape(1, d), p['xattn_wkv'].astype(BF16),
                    lambda a: a, BF16, _pick(nb * m, 512), 1024, "kv_proj")
    n_r = N_GROUPS + N_EXPERTS
    wr = jnp.pad(jnp.concatenate([p['router_group_w'], p['router_expert_w']], axis=1),
                 ((0, 0), (0, LANES - n_r))).astype(BF16)
    br = jnp.pad(jnp.concatenate([p['router_group_b'], p['router_expert_b']]), (0, LANES - n_r)).reshape(1, LANES)
    x2, h3, logits = _attention(x1, kv.reshape(nb, m, 2 * d), p['norm_xattn_w'].reshape(1, d),
                                p['xattn_wq'].astype(BF16), p['xattn_wo'].astype(BF16),
                                p['norm_ffn_w'].reshape(1, d), wr, br, tq)

    tc = _pick(t, 512)
    gate, pos, cnt = _plan(logits.reshape(t, LANES), tc)
    n_blocks = (2 * t) // MOE_ROWS + N_EXPERTS
    blk_exp, n_active = _block_experts(cnt[0, :N_EXPERTS].astype(jnp.int32), n_blocks)
    pos_blk = pos[:, :2].reshape(t // tc, tc, 2).transpose(0, 2, 1).reshape(t // tc, 1, 2 * tc)
    xs = _dispatch(pos_blk, h3.reshape(t, d), jnp.zeros((n_blocks * MOE_ROWS, d), F32))
    ys = _experts(blk_exp, n_active, xs, p['expert_w_gate'], p['expert_w_up'], p['expert_w_down'])
    out = _combine(pos_blk, x2.reshape(t, d), gate, p['norm_final_w'].reshape(1, d), ys, tc)
    return out.reshape(nb, s, d)


def kernel(x_prompt, x_sample, mem_prompt, mem_sample, norm_mix_w, w_in, conv_dw_w, conv_dw_b, conv_ln_w, conv_ln_b, conv_pw_out, ssd_conv_w, ssd_conv_b, ssd_dt_bias_f, ssd_dt_bias_b, ssd_a_log_f, ssd_a_log_b, ssd_d, ssd_norm_w, ssd_out, w_out, norm_xattn_w, norm_mem_w, xattn_wq, xattn_wkv, xattn_wo, norm_ffn_w, router_group_w, router_group_b, router_expert_w, router_expert_b, expert_w_gate, expert_w_up, expert_w_down, norm_final_w):
    p = {
        'norm_mix_w': norm_mix_w[0], 'w_in': w_in[0], 'conv_dw_w': conv_dw_w[0], 'conv_dw_b': conv_dw_b[0],
        'conv_ln_w': conv_ln_w[0], 'conv_ln_b': conv_ln_b[0], 'conv_pw_out': conv_pw_out[0],
        'ssd_conv_w': ssd_conv_w[0], 'ssd_conv_b': ssd_conv_b[0], 'ssd_dt_bias_f': ssd_dt_bias_f[0],
        'ssd_dt_bias_b': ssd_dt_bias_b[0], 'ssd_a_log_f': ssd_a_log_f[0], 'ssd_a_log_b': ssd_a_log_b[0],
        'ssd_d': ssd_d[0], 'ssd_norm_w': ssd_norm_w[0], 'ssd_out': ssd_out[0], 'w_out': w_out[0],
        'norm_xattn_w': norm_xattn_w[0], 'norm_mem_w': norm_mem_w[0], 'xattn_wq': xattn_wq[0],
        'xattn_wkv': xattn_wkv[0], 'xattn_wo': xattn_wo[0], 'norm_ffn_w': norm_ffn_w[0],
        'router_group_w': router_group_w[0], 'router_group_b': router_group_b[0],
        'router_expert_w': router_expert_w[0], 'router_expert_b': router_expert_b[0],
        'expert_w_gate': expert_w_gate[0], 'expert_w_up': expert_w_up[0], 'expert_w_down': expert_w_down[0],
        'norm_final_w': norm_final_w,
    }
    n_prompt = x_prompt.shape[0]
    x = jnp.concatenate([x_prompt, x_sample], axis=0)
    mem = jnp.concatenate([mem_prompt, mem_sample], axis=0)
    y = _encoder(x, mem, p)
    return (y[:n_prompt], y[n_prompt:])
```

```python
import functools
import math

import jax
import jax.numpy as jnp
from jax import lax
from jax.experimental import pallas as pl
from jax.experimental.pallas import tpu as pltpu

F32 = jnp.float32
BF16 = jnp.bfloat16

D_MODEL = 1024
CONV_WIDTH = 31
SSD_D_INNER = 2048
SSD_HEAD_DIM = 64
SSD_HEADS = 32
SSD_GROUPS = 8
SSD_STATE = 128
SSD_CONV_WIDTH = 5
SSD_CHUNK = 128
GROUP_CH = SSD_D_INNER // SSD_GROUPS
HEADS_PER_GROUP = SSD_HEADS // SSD_GROUPS
XATTN_HEADS = 4
XATTN_HEAD_DIM = D_MODEL // XATTN_HEADS
N_GROUPS = 8
EXPERTS_PER_GROUP = 8
N_EXPERTS = 64
EXPERT_FF = 512
MOE_BLOCK = 128
RMS_EPS = 1e-6
LN_EPS = 1e-5

LANES = 128
HALO = 16
VMEM_LIMIT = 56 * 1024 * 1024
PROJ_TN = 2048


def _cparams(sem):
    return pltpu.CompilerParams(dimension_semantics=sem, vmem_limit_bytes=VMEM_LIMIT)


def _rms(x, w):
    ms = jnp.mean(x * x, axis=-1, keepdims=True)
    return x * lax.rsqrt(ms + RMS_EPS) * w


def _sigmoid(x):
    return 1.0 / (1.0 + jnp.exp(-x))


def _silu(x):
    return x * _sigmoid(x)


def _softplus(x):
    return jnp.maximum(x, 0.0) + jnp.log1p(jnp.exp(-jnp.abs(x)))


CONV_STRIDE = 4
CONV_ROWS = 8 * CONV_STRIDE
CONV_UNROLL = 4


def _depthwise_conv(src, dst, slab, w_ref, b_ref, n_rows, width, first_row, epilogue):
    lanes = pl.ds(slab * LANES, LANES)
    taps = [jnp.broadcast_to(w_ref[k:k + 1, lanes], (8, LANES)) for k in range(width)]
    bias = jnp.broadcast_to(b_ref[:, lanes], (8, LANES))
    step = CONV_ROWS * CONV_UNROLL

    def body(r, carry):
        base = r * step
        for t0 in range(CONV_STRIDE * CONV_UNROLL):
            row = base + (t0 // CONV_STRIDE) * CONV_ROWS + t0 % CONV_STRIDE
            acc = bias
            for k in range(width):
                acc = acc + src[slab, pl.ds(row + first_row - width // 2 + k, 8, stride=CONV_STRIDE), :] * taps[k]
            dst[slab, pl.ds(row, 8, stride=CONV_STRIDE), :] = epilogue(acc)
        return carry

    lax.fori_loop(0, n_rows // step, body, 0)


def _glu_kernel(x_ref, nw_ref, wv_ref, wg_ref, o_ref, h_scr):
    @pl.when(pl.program_id(1) == 0)
    def _():
        h_scr[...] = _rms(x_ref[...], nw_ref[...]).astype(BF16)

    h = h_scr[...]
    v = jnp.dot(h, wv_ref[...], preferred_element_type=F32)
    g = jnp.dot(h, wg_ref[...], preferred_element_type=F32)
    o_ref[...] = (v * _sigmoid(g)).astype(o_ref.dtype)


def _glu_proj(x, nw, wv, wg, tm, tn):
    t, d = x.shape
    n = wv.shape[1]
    return pl.pallas_call(
        _glu_kernel,
        grid=(t // tm, n // tn),
        in_specs=[
            pl.BlockSpec((tm, d), lambda i, j: (i, 0)),
            pl.BlockSpec((1, d), lambda i, j: (0, 0)),
            pl.BlockSpec((d, tn), lambda i, j: (0, j)),
            pl.BlockSpec((d, tn), lambda i, j: (0, j)),
        ],
        out_specs=pl.BlockSpec((tm, tn), lambda i, j: (i, j)),
        out_shape=jax.ShapeDtypeStruct((t, n), BF16),
        scratch_shapes=[pltpu.VMEM((tm, d), BF16)],
        compiler_params=_cparams(("parallel", "arbitrary")),
        name="glu_proj",
    )(x, nw, wv, wg)


def _norm_proj_kernel(epi, x_ref, nw_ref, w_ref, o_ref, h_scr):
    @pl.when(pl.program_id(1) == 0)
    def _():
        h_scr[...] = _rms(x_ref[...], nw_ref[...]).astype(BF16)

    acc = jnp.dot(h_scr[...], w_ref[...], preferred_element_type=F32)
    o_ref[...] = epi(acc).astype(o_ref.dtype)


def _norm_proj(x, nw, w, epi, out_dtype, tm, tn, name):
    t, d = x.shape
    n = w.shape[1]
    return pl.pallas_call(
        functools.partial(_norm_proj_kernel, epi),
        grid=(t // tm, n // tn),
        in_specs=[
            pl.BlockSpec((tm, d), lambda i, j: (i, 0)),
            pl.BlockSpec((1, d), lambda i, j: (0, 0)),
            pl.BlockSpec((d, tn), lambda i, j: (0, j)),
        ],
        out_specs=pl.BlockSpec((tm, tn), lambda i, j: (i, j)),
        out_shape=jax.ShapeDtypeStruct((t, n), out_dtype),
        scratch_shapes=[pltpu.VMEM((tm, d), BF16)],
        compiler_params=_cparams(("parallel", "arbitrary")),
        name=name,
    )(x, nw, w)


def _gated_proj_kernel(y_ref, w_ref, g_ref, o_ref):
    acc = jnp.dot(y_ref[...], w_ref[...], preferred_element_type=F32)
    o_ref[...] = (acc * g_ref[...].astype(F32)).astype(o_ref.dtype)


def _gated_proj(y, w, g, g_col_block, tm):
    t, k = y.shape
    n = w.shape[1]
    return pl.pallas_call(
        _gated_proj_kernel,
        grid=(t // tm,),
        in_specs=[
            pl.BlockSpec((tm, k), lambda i: (i, 0)),
            pl.BlockSpec((k, n), lambda i: (0, 0)),
            pl.BlockSpec((tm, n), lambda i: (i, g_col_block)),
        ],
        out_specs=pl.BlockSpec((tm, n), lambda i: (i, 0)),
        out_shape=jax.ShapeDtypeStruct((t, n), BF16),
        compiler_params=_cparams(("parallel",)),
        name="ssd_out_proj",
    )(y, w, g)


LOG2E = 1.4426950408889634


def _split_bf16(v):
    hi = v.astype(BF16)
    return hi, (v - hi.astype(F32)).astype(BF16)


def _dtprep_kernel(raw_ref, bias_ref, alog_ref, acs2_ref, rowt_ref, ehi_ref, elo_ref, whi_ref, wlo_ref, dec_ref):
    rows = raw_ref.shape[0]
    a_head = -jnp.exp(alog_ref[...])
    row = lax.broadcasted_iota(jnp.int32, (SSD_CHUNK, LANES), 0)
    lane = lax.broadcasted_iota(jnp.int32, (SSD_CHUNK, LANES), 1)
    is_bwd = (lane % (2 * HEADS_PER_GROUP)) >= HEADS_PER_GROUP
    for c in range(rows // SSD_CHUNK):
        sl = pl.ds(c * SSD_CHUNK, SSD_CHUNK)
        dt = _softplus(raw_ref[sl, :] + bias_ref[...])
        a = dt * a_head
        fwd = a
        bwd = a
        sh = 1
        while sh < SSD_CHUNK:
            fwd = fwd + jnp.where(row >= sh, pltpu.roll(fwd, sh, 0), 0.0)
            bwd = bwd + jnp.where(row < SSD_CHUNK - sh, pltpu.roll(bwd, SSD_CHUNK - sh, 0), 0.0)
            sh *= 2
        acs = jnp.where(is_bwd, bwd, fwd)
        a_end = jnp.where(is_bwd[0:1, :], bwd[0:1, :], fwd[SSD_CHUNK - 1:SSD_CHUNK, :])
        acs2 = acs * LOG2E
        acs2_ref[sl, :] = acs2
        rowt_ref[sl, :] = acs2 - jnp.log(dt) * LOG2E
        ehi_ref[sl, :], elo_ref[sl, :] = _split_bf16(jnp.exp(acs))
        whi_ref[sl, :], wlo_ref[sl, :] = _split_bf16(jnp.exp(a_end - acs) * dt)
        dec_ref[c:c + 1, :] = jnp.exp(a_end)


def _dtprep(raw, bias, alog, tm):
    t = raw.shape[0]
    spec = pl.BlockSpec((tm, LANES), lambda i: (i, 0))
    vec = pl.BlockSpec((1, LANES), lambda i: (0, 0))
    return pl.pallas_call(
        _dtprep_kernel,
        grid=(t // tm,),
        in_specs=[spec, vec, vec],
        out_specs=[spec] * 6 + [pl.BlockSpec((tm // SSD_CHUNK, LANES), lambda i: (i, 0))],
        out_shape=[jax.ShapeDtypeStruct((t, LANES), F32)] * 2 + [jax.ShapeDtypeStruct((t, LANES), BF16)] * 4
        + [jax.ShapeDtypeStruct((t // SSD_CHUNK, LANES), F32)],
        compiler_params=_cparams(("parallel",)),
        name="dt_prep",
    )(raw, bias, alog)


SSD_PAD = 8
XBC_W = GROUP_CH + 2 * SSD_STATE


EW_COLS = 4 * 2 * HEADS_PER_GROUP


def _head_expand_matrix():
    col = jnp.arange(4 * GROUP_CH)
    part = col // GROUP_CH
    is_w = (part == 1) | (part == 2)
    is_bwd = part >= 2
    head = is_bwd * HEADS_PER_GROUP + (col % GROUP_CH) // SSD_HEAD_DIM
    hi_row = is_w * 16 + head
    row = jnp.arange(EW_COLS)[:, None]
    return ((row == hi_row[None, :]) | (row == hi_row[None, :] + 8)).astype(BF16)


def _ssd_kernel(xs_ref, b_ref, c_ref, sz_ref, cw_ref, cb_ref, rowt_ref, acs_ref, ew_ref, sel_ref, dec_ref,
                dskip_ref, nw_ref, o_ref, raw_scr, act_scr, y_scr, upd_scr, st_scr):
    s = xs_ref.shape[1]
    nc = s // SSD_CHUNK
    L = SSD_CHUNK

    zpad = jnp.zeros((SSD_PAD, LANES), F32)
    for j in range(XBC_W // LANES):
        raw_scr[j, pl.ds(0, SSD_PAD), :] = zpad
        raw_scr[j, pl.ds(SSD_PAD + s, SSD_PAD), :] = zpad
    raw_scr[0, pl.ds(SSD_PAD, s), :] = xs_ref[0, :, :LANES].astype(F32)
    raw_scr[1, pl.ds(SSD_PAD, s), :] = xs_ref[0, :, LANES:].astype(F32)
    raw_scr[2, pl.ds(SSD_PAD, s), :] = b_ref[0].astype(F32)
    raw_scr[3, pl.ds(SSD_PAD, s), :] = c_ref[0].astype(F32)
    for j in range(XBC_W // LANES):
        _depthwise_conv(raw_scr, act_scr, j, cw_ref.at[0], cb_ref.at[0], s, SSD_CONV_WIDTH, SSD_PAD, _silu)

    li = lax.broadcasted_iota(jnp.int32, (L, L), 0)
    si = lax.broadcasted_iota(jnp.int32, (L, L), 1)
    head_of_ch = lax.broadcasted_iota(jnp.int32, (L, GROUP_CH), 1) // SSD_HEAD_DIM

    keep = (li >= si, si >= li)
    neg_inf = jnp.float32(-jnp.inf)
    n_heads = 2 * HEADS_PER_GROUP
    st_scr[...] = jnp.zeros_like(st_scr)

    def local_body(c, carry):
        r0 = pl.multiple_of(c * L, L)
        x = jnp.concatenate([act_scr[0, pl.ds(r0, L), :], act_scr[1, pl.ds(r0, L), :]], axis=1)
        xb = x.astype(BF16)
        bk = act_scr[2, pl.ds(r0, L), :].astype(BF16)
        ck = act_scr[3, pl.ds(r0, L), :].astype(BF16)
        acs = acs_ref[0, pl.ds(r0, L), :]
        rowt = rowt_ref[0, 0, c]
        ew = jnp.dot(ew_ref[0, pl.ds(r0, L), :], sel_ref[:, :3 * GROUP_CH],
                     preferred_element_type=F32)

        cb = lax.dot_general(ck, bk, (((1,), (1,)), ((), ())), preferred_element_type=F32)
        ms = []
        for j in range(n_heads):
            diff = acs[:, j:j + 1] - rowt[j:j + 1, :]
            decay_dt = jnp.exp2(jnp.where(keep[j // HEADS_PER_GROUP], diff, neg_inf))
            ms.append((cb * decay_dt).astype(BF16))
        xbd = jnp.concatenate([jnp.where(head_of_ch == h, xb, jnp.zeros_like(xb))
                               for h in range(HEADS_PER_GROUP)], axis=0)
        y = jnp.dot(jnp.concatenate(ms, axis=1), jnp.concatenate([xbd, xbd], axis=0),
                    preferred_element_type=F32) + x * dskip_ref[0]

        xw = jnp.concatenate([x * ew[:, GROUP_CH:2 * GROUP_CH], x * ew[:, 2 * GROUP_CH:]], axis=1)
        upd = lax.dot_general(bk, xw.astype(BF16), (((0,), (0,)), ((), ())), preferred_element_type=F32)

        state = st_scr[0]
        y = y + jnp.dot(ck, state.astype(BF16), preferred_element_type=F32) * ew[:, :GROUP_CH]
        st_scr[0] = state * dec_ref[0, 0, c, 0:1, :] + upd[:, :GROUP_CH]
        y_scr[pl.ds(r0, L), :] = y
        upd_scr[c] = upd[:, GROUP_CH:]
        return carry

    lax.fori_loop(0, nc, local_body, 0, unroll=2)

    def bwd_body(i, carry):
        c = nc - 1 - i
        r0 = pl.multiple_of(c * L, L)
        ck = act_scr[3, pl.ds(r0, L), :].astype(BF16)
        e_bwd = jnp.dot(ew_ref[0, pl.ds(r0, L), :], sel_ref[:, 3 * GROUP_CH:], preferred_element_type=F32)
        state = st_scr[1]
        y = y_scr[pl.ds(r0, L), :] + jnp.dot(ck, state.astype(BF16), preferred_element_type=F32) * e_bwd
        st_scr[1] = state * dec_ref[0, 0, c, 1:2, :] + upd_scr[c]
        yz = y * sz_ref[0, pl.ds(r0, L), :].astype(F32)
        ms = jnp.mean(yz * yz, axis=-1, keepdims=True)
        o_ref[0, pl.ds(r0, L), :] = (yz * lax.rsqrt(ms + RMS_EPS) * nw_ref[0]).astype(o_ref.dtype)
        return carry

    lax.fori_loop(0, nc, bwd_body, 0, unroll=2)


def _ssd(xbc, sz, cw, cb, rowt_hm, acs_tm, ew_tm, dec, dskip, nw):
    nb, s, _ = xbc.shape
    nc = s // SSD_CHUNK
    return pl.pallas_call(
        _ssd_kernel,
        grid=(nb, SSD_GROUPS),
        in_specs=[
            pl.BlockSpec((1, s, GROUP_CH), lambda b, g: (b, 0, g)),
            pl.BlockSpec((1, s, SSD_STATE), lambda b, g: (b, 0, SSD_D_INNER // SSD_STATE + g)),
            pl.BlockSpec((1, s, SSD_STATE), lambda b, g: (b, 0, SSD_D_INNER // SSD_STATE + SSD_GROUPS + g)),
            pl.BlockSpec((1, s, GROUP_CH), lambda b, g: (b, 0, g)),
            pl.BlockSpec((1, 8, XBC_W), lambda b, g: (g, 0, 0)),
            pl.BlockSpec((1, 1, XBC_W), lambda b, g: (g, 0, 0)),
            pl.BlockSpec((1, 1, nc, 8, SSD_CHUNK), lambda b, g: (b, g, 0, 0, 0)),
            pl.BlockSpec((1, s, 8), lambda b, g: (b * SSD_GROUPS + g, 0, 0)),
            pl.BlockSpec((1, s, EW_COLS), lambda b, g: (b * SSD_GROUPS + g, 0, 0)),
            pl.BlockSpec((EW_COLS, 4 * GROUP_CH), lambda b, g: (0, 0)),
            pl.BlockSpec((1, 1, nc, 2, GROUP_CH), lambda b, g: (b, g, 0, 0, 0)),
            pl.BlockSpec((1, 1, GROUP_CH), lambda b, g: (g, 0, 0)),
            pl.BlockSpec((1, 1, GROUP_CH), lambda b, g: (g, 0, 0)),
        ],
        out_specs=pl.BlockSpec((1, s, GROUP_CH), lambda b, g: (b, 0, g)),
        out_shape=jax.ShapeDtypeStruct((nb, s, SSD_D_INNER), BF16),
        scratch_shapes=[
            pltpu.VMEM((XBC_W // LANES, s + 2 * SSD_PAD, LANES), F32),
            pltpu.VMEM((XBC_W // LANES, s, LANES), F32),
            pltpu.VMEM((s, GROUP_CH), F32),
            pltpu.VMEM((nc, SSD_STATE, GROUP_CH), F32),
            pltpu.VMEM((2, SSD_STATE, GROUP_CH), F32),
        ],
        compiler_params=_cparams(("parallel", "parallel")),
        name="ssd_scan",
    )(xbc, xbc, xbc, sz, cw, cb, rowt_hm, acs_tm, ew_tm, _head_expand_matrix(), dec, dskip, nw)


D_SLABS = D_MODEL // LANES


def _conformer_kernel(u_ref, up_ref, un_ref, mb_ref, g_ref, x_ref, dww_ref, dwb_ref, lnw_ref, lnb_ref,
                      pw_ref, wo_ref, o_ref, u_scr, c_scr):
    i = pl.program_id(1)
    n = pl.num_programs(1)
    tq = u_ref.shape[1]
    for j in range(D_SLABS):
        cs = slice(j * LANES, (j + 1) * LANES)
        u_scr[j, pl.ds(0, HALO), :] = jnp.where(i > 0, up_ref[0, :, cs].astype(F32), 0.0)
        u_scr[j, pl.ds(HALO, tq), :] = u_ref[0, :, cs].astype(F32)
        u_scr[j, pl.ds(HALO + tq, HALO), :] = jnp.where(i < n - 1, un_ref[0, :, cs].astype(F32), 0.0)
    for j in range(D_SLABS):
        _depthwise_conv(u_scr, c_scr, j, dww_ref, dwb_ref, tq, CONV_WIDTH, HALO, lambda a: a)

    cv = jnp.concatenate([c_scr[j] for j in range(D_SLABS)], axis=1)
    mu = jnp.mean(cv, axis=-1, keepdims=True)
    xc = cv - mu
    var = jnp.mean(xc * xc, axis=-1, keepdims=True)
    y = xc * lax.rsqrt(var + LN_EPS) * lnw_ref[...] + lnb_ref[...]
    a = jnp.dot(_silu(y).astype(BF16), pw_ref[...], preferred_element_type=F32)
    merged = g_ref[0].astype(F32) * a + mb_ref[0].astype(F32)
    o_ref[0] = x_ref[0] + jnp.dot(merged.astype(BF16), wo_ref[...], preferred_element_type=F32)


def _conformer(u, mb, gates, x, dww, dwb, lnw, lnb, pw, wo, tq):
    nb, s, d = x.shape
    hb = tq // HALO
    nh = s // HALO
    wspec = pl.BlockSpec((d, d), lambda b, i: (0, 0))
    vspec = pl.BlockSpec((1, d), lambda b, i: (0, 0))
    return pl.pallas_call(
        _conformer_kernel,
        grid=(nb, s // tq),
        in_specs=[
            pl.BlockSpec((1, tq, d), lambda b, i: (b, i, 0)),
            pl.BlockSpec((1, HALO, d), lambda b, i: (b, jnp.maximum(i * hb - 1, 0), 0)),
            pl.BlockSpec((1, HALO, d), lambda b, i: (b, jnp.minimum((i + 1) * hb, nh - 1), 0)),
            pl.BlockSpec((1, tq, d), lambda b, i: (b, i, 0)),
            pl.BlockSpec((1, tq, d), lambda b, i: (b, i, 0)),
            pl.BlockSpec((1, tq, d), lambda b, i: (b, i, 0)),
            pl.BlockSpec((32, d), lambda b, i: (0, 0)),
            vspec, vspec, vspec, wspec, wspec,
        ],
        out_specs=pl.BlockSpec((1, tq, d), lambda b, i: (b, i, 0)),
        out_shape=jax.ShapeDtypeStruct((nb, s, d), F32),
        scratch_shapes=[pltpu.VMEM((D_SLABS, tq + 2 * HALO, LANES), F32), pltpu.VMEM((D_SLABS, tq, LANES), F32)],
        compiler_params=_cparams(("parallel", "parallel")),
        name="conformer_merge",
    )(u, u, u, mb, gates, x, dww, dwb, lnw, lnb, pw, wo)


def _attn_kernel(x_ref, k_ref, v_ref, nw_ref, wq_ref, wo_ref, nf_ref, wr_ref, br_ref, x2_ref, h3_ref, lg_ref):
    x = x_ref[0]
    h = _rms(x, nw_ref[...]).astype(BF16)
    q = jnp.dot(h, wq_ref[...], preferred_element_type=F32)
    scale = 1.0 / math.sqrt(XATTN_HEAD_DIM)
    outs = []
    for hd in range(XATTN_HEADS):
        cs = slice(hd * XATTN_HEAD_DIM, (hd + 1) * XATTN_HEAD_DIM)
        sc = lax.dot_general(q[:, cs].astype(BF16), k_ref[0, :, cs], (((1,), (1,)), ((), ())),
                             preferred_element_type=F32) * scale
        m = jnp.max(sc, axis=-1, keepdims=True)
        e = jnp.exp(sc - m)
        p = e / jnp.sum(e, axis=-1, keepdims=True)
        outs.append(jnp.dot(p.astype(BF16), v_ref[0, :, cs], preferred_element_type=F32))
    o = jnp.concatenate(outs, axis=1).astype(BF16)
    x2 = x + jnp.dot(o, wo_ref[...], preferred_element_type=F32)
    x2_ref[0] = x2
    h3 = _rms(x2, nf_ref[...])
    h3_ref[0] = h3
    lg_ref[0] = jnp.dot(h3.astype(BF16), wr_ref[...], preferred_element_type=F32) + br_ref[...]


def _attention(x, kv, nw, wq, wo, nf, wr, br, tq):
    nb, s, d = x.shape
    m = kv.shape[1]
    wspec = pl.BlockSpec((d, d), lambda b, i: (0, 0))
    vspec = pl.BlockSpec((1, d), lambda b, i: (0, 0))
    return pl.pallas_call(
        _attn_kernel,
        grid=(nb, s // tq),
        in_specs=[
            pl.BlockSpec((1, tq, d), lambda b, i: (b, i, 0)),
            pl.BlockSpec((1, m, d), lambda b, i: (b, 0, 0)),
            pl.BlockSpec((1, m, d), lambda b, i: (b, 0, 1)),
            vspec, wspec, wspec, vspec,
            pl.BlockSpec((d, LANES), lambda b, i: (0, 0)),
            pl.BlockSpec((1, LANES), lambda b, i: (0, 0)),
        ],
        out_specs=[
            pl.BlockSpec((1, tq, d), lambda b, i: (b, i, 0)),
            pl.BlockSpec((1, tq, d), lambda b, i: (b, i, 0)),
            pl.BlockSpec((1, tq, LANES), lambda b, i: (b, i, 0)),
        ],
        out_shape=[
            jax.ShapeDtypeStruct((nb, s, d), F32),
            jax.ShapeDtypeStruct((nb, s, d), F32),
            jax.ShapeDtypeStruct((nb, s, LANES), F32),
        ],
        compiler_params=_cparams(("parallel", "parallel")),
        name="cross_attn_router",
    )(x, kv, kv, nw, wq, wo, nf, wr, br)


MOE_ROWS = 256


def _route_math(lg, lane):
    lane_f = lane.astype(F32)
    big = float(LANES)
    neg = -jnp.inf

    is_g = lane < N_GROUPS
    gl = jnp.where(is_g, lg, neg)
    gmax = jnp.max(gl, axis=-1, keepdims=True)
    g_idx = jnp.min(jnp.where(gl == gmax, lane_f, big), axis=-1, keepdims=True)
    g_sum = jnp.sum(jnp.where(is_g, jnp.exp(gl - gmax), 0.0), axis=-1, keepdims=True)
    g_w = 1.0 / g_sum

    e_grp = lax.shift_right_arithmetic(lane - N_GROUPS, 3).astype(F32)
    is_e = (lane >= N_GROUPS) & (lane < N_GROUPS + N_EXPERTS) & (e_grp == g_idx)
    el = jnp.where(is_e, lg, neg)
    emax = jnp.max(el, axis=-1, keepdims=True)
    ee = jnp.where(is_e, jnp.exp(el - emax), 0.0)
    ep = ee / jnp.sum(ee, axis=-1, keepdims=True)
    ep = jnp.where(is_e, ep, -1.0)
    v1 = jnp.max(ep, axis=-1, keepdims=True)
    i1 = jnp.min(jnp.where(ep == v1, lane_f, big), axis=-1, keepdims=True)
    ep2 = jnp.where(lane_f == i1, -1.0, ep)
    v2 = jnp.max(ep2, axis=-1, keepdims=True)
    i2 = jnp.min(jnp.where(ep2 == v2, lane_f, big), axis=-1, keepdims=True)
    tot = v1 + v2
    gate1 = g_w * (v1 / tot)
    gate2 = g_w * (v2 / tot)
    return i1 - N_GROUPS, i2 - N_GROUPS, gate1, gate2


def _plan_kernel(lg_ref, gate_ref, pos_ref, cnt_ref, carry_scr, base_scr):
    p = pl.program_id(0)
    i = pl.program_id(1)
    tm = lg_ref.shape[0]
    lane = lax.broadcasted_iota(jnp.int32, (tm, LANES), 1)
    lane_f = lane.astype(F32)
    e1, e2, gate1, gate2 = _route_math(lg_ref[...], lane)
    oh1 = lane_f == e1
    oh2 = lane_f == e2
    m = jnp.where(oh1 | oh2, 1.0, 0.0)
    colsum = jnp.sum(m, axis=0, keepdims=True)

    @pl.when((p == 0) & (i == 0))
    def _():
        carry_scr[...] = jnp.zeros_like(carry_scr)

    @pl.when(p == 0)
    def _():
        carry_scr[...] += colsum

    @pl.when((p == 1) & (i == 0))
    def _():
        counts = carry_scr[...]
        cnt_ref[...] = counts
        shift = MOE_ROWS.bit_length() - 1
        blocks = lax.shift_right_logical(counts.astype(jnp.int32) + (MOE_ROWS - 1), shift)
        padded = (blocks * MOE_ROWS).astype(F32)
        lane8 = lax.broadcasted_iota(jnp.int32, padded.shape, 1)
        inc = padded
        sh = 1
        while sh < LANES:
            inc = inc + jnp.where(lane8 >= sh, pltpu.roll(inc, sh, 1), 0.0)
            sh *= 2
        base_scr[...] = inc - padded
        carry_scr[...] = jnp.zeros_like(carry_scr)

    @pl.when(p == 1)
    def _():
        r = lax.broadcasted_iota(jnp.int32, (tm, tm), 0)
        c = lax.broadcasted_iota(jnp.int32, (tm, tm), 1)
        earlier = jnp.where(r > c, 1.0, 0.0).astype(BF16)
        tot = (jnp.dot(earlier, m.astype(BF16), preferred_element_type=F32)
               + base_scr[0:1, :] + carry_scr[0:1, :])
        pos1 = jnp.sum(jnp.where(oh1, tot, 0.0), axis=1, keepdims=True)
        pos2 = jnp.sum(jnp.where(oh2, tot, 0.0), axis=1, keepdims=True)
        pos_ref[...] = jnp.where(lane == 0, pos1, jnp.where(lane == 1, pos2, 0.0)).astype(jnp.int32)
        gate_ref[...] = jnp.where(lane == 0, gate1, jnp.where(lane == 1, gate2, 0.0))
        carry_scr[...] += colsum


def _plan(lg, tm):
    t = lg.shape[0]
    out_spec = pl.BlockSpec((tm, LANES), lambda p, i: (i * p, 0))
    return pl.pallas_call(
        _plan_kernel,
        grid=(2, t // tm),
        in_specs=[pl.BlockSpec((tm, LANES), lambda p, i: (i, 0))],
        out_specs=[out_spec, out_spec, pl.BlockSpec((8, LANES), lambda p, i: (0, 0))],
        out_shape=[jax.ShapeDtypeStruct((t, LANES), F32), jax.ShapeDtypeStruct((t, LANES), jnp.int32),
                   jax.ShapeDtypeStruct((8, LANES), F32)],
        scratch_shapes=[pltpu.VMEM((8, LANES), F32), pltpu.VMEM((8, LANES), F32)],
        compiler_params=_cparams(("arbitrary", "arbitrary")),
        name="moe_plan",
    )(lg)


def _dispatch_kernel(last_ref, pos_ref, h_ref, xs_hbm, zero_scr, sem):
    td = h_ref.shape[0]

    @pl.when(pl.program_id(0) == 0)
    def _():
        zero_scr[...] = jnp.zeros_like(zero_scr)

        def block_copy(e):
            return pltpu.make_async_copy(zero_scr, xs_hbm.at[pl.ds(last_ref[e] * MOE_ROWS, MOE_ROWS)], sem)

        def fill(e, carry):
            @pl.when(last_ref[e] >= 0)
            def _():
                block_copy(e).start()
            return carry

        def drain(e, carry):
            @pl.when(last_ref[e] >= 0)
            def _():
                block_copy(e).wait()
            return carry

        lax.fori_loop(0, last_ref.shape[0], fill, 0)
        lax.fori_loop(0, last_ref.shape[0], drain, 0)

    def body(r, carry):
        src = h_ref.at[pl.ds(r, 1)]
        pltpu.make_async_copy(src, xs_hbm.at[pl.ds(pos_ref[0, 0, r], 1)], sem).start()
        pltpu.make_async_copy(src, xs_hbm.at[pl.ds(pos_ref[0, 0, td + r], 1)], sem).start()
        return carry

    lax.fori_loop(0, td, body, 0, unroll=8)
    for _ in range(2):
        pltpu.make_async_copy(h_ref, xs_hbm.at[pl.ds(0, td)], sem).wait()


def _dispatch(last_blk, pos_blk, h3, n_blocks):
    n_steps = pos_blk.shape[0]
    td = pos_blk.shape[2] // 2
    d = h3.shape[1]
    grid_spec = pltpu.PrefetchScalarGridSpec(
        num_scalar_prefetch=1,
        grid=(n_steps,),
        in_specs=[
            pl.BlockSpec((1, 1, 2 * td), lambda i, last: (i, 0, 0), memory_space=pltpu.SMEM),
            pl.BlockSpec((td, d), lambda i, last: (i, 0)),
        ],
        out_specs=pl.BlockSpec(memory_space=pl.ANY),
        scratch_shapes=[pltpu.VMEM((MOE_ROWS, d), F32), pltpu.SemaphoreType.DMA(())],
    )
    return pl.pallas_call(
        _dispatch_kernel,
        grid_spec=grid_spec,
        out_shape=jax.ShapeDtypeStruct((n_blocks * MOE_ROWS, d), F32),
        compiler_params=_cparams(("arbitrary",)),
        name="moe_dispatch",
    )(last_blk, pos_blk, h3)


def _expert_kernel(be_ref, na_ref, x_ref, wg_ref, wu_ref, wd_ref, o_ref, wg_scr, wu_scr, wd_scr):
    j = pl.program_id(0)
    prev = be_ref[jnp.maximum(j - 1, 0)]

    @pl.when((j == 0) | (be_ref[j] != prev))
    def _():
        wg_scr[...] = wg_ref[0].astype(BF16)
        wu_scr[...] = wu_ref[0].astype(BF16)
        wd_scr[...] = wd_ref[0].astype(BF16)

    @pl.when(j < na_ref[0])
    def _():
        x = x_ref[...].astype(BF16)
        g = jnp.dot(x, wg_scr[...], preferred_element_type=F32)
        u = jnp.dot(x, wu_scr[...], preferred_element_type=F32)
        hid = (_silu(g) * u).astype(BF16)
        o_ref[...] = jnp.dot(hid, wd_scr[...], preferred_element_type=F32)

    @pl.when(j >= na_ref[0])
    def _():
        o_ref[...] = jnp.zeros_like(o_ref)


def _experts(blk_exp, n_active, xs, wg, wu, wd):
    n_blocks = blk_exp.shape[0]
    d = xs.shape[1]
    ff = wg.shape[2]
    grid_spec = pltpu.PrefetchScalarGridSpec(
        num_scalar_prefetch=2,
        grid=(n_blocks,),
        in_specs=[
            pl.BlockSpec((MOE_ROWS, d), lambda j, be, na: (jnp.minimum(j, na[0] - 1), 0)),
            pl.BlockSpec((1, d, ff), lambda j, be, na: (be[j], 0, 0)),
            pl.BlockSpec((1, d, ff), lambda j, be, na: (be[j], 0, 0)),
            pl.BlockSpec((1, ff, d), lambda j, be, na: (be[j], 0, 0)),
        ],
        out_specs=pl.BlockSpec((MOE_ROWS, d), lambda j, be, na: (j, 0)),
        scratch_shapes=[pltpu.VMEM((d, ff), BF16), pltpu.VMEM((d, ff), BF16), pltpu.VMEM((ff, d), BF16)],
    )
    return pl.pallas_call(
        _expert_kernel,
        grid_spec=grid_spec,
        out_shape=jax.ShapeDtypeStruct((n_blocks * MOE_ROWS, d), F32),
        compiler_params=_cparams(("arbitrary",)),
        name="moe_experts",
    )(blk_exp, n_active, xs, wg, wu, wd)


def _combine_kernel(pos_ref, x_ref, gate_ref, nw_ref, ys_hbm, o_ref, y_scr, sem):
    tm = x_ref.shape[0]

    def body(r, carry):
        pltpu.make_async_copy(ys_hbm.at[pl.ds(pos_ref[0, 0, r], 1)], y_scr.at[pl.ds(r, 1)], sem).start()
        return carry

    lax.fori_loop(0, 2 * tm, body, 0, unroll=8)
    pltpu.make_async_copy(ys_hbm.at[pl.ds(0, 2 * tm)], y_scr, sem).wait()
    gate = gate_ref[...]
    y = x_ref[...] + (gate[:, 0:1] * y_scr[pl.ds(0, tm), :] + gate[:, 1:2] * y_scr[pl.ds(tm, tm), :])
    o_ref[...] = _rms(y, nw_ref[...])


def _combine(pos_blk, x2, gates, nw, ys, tm):
    t, d = x2.shape
    return pl.pallas_call(
        _combine_kernel,
        grid=(t // tm,),
        in_specs=[
            pl.BlockSpec((1, 1, 2 * tm), lambda i: (i, 0, 0), memory_space=pltpu.SMEM),
            pl.BlockSpec((tm, d), lambda i: (i, 0)),
            pl.BlockSpec((tm, LANES), lambda i: (i, 0)),
            pl.BlockSpec((1, d), lambda i: (0, 0)),
            pl.BlockSpec(memory_space=pl.ANY),
        ],
        out_specs=pl.BlockSpec((tm, d), lambda i: (i, 0)),
        out_shape=jax.ShapeDtypeStruct((t, d), F32),
        scratch_shapes=[pltpu.VMEM((2 * tm, d), F32), pltpu.SemaphoreType.DMA(())],
        compiler_params=_cparams(("arbitrary",)),
        name="moe_combine",
    )(pos_blk, x2, gates, nw, ys)


def _block_experts(counts, n_blocks):
    per_expert = (counts + MOE_ROWS - 1) // MOE_ROWS
    ends = jnp.cumsum(per_expert)
    blk = jnp.arange(n_blocks, dtype=jnp.int32)
    blk_exp = jnp.minimum(jnp.sum(blk[:, None] >= ends[None, :], axis=1), N_EXPERTS - 1).astype(jnp.int32)
    last_blk = jnp.where(per_expert > 0, ends - 1, -1)
    tail = n_blocks - 1 - jnp.arange(N_EXPERTS)
    to_zero = jnp.concatenate([last_blk, jnp.where(tail >= ends[-1], tail, -1)]).astype(jnp.int32)
    return blk_exp, ends[-1:].astype(jnp.int32), to_zero


def _pick(n, pref):
    t = pref
    while n % t:
        t //= 2
    return t


def _encoder(x, mem, p):
    nb, s, d = x.shape
    t = nb * s
    nc = s // SSD_CHUNK
    tm = _pick(t, 1024)
    tq = _pick(s, 512)
    xf = x.reshape(t, d)

    perm = jnp.arange(SSD_HEADS).reshape(SSD_GROUPS, HEADS_PER_GROUP)
    perm = jnp.concatenate([perm, perm + SSD_HEADS], axis=1).reshape(-1)

    w_in = p['w_in']
    o_z = 2 * D_MODEL
    o_x = o_z + SSD_D_INNER
    o_dt = o_x + SSD_D_INNER + 2 * SSD_GROUPS * SSD_STATE
    o_g = o_dt + 2 * SSD_HEADS
    w_val = w_in[:, :D_MODEL].astype(BF16)
    w_gate = w_in[:, D_MODEL:o_z].astype(BF16)
    w_z = w_in[:, o_z:o_x].astype(BF16)
    w_xbc = w_in[:, o_x:o_dt].astype(BF16)
    w_dt = jnp.pad(w_in[:, o_dt:o_g][:, perm], ((0, 0), (0, LANES - 2 * SSD_HEADS))).astype(BF16)
    w_g = w_in[:, o_g:].astype(BF16)
    nmix = p['norm_mix_w'].reshape(1, d)

    u = _glu_proj(xf, nmix, w_val, w_gate, tm, PROJ_TN // 2)
    sz = _norm_proj(xf, nmix, w_z, _silu, BF16, tm, PROJ_TN, "z_proj")
    xbc = _norm_proj(xf, nmix, w_xbc, lambda a: a, BF16, tm, PROJ_TN, "xbc_proj")
    dt_raw = _norm_proj(xf, nmix, w_dt, lambda a: a, F32, tm, LANES, "dt_proj")
    gates = _norm_proj(xf, nmix, w_g, _sigmoid, BF16, tm, PROJ_TN, "gate_proj")

    pad = LANES - 2 * SSD_HEADS
    dt_bias = jnp.pad(jnp.concatenate([p['ssd_dt_bias_f'], p['ssd_dt_bias_b']])[perm], (0, pad)).reshape(1, LANES)
    a_log = jnp.pad(jnp.concatenate([p['ssd_a_log_f'], p['ssd_a_log_b']])[perm], (0, pad)).reshape(1, LANES)
    acs2, rowt, e_hi, e_lo, w_hi, w_lo, dec = _dtprep(dt_raw, dt_bias, a_log, tm)

    def per_group_tm(a):
        return a[:, :2 * SSD_HEADS].reshape(nb, s, SSD_GROUPS, 8).transpose(0, 2, 1, 3)

    def per_group_hm(a):
        return a[:, :2 * SSD_HEADS].reshape(nb, nc, SSD_CHUNK, SSD_GROUPS, 8).transpose(0, 3, 1, 4, 2)

    acs_tm = per_group_tm(acs2).reshape(nb * SSD_GROUPS, s, 8)
    ew_tm = jnp.concatenate([per_group_tm(v) for v in (e_hi, e_lo, w_hi, w_lo)], axis=-1)
    ew_tm = ew_tm.reshape(nb * SSD_GROUPS, s, EW_COLS)
    rowt_hm = per_group_hm(rowt)
    dec = dec[:, :2 * SSD_HEADS].reshape(nb, nc, SSD_GROUPS, 2, HEADS_PER_GROUP).transpose(0, 2, 1, 3, 4)
    dec = jnp.repeat(dec, SSD_HEAD_DIM, axis=-1)

    cwx = p['ssd_conv_w']
    cbx = p['ssd_conv_b']

    def per_group_conv(a):
        xs_ = a[:, :SSD_D_INNER].reshape(-1, SSD_GROUPS, GROUP_CH)
        b_ = a[:, SSD_D_INNER:SSD_D_INNER + SSD_GROUPS * SSD_STATE].reshape(-1, SSD_GROUPS, SSD_STATE)
        c_ = a[:, SSD_D_INNER + SSD_GROUPS * SSD_STATE:].reshape(-1, SSD_GROUPS, SSD_STATE)
        return jnp.concatenate([xs_, b_, c_], axis=-1).transpose(1, 0, 2)

    cw = jnp.pad(per_group_conv(cwx), ((0, 0), (0, 8 - SSD_CONV_WIDTH), (0, 0)))
    cb = per_group_conv(cbx.reshape(1, -1))
    dskip = jnp.repeat(p['ssd_d'], SSD_HEAD_DIM).reshape(SSD_GROUPS, 1, GROUP_CH)
    nw_ssd = p['ssd_norm_w'].reshape(SSD_GROUPS, 1, GROUP_CH)

    yn = _ssd(xbc.reshape(nb, s, -1), sz.reshape(nb, s, -1), cw, cb, rowt_hm, acs_tm, ew_tm, dec, dskip, nw_ssd)
    mb = _gated_proj(yn.reshape(t, SSD_D_INNER), p['ssd_out'].astype(BF16), gates, 1, tm)

    dww = jnp.pad(p['conv_dw_w'], ((0, 32 - CONV_WIDTH), (0, 0)))
    x1 = _conformer(u.reshape(nb, s, d), mb.reshape(nb, s, d), gates.reshape(nb, s, 2 * d), x, dww,
                    p['conv_dw_b'].reshape(1, d), p['conv_ln_w'].reshape(1, d), p['conv_ln_b'].reshape(1, d),
                    p['conv_pw_out'].astype(BF16), p['w_out'].astype(BF16), tq)

    m = mem.shape[1]
    kv = _norm_proj(mem.reshape(nb * m, d), p['norm_mem_w'].reshape(1, d), p['xattn_wkv'].astype(BF16),
                    lambda a: a, BF16, _pick(nb * m, 512), 1024, "kv_proj")
    n_r = N_GROUPS + N_EXPERTS
    wr = jnp.pad(jnp.concatenate([p['router_group_w'], p['router_expert_w']], axis=1),
                 ((0, 0), (0, LANES - n_r))).astype(BF16)
    br = jnp.pad(jnp.concatenate([p['router_group_b'], p['router_expert_b']]), (0, LANES - n_r)).reshape(1, LANES)
    x2, h3, logits = _attention(x1, kv.reshape(nb, m, 2 * d), p['norm_xattn_w'].reshape(1, d),
                                p['xattn_wq'].astype(BF16), p['xattn_wo'].astype(BF16),
                                p['norm_ffn_w'].reshape(1, d), wr, br, tq)

    tc = _pick(t, 512)
    gate, pos, cnt = _plan(logits.reshape(t, LANES), tc)
    n_blocks = (2 * t) // MOE_ROWS + N_EXPERTS
    blk_exp, n_active, last_blk = _block_experts(cnt[0, :N_EXPERTS].astype(jnp.int32), n_blocks)
    pos_blk = pos[:, :2].reshape(t // tc, tc, 2).transpose(0, 2, 1).reshape(t // tc, 1, 2 * tc)
    xs = _dispatch(last_blk, pos_blk, h3.reshape(t, d), n_blocks)
    ys = _experts(blk_exp, n_active, xs, p['expert_w_gate'], p['expert_w_up'], p['expert_w_down'])
    out = _combine(pos_blk, x2.reshape(t, d), gate, p['norm_final_w'].reshape(1, d), ys, tc)
    return out.reshape(nb, s, d)


def kernel(x_prompt, x_sample, mem_prompt, mem_sample, norm_mix_w, w_in, conv_dw_w, conv_dw_b, conv_ln_w, conv_ln_b, conv_pw_out, ssd_conv_w, ssd_conv_b, ssd_dt_bias_f, ssd_dt_bias_b, ssd_a_log_f, ssd_a_log_b, ssd_d, ssd_norm_w, ssd_out, w_out, norm_xattn_w, norm_mem_w, xattn_wq, xattn_wkv, xattn_wo, norm_ffn_w, router_group_w, router_group_b, router_expert_w, router_expert_b, expert_w_gate, expert_w_up, expert_w_down, norm_final_w):
    p = {
        'norm_mix_w': norm_mix_w[0], 'w_in': w_in[0], 'conv_dw_w': conv_dw_w[0], 'conv_dw_b': conv_dw_b[0],
        'conv_ln_w': conv_ln_w[0], 'conv_ln_b': conv_ln_b[0], 'conv_pw_out': conv_pw_out[0],
        'ssd_conv_w': ssd_conv_w[0], 'ssd_conv_b': ssd_conv_b[0], 'ssd_dt_bias_f': ssd_dt_bias_f[0],
        'ssd_dt_bias_b': ssd_dt_bias_b[0], 'ssd_a_log_f': ssd_a_log_f[0], 'ssd_a_log_b': ssd_a_log_b[0],
        'ssd_d': ssd_d[0], 'ssd_norm_w': ssd_norm_w[0], 'ssd_out': ssd_out[0], 'w_out': w_out[0],
        'norm_xattn_w': norm_xattn_w[0], 'norm_mem_w': norm_mem_w[0], 'xattn_wq': xattn_wq[0],
        'xattn_wkv': xattn_wkv[0], 'xattn_wo': xattn_wo[0], 'norm_ffn_w': norm_ffn_w[0],
        'router_group_w': router_group_w[0], 'router_group_b': router_group_b[0],
        'router_expert_w': router_expert_w[0], 'router_expert_b': router_expert_b[0],
        'expert_w_gate': expert_w_gate[0], 'expert_w_up': expert_w_up[0], 'expert_w_down': expert_w_down[0],
        'norm_final_w': norm_final_w,
    }
    n_prompt = x_prompt.shape[0]
    x = jnp.concatenate([x_prompt, x_sample], axis=0)
    mem = jnp.concatenate([mem_prompt, mem_sample], axis=0)
    y = _encoder(x, mem, p)
    return (y[:n_prompt], y[n_prompt:])
```

```python
import functools
import math

import jax
import jax.numpy as jnp
from jax import lax
from jax.experimental import pallas as pl
from jax.experimental.pallas import tpu as pltpu

F32 = jnp.float32
BF16 = jnp.bfloat16

D_MODEL = 1024
CONV_WIDTH = 31
SSD_D_INNER = 2048
SSD_HEAD_DIM = 64
SSD_HEADS = 32
SSD_GROUPS = 8
SSD_STATE = 128
SSD_CONV_WIDTH = 5
SSD_CHUNK = 128
GROUP_CH = SSD_D_INNER // SSD_GROUPS
HEADS_PER_GROUP = SSD_HEADS // SSD_GROUPS
XATTN_HEADS = 4
XATTN_HEAD_DIM = D_MODEL // XATTN_HEADS
N_GROUPS = 8
EXPERTS_PER_GROUP = 8
N_EXPERTS = 64
EXPERT_FF = 512
MOE_BLOCK = 128
RMS_EPS = 1e-6
LN_EPS = 1e-5

LANES = 128
HALO = 16
VMEM_LIMIT = 56 * 1024 * 1024
PROJ_TN = 2048


def _cparams(sem):
    return pltpu.CompilerParams(dimension_semantics=sem, vmem_limit_bytes=VMEM_LIMIT)


def _rms(x, w):
    ms = jnp.mean(x * x, axis=-1, keepdims=True)
    return x * lax.rsqrt(ms + RMS_EPS) * w


def _sigmoid(x):
    return 1.0 / (1.0 + jnp.exp(-x))


def _silu(x):
    return x * _sigmoid(x)


def _softplus(x):
    return jnp.maximum(x, 0.0) + jnp.log1p(jnp.exp(-jnp.abs(x)))


CONV_STRIDE = 4
CONV_ROWS = 8 * CONV_STRIDE
CONV_UNROLL = 4


def _depthwise_conv(src, dst, slab, w_ref, b_ref, n_rows, width, first_row, epilogue):
    lanes = pl.ds(slab * LANES, LANES)
    taps = [jnp.broadcast_to(w_ref[k:k + 1, lanes], (8, LANES)) for k in range(width)]
    bias = jnp.broadcast_to(b_ref[:, lanes], (8, LANES))
    step = CONV_ROWS * CONV_UNROLL

    def body(r, carry):
        _conv_rows(src, dst, slab, taps, bias, r * step, step, width, first_row, epilogue)
        return carry

    lax.fori_loop(0, n_rows // step, body, 0)


def _conv_rows(src, dst, slab, taps, bias, base, n_rows, width, first_row, epilogue):
    for t0 in range(n_rows // 8):
        row = base + (t0 // CONV_STRIDE) * CONV_ROWS + t0 % CONV_STRIDE
        acc = bias
        for k in range(width):
            acc = acc + src[slab, pl.ds(row + first_row - width // 2 + k, 8, stride=CONV_STRIDE), :] * taps[k]
        dst[slab, pl.ds(row, 8, stride=CONV_STRIDE), :] = epilogue(acc)


def _glu_kernel(x_ref, nw_ref, wv_ref, wg_ref, o_ref, h_scr):
    @pl.when(pl.program_id(1) == 0)
    def _():
        h_scr[...] = _rms(x_ref[...], nw_ref[...]).astype(BF16)

    h = h_scr[...]
    v = jnp.dot(h, wv_ref[...], preferred_element_type=F32)
    g = jnp.dot(h, wg_ref[...], preferred_element_type=F32)
    o_ref[...] = (v * _sigmoid(g)).astype(o_ref.dtype)


def _glu_proj(x, nw, wv, wg, tm, tn):
    t, d = x.shape
    n = wv.shape[1]
    return pl.pallas_call(
        _glu_kernel,
        grid=(t // tm, n // tn),
        in_specs=[
            pl.BlockSpec((tm, d), lambda i, j: (i, 0)),
            pl.BlockSpec((1, d), lambda i, j: (0, 0)),
            pl.BlockSpec((d, tn), lambda i, j: (0, j)),
            pl.BlockSpec((d, tn), lambda i, j: (0, j)),
        ],
        out_specs=pl.BlockSpec((tm, tn), lambda i, j: (i, j)),
        out_shape=jax.ShapeDtypeStruct((t, n), BF16),
        scratch_shapes=[pltpu.VMEM((tm, d), BF16)],
        compiler_params=_cparams(("parallel", "arbitrary")),
        name="glu_proj",
    )(x, nw, wv, wg)


def _norm_proj_kernel(epi, x_ref, nw_ref, w_ref, o_ref, h_scr):
    @pl.when(pl.program_id(1) == 0)
    def _():
        h_scr[...] = _rms(x_ref[...], nw_ref[...]).astype(BF16)

    acc = jnp.dot(h_scr[...], w_ref[...], preferred_element_type=F32)
    o_ref[...] = epi(acc).astype(o_ref.dtype)


def _norm_proj(x, nw, w, epi, out_dtype, tm, tn, name):
    t, d = x.shape
    n = w.shape[1]
    return pl.pallas_call(
        functools.partial(_norm_proj_kernel, epi),
        grid=(t // tm, n // tn),
        in_specs=[
            pl.BlockSpec((tm, d), lambda i, j: (i, 0)),
            pl.BlockSpec((1, d), lambda i, j: (0, 0)),
            pl.BlockSpec((d, tn), lambda i, j: (0, j)),
        ],
        out_specs=pl.BlockSpec((tm, tn), lambda i, j: (i, j)),
        out_shape=jax.ShapeDtypeStruct((t, n), out_dtype),
        scratch_shapes=[pltpu.VMEM((tm, d), BF16)],
        compiler_params=_cparams(("parallel", "arbitrary")),
        name=name,
    )(x, nw, w)


def _gated_proj_kernel(y_ref, w_ref, g_ref, o_ref):
    acc = jnp.dot(y_ref[...], w_ref[...], preferred_element_type=F32)
    o_ref[...] = (acc * g_ref[...].astype(F32)).astype(o_ref.dtype)


def _gated_proj(y, w, g, g_col_block, tm):
    t, k = y.shape
    n = w.shape[1]
    return pl.pallas_call(
        _gated_proj_kernel,
        grid=(t // tm,),
        in_specs=[
            pl.BlockSpec((tm, k), lambda i: (i, 0)),
            pl.BlockSpec((k, n), lambda i: (0, 0)),
            pl.BlockSpec((tm, n), lambda i: (i, g_col_block)),
        ],
        out_specs=pl.BlockSpec((tm, n), lambda i: (i, 0)),
        out_shape=jax.ShapeDtypeStruct((t, n), BF16),
        compiler_params=_cparams(("parallel",)),
        name="ssd_out_proj",
    )(y, w, g)


LOG2E = 1.4426950408889634


def _split_bf16(v):
    hi = v.astype(BF16)
    return hi, (v - hi.astype(F32)).astype(BF16)


def _dtprep_kernel(raw_ref, bias_ref, alog_ref, acs2_ref, rowt_ref, ehi_ref, elo_ref, whi_ref, wlo_ref, dec_ref):
    rows = raw_ref.shape[0]
    a_head = -jnp.exp(alog_ref[...])
    row = lax.broadcasted_iota(jnp.int32, (SSD_CHUNK, LANES), 0)
    lane = lax.broadcasted_iota(jnp.int32, (SSD_CHUNK, LANES), 1)
    is_bwd = (lane % (2 * HEADS_PER_GROUP)) >= HEADS_PER_GROUP
    for c in range(rows // SSD_CHUNK):
        sl = pl.ds(c * SSD_CHUNK, SSD_CHUNK)
        dt = _softplus(raw_ref[sl, :] + bias_ref[...])
        a = dt * a_head
        fwd = a
        bwd = a
        sh = 1
        while sh < SSD_CHUNK:
            fwd = fwd + jnp.where(row >= sh, pltpu.roll(fwd, sh, 0), 0.0)
            bwd = bwd + jnp.where(row < SSD_CHUNK - sh, pltpu.roll(bwd, SSD_CHUNK - sh, 0), 0.0)
            sh *= 2
        acs = jnp.where(is_bwd, bwd, fwd)
        a_end = jnp.where(is_bwd[0:1, :], bwd[0:1, :], fwd[SSD_CHUNK - 1:SSD_CHUNK, :])
        acs2 = acs * LOG2E
        acs2_ref[sl, :] = acs2
        rowt_ref[sl, :] = acs2 - jnp.log(dt) * LOG2E
        ehi_ref[sl, :], elo_ref[sl, :] = _split_bf16(jnp.exp(acs))
        whi_ref[sl, :], wlo_ref[sl, :] = _split_bf16(jnp.exp(a_end - acs) * dt)
        dec_ref[c:c + 1, :] = jnp.exp(a_end)


def _dtprep(raw, bias, alog, tm):
    t = raw.shape[0]
    spec = pl.BlockSpec((tm, LANES), lambda i: (i, 0))
    vec = pl.BlockSpec((1, LANES), lambda i: (0, 0))
    return pl.pallas_call(
        _dtprep_kernel,
        grid=(t // tm,),
        in_specs=[spec, vec, vec],
        out_specs=[spec] * 6 + [pl.BlockSpec((tm // SSD_CHUNK, LANES), lambda i: (i, 0))],
        out_shape=[jax.ShapeDtypeStruct((t, LANES), F32)] * 2 + [jax.ShapeDtypeStruct((t, LANES), BF16)] * 4
        + [jax.ShapeDtypeStruct((t // SSD_CHUNK, LANES), F32)],
        compiler_params=_cparams(("parallel",)),
        name="dt_prep",
    )(raw, bias, alog)


SSD_PAD = 8
XBC_W = GROUP_CH + 2 * SSD_STATE


EW_COLS = 4 * 2 * HEADS_PER_GROUP


def _head_expand_matrix():
    col = jnp.arange(4 * GROUP_CH)
    part = col // GROUP_CH
    is_w = (part == 1) | (part == 2)
    is_bwd = part >= 2
    head = is_bwd * HEADS_PER_GROUP + (col % GROUP_CH) // SSD_HEAD_DIM
    hi_row = is_w * 16 + head
    row = jnp.arange(EW_COLS)[:, None]
    return ((row == hi_row[None, :]) | (row == hi_row[None, :] + 8)).astype(BF16)


def _ssd_kernel(xs_ref, b_ref, c_ref, sz_ref, cw_ref, cb_ref, rowt_ref, acs_ref, ew_ref, sel_ref, dec_ref,
                dskip_ref, nw_ref, o_ref, raw_scr, act_scr, y_scr, upd_scr, st_scr):
    s = xs_ref.shape[1]
    nc = s // SSD_CHUNK
    L = SSD_CHUNK

    n_slabs = XBC_W // LANES
    for j in range(n_slabs):
        raw_scr[j, pl.ds(0, SSD_PAD), :] = jnp.zeros((SSD_PAD, LANES), F32)
        raw_scr[j, pl.ds(SSD_PAD + s, L + SSD_PAD), :] = jnp.zeros((L + SSD_PAD, LANES), F32)
    raw_scr[0, pl.ds(SSD_PAD, s), :] = xs_ref[0, :, :LANES].astype(F32)
    raw_scr[1, pl.ds(SSD_PAD, s), :] = xs_ref[0, :, LANES:].astype(F32)
    raw_scr[2, pl.ds(SSD_PAD, s), :] = b_ref[0].astype(F32)
    raw_scr[3, pl.ds(SSD_PAD, s), :] = c_ref[0].astype(F32)

    def conv_chunk(c):
        for j in range(n_slabs):
            lanes = pl.ds(j * LANES, LANES)
            taps = [jnp.broadcast_to(cw_ref[0, k:k + 1, lanes], (8, LANES)) for k in range(SSD_CONV_WIDTH)]
            bias = jnp.broadcast_to(cb_ref[0, :, lanes], (8, LANES))
            _conv_rows(raw_scr, act_scr, j, taps, bias, c * L, L, SSD_CONV_WIDTH, SSD_PAD, _silu)

    conv_chunk(0)

    li = lax.broadcasted_iota(jnp.int32, (L, L), 0)
    si = lax.broadcasted_iota(jnp.int32, (L, L), 1)
    head_of_ch = lax.broadcasted_iota(jnp.int32, (L, GROUP_CH), 1) // SSD_HEAD_DIM

    keep = (li >= si, si >= li)
    neg_inf = jnp.float32(-jnp.inf)
    n_heads = 2 * HEADS_PER_GROUP
    st_scr[...] = jnp.zeros_like(st_scr)

    def local_body(c, carry):
        r0 = pl.multiple_of(c * L, L)
        x = jnp.concatenate([act_scr[0, pl.ds(r0, L), :], act_scr[1, pl.ds(r0, L), :]], axis=1)
        xb = x.astype(BF16)
        bk = act_scr[2, pl.ds(r0, L), :].astype(BF16)
        ck = act_scr[3, pl.ds(r0, L), :].astype(BF16)
        acs = acs_ref[0, pl.ds(r0, L), :]
        rowt = rowt_ref[0, 0, c]
        ew = jnp.dot(ew_ref[0, pl.ds(r0, L), :], sel_ref[:, :3 * GROUP_CH],
                     preferred_element_type=F32)

        cb = lax.dot_general(ck, bk, (((1,), (1,)), ((), ())), preferred_element_type=F32)
        ms = []
        for j in range(n_heads):
            diff = acs[:, j:j + 1] - rowt[j:j + 1, :]
            decay_dt = jnp.exp2(jnp.where(keep[j // HEADS_PER_GROUP], diff, neg_inf))
            ms.append((cb * decay_dt).astype(BF16))
        xbd = jnp.concatenate([jnp.where(head_of_ch == h, xb, jnp.zeros_like(xb))
                               for h in range(HEADS_PER_GROUP)], axis=0)
        y = jnp.dot(jnp.concatenate(ms, axis=1), jnp.concatenate([xbd, xbd], axis=0),
                    preferred_element_type=F32) + x * dskip_ref[0]

        xw = jnp.concatenate([x * ew[:, GROUP_CH:2 * GROUP_CH], x * ew[:, 2 * GROUP_CH:]], axis=1)
        upd = lax.dot_general(bk, xw.astype(BF16), (((0,), (0,)), ((), ())), preferred_element_type=F32)

        state = st_scr[0]
        y = y + jnp.dot(ck, state.astype(BF16), preferred_element_type=F32) * ew[:, :GROUP_CH]
        st_scr[0] = state * dec_ref[0, 0, c, 0:1, :] + upd[:, :GROUP_CH]
        y_scr[pl.ds(r0, L), :] = y
        upd_scr[c] = upd[:, GROUP_CH:]
        conv_chunk(c + 1)
        return carry

    lax.fori_loop(0, nc, local_body, 0, unroll=2)

    def bwd_body(i, carry):
        c = nc - 1 - i
        r0 = pl.multiple_of(c * L, L)
        ck = act_scr[3, pl.ds(r0, L), :].astype(BF16)
        e_bwd = jnp.dot(ew_ref[0, pl.ds(r0, L), :], sel_ref[:, 3 * GROUP_CH:], preferred_element_type=F32)
        state = st_scr[1]
        y = y_scr[pl.ds(r0, L), :] + jnp.dot(ck, state.astype(BF16), preferred_element_type=F32) * e_bwd
        st_scr[1] = state * dec_ref[0, 0, c, 1:2, :] + upd_scr[c]
        yz = y * sz_ref[0, pl.ds(r0, L), :].astype(F32)
        ms = jnp.mean(yz * yz, axis=-1, keepdims=True)
        o_ref[0, pl.ds(r0, L), :] = (yz * lax.rsqrt(ms + RMS_EPS) * nw_ref[0]).astype(o_ref.dtype)
        return carry

    lax.fori_loop(0, nc, bwd_body, 0, unroll=2)


def _ssd(xbc, sz, cw, cb, rowt_hm, acs_tm, ew_tm, dec, dskip, nw):
    nb, s, _ = xbc.shape
    nc = s // SSD_CHUNK
    return pl.pallas_call(
        _ssd_kernel,
        grid=(nb, SSD_GROUPS),
        in_specs=[
            pl.BlockSpec((1, s, GROUP_CH), lambda b, g: (b, 0, g)),
            pl.BlockSpec((1, s, SSD_STATE), lambda b, g: (b, 0, SSD_D_INNER // SSD_STATE + g)),
            pl.BlockSpec((1, s, SSD_STATE), lambda b, g: (b, 0, SSD_D_INNER // SSD_STATE + SSD_GROUPS + g)),
            pl.BlockSpec((1, s, GROUP_CH), lambda b, g: (b, 0, g)),
            pl.BlockSpec((1, 8, XBC_W), lambda b, g: (g, 0, 0)),
            pl.BlockSpec((1, 1, XBC_W), lambda b, g: (g, 0, 0)),
            pl.BlockSpec((1, 1, nc, 8, SSD_CHUNK), lambda b, g: (b, g, 0, 0, 0)),
            pl.BlockSpec((1, s, 8), lambda b, g: (b * SSD_GROUPS + g, 0, 0)),
            pl.BlockSpec((1, s, EW_COLS), lambda b, g: (b * SSD_GROUPS + g, 0, 0)),
            pl.BlockSpec((EW_COLS, 4 * GROUP_CH), lambda b, g: (0, 0)),
            pl.BlockSpec((1, 1, nc, 2, GROUP_CH), lambda b, g: (b, g, 0, 0, 0)),
            pl.BlockSpec((1, 1, GROUP_CH), lambda b, g: (g, 0, 0)),
            pl.BlockSpec((1, 1, GROUP_CH), lambda b, g: (g, 0, 0)),
        ],
        out_specs=pl.BlockSpec((1, s, GROUP_CH), lambda b, g: (b, 0, g)),
        out_shape=jax.ShapeDtypeStruct((nb, s, SSD_D_INNER), BF16),
        scratch_shapes=[
            pltpu.VMEM((XBC_W // LANES, s + SSD_CHUNK + 2 * SSD_PAD, LANES), F32),
            pltpu.VMEM((XBC_W // LANES, s + SSD_CHUNK, LANES), F32),
            pltpu.VMEM((s, GROUP_CH), F32),
            pltpu.VMEM((nc, SSD_STATE, GROUP_CH), F32),
            pltpu.VMEM((2, SSD_STATE, GROUP_CH), F32),
        ],
        compiler_params=_cparams(("parallel", "parallel")),
        name="ssd_scan",
    )(xbc, xbc, xbc, sz, cw, cb, rowt_hm, acs_tm, ew_tm, _head_expand_matrix(), dec, dskip, nw)


D_SLABS = D_MODEL // LANES


def _conformer_kernel(u_ref, up_ref, un_ref, mb_ref, g_ref, x_ref, dww_ref, dwb_ref, lnw_ref, lnb_ref,
                      pw_ref, wo_ref, o_ref, u_scr, c_scr):
    i = pl.program_id(1)
    n = pl.num_programs(1)
    tq = u_ref.shape[1]
    for j in range(D_SLABS):
        cs = slice(j * LANES, (j + 1) * LANES)
        u_scr[j, pl.ds(0, HALO), :] = jnp.where(i > 0, up_ref[0, :, cs].astype(F32), 0.0)
        u_scr[j, pl.ds(HALO, tq), :] = u_ref[0, :, cs].astype(F32)
        u_scr[j, pl.ds(HALO + tq, HALO), :] = jnp.where(i < n - 1, un_ref[0, :, cs].astype(F32), 0.0)
    for j in range(D_SLABS):
        _depthwise_conv(u_scr, c_scr, j, dww_ref, dwb_ref, tq, CONV_WIDTH, HALO, lambda a: a)

    cv = jnp.concatenate([c_scr[j] for j in range(D_SLABS)], axis=1)
    mu = jnp.mean(cv, axis=-1, keepdims=True)
    xc = cv - mu
    var = jnp.mean(xc * xc, axis=-1, keepdims=True)
    y = xc * lax.rsqrt(var + LN_EPS) * lnw_ref[...] + lnb_ref[...]
    a = jnp.dot(_silu(y).astype(BF16), pw_ref[...], preferred_element_type=F32)
    merged = g_ref[0].astype(F32) * a + mb_ref[0].astype(F32)
    o_ref[0] = x_ref[0] + jnp.dot(merged.astype(BF16), wo_ref[...], preferred_element_type=F32)


def _conformer(u, mb, gates, x, dww, dwb, lnw, lnb, pw, wo, tq):
    nb, s, d = x.shape
    hb = tq // HALO
    nh = s // HALO
    wspec = pl.BlockSpec((d, d), lambda b, i: (0, 0))
    vspec = pl.BlockSpec((1, d), lambda b, i: (0, 0))
    return pl.pallas_call(
        _conformer_kernel,
        grid=(nb, s // tq),
        in_specs=[
            pl.BlockSpec((1, tq, d), lambda b, i: (b, i, 0)),
            pl.BlockSpec((1, HALO, d), lambda b, i: (b, jnp.maximum(i * hb - 1, 0), 0)),
            pl.BlockSpec((1, HALO, d), lambda b, i: (b, jnp.minimum((i + 1) * hb, nh - 1), 0)),
            pl.BlockSpec((1, tq, d), lambda b, i: (b, i, 0)),
            pl.BlockSpec((1, tq, d), lambda b, i: (b, i, 0)),
            pl.BlockSpec((1, tq, d), lambda b, i: (b, i, 0)),
            pl.BlockSpec((32, d), lambda b, i: (0, 0)),
            vspec, vspec, vspec, wspec, wspec,
        ],
        out_specs=pl.BlockSpec((1, tq, d), lambda b, i: (b, i, 0)),
        out_shape=jax.ShapeDtypeStruct((nb, s, d), F32),
        scratch_shapes=[pltpu.VMEM((D_SLABS, tq + 2 * HALO, LANES), F32), pltpu.VMEM((D_SLABS, tq, LANES), F32)],
        compiler_params=_cparams(("parallel", "parallel")),
        name="conformer_merge",
    )(u, u, u, mb, gates, x, dww, dwb, lnw, lnb, pw, wo)


TILE_ROWS = D_MODEL // LANES


def _store_token_tiles(ref, val):
    n = val.shape[0]
    for j in range(TILE_ROWS):
        ref[pl.ds(j, n, stride=TILE_ROWS), :] = val[:, j * LANES:(j + 1) * LANES]


def _load_token_tiles(ref, first, n):
    return jnp.concatenate([ref[pl.ds(first * TILE_ROWS + j, n, stride=TILE_ROWS), :] for j in range(TILE_ROWS)],
                           axis=1)


def _attn_kernel(x_ref, k_ref, v_ref, nw_ref, wq_ref, wo_ref, nf_ref, wr_ref, br_ref, x2_ref, h3_ref, lg_ref):
    x = x_ref[0]
    h = _rms(x, nw_ref[...]).astype(BF16)
    q = jnp.dot(h, wq_ref[...], preferred_element_type=F32)
    scale = 1.0 / math.sqrt(XATTN_HEAD_DIM)
    outs = []
    for hd in range(XATTN_HEADS):
        cs = slice(hd * XATTN_HEAD_DIM, (hd + 1) * XATTN_HEAD_DIM)
        sc = lax.dot_general(q[:, cs].astype(BF16), k_ref[0, :, cs], (((1,), (1,)), ((), ())),
                             preferred_element_type=F32) * scale
        m = jnp.max(sc, axis=-1, keepdims=True)
        e = jnp.exp(sc - m)
        p = e / jnp.sum(e, axis=-1, keepdims=True)
        outs.append(jnp.dot(p.astype(BF16), v_ref[0, :, cs], preferred_element_type=F32))
    o = jnp.concatenate(outs, axis=1).astype(BF16)
    x2 = x + jnp.dot(o, wo_ref[...], preferred_element_type=F32)
    x2_ref[0] = x2
    h3 = _rms(x2, nf_ref[...])
    _store_token_tiles(h3_ref, h3)
    lg_ref[0] = jnp.dot(h3.astype(BF16), wr_ref[...], preferred_element_type=F32) + br_ref[...]


def _attention(x, kv, nw, wq, wo, nf, wr, br, tq):
    nb, s, d = x.shape
    m = kv.shape[1]
    wspec = pl.BlockSpec((d, d), lambda b, i: (0, 0))
    vspec = pl.BlockSpec((1, d), lambda b, i: (0, 0))
    return pl.pallas_call(
        _attn_kernel,
        grid=(nb, s // tq),
        in_specs=[
            pl.BlockSpec((1, tq, d), lambda b, i: (b, i, 0)),
            pl.BlockSpec((1, m, d), lambda b, i: (b, 0, 0)),
            pl.BlockSpec((1, m, d), lambda b, i: (b, 0, 1)),
            vspec, wspec, wspec, vspec,
            pl.BlockSpec((d, LANES), lambda b, i: (0, 0)),
            pl.BlockSpec((1, LANES), lambda b, i: (0, 0)),
        ],
        out_specs=[
            pl.BlockSpec((1, tq, d), lambda b, i: (b, i, 0)),
            pl.BlockSpec((tq * TILE_ROWS, LANES), lambda b, i: (b * (s // tq) + i, 0)),
            pl.BlockSpec((1, tq, LANES), lambda b, i: (b, i, 0)),
        ],
        out_shape=[
            jax.ShapeDtypeStruct((nb, s, d), F32),
            jax.ShapeDtypeStruct((nb * s * TILE_ROWS, LANES), F32),
            jax.ShapeDtypeStruct((nb, s, LANES), F32),
        ],
        compiler_params=_cparams(("parallel", "parallel")),
        name="cross_attn_router",
    )(x, kv, kv, nw, wq, wo, nf, wr, br)


MOE_ROWS = 256


def _route_math(lg, lane):
    lane_f = lane.astype(F32)
    big = float(LANES)
    neg = -jnp.inf

    is_g = lane < N_GROUPS
    gl = jnp.where(is_g, lg, neg)
    gmax = jnp.max(gl, axis=-1, keepdims=True)
    g_idx = jnp.min(jnp.where(gl == gmax, lane_f, big), axis=-1, keepdims=True)
    g_sum = jnp.sum(jnp.where(is_g, jnp.exp(gl - gmax), 0.0), axis=-1, keepdims=True)
    g_w = 1.0 / g_sum

    e_grp = lax.shift_right_arithmetic(lane - N_GROUPS, 3).astype(F32)
    is_e = (lane >= N_GROUPS) & (lane < N_GROUPS + N_EXPERTS) & (e_grp == g_idx)
    el = jnp.where(is_e, lg, neg)
    emax = jnp.max(el, axis=-1, keepdims=True)
    ee = jnp.where(is_e, jnp.exp(el - emax), 0.0)
    ep = ee / jnp.sum(ee, axis=-1, keepdims=True)
    ep = jnp.where(is_e, ep, -1.0)
    v1 = jnp.max(ep, axis=-1, keepdims=True)
    i1 = jnp.min(jnp.where(ep == v1, lane_f, big), axis=-1, keepdims=True)
    ep2 = jnp.where(lane_f == i1, -1.0, ep)
    v2 = jnp.max(ep2, axis=-1, keepdims=True)
    i2 = jnp.min(jnp.where(ep2 == v2, lane_f, big), axis=-1, keepdims=True)
    tot = v1 + v2
    gate1 = g_w * (v1 / tot)
    gate2 = g_w * (v2 / tot)
    return i1 - N_GROUPS, i2 - N_GROUPS, gate1, gate2


def _plan_kernel(lg_ref, gate_ref, pos_ref, cnt_ref, carry_scr, base_scr):
    p = pl.program_id(0)
    i = pl.program_id(1)
    tm = lg_ref.shape[0]
    lane = lax.broadcasted_iota(jnp.int32, (tm, LANES), 1)
    lane_f = lane.astype(F32)
    e1, e2, gate1, gate2 = _route_math(lg_ref[...], lane)
    oh1 = lane_f == e1
    oh2 = lane_f == e2
    m = jnp.where(oh1 | oh2, 1.0, 0.0)
    colsum = jnp.sum(m, axis=0, keepdims=True)

    @pl.when((p == 0) & (i == 0))
    def _():
        carry_scr[...] = jnp.zeros_like(carry_scr)

    @pl.when(p == 0)
    def _():
        carry_scr[...] += colsum

    @pl.when((p == 1) & (i == 0))
    def _():
        counts = carry_scr[...]
        cnt_ref[...] = counts
        shift = MOE_ROWS.bit_length() - 1
        blocks = lax.shift_right_logical(counts.astype(jnp.int32) + (MOE_ROWS - 1), shift)
        padded = (blocks * MOE_ROWS).astype(F32)
        lane8 = lax.broadcasted_iota(jnp.int32, padded.shape, 1)
        inc = padded
        sh = 1
        while sh < LANES:
            inc = inc + jnp.where(lane8 >= sh, pltpu.roll(inc, sh, 1), 0.0)
            sh *= 2
        base_scr[...] = inc - padded
        carry_scr[...] = jnp.zeros_like(carry_scr)

    @pl.when(p == 1)
    def _():
        r = lax.broadcasted_iota(jnp.int32, (tm, tm), 0)
        c = lax.broadcasted_iota(jnp.int32, (tm, tm), 1)
        earlier = jnp.where(r > c, 1.0, 0.0).astype(BF16)
        tot = (jnp.dot(earlier, m.astype(BF16), preferred_element_type=F32)
               + base_scr[0:1, :] + carry_scr[0:1, :])
        pos1 = jnp.sum(jnp.where(oh1, tot, 0.0), axis=1, keepdims=True)
        pos2 = jnp.sum(jnp.where(oh2, tot, 0.0), axis=1, keepdims=True)
        pos_ref[...] = jnp.where(lane == 0, pos1, jnp.where(lane == 1, pos2, 0.0)).astype(jnp.int32)
        gate_ref[...] = jnp.where(lane == 0, gate1, jnp.where(lane == 1, gate2, 0.0))
        carry_scr[...] += colsum


def _plan(lg, tm):
    t = lg.shape[0]
    out_spec = pl.BlockSpec((tm, LANES), lambda p, i: (i * p, 0))
    return pl.pallas_call(
        _plan_kernel,
        grid=(2, t // tm),
        in_specs=[pl.BlockSpec((tm, LANES), lambda p, i: (i, 0))],
        out_specs=[out_spec, out_spec, pl.BlockSpec((8, LANES), lambda p, i: (0, 0))],
        out_shape=[jax.ShapeDtypeStruct((t, LANES), F32), jax.ShapeDtypeStruct((t, LANES), jnp.int32),
                   jax.ShapeDtypeStruct((8, LANES), F32)],
        scratch_shapes=[pltpu.VMEM((8, LANES), F32), pltpu.VMEM((8, LANES), F32)],
        compiler_params=_cparams(("arbitrary", "arbitrary")),
        name="moe_plan",
    )(lg)


def _dispatch_kernel(last_ref, pos_ref, h_ref, xs_hbm, zero_scr, sem):
    td = pos_ref.shape[2] // 2

    @pl.when(pl.program_id(0) == 0)
    def _():
        zero_scr[...] = jnp.zeros_like(zero_scr)

        def block_copy(e):
            first = pl.multiple_of(last_ref[e] * (MOE_ROWS * TILE_ROWS), MOE_ROWS * TILE_ROWS)
            return pltpu.make_async_copy(zero_scr, xs_hbm.at[pl.ds(first, MOE_ROWS * TILE_ROWS)], sem)

        def fill(e, carry):
            @pl.when(last_ref[e] >= 0)
            def _():
                block_copy(e).start()
            return carry

        def drain(e, carry):
            @pl.when(last_ref[e] >= 0)
            def _():
                block_copy(e).wait()
            return carry

        lax.fori_loop(0, last_ref.shape[0], fill, 0)
        lax.fori_loop(0, last_ref.shape[0], drain, 0)

    def tile(ref, tok):
        return ref.at[pl.ds(pl.multiple_of(tok * TILE_ROWS, TILE_ROWS), TILE_ROWS)]

    def body(r, carry):
        src = tile(h_ref, r)
        pltpu.make_async_copy(src, tile(xs_hbm, pos_ref[0, 0, r]), sem).start()
        pltpu.make_async_copy(src, tile(xs_hbm, pos_ref[0, 0, td + r]), sem).start()
        return carry

    lax.fori_loop(0, td, body, 0, unroll=8)
    for _ in range(2):
        pltpu.make_async_copy(h_ref, xs_hbm.at[pl.ds(0, td * TILE_ROWS)], sem).wait()


def _dispatch(last_blk, pos_blk, h3_tiles, n_blocks):
    n_steps = pos_blk.shape[0]
    td = pos_blk.shape[2] // 2
    grid_spec = pltpu.PrefetchScalarGridSpec(
        num_scalar_prefetch=1,
        grid=(n_steps,),
        in_specs=[
            pl.BlockSpec((1, 1, 2 * td), lambda i, last: (i, 0, 0), memory_space=pltpu.SMEM),
            pl.BlockSpec((td * TILE_ROWS, LANES), lambda i, last: (i, 0)),
        ],
        out_specs=pl.BlockSpec(memory_space=pl.ANY),
        scratch_shapes=[pltpu.VMEM((MOE_ROWS * TILE_ROWS, LANES), F32), pltpu.SemaphoreType.DMA(())],
    )
    return pl.pallas_call(
        _dispatch_kernel,
        grid_spec=grid_spec,
        out_shape=jax.ShapeDtypeStruct((n_blocks * MOE_ROWS * TILE_ROWS, LANES), F32),
        compiler_params=_cparams(("arbitrary",)),
        name="moe_dispatch",
    )(last_blk, pos_blk, h3_tiles)


def _expert_kernel(be_ref, na_ref, x_ref, wg_ref, wu_ref, wd_ref, o_ref, wg_scr, wu_scr, wd_scr):
    j = pl.program_id(0)
    prev = be_ref[jnp.maximum(j - 1, 0)]

    @pl.when((j == 0) | (be_ref[j] != prev))
    def _():
        wg_scr[...] = wg_ref[0].astype(BF16)
        wu_scr[...] = wu_ref[0].astype(BF16)
        wd_scr[...] = wd_ref[0].astype(BF16)

    @pl.when(j < na_ref[0])
    def _():
        x = _load_token_tiles(x_ref, 0, MOE_ROWS).astype(BF16)
        g = jnp.dot(x, wg_scr[...], preferred_element_type=F32)
        u = jnp.dot(x, wu_scr[...], preferred_element_type=F32)
        hid = (_silu(g) * u).astype(BF16)
        _store_token_tiles(o_ref, jnp.dot(hid, wd_scr[...], preferred_element_type=F32))

    @pl.when(j >= na_ref[0])
    def _():
        o_ref[...] = jnp.zeros_like(o_ref)


def _experts(blk_exp, n_active, xs_tiles, wg, wu, wd):
    n_blocks = blk_exp.shape[0]
    d, ff = wg.shape[1], wg.shape[2]
    blk = (MOE_ROWS * TILE_ROWS, LANES)
    grid_spec = pltpu.PrefetchScalarGridSpec(
        num_scalar_prefetch=2,
        grid=(n_blocks,),
        in_specs=[
            pl.BlockSpec(blk, lambda j, be, na: (jnp.minimum(j, na[0] - 1), 0)),
            pl.BlockSpec((1, d, ff), lambda j, be, na: (be[j], 0, 0)),
            pl.BlockSpec((1, d, ff), lambda j, be, na: (be[j], 0, 0)),
            pl.BlockSpec((1, ff, d), lambda j, be, na: (be[j], 0, 0)),
        ],
        out_specs=pl.BlockSpec(blk, lambda j, be, na: (j, 0)),
        scratch_shapes=[pltpu.VMEM((d, ff), BF16), pltpu.VMEM((d, ff), BF16), pltpu.VMEM((ff, d), BF16)],
    )
    return pl.pallas_call(
        _expert_kernel,
        grid_spec=grid_spec,
        out_shape=jax.ShapeDtypeStruct(xs_tiles.shape, F32),
        compiler_params=_cparams(("arbitrary",)),
        name="moe_experts",
    )(blk_exp, n_active, xs_tiles, wg, wu, wd)


def _combine_kernel(pos_ref, x_ref, gate_ref, nw_ref, ys_hbm, o_ref, y_scr, sem):
    tm = x_ref.shape[0]

    def tile(ref, tok):
        return ref.at[pl.ds(pl.multiple_of(tok * TILE_ROWS, TILE_ROWS), TILE_ROWS)]

    def body(r, carry):
        pltpu.make_async_copy(tile(ys_hbm, pos_ref[0, 0, r]), tile(y_scr, r), sem).start()
        return carry

    lax.fori_loop(0, 2 * tm, body, 0, unroll=8)
    pltpu.make_async_copy(ys_hbm.at[pl.ds(0, 2 * tm * TILE_ROWS)], y_scr, sem).wait()
    gate = gate_ref[...]
    y = x_ref[...] + (gate[:, 0:1] * _load_token_tiles(y_scr, 0, tm) + gate[:, 1:2] * _load_token_tiles(y_scr, tm, tm))
    o_ref[...] = _rms(y, nw_ref[...])


def _combine(pos_blk, x2, gates, nw, ys, tm):
    t, d = x2.shape
    return pl.pallas_call(
        _combine_kernel,
        grid=(t // tm,),
        in_specs=[
            pl.BlockSpec((1, 1, 2 * tm), lambda i: (i, 0, 0), memory_space=pltpu.SMEM),
            pl.BlockSpec((tm, d), lambda i: (i, 0)),
            pl.BlockSpec((tm, LANES), lambda i: (i, 0)),
            pl.BlockSpec((1, d), lambda i: (0, 0)),
            pl.BlockSpec(memory_space=pl.ANY),
        ],
        out_specs=pl.BlockSpec((tm, d), lambda i: (i, 0)),
        out_shape=jax.ShapeDtypeStruct((t, d), F32),
        scratch_shapes=[pltpu.VMEM((2 * tm * TILE_ROWS, LANES), F32), pltpu.SemaphoreType.DMA(())],
        compiler_params=_cparams(("arbitrary",)),
        name="moe_combine",
    )(pos_blk, x2, gates, nw, ys)


def _block_experts(counts, n_blocks):
    per_expert = (counts + MOE_ROWS - 1) // MOE_ROWS
    ends = jnp.cumsum(per_expert)
    blk = jnp.arange(n_blocks, dtype=jnp.int32)
    blk_exp = jnp.minimum(jnp.sum(blk[:, None] >= ends[None, :], axis=1), N_EXPERTS - 1).astype(jnp.int32)
    last_blk = jnp.where(per_expert > 0, ends - 1, -1)
    tail = n_blocks - 1 - jnp.arange(N_EXPERTS)
    to_zero = jnp.concatenate([last_blk, jnp.where(tail >= ends[-1], tail, -1)]).astype(jnp.int32)
    return blk_exp, ends[-1:].astype(jnp.int32), to_zero


def _pick(n, pref):
    t = pref
    while n % t:
        t //= 2
    return t


def _encoder(x, mem, p):
    nb, s, d = x.shape
    t = nb * s
    nc = s // SSD_CHUNK
    tm = _pick(t, 1024)
    tq = _pick(s, 512)
    xf = x.reshape(t, d)

    perm = jnp.arange(SSD_HEADS).reshape(SSD_GROUPS, HEADS_PER_GROUP)
    perm = jnp.concatenate([perm, perm + SSD_HEADS], axis=1).reshape(-1)

    w_in = p['w_in']
    o_z = 2 * D_MODEL
    o_x = o_z + SSD_D_INNER
    o_dt = o_x + SSD_D_INNER + 2 * SSD_GROUPS * SSD_STATE
    o_g = o_dt + 2 * SSD_HEADS
    w_val = w_in[:, :D_MODEL].astype(BF16)
    w_gate = w_in[:, D_MODEL:o_z].astype(BF16)
    w_z = w_in[:, o_z:o_x].astype(BF16)
    w_xbc = w_in[:, o_x:o_dt].astype(BF16)
    w_dt = jnp.pad(w_in[:, o_dt:o_g][:, perm], ((0, 0), (0, LANES - 2 * SSD_HEADS))).astype(BF16)
    w_g = w_in[:, o_g:].astype(BF16)
    nmix = p['norm_mix_w'].reshape(1, d)

    u = _glu_proj(xf, nmix, w_val, w_gate, tm, PROJ_TN // 2)
    sz = _norm_proj(xf, nmix, w_z, _silu, BF16, tm, PROJ_TN, "z_proj")
    xbc = _norm_proj(xf, nmix, w_xbc, lambda a: a, BF16, tm, PROJ_TN, "xbc_proj")
    dt_raw = _norm_proj(xf, nmix, w_dt, lambda a: a, F32, tm, LANES, "dt_proj")
    gates = _norm_proj(xf, nmix, w_g, _sigmoid, BF16, tm, PROJ_TN, "gate_proj")

    pad = LANES - 2 * SSD_HEADS
    dt_bias = jnp.pad(jnp.concatenate([p['ssd_dt_bias_f'], p['ssd_dt_bias_b']])[perm], (0, pad)).reshape(1, LANES)
    a_log = jnp.pad(jnp.concatenate([p['ssd_a_log_f'], p['ssd_a_log_b']])[perm], (0, pad)).reshape(1, LANES)
    acs2, rowt, e_hi, e_lo, w_hi, w_lo, dec = _dtprep(dt_raw, dt_bias, a_log, tm)

    def per_group_tm(a):
        return a[:, :2 * SSD_HEADS].reshape(nb, s, SSD_GROUPS, 8).transpose(0, 2, 1, 3)

    def per_group_hm(a):
        return a[:, :2 * SSD_HEADS].reshape(nb, nc, SSD_CHUNK, SSD_GROUPS, 8).transpose(0, 3, 1, 4, 2)

    acs_tm = per_group_tm(acs2).reshape(nb * SSD_GROUPS, s, 8)
    ew_tm = jnp.concatenate([per_group_tm(v) for v in (e_hi, e_lo, w_hi, w_lo)], axis=-1)
    ew_tm = ew_tm.reshape(nb * SSD_GROUPS, s, EW_COLS)
    rowt_hm = per_group_hm(rowt)
    dec = dec[:, :2 * SSD_HEADS].reshape(nb, nc, SSD_GROUPS, 2, HEADS_PER_GROUP).transpose(0, 2, 1, 3, 4)
    dec = jnp.repeat(dec, SSD_HEAD_DIM, axis=-1)

    cwx = p['ssd_conv_w']
    cbx = p['ssd_conv_b']

    def per_group_conv(a):
        xs_ = a[:, :SSD_D_INNER].reshape(-1, SSD_GROUPS, GROUP_CH)
        b_ = a[:, SSD_D_INNER:SSD_D_INNER + SSD_GROUPS * SSD_STATE].reshape(-1, SSD_GROUPS, SSD_STATE)
        c_ = a[:, SSD_D_INNER + SSD_GROUPS * SSD_STATE:].reshape(-1, SSD_GROUPS, SSD_STATE)
        return jnp.concatenate([xs_, b_, c_], axis=-1).transpose(1, 0, 2)

    cw = jnp.pad(per_group_conv(cwx), ((0, 0), (0, 8 - SSD_CONV_WIDTH), (0, 0)))
    cb = per_group_conv(cbx.reshape(1, -1))
    dskip = jnp.repeat(p['ssd_d'], SSD_HEAD_DIM).reshape(SSD_GROUPS, 1, GROUP_CH)
    nw_ssd = p['ssd_norm_w'].reshape(SSD_GROUPS, 1, GROUP_CH)

    yn = _ssd(xbc.reshape(nb, s, -1), sz.reshape(nb, s, -1), cw, cb, rowt_hm, acs_tm, ew_tm, dec, dskip, nw_ssd)
    mb = _gated_proj(yn.reshape(t, SSD_D_INNER), p['ssd_out'].astype(BF16), gates, 1, tm)

    dww = jnp.pad(p['conv_dw_w'], ((0, 32 - CONV_WIDTH), (0, 0)))
    x1 = _conformer(u.reshape(nb, s, d), mb.reshape(nb, s, d), gates.reshape(nb, s, 2 * d), x, dww,
                    p['conv_dw_b'].reshape(1, d), p['conv_ln_w'].reshape(1, d), p['conv_ln_b'].reshape(1, d),
                    p['conv_pw_out'].astype(BF16), p['w_out'].astype(BF16), tq)

    m = mem.shape[1]
    kv = _norm_proj(mem.reshape(nb * m, d), p['norm_mem_w'].reshape(1, d), p['xattn_wkv'].astype(BF16),
                    lambda a: a, BF16, _pick(nb * m, 512), 1024, "kv_proj")
    n_r = N_GROUPS + N_EXPERTS
    wr = jnp.pad(jnp.concatenate([p['router_group_w'], p['router_expert_w']], axis=1),
                 ((0, 0), (0, LANES - n_r))).astype(BF16)
    br = jnp.pad(jnp.concatenate([p['router_group_b'], p['router_expert_b']]), (0, LANES - n_r)).reshape(1, LANES)
    x2, h3, logits = _attention(x1, kv.reshape(nb, m, 2 * d), p['norm_xattn_w'].reshape(1, d),
                                p['xattn_wq'].astype(BF16), p['xattn_wo'].astype(BF16),
                                p['norm_ffn_w'].reshape(1, d), wr, br, tq)

    tc = _pick(t, 512)
    gate, pos, cnt = _plan(logits.reshape(t, LANES), tc)
    n_blocks = (2 * t) // MOE_ROWS + N_EXPERTS
    blk_exp, n_active, last_blk = _block_experts(cnt[0, :N_EXPERTS].astype(jnp.int32), n_blocks)
    pos_blk = pos[:, :2].reshape(t // tc, tc, 2).transpose(0, 2, 1).reshape(t // tc, 1, 2 * tc)
    xs = _dispatch(last_blk, pos_blk, h3, n_blocks)
    ys = _experts(blk_exp, n_active, xs, p['expert_w_gate'], p['expert_w_up'], p['expert_w_down'])
    out = _combine(pos_blk, x2.reshape(t, d), gate, p['norm_final_w'].reshape(1, d), ys, tc)
    return out.reshape(nb, s, d)


def kernel(x_prompt, x_sample, mem_prompt, mem_sample, norm_mix_w, w_in, conv_dw_w, conv_dw_b, conv_ln_w, conv_ln_b, conv_pw_out, ssd_conv_w, ssd_conv_b, ssd_dt_bias_f, ssd_dt_bias_b, ssd_a_log_f, ssd_a_log_b, ssd_d, ssd_norm_w, ssd_out, w_out, norm_xattn_w, norm_mem_w, xattn_wq, xattn_wkv, xattn_wo, norm_ffn_w, router_group_w, router_group_b, router_expert_w, router_expert_b, expert_w_gate, expert_w_up, expert_w_down, norm_final_w):
    p = {
        'norm_mix_w': norm_mix_w[0], 'w_in': w_in[0], 'conv_dw_w': conv_dw_w[0], 'conv_dw_b': conv_dw_b[0],
        'conv_ln_w': conv_ln_w[0], 'conv_ln_b': conv_ln_b[0], 'conv_pw_out': conv_pw_out[0],
        'ssd_conv_w': ssd_conv_w[0], 'ssd_conv_b': ssd_conv_b[0], 'ssd_dt_bias_f': ssd_dt_bias_f[0],
        'ssd_dt_bias_b': ssd_dt_bias_b[0], 'ssd_a_log_f': ssd_a_log_f[0], 'ssd_a_log_b': ssd_a_log_b[0],
        'ssd_d': ssd_d[0], 'ssd_norm_w': ssd_norm_w[0], 'ssd_out': ssd_out[0], 'w_out': w_out[0],
        'norm_xattn_w': norm_xattn_w[0], 'norm_mem_w': norm_mem_w[0], 'xattn_wq': xattn_wq[0],
        'xattn_wkv': xattn_wkv[0], 'xattn_wo': xattn_wo[0], 'norm_ffn_w': norm_ffn_w[0],
        'router_group_w': router_group_w[0], 'router_group_b': router_group_b[0],
        'router_expert_w': router_expert_w[0], 'router_expert_b': router_expert_b[0],
        'expert_w_gate': expert_w_gate[0], 'expert_w_up': expert_w_up[0], 'expert_w_down': expert_w_down[0],
        'norm_final_w': norm_final_w,
    }
    n_prompt = x_prompt.shape[0]
    x = jnp.concatenate([x_prompt, x_sample], axis=0)
    mem = jnp.concatenate([mem_prompt, mem_sample], axis=0)
    y = _encoder(x, mem, p)
    return (y[:n_prompt], y[n_prompt:])
```

```python
import functools
import math

import jax
import jax.numpy as jnp
from jax import lax
from jax.experimental import pallas as pl
from jax.experimental.pallas import tpu as pltpu

F32 = jnp.float32
BF16 = jnp.bfloat16

D_MODEL = 1024
CONV_WIDTH = 31
SSD_D_INNER = 2048
SSD_HEAD_DIM = 64
SSD_HEADS = 32
SSD_GROUPS = 8
SSD_STATE = 128
SSD_CONV_WIDTH = 5
SSD_CHUNK = 128
GROUP_CH = SSD_D_INNER // SSD_GROUPS
HEADS_PER_GROUP = SSD_HEADS // SSD_GROUPS
XATTN_HEADS = 4
XATTN_HEAD_DIM = D_MODEL // XATTN_HEADS
N_GROUPS = 8
EXPERTS_PER_GROUP = 8
N_EXPERTS = 64
EXPERT_FF = 512
MOE_BLOCK = 128
RMS_EPS = 1e-6
LN_EPS = 1e-5

LANES = 128
HALO = 16
VMEM_LIMIT = 56 * 1024 * 1024
PROJ_TN = 2048


def _cparams(sem):
    return pltpu.CompilerParams(dimension_semantics=sem, vmem_limit_bytes=VMEM_LIMIT)


def _rms(x, w):
    ms = jnp.mean(x * x, axis=-1, keepdims=True)
    return x * lax.rsqrt(ms + RMS_EPS) * w


def _sigmoid(x):
    return 1.0 / (1.0 + jnp.exp(-x))


def _silu(x):
    return x * _sigmoid(x)


def _softplus(x):
    return jnp.maximum(x, 0.0) + jnp.log1p(jnp.exp(-jnp.abs(x)))


CONV_STRIDE = 4
CONV_ROWS = 8 * CONV_STRIDE
CONV_UNROLL = 4


def _depthwise_conv(src, dst, slab, w_ref, b_ref, n_rows, width, first_row, epilogue):
    lanes = pl.ds(slab * LANES, LANES)
    taps = [jnp.broadcast_to(w_ref[k:k + 1, lanes], (8, LANES)) for k in range(width)]
    bias = jnp.broadcast_to(b_ref[:, lanes], (8, LANES))
    step = CONV_ROWS * CONV_UNROLL

    def body(r, carry):
        _conv_rows(src, dst, slab, taps, bias, r * step, step, width, first_row, epilogue)
        return carry

    lax.fori_loop(0, n_rows // step, body, 0)


def _conv_rows(src, dst, slab, taps, bias, base, n_rows, width, first_row, epilogue):
    for t0 in range(n_rows // 8):
        row = base + (t0 // CONV_STRIDE) * CONV_ROWS + t0 % CONV_STRIDE
        acc = bias
        for k in range(width):
            acc = acc + src[slab, pl.ds(row + first_row - width // 2 + k, 8, stride=CONV_STRIDE), :] * taps[k]
        dst[slab, pl.ds(row, 8, stride=CONV_STRIDE), :] = epilogue(acc)


def _first_or_second(i, n_first):
    return jnp.minimum(i, n_first - 1), jnp.maximum(i - n_first, 0)


def _prenorm_kernel(n_first, xa_ref, xb_ref, nw_ref, o_ref):
    x = jnp.where(pl.program_id(0) < n_first, xa_ref[...], xb_ref[...])
    o_ref[...] = _rms(x, nw_ref[...]).astype(o_ref.dtype)


def _prenorm(xa, xb, nw, tm):
    (ta, d), tb = xa.shape, xb.shape[0]
    na = ta // tm
    return pl.pallas_call(
        functools.partial(_prenorm_kernel, na),
        grid=((ta + tb) // tm,),
        in_specs=[
            pl.BlockSpec((tm, d), lambda i: (_first_or_second(i, na)[0], 0)),
            pl.BlockSpec((tm, d), lambda i: (_first_or_second(i, na)[1], 0)),
            pl.BlockSpec((1, d), lambda i: (0, 0)),
        ],
        out_specs=pl.BlockSpec((tm, d), lambda i: (i, 0)),
        out_shape=jax.ShapeDtypeStruct((ta + tb, d), BF16),
        compiler_params=_cparams(("arbitrary",)),
        name="mix_norm",
    )(xa, xb, nw)


def _glu_kernel(h_ref, wv_ref, wg_ref, o_ref):
    h = h_ref[...]
    v = jnp.dot(h, wv_ref[...], preferred_element_type=F32)
    g = jnp.dot(h, wg_ref[...], preferred_element_type=F32)
    o_ref[...] = (v * _sigmoid(g)).astype(o_ref.dtype)


def _glu_proj(h, wv, wg, tm, tn):
    t, d = h.shape
    n = wv.shape[1]
    return pl.pallas_call(
        _glu_kernel,
        grid=(t // tm, n // tn),
        in_specs=[
            pl.BlockSpec((tm, d), lambda i, j: (i, 0)),
            pl.BlockSpec((d, tn), lambda i, j: (0, j)),
            pl.BlockSpec((d, tn), lambda i, j: (0, j)),
        ],
        out_specs=pl.BlockSpec((tm, tn), lambda i, j: (i, j)),
        out_shape=jax.ShapeDtypeStruct((t, n), BF16),
        compiler_params=_cparams(("parallel", "arbitrary")),
        name="glu_proj",
    )(h, wv, wg)


def _proj_kernel(epi, h_ref, w_ref, o_ref):
    o_ref[...] = epi(jnp.dot(h_ref[...], w_ref[...], preferred_element_type=F32)).astype(o_ref.dtype)


def _proj(h, w, epi, out_dtype, tm, tn, name):
    t, d = h.shape
    n = w.shape[1]
    return pl.pallas_call(
        functools.partial(_proj_kernel, epi),
        grid=(t // tm, n // tn),
        in_specs=[
            pl.BlockSpec((tm, d), lambda i, j: (i, 0)),
            pl.BlockSpec((d, tn), lambda i, j: (0, j)),
        ],
        out_specs=pl.BlockSpec((tm, tn), lambda i, j: (i, j)),
        out_shape=jax.ShapeDtypeStruct((t, n), out_dtype),
        compiler_params=_cparams(("parallel", "arbitrary")),
        name=name,
    )(h, w)


def _norm_proj_kernel(epi, x_ref, nw_ref, w_ref, o_ref, h_scr):
    @pl.when(pl.program_id(1) == 0)
    def _():
        h_scr[...] = _rms(x_ref[...], nw_ref[...]).astype(BF16)

    acc = jnp.dot(h_scr[...], w_ref[...], preferred_element_type=F32)
    o_ref[...] = epi(acc).astype(o_ref.dtype)


def _norm_proj(x, nw, w, epi, out_dtype, tm, tn, name):
    t, d = x.shape
    n = w.shape[1]
    return pl.pallas_call(
        functools.partial(_norm_proj_kernel, epi),
        grid=(t // tm, n // tn),
        in_specs=[
            pl.BlockSpec((tm, d), lambda i, j: (i, 0)),
            pl.BlockSpec((1, d), lambda i, j: (0, 0)),
            pl.BlockSpec((d, tn), lambda i, j: (0, j)),
        ],
        out_specs=pl.BlockSpec((tm, tn), lambda i, j: (i, j)),
        out_shape=jax.ShapeDtypeStruct((t, n), out_dtype),
        scratch_shapes=[pltpu.VMEM((tm, d), BF16)],
        compiler_params=_cparams(("parallel", "arbitrary")),
        name=name,
    )(x, nw, w)


def _gated_proj_kernel(y_ref, w_ref, g_ref, o_ref):
    acc = jnp.dot(y_ref[...], w_ref[...], preferred_element_type=F32)
    o_ref[...] = (acc * g_ref[...].astype(F32)).astype(o_ref.dtype)


def _gated_proj(y, w, g, g_col_block, tm):
    t, k = y.shape
    n = w.shape[1]
    return pl.pallas_call(
        _gated_proj_kernel,
        grid=(t // tm,),
        in_specs=[
            pl.BlockSpec((tm, k), lambda i: (i, 0)),
            pl.BlockSpec((k, n), lambda i: (0, 0)),
            pl.BlockSpec((tm, n), lambda i: (i, g_col_block)),
        ],
        out_specs=pl.BlockSpec((tm, n), lambda i: (i, 0)),
        out_shape=jax.ShapeDtypeStruct((t, n), BF16),
        compiler_params=_cparams(("parallel",)),
        name="ssd_out_proj",
    )(y, w, g)


LOG2E = 1.4426950408889634


def _split_bf16(v):
    hi = v.astype(BF16)
    return hi, (v - hi.astype(F32)).astype(BF16)


def _dtprep_kernel(raw_ref, bias_ref, alog_ref, acs2_ref, rowt_ref, ehi_ref, elo_ref, whi_ref, wlo_ref, dec_ref):
    rows = raw_ref.shape[0]
    a_head = -jnp.exp(alog_ref[...])
    row = lax.broadcasted_iota(jnp.int32, (SSD_CHUNK, LANES), 0)
    lane = lax.broadcasted_iota(jnp.int32, (SSD_CHUNK, LANES), 1)
    is_bwd = (lane % (2 * HEADS_PER_GROUP)) >= HEADS_PER_GROUP
    for c in range(rows // SSD_CHUNK):
        sl = pl.ds(c * SSD_CHUNK, SSD_CHUNK)
        dt = _softplus(raw_ref[sl, :] + bias_ref[...])
        a = dt * a_head
        fwd = a
        bwd = a
        sh = 1
        while sh < SSD_CHUNK:
            fwd = fwd + jnp.where(row >= sh, pltpu.roll(fwd, sh, 0), 0.0)
            bwd = bwd + jnp.where(row < SSD_CHUNK - sh, pltpu.roll(bwd, SSD_CHUNK - sh, 0), 0.0)
            sh *= 2
        acs = jnp.where(is_bwd, bwd, fwd)
        a_end = jnp.where(is_bwd[0:1, :], bwd[0:1, :], fwd[SSD_CHUNK - 1:SSD_CHUNK, :])
        acs2 = acs * LOG2E
        acs2_ref[sl, :] = acs2
        rowt_ref[sl, :] = acs2 - jnp.log(dt) * LOG2E
        ehi_ref[sl, :], elo_ref[sl, :] = _split_bf16(jnp.exp(acs))
        whi_ref[sl, :], wlo_ref[sl, :] = _split_bf16(jnp.exp(a_end - acs) * dt)
        dec_ref[c:c + 1, :] = jnp.exp(a_end)


def _dtprep(raw, bias, alog, tm):
    t = raw.shape[0]
    spec = pl.BlockSpec((tm, LANES), lambda i: (i, 0))
    vec = pl.BlockSpec((1, LANES), lambda i: (0, 0))
    return pl.pallas_call(
        _dtprep_kernel,
        grid=(t // tm,),
        in_specs=[spec, vec, vec],
        out_specs=[spec] * 6 + [pl.BlockSpec((tm // SSD_CHUNK, LANES), lambda i: (i, 0))],
        out_shape=[jax.ShapeDtypeStruct((t, LANES), F32)] * 2 + [jax.ShapeDtypeStruct((t, LANES), BF16)] * 4
        + [jax.ShapeDtypeStruct((t // SSD_CHUNK, LANES), F32)],
        compiler_params=_cparams(("parallel",)),
        name="dt_prep",
    )(raw, bias, alog)


SSD_PAD = 8
XBC_W = GROUP_CH + 2 * SSD_STATE


EW_COLS = 4 * 2 * HEADS_PER_GROUP


def _head_expand_matrix():
    col = jnp.arange(4 * GROUP_CH)
    part = col // GROUP_CH
    is_w = (part == 1) | (part == 2)
    is_bwd = part >= 2
    head = is_bwd * HEADS_PER_GROUP + (col % GROUP_CH) // SSD_HEAD_DIM
    hi_row = is_w * 16 + head
    row = jnp.arange(EW_COLS)[:, None]
    return ((row == hi_row[None, :]) | (row == hi_row[None, :] + 8)).astype(BF16)


def _ssd_kernel(xs_ref, b_ref, c_ref, sz_ref, cw_ref, cb_ref, rowt_ref, acs_ref, ew_ref, sel_ref, dec_ref,
                dskip_ref, nw_ref, o_ref, raw_scr, act_scr, y_scr, upd_scr, st_scr):
    s = xs_ref.shape[1]
    nc = s // SSD_CHUNK
    L = SSD_CHUNK

    n_slabs = XBC_W // LANES
    for j in range(n_slabs):
        raw_scr[j, pl.ds(0, SSD_PAD), :] = jnp.zeros((SSD_PAD, LANES), F32)
        raw_scr[j, pl.ds(SSD_PAD + s, L + SSD_PAD), :] = jnp.zeros((L + SSD_PAD, LANES), F32)
    raw_scr[0, pl.ds(SSD_PAD, s), :] = xs_ref[0, :, :LANES].astype(F32)
    raw_scr[1, pl.ds(SSD_PAD, s), :] = xs_ref[0, :, LANES:].astype(F32)
    raw_scr[2, pl.ds(SSD_PAD, s), :] = b_ref[0].astype(F32)
    raw_scr[3, pl.ds(SSD_PAD, s), :] = c_ref[0].astype(F32)

    def conv_chunk(c):
        for j in range(n_slabs):
            lanes = pl.ds(j * LANES, LANES)
            taps = [jnp.broadcast_to(cw_ref[0, k:k + 1, lanes], (8, LANES)) for k in range(SSD_CONV_WIDTH)]
            bias = jnp.broadcast_to(cb_ref[0, :, lanes], (8, LANES))
            _conv_rows(raw_scr, act_scr, j, taps, bias, c * L, L, SSD_CONV_WIDTH, SSD_PAD, _silu)

    conv_chunk(0)

    li = lax.broadcasted_iota(jnp.int32, (L, L), 0)
    si = lax.broadcasted_iota(jnp.int32, (L, L), 1)
    head_of_ch = lax.broadcasted_iota(jnp.int32, (L, GROUP_CH), 1) // SSD_HEAD_DIM

    keep = (li >= si, si >= li)
    neg_inf = jnp.float32(-jnp.inf)
    n_heads = 2 * HEADS_PER_GROUP
    st_scr[...] = jnp.zeros_like(st_scr)

    def local_body(c, carry):
        r0 = pl.multiple_of(c * L, L)
        x = jnp.concatenate([act_scr[0, pl.ds(r0, L), :], act_scr[1, pl.ds(r0, L), :]], axis=1)
        xb = x.astype(BF16)
        bk = act_scr[2, pl.ds(r0, L), :].astype(BF16)
        ck = act_scr[3, pl.ds(r0, L), :].astype(BF16)
        acs = acs_ref[0, pl.ds(r0, L), :]
        rowt = rowt_ref[0, 0, c]
        ew = jnp.dot(ew_ref[0, pl.ds(r0, L), :], sel_ref[:, :3 * GROUP_CH],
                     preferred_element_type=F32)

        cb = lax.dot_general(ck, bk, (((1,), (1,)), ((), ())), preferred_element_type=F32)
        ms = []
        for j in range(n_heads):
            diff = acs[:, j:j + 1] - rowt[j:j + 1, :]
            decay_dt = jnp.exp2(jnp.where(keep[j // HEADS_PER_GROUP], diff, neg_inf))
            ms.append((cb * decay_dt).astype(BF16))
        xbd = jnp.concatenate([jnp.where(head_of_ch == h, xb, jnp.zeros_like(xb))
                               for h in range(HEADS_PER_GROUP)], axis=0)
        y = jnp.dot(jnp.concatenate(ms, axis=1), jnp.concatenate([xbd, xbd], axis=0),
                    preferred_element_type=F32) + x * dskip_ref[0]

        xw = jnp.concatenate([x * ew[:, GROUP_CH:2 * GROUP_CH], x * ew[:, 2 * GROUP_CH:]], axis=1)
        upd = lax.dot_general(bk, xw.astype(BF16), (((0,), (0,)), ((), ())), preferred_element_type=F32)

        state = st_scr[0]
        y = y + jnp.dot(ck, state.astype(BF16), preferred_element_type=F32) * ew[:, :GROUP_CH]
        st_scr[0] = state * dec_ref[0, 0, c, 0:1, :] + upd[:, :GROUP_CH]
        y_scr[pl.ds(r0, L), :] = y
        upd_scr[c] = upd[:, GROUP_CH:]
        conv_chunk(c + 1)
        return carry

    lax.fori_loop(0, nc, local_body, 0, unroll=2)

    def bwd_body(i, carry):
        c = nc - 1 - i
        r0 = pl.multiple_of(c * L, L)
        ck = act_scr[3, pl.ds(r0, L), :].astype(BF16)
        e_bwd = jnp.dot(ew_ref[0, pl.ds(r0, L), :], sel_ref[:, 3 * GROUP_CH:], preferred_element_type=F32)
        state = st_scr[1]
        y = y_scr[pl.ds(r0, L), :] + jnp.dot(ck, state.astype(BF16), preferred_element_type=F32) * e_bwd
        st_scr[1] = state * dec_ref[0, 0, c, 1:2, :] + upd_scr[c]
        yz = y * sz_ref[0, pl.ds(r0, L), :].astype(F32)
        ms = jnp.mean(yz * yz, axis=-1, keepdims=True)
        o_ref[0, pl.ds(r0, L), :] = (yz * lax.rsqrt(ms + RMS_EPS) * nw_ref[0]).astype(o_ref.dtype)
        return carry

    lax.fori_loop(0, nc, bwd_body, 0, unroll=2)


def _ssd(xbc, sz, cw, cb, rowt_hm, acs_tm, ew_tm, dec, dskip, nw):
    nb, s, _ = xbc.shape
    nc = s // SSD_CHUNK
    return pl.pallas_call(
        _ssd_kernel,
        grid=(nb, SSD_GROUPS),
        in_specs=[
            pl.BlockSpec((1, s, GROUP_CH), lambda b, g: (b, 0, g)),
            pl.BlockSpec((1, s, SSD_STATE), lambda b, g: (b, 0, SSD_D_INNER // SSD_STATE + g)),
            pl.BlockSpec((1, s, SSD_STATE), lambda b, g: (b, 0, SSD_D_INNER // SSD_STATE + SSD_GROUPS + g)),
            pl.BlockSpec((1, s, GROUP_CH), lambda b, g: (b, 0, g)),
            pl.BlockSpec((1, 8, XBC_W), lambda b, g: (g, 0, 0)),
            pl.BlockSpec((1, 1, XBC_W), lambda b, g: (g, 0, 0)),
            pl.BlockSpec((1, 1, nc, 8, SSD_CHUNK), lambda b, g: (b, g, 0, 0, 0)),
            pl.BlockSpec((1, s, 8), lambda b, g: (b * SSD_GROUPS + g, 0, 0)),
            pl.BlockSpec((1, s, EW_COLS), lambda b, g: (b * SSD_GROUPS + g, 0, 0)),
            pl.BlockSpec((EW_COLS, 4 * GROUP_CH), lambda b, g: (0, 0)),
            pl.BlockSpec((1, 1, nc, 2, GROUP_CH), lambda b, g: (b, g, 0, 0, 0)),
            pl.BlockSpec((1, 1, GROUP_CH), lambda b, g: (g, 0, 0)),
            pl.BlockSpec((1, 1, GROUP_CH), lambda b, g: (g, 0, 0)),
        ],
        out_specs=pl.BlockSpec((1, s, GROUP_CH), lambda b, g: (b, 0, g)),
        out_shape=jax.ShapeDtypeStruct((nb, s, SSD_D_INNER), BF16),
        scratch_shapes=[
            pltpu.VMEM((XBC_W // LANES, s + SSD_CHUNK + 2 * SSD_PAD, LANES), F32),
            pltpu.VMEM((XBC_W // LANES, s + SSD_CHUNK, LANES), F32),
            pltpu.VMEM((s, GROUP_CH), F32),
            pltpu.VMEM((nc, SSD_STATE, GROUP_CH), F32),
            pltpu.VMEM((2, SSD_STATE, GROUP_CH), F32),
        ],
        compiler_params=_cparams(("parallel", "parallel")),
        name="ssd_scan",
    )(xbc, xbc, xbc, sz, cw, cb, rowt_hm, acs_tm, ew_tm, _head_expand_matrix(), dec, dskip, nw)


D_SLABS = D_MODEL // LANES


def _conformer_kernel(n_first, u_ref, up_ref, un_ref, mb_ref, g_ref, xa_ref, xb_ref, dww_ref, dwb_ref, lnw_ref,
                      lnb_ref, pw_ref, wo_ref, o_ref, u_scr, c_scr):
    i = pl.program_id(1)
    n = pl.num_programs(1)
    tq = u_ref.shape[1]
    x = jnp.where(pl.program_id(0) * n + i < n_first, xa_ref[...], xb_ref[...])
    for j in range(D_SLABS):
        cs = slice(j * LANES, (j + 1) * LANES)
        u_scr[j, pl.ds(0, HALO), :] = jnp.where(i > 0, up_ref[0, :, cs].astype(F32), 0.0)
        u_scr[j, pl.ds(HALO, tq), :] = u_ref[0, :, cs].astype(F32)
        u_scr[j, pl.ds(HALO + tq, HALO), :] = jnp.where(i < n - 1, un_ref[0, :, cs].astype(F32), 0.0)
    for j in range(D_SLABS):
        _depthwise_conv(u_scr, c_scr, j, dww_ref, dwb_ref, tq, CONV_WIDTH, HALO, lambda a: a)

    cv = jnp.concatenate([c_scr[j] for j in range(D_SLABS)], axis=1)
    mu = jnp.mean(cv, axis=-1, keepdims=True)
    xc = cv - mu
    var = jnp.mean(xc * xc, axis=-1, keepdims=True)
    y = xc * lax.rsqrt(var + LN_EPS) * lnw_ref[...] + lnb_ref[...]
    a = jnp.dot(_silu(y).astype(BF16), pw_ref[...], preferred_element_type=F32)
    merged = g_ref[0].astype(F32) * a + mb_ref[0].astype(F32)
    o_ref[0] = x + jnp.dot(merged.astype(BF16), wo_ref[...], preferred_element_type=F32)


def _conformer(u, mb, gates, xa, xb, dww, dwb, lnw, lnb, pw, wo, tq):
    nb, s, d = u.shape
    hb = tq // HALO
    nh = s // HALO
    nq = s // tq
    na = xa.shape[0] // tq
    wspec = pl.BlockSpec((d, d), lambda b, i: (0, 0))
    vspec = pl.BlockSpec((1, d), lambda b, i: (0, 0))
    return pl.pallas_call(
        functools.partial(_conformer_kernel, na),
        grid=(nb, nq),
        in_specs=[
            pl.BlockSpec((1, tq, d), lambda b, i: (b, i, 0)),
            pl.BlockSpec((1, HALO, d), lambda b, i: (b, jnp.maximum(i * hb - 1, 0), 0)),
            pl.BlockSpec((1, HALO, d), lambda b, i: (b, jnp.minimum((i + 1) * hb, nh - 1), 0)),
            pl.BlockSpec((1, tq, d), lambda b, i: (b, i, 0)),
            pl.BlockSpec((1, tq, d), lambda b, i: (b, i, 0)),
            pl.BlockSpec((tq, d), lambda b, i: (_first_or_second(b * nq + i, na)[0], 0)),
            pl.BlockSpec((tq, d), lambda b, i: (_first_or_second(b * nq + i, na)[1], 0)),
            pl.BlockSpec((32, d), lambda b, i: (0, 0)),
            vspec, vspec, vspec, wspec, wspec,
        ],
        out_specs=pl.BlockSpec((1, tq, d), lambda b, i: (b, i, 0)),
        out_shape=jax.ShapeDtypeStruct((nb, s, d), F32),
        scratch_shapes=[pltpu.VMEM((D_SLABS, tq + 2 * HALO, LANES), F32), pltpu.VMEM((D_SLABS, tq, LANES), F32)],
        compiler_params=_cparams(("arbitrary", "arbitrary")),
        name="conformer_merge",
    )(u, u, u, mb, gates, xa, xb, dww, dwb, lnw, lnb, pw, wo)


TILE_ROWS = D_MODEL // LANES


def _store_token_tiles(ref, val):
    n = val.shape[0]
    for j in range(TILE_ROWS):
        ref[pl.ds(j, n, stride=TILE_ROWS), :] = val[:, j * LANES:(j + 1) * LANES]


def _load_token_tiles(ref, first, n):
    return jnp.concatenate([ref[pl.ds(first * TILE_ROWS + j, n, stride=TILE_ROWS), :] for j in range(TILE_ROWS)],
                           axis=1)


def _attn_kernel(x_ref, k_ref, v_ref, nw_ref, wq_ref, wo_ref, nf_ref, wr_ref, br_ref, x2_ref, h3_ref, lg_ref):
    x = x_ref[0]
    h = _rms(x, nw_ref[...]).astype(BF16)
    q = jnp.dot(h, wq_ref[...], preferred_element_type=F32)
    scale = 1.0 / math.sqrt(XATTN_HEAD_DIM)
    outs = []
    for hd in range(XATTN_HEADS):
        cs = slice(hd * XATTN_HEAD_DIM, (hd + 1) * XATTN_HEAD_DIM)
        sc = lax.dot_general(q[:, cs].astype(BF16), k_ref[0, :, cs], (((1,), (1,)), ((), ())),
                             preferred_element_type=F32) * scale
        m = jnp.max(sc, axis=-1, keepdims=True)
        e = jnp.exp(sc - m)
        p = e / jnp.sum(e, axis=-1, keepdims=True)
        outs.append(jnp.dot(p.astype(BF16), v_ref[0, :, cs], preferred_element_type=F32))
    o = jnp.concatenate(outs, axis=1).astype(BF16)
    x2 = x + jnp.dot(o, wo_ref[...], preferred_element_type=F32)
    x2_ref[0] = x2
    h3 = _rms(x2, nf_ref[...])
    _store_token_tiles(h3_ref, h3)
    lg_ref[0] = jnp.dot(h3.astype(BF16), wr_ref[...], preferred_element_type=F32) + br_ref[...]


def _attention(x, kv, nw, wq, wo, nf, wr, br, tq):
    nb, s, d = x.shape
    m = kv.shape[1]
    wspec = pl.BlockSpec((d, d), lambda b, i: (0, 0))
    vspec = pl.BlockSpec((1, d), lambda b, i: (0, 0))
    return pl.pallas_call(
        _attn_kernel,
        grid=(nb, s // tq),
        in_specs=[
            pl.BlockSpec((1, tq, d), lambda b, i: (b, i, 0)),
            pl.BlockSpec((1, m, d), lambda b, i: (b, 0, 0)),
            pl.BlockSpec((1, m, d), lambda b, i: (b, 0, 1)),
            vspec, wspec, wspec, vspec,
            pl.BlockSpec((d, LANES), lambda b, i: (0, 0)),
            pl.BlockSpec((1, LANES), lambda b, i: (0, 0)),
        ],
        out_specs=[
            pl.BlockSpec((1, tq, d), lambda b, i: (b, i, 0)),
            pl.BlockSpec((tq * TILE_ROWS, LANES), lambda b, i: (b * (s // tq) + i, 0)),
            pl.BlockSpec((1, tq, LANES), lambda b, i: (b, i, 0)),
        ],
        out_shape=[
            jax.ShapeDtypeStruct((nb, s, d), F32),
            jax.ShapeDtypeStruct((nb * s * TILE_ROWS, LANES), F32),
            jax.ShapeDtypeStruct((nb, s, LANES), F32),
        ],
        compiler_params=_cparams(("parallel", "parallel")),
        name="cross_attn_router",
    )(x, kv, kv, nw, wq, wo, nf, wr, br)


MOE_ROWS = 256


def _route_math(lg, lane):
    lane_f = lane.astype(F32)
    big = float(LANES)
    neg = -jnp.inf

    is_g = lane < N_GROUPS
    gl = jnp.where(is_g, lg, neg)
    gmax = jnp.max(gl, axis=-1, keepdims=True)
    g_idx = jnp.min(jnp.where(gl == gmax, lane_f, big), axis=-1, keepdims=True)
    g_sum = jnp.sum(jnp.where(is_g, jnp.exp(gl - gmax), 0.0), axis=-1, keepdims=True)
    g_w = 1.0 / g_sum

    e_grp = lax.shift_right_arithmetic(lane - N_GROUPS, 3).astype(F32)
    is_e = (lane >= N_GROUPS) & (lane < N_GROUPS + N_EXPERTS) & (e_grp == g_idx)
    el = jnp.where(is_e, lg, neg)
    emax = jnp.max(el, axis=-1, keepdims=True)
    ee = jnp.where(is_e, jnp.exp(el - emax), 0.0)
    ep = ee / jnp.sum(ee, axis=-1, keepdims=True)
    ep = jnp.where(is_e, ep, -1.0)
    v1 = jnp.max(ep, axis=-1, keepdims=True)
    i1 = jnp.min(jnp.where(ep == v1, lane_f, big), axis=-1, keepdims=True)
    ep2 = jnp.where(lane_f == i1, -1.0, ep)
    v2 = jnp.max(ep2, axis=-1, keepdims=True)
    i2 = jnp.min(jnp.where(ep2 == v2, lane_f, big), axis=-1, keepdims=True)
    tot = v1 + v2
    gate1 = g_w * (v1 / tot)
    gate2 = g_w * (v2 / tot)
    return i1 - N_GROUPS, i2 - N_GROUPS, gate1, gate2


def _plan_kernel(lg_ref, gate_ref, pos_ref, cnt_ref, carry_scr, base_scr):
    p = pl.program_id(0)
    i = pl.program_id(1)
    tm = lg_ref.shape[0]
    lane = lax.broadcasted_iota(jnp.int32, (tm, LANES), 1)
    lane_f = lane.astype(F32)
    e1, e2, gate1, gate2 = _route_math(lg_ref[...], lane)
    oh1 = lane_f == e1
    oh2 = lane_f == e2
    m = jnp.where(oh1 | oh2, 1.0, 0.0)
    colsum = jnp.sum(m, axis=0, keepdims=True)

    @pl.when((p == 0) & (i == 0))
    def _():
        carry_scr[...] = jnp.zeros_like(carry_scr)

    @pl.when(p == 0)
    def _():
        carry_scr[...] += colsum

    @pl.when((p == 1) & (i == 0))
    def _():
        counts = carry_scr[...]
        cnt_ref[...] = counts
        shift = MOE_ROWS.bit_length() - 1
        blocks = lax.shift_right_logical(counts.astype(jnp.int32) + (MOE_ROWS - 1), shift)
        padded = (blocks * MOE_ROWS).astype(F32)
        lane8 = lax.broadcasted_iota(jnp.int32, padded.shape, 1)
        inc = padded
        sh = 1
        while sh < LANES:
            inc = inc + jnp.where(lane8 >= sh, pltpu.roll(inc, sh, 1), 0.0)
            sh *= 2
        base_scr[...] = inc - padded
        carry_scr[...] = jnp.zeros_like(carry_scr)

    @pl.when(p == 1)
    def _():
        r = lax.broadcasted_iota(jnp.int32, (tm, tm), 0)
        c = lax.broadcasted_iota(jnp.int32, (tm, tm), 1)
        earlier = jnp.where(r > c, 1.0, 0.0).astype(BF16)
        tot = (jnp.dot(earlier, m.astype(BF16), preferred_element_type=F32)
               + base_scr[0:1, :] + carry_scr[0:1, :])
        pos1 = jnp.sum(jnp.where(oh1, tot, 0.0), axis=1, keepdims=True)
        pos2 = jnp.sum(jnp.where(oh2, tot, 0.0), axis=1, keepdims=True)
        pos_ref[...] = jnp.where(lane == 0, pos1, jnp.where(lane == 1, pos2, 0.0)).astype(jnp.int32)
        gate_ref[...] = jnp.where(lane == 0, gate1, jnp.where(lane == 1, gate2, 0.0))
        carry_scr[...] += colsum


def _plan(lg, tm):
    t = lg.shape[0]
    out_spec = pl.BlockSpec((tm, LANES), lambda p, i: (i * p, 0))
    return pl.pallas_call(
        _plan_kernel,
        grid=(2, t // tm),
        in_specs=[pl.BlockSpec((tm, LANES), lambda p, i: (i, 0))],
        out_specs=[out_spec, out_spec, pl.BlockSpec((8, LANES), lambda p, i: (0, 0))],
        out_shape=[jax.ShapeDtypeStruct((t, LANES), F32), jax.ShapeDtypeStruct((t, LANES), jnp.int32),
                   jax.ShapeDtypeStruct((8, LANES), F32)],
        scratch_shapes=[pltpu.VMEM((8, LANES), F32), pltpu.VMEM((8, LANES), F32)],
        compiler_params=_cparams(("arbitrary", "arbitrary")),
        name="moe_plan",
    )(lg)


def _dispatch_kernel(last_ref, pos_ref, h_ref, xs_hbm, zero_scr, sem):
    td = pos_ref.shape[2] // 2

    @pl.when(pl.program_id(0) == 0)
    def _():
        zero_scr[...] = jnp.zeros_like(zero_scr)

        def block_copy(e):
            first = pl.multiple_of(last_ref[e] * (MOE_ROWS * TILE_ROWS), MOE_ROWS * TILE_ROWS)
            return pltpu.make_async_copy(zero_scr, xs_hbm.at[pl.ds(first, MOE_ROWS * TILE_ROWS)], sem)

        def fill(e, carry):
            @pl.when(last_ref[e] >= 0)
            def _():
                block_copy(e).start()
            return carry

        def drain(e, carry):
            @pl.when(last_ref[e] >= 0)
            def _():
                block_copy(e).wait()
            return carry

        lax.fori_loop(0, last_ref.shape[0], fill, 0)
        lax.fori_loop(0, last_ref.shape[0], drain, 0)

    def tile(ref, tok):
        return ref.at[pl.ds(pl.multiple_of(tok * TILE_ROWS, TILE_ROWS), TILE_ROWS)]

    def body(r, carry):
        src = tile(h_ref, r)
        pltpu.make_async_copy(src, tile(xs_hbm, pos_ref[0, 0, r]), sem).start(priority=0)
        pltpu.make_async_copy(src, tile(xs_hbm, pos_ref[0, 0, td + r]), sem).start(priority=1)
        return carry

    lax.fori_loop(0, td, body, 0, unroll=8)
    for _ in range(2):
        pltpu.make_async_copy(h_ref, xs_hbm.at[pl.ds(0, td * TILE_ROWS)], sem).wait()


def _dispatch(last_blk, pos_blk, h3_tiles, n_blocks):
    n_steps = pos_blk.shape[0]
    td = pos_blk.shape[2] // 2
    grid_spec = pltpu.PrefetchScalarGridSpec(
        num_scalar_prefetch=1,
        grid=(n_steps,),
        in_specs=[
            pl.BlockSpec((1, 1, 2 * td), lambda i, last: (i, 0, 0), memory_space=pltpu.SMEM),
            pl.BlockSpec((td * TILE_ROWS, LANES), lambda i, last: (i, 0)),
        ],
        out_specs=pl.BlockSpec(memory_space=pl.ANY),
        scratch_shapes=[pltpu.VMEM((MOE_ROWS * TILE_ROWS, LANES), F32), pltpu.SemaphoreType.DMA(())],
    )
    return pl.pallas_call(
        _dispatch_kernel,
        grid_spec=grid_spec,
        out_shape=jax.ShapeDtypeStruct((n_blocks * MOE_ROWS * TILE_ROWS, LANES), F32),
        compiler_params=_cparams(("arbitrary",)),
        name="moe_dispatch",
    )(last_blk, pos_blk, h3_tiles)


def _expert_kernel(be_ref, na_ref, x_ref, wg_ref, wu_ref, wd_ref, o_ref, wg_scr, wu_scr, wd_scr):
    j = pl.program_id(0)
    prev = be_ref[jnp.maximum(j - 1, 0)]

    @pl.when((j == 0) | (be_ref[j] != prev))
    def _():
        wg_scr[...] = wg_ref[0].astype(BF16)
        wu_scr[...] = wu_ref[0].astype(BF16)
        wd_scr[...] = wd_ref[0].astype(BF16)

    @pl.when(j < na_ref[0])
    def _():
        x = _load_token_tiles(x_ref, 0, MOE_ROWS).astype(BF16)
        g = jnp.dot(x, wg_scr[...], preferred_element_type=F32)
        u = jnp.dot(x, wu_scr[...], preferred_element_type=F32)
        hid = (_silu(g) * u).astype(BF16)
        _store_token_tiles(o_ref, jnp.dot(hid, wd_scr[...], preferred_element_type=F32))

    @pl.when(j >= na_ref[0])
    def _():
        o_ref[...] = jnp.zeros_like(o_ref)


def _experts(blk_exp, n_active, xs_tiles, wg, wu, wd):
    n_blocks = blk_exp.shape[0]
    d, ff = wg.shape[1], wg.shape[2]
    blk = (MOE_ROWS * TILE_ROWS, LANES)
    grid_spec = pltpu.PrefetchScalarGridSpec(
        num_scalar_prefetch=2,
        grid=(n_blocks,),
        in_specs=[
            pl.BlockSpec(blk, lambda j, be, na: (jnp.minimum(j, na[0] - 1), 0)),
            pl.BlockSpec((1, d, ff), lambda j, be, na: (be[j], 0, 0)),
            pl.BlockSpec((1, d, ff), lambda j, be, na: (be[j], 0, 0)),
            pl.BlockSpec((1, ff, d), lambda j, be, na: (be[j], 0, 0)),
        ],
        out_specs=pl.BlockSpec(blk, lambda j, be, na: (j, 0)),
        scratch_shapes=[pltpu.VMEM((d, ff), BF16), pltpu.VMEM((d, ff), BF16), pltpu.VMEM((ff, d), BF16)],
    )
    return pl.pallas_call(
        _expert_kernel,
        grid_spec=grid_spec,
        out_shape=jax.ShapeDtypeStruct(xs_tiles.shape, F32),
        compiler_params=_cparams(("arbitrary",)),
        name="moe_experts",
    )(blk_exp, n_active, xs_tiles, wg, wu, wd)


def _combine_kernel(n_first, pos_ref, x_ref, gate_ref, nw_ref, ys_hbm, oa_ref, ob_ref, y_scr, sem):
    tm = x_ref.shape[0]

    def tile(ref, tok):
        return ref.at[pl.ds(pl.multiple_of(tok * TILE_ROWS, TILE_ROWS), TILE_ROWS)]

    def body(r, carry):
        pltpu.make_async_copy(tile(ys_hbm, pos_ref[0, 0, r]), tile(y_scr, r), sem).start(priority=0)
        pltpu.make_async_copy(tile(ys_hbm, pos_ref[0, 0, tm + r]), tile(y_scr, tm + r), sem).start(priority=1)
        return carry

    lax.fori_loop(0, tm, body, 0, unroll=8)
    pltpu.make_async_copy(ys_hbm.at[pl.ds(0, 2 * tm * TILE_ROWS)], y_scr, sem).wait()
    gate = gate_ref[...]
    y = x_ref[...] + (gate[:, 0:1] * _load_token_tiles(y_scr, 0, tm) + gate[:, 1:2] * _load_token_tiles(y_scr, tm, tm))
    out = _rms(y, nw_ref[...])

    @pl.when(pl.program_id(0) < n_first)
    def _():
        oa_ref[...] = out

    @pl.when(pl.program_id(0) >= n_first)
    def _():
        ob_ref[...] = out


def _combine(pos_blk, x2, gates, nw, ys, tm, t_first):
    t, d = x2.shape
    na = t_first // tm
    return pl.pallas_call(
        functools.partial(_combine_kernel, na),
        grid=(t // tm,),
        in_specs=[
            pl.BlockSpec((1, 1, 2 * tm), lambda i: (i, 0, 0), memory_space=pltpu.SMEM),
            pl.BlockSpec((tm, d), lambda i: (i, 0)),
            pl.BlockSpec((tm, LANES), lambda i: (i, 0)),
            pl.BlockSpec((1, d), lambda i: (0, 0)),
            pl.BlockSpec(memory_space=pl.ANY),
        ],
        out_specs=[
            pl.BlockSpec((tm, d), lambda i: (_first_or_second(i, na)[0], 0)),
            pl.BlockSpec((tm, d), lambda i: (_first_or_second(i, na)[1], 0)),
        ],
        out_shape=[jax.ShapeDtypeStruct((t_first, d), F32), jax.ShapeDtypeStruct((t - t_first, d), F32)],
        scratch_shapes=[pltpu.VMEM((2 * tm * TILE_ROWS, LANES), F32), pltpu.SemaphoreType.DMA(())],
        compiler_params=_cparams(("arbitrary",)),
        name="moe_combine",
    )(pos_blk, x2, gates, nw, ys)


def _block_experts(counts, n_blocks):
    per_expert = (counts + MOE_ROWS - 1) // MOE_ROWS
    ends = jnp.cumsum(per_expert)
    blk = jnp.arange(n_blocks, dtype=jnp.int32)
    blk_exp = jnp.minimum(jnp.sum(blk[:, None] >= ends[None, :], axis=1), N_EXPERTS - 1).astype(jnp.int32)
    last_blk = jnp.where(per_expert > 0, ends - 1, -1)
    tail = n_blocks - 1 - jnp.arange(N_EXPERTS)
    to_zero = jnp.concatenate([last_blk, jnp.where(tail >= ends[-1], tail, -1)]).astype(jnp.int32)
    return blk_exp, ends[-1:].astype(jnp.int32), to_zero


def _pick(n, pref):
    t = pref
    while n % t:
        t //= 2
    return t


def _encoder(xa, xb, mem, p):
    (nba, s, d), nbb = xa.shape, xb.shape[0]
    nb = nba + nbb
    t = nb * s
    t_first = nba * s
    nc = s // SSD_CHUNK
    tm = _pick(math.gcd(t_first, t - t_first), 1024)
    tq = _pick(s, 512)
    xa = xa.reshape(t_first, d)
    xb = xb.reshape(t - t_first, d)

    perm = jnp.arange(SSD_HEADS).reshape(SSD_GROUPS, HEADS_PER_GROUP)
    perm = jnp.concatenate([perm, perm + SSD_HEADS], axis=1).reshape(-1)

    w_in = p['w_in']
    o_z = 2 * D_MODEL
    o_x = o_z + SSD_D_INNER
    o_dt = o_x + SSD_D_INNER + 2 * SSD_GROUPS * SSD_STATE
    o_g = o_dt + 2 * SSD_HEADS
    w_val = w_in[:, :D_MODEL].astype(BF16)
    w_gate = w_in[:, D_MODEL:o_z].astype(BF16)
    w_z = w_in[:, o_z:o_x].astype(BF16)
    w_xbc = w_in[:, o_x:o_dt].astype(BF16)
    w_dt = jnp.pad(w_in[:, o_dt:o_g][:, perm], ((0, 0), (0, LANES - 2 * SSD_HEADS))).astype(BF16)
    w_g = w_in[:, o_g:].astype(BF16)
    nmix = p['norm_mix_w'].reshape(1, d)

    h = _prenorm(xa, xb, nmix, tm)
    u = _glu_proj(h, w_val, w_gate, tm, PROJ_TN // 2)
    sz = _proj(h, w_z, _silu, BF16, tm, PROJ_TN, "z_proj")
    xbc = _proj(h, w_xbc, lambda a: a, BF16, tm, PROJ_TN, "xbc_proj")
    dt_raw = _proj(h, w_dt, lambda a: a, F32, tm, LANES, "dt_proj")
    gates = _proj(h, w_g, _sigmoid, BF16, tm, PROJ_TN, "gate_proj")

    pad = LANES - 2 * SSD_HEADS
    dt_bias = jnp.pad(jnp.concatenate([p['ssd_dt_bias_f'], p['ssd_dt_bias_b']])[perm], (0, pad)).reshape(1, LANES)
    a_log = jnp.pad(jnp.concatenate([p['ssd_a_log_f'], p['ssd_a_log_b']])[perm], (0, pad)).reshape(1, LANES)
    acs2, rowt, e_hi, e_lo, w_hi, w_lo, dec = _dtprep(dt_raw, dt_bias, a_log, tm)

    def per_group_tm(a):
        return a[:, :2 * SSD_HEADS].reshape(nb, s, SSD_GROUPS, 8).transpose(0, 2, 1, 3)

    def per_group_hm(a):
        return a[:, :2 * SSD_HEADS].reshape(nb, nc, SSD_CHUNK, SSD_GROUPS, 8).transpose(0, 3, 1, 4, 2)

    acs_tm = per_group_tm(acs2).reshape(nb * SSD_GROUPS, s, 8)
    ew_tm = jnp.concatenate([per_group_tm(v) for v in (e_hi, e_lo, w_hi, w_lo)], axis=-1)
    ew_tm = ew_tm.reshape(nb * SSD_GROUPS, s, EW_COLS)
    rowt_hm = per_group_hm(rowt)
    dec = dec[:, :2 * SSD_HEADS].reshape(nb, nc, SSD_GROUPS, 2, HEADS_PER_GROUP).transpose(0, 2, 1, 3, 4)
    dec = jnp.repeat(dec, SSD_HEAD_DIM, axis=-1)

    cwx = p['ssd_conv_w']
    cbx = p['ssd_conv_b']

    def per_group_conv(a):
        xs_ = a[:, :SSD_D_INNER].reshape(-1, SSD_GROUPS, GROUP_CH)
        b_ = a[:, SSD_D_INNER:SSD_D_INNER + SSD_GROUPS * SSD_STATE].reshape(-1, SSD_GROUPS, SSD_STATE)
        c_ = a[:, SSD_D_INNER + SSD_GROUPS * SSD_STATE:].reshape(-1, SSD_GROUPS, SSD_STATE)
        return jnp.concatenate([xs_, b_, c_], axis=-1).transpose(1, 0, 2)

    cw = jnp.pad(per_group_conv(cwx), ((0, 0), (0, 8 - SSD_CONV_WIDTH), (0, 0)))
    cb = per_group_conv(cbx.reshape(1, -1))
    dskip = jnp.repeat(p['ssd_d'], SSD_HEAD_DIM).reshape(SSD_GROUPS, 1, GROUP_CH)
    nw_ssd = p['ssd_norm_w'].reshape(SSD_GROUPS, 1, GROUP_CH)

    yn = _ssd(xbc.reshape(nb, s, -1), sz.reshape(nb, s, -1), cw, cb, rowt_hm, acs_tm, ew_tm, dec, dskip, nw_ssd)
    mb = _gated_proj(yn.reshape(t, SSD_D_INNER), p['ssd_out'].astype(BF16), gates, 1, tm)

    dww = jnp.pad(p['conv_dw_w'], ((0, 32 - CONV_WIDTH), (0, 0)))
    x1 = _conformer(u.reshape(nb, s, d), mb.reshape(nb, s, d), gates.reshape(nb, s, 2 * d), xa, xb, dww,
                    p['conv_dw_b'].reshape(1, d), p['conv_ln_w'].reshape(1, d), p['conv_ln_b'].reshape(1, d),
                    p['conv_pw_out'].astype(BF16), p['w_out'].astype(BF16), tq)

    m = mem.shape[1]
    kv = _norm_proj(mem.reshape(nb * m, d), p['norm_mem_w'].reshape(1, d), p['xattn_wkv'].astype(BF16),
                    lambda a: a, BF16, _pick(nb * m, 512), 1024, "kv_proj")
    n_r = N_GROUPS + N_EXPERTS
    wr = jnp.pad(jnp.concatenate([p['router_group_w'], p['router_expert_w']], axis=1),
                 ((0, 0), (0, LANES - n_r))).astype(BF16)
    br = jnp.pad(jnp.concatenate([p['router_group_b'], p['router_expert_b']]), (0, LANES - n_r)).reshape(1, LANES)
    x2, h3, logits = _attention(x1, kv.reshape(nb, m, 2 * d), p['norm_xattn_w'].reshape(1, d),
                                p['xattn_wq'].astype(BF16), p['xattn_wo'].astype(BF16),
                                p['norm_ffn_w'].reshape(1, d), wr, br, tq)

    tc = _pick(math.gcd(t_first, t - t_first), 512)
    gate, pos, cnt = _plan(logits.reshape(t, LANES), tc)
    n_blocks = (2 * t) // MOE_ROWS + N_EXPERTS
    blk_exp, n_active, last_blk = _block_experts(cnt[0, :N_EXPERTS].astype(jnp.int32), n_blocks)
    pos_blk = pos[:, :2].reshape(t // tc, tc, 2).transpose(0, 2, 1).reshape(t // tc, 1, 2 * tc)
    xs = _dispatch(last_blk, pos_blk, h3, n_blocks)
    ys = _experts(blk_exp, n_active, xs, p['expert_w_gate'], p['expert_w_up'], p['expert_w_down'])
    return _combine(pos_blk, x2.reshape(t, d), gate, p['norm_final_w'].reshape(1, d), ys, tc, t_first)


def kernel(x_prompt, x_sample, mem_prompt, mem_sample, norm_mix_w, w_in, conv_dw_w, conv_dw_b, conv_ln_w, conv_ln_b, conv_pw_out, ssd_conv_w, ssd_conv_b, ssd_dt_bias_f, ssd_dt_bias_b, ssd_a_log_f, ssd_a_log_b, ssd_d, ssd_norm_w, ssd_out, w_out, norm_xattn_w, norm_mem_w, xattn_wq, xattn_wkv, xattn_wo, norm_ffn_w, router_group_w, router_group_b, router_expert_w, router_expert_b, expert_w_gate, expert_w_up, expert_w_down, norm_final_w):
    p = {
        'norm_mix_w': norm_mix_w[0], 'w_in': w_in[0], 'conv_dw_w': conv_dw_w[0], 'conv_dw_b': conv_dw_b[0],
        'conv_ln_w': conv_ln_w[0], 'conv_ln_b': conv_ln_b[0], 'conv_pw_out': conv_pw_out[0],
        'ssd_conv_w': ssd_conv_w[0], 'ssd_conv_b': ssd_conv_b[0], 'ssd_dt_bias_f': ssd_dt_bias_f[0],
        'ssd_dt_bias_b': ssd_dt_bias_b[0], 'ssd_a_log_f': ssd_a_log_f[0], 'ssd_a_log_b': ssd_a_log_b[0],
        'ssd_d': ssd_d[0], 'ssd_norm_w': ssd_norm_w[0], 'ssd_out': ssd_out[0], 'w_out': w_out[0],
        'norm_xattn_w': norm_xattn_w[0], 'norm_mem_w': norm_mem_w[0], 'xattn_wq': xattn_wq[0],
        'xattn_wkv': xattn_wkv[0], 'xattn_wo': xattn_wo[0], 'norm_ffn_w': norm_ffn_w[0],
        'router_group_w': router_group_w[0], 'router_group_b': router_group_b[0],
        'router_expert_w': router_expert_w[0], 'router_expert_b': router_expert_b[0],
        'expert_w_gate': expert_w_gate[0], 'expert_w_up': expert_w_up[0], 'expert_w_down': expert_w_down[0],
        'norm_final_w': norm_final_w,
    }
    mem = jnp.concatenate([mem_prompt, mem_sample], axis=0)
    y_prompt, y_sample = _encoder(x_prompt, x_sample, mem, p)
    return (y_prompt.reshape(x_prompt.shape), y_sample.reshape(x_sample.shape))
```

```python
import functools
import math

import jax
import jax.numpy as jnp
from jax import lax
from jax.experimental import pallas as pl
from jax.experimental.pallas import tpu as pltpu

F32 = jnp.float32
BF16 = jnp.bfloat16

D_MODEL = 1024
CONV_WIDTH = 31
SSD_D_INNER = 2048
SSD_HEAD_DIM = 64
SSD_HEADS = 32
SSD_GROUPS = 8
SSD_STATE = 128
SSD_CONV_WIDTH = 5
SSD_CHUNK = 128
GROUP_CH = SSD_D_INNER // SSD_GROUPS
HEADS_PER_GROUP = SSD_HEADS // SSD_GROUPS
XATTN_HEADS = 4
XATTN_HEAD_DIM = D_MODEL // XATTN_HEADS
N_GROUPS = 8
EXPERTS_PER_GROUP = 8
N_EXPERTS = 64
EXPERT_FF = 512
MOE_BLOCK = 128
RMS_EPS = 1e-6
LN_EPS = 1e-5

LANES = 128
HALO = 16
VMEM_LIMIT = 56 * 1024 * 1024
PROJ_TN = 2048


def _cparams(sem):
    return pltpu.CompilerParams(dimension_semantics=sem, vmem_limit_bytes=VMEM_LIMIT)


def _rms(x, w):
    ms = jnp.mean(x * x, axis=-1, keepdims=True)
    return x * lax.rsqrt(ms + RMS_EPS) * w


def _sigmoid(x):
    return 1.0 / (1.0 + jnp.exp(-x))


def _silu(x):
    return x * _sigmoid(x)


def _softplus(x):
    return jnp.maximum(x, 0.0) + jnp.log1p(jnp.exp(-jnp.abs(x)))


CONV_STRIDE = 4
CONV_ROWS = 8 * CONV_STRIDE
CONV_UNROLL = 4


def _depthwise_conv(src, dst, slab, w_ref, b_ref, n_rows, width, first_row, epilogue):
    lanes = pl.ds(slab * LANES, LANES)
    taps = [jnp.broadcast_to(w_ref[k:k + 1, lanes], (8, LANES)) for k in range(width)]
    bias = jnp.broadcast_to(b_ref[:, lanes], (8, LANES))
    step = CONV_ROWS * CONV_UNROLL

    def body(r, carry):
        _conv_rows(src, dst, slab, taps, bias, r * step, step, width, first_row, epilogue)
        return carry

    lax.fori_loop(0, n_rows // step, body, 0)


def _conv_rows(src, dst, slab, taps, bias, base, n_rows, width, first_row, epilogue):
    for t0 in range(n_rows // 8):
        row = base + (t0 // CONV_STRIDE) * CONV_ROWS + t0 % CONV_STRIDE
        acc = bias
        for k in range(width):
            acc = acc + src[slab, pl.ds(row + first_row - width // 2 + k, 8, stride=CONV_STRIDE), :] * taps[k]
        dst[slab, pl.ds(row, 8, stride=CONV_STRIDE), :] = epilogue(acc)


def _first_or_second(i, n_first):
    return jnp.minimum(i, n_first - 1), jnp.maximum(i - n_first, 0)


def _prenorm_kernel(n_first, xa_ref, xb_ref, nw_ref, o_ref):
    x = jnp.where(pl.program_id(0) < n_first, xa_ref[...], xb_ref[...])
    o_ref[...] = _rms(x, nw_ref[...]).astype(o_ref.dtype)


def _prenorm(xa, xb, nw, tm):
    (ta, d), tb = xa.shape, xb.shape[0]
    na = ta // tm
    return pl.pallas_call(
        functools.partial(_prenorm_kernel, na),
        grid=((ta + tb) // tm,),
        in_specs=[
            pl.BlockSpec((tm, d), lambda i: (_first_or_second(i, na)[0], 0)),
            pl.BlockSpec((tm, d), lambda i: (_first_or_second(i, na)[1], 0)),
            pl.BlockSpec((1, d), lambda i: (0, 0)),
        ],
        out_specs=pl.BlockSpec((tm, d), lambda i: (i, 0)),
        out_shape=jax.ShapeDtypeStruct((ta + tb, d), BF16),
        compiler_params=_cparams(("arbitrary",)),
        name="mix_norm",
    )(xa, xb, nw)


def _glu_kernel(h_ref, wv_ref, wg_ref, o_ref):
    h = h_ref[...]
    v = jnp.dot(h, wv_ref[...], preferred_element_type=F32)
    g = jnp.dot(h, wg_ref[...], preferred_element_type=F32)
    o_ref[...] = (v * _sigmoid(g)).astype(o_ref.dtype)


def _glu_proj(h, wv, wg, tm, tn):
    t, d = h.shape
    n = wv.shape[1]
    return pl.pallas_call(
        _glu_kernel,
        grid=(t // tm, n // tn),
        in_specs=[
            pl.BlockSpec((tm, d), lambda i, j: (i, 0)),
            pl.BlockSpec((d, tn), lambda i, j: (0, j)),
            pl.BlockSpec((d, tn), lambda i, j: (0, j)),
        ],
        out_specs=pl.BlockSpec((tm, tn), lambda i, j: (i, j)),
        out_shape=jax.ShapeDtypeStruct((t, n), BF16),
        compiler_params=_cparams(("parallel", "arbitrary")),
        name="glu_proj",
    )(h, wv, wg)


def _proj_kernel(epi, h_ref, w_ref, o_ref):
    o_ref[...] = epi(jnp.dot(h_ref[...], w_ref[...], preferred_element_type=F32)).astype(o_ref.dtype)


def _proj(h, w, epi, out_dtype, tm, tn, name):
    t, d = h.shape
    n = w.shape[1]
    return pl.pallas_call(
        functools.partial(_proj_kernel, epi),
        grid=(t // tm, n // tn),
        in_specs=[
            pl.BlockSpec((tm, d), lambda i, j: (i, 0)),
            pl.BlockSpec((d, tn), lambda i, j: (0, j)),
        ],
        out_specs=pl.BlockSpec((tm, tn), lambda i, j: (i, j)),
        out_shape=jax.ShapeDtypeStruct((t, n), out_dtype),
        compiler_params=_cparams(("parallel", "arbitrary")),
        name=name,
    )(h, w)


def _norm_proj_kernel(epi, x_ref, nw_ref, w_ref, o_ref, h_scr):
    @pl.when(pl.program_id(1) == 0)
    def _():
        h_scr[...] = _rms(x_ref[...], nw_ref[...]).astype(BF16)

    acc = jnp.dot(h_scr[...], w_ref[...], preferred_element_type=F32)
    o_ref[...] = epi(acc).astype(o_ref.dtype)


def _norm_proj(x, nw, w, epi, out_dtype, tm, tn, name):
    t, d = x.shape
    n = w.shape[1]
    return pl.pallas_call(
        functools.partial(_norm_proj_kernel, epi),
        grid=(t // tm, n // tn),
        in_specs=[
            pl.BlockSpec((tm, d), lambda i, j: (i, 0)),
            pl.BlockSpec((1, d), lambda i, j: (0, 0)),
            pl.BlockSpec((d, tn), lambda i, j: (0, j)),
        ],
        out_specs=pl.BlockSpec((tm, tn), lambda i, j: (i, j)),
        out_shape=jax.ShapeDtypeStruct((t, n), out_dtype),
        scratch_shapes=[pltpu.VMEM((tm, d), BF16)],
        compiler_params=_cparams(("parallel", "arbitrary")),
        name=name,
    )(x, nw, w)


def _gated_proj_kernel(y_ref, w_ref, g_ref, o_ref):
    acc = jnp.dot(y_ref[...], w_ref[...], preferred_element_type=F32)
    o_ref[...] = (acc * g_ref[...].astype(F32)).astype(o_ref.dtype)


def _gated_proj(y, w, g, g_col_block, tm):
    t, k = y.shape
    n = w.shape[1]
    return pl.pallas_call(
        _gated_proj_kernel,
        grid=(t // tm,),
        in_specs=[
            pl.BlockSpec((tm, k), lambda i: (i, 0)),
            pl.BlockSpec((k, n), lambda i: (0, 0)),
            pl.BlockSpec((tm, n), lambda i: (i, g_col_block)),
        ],
        out_specs=pl.BlockSpec((tm, n), lambda i: (i, 0)),
        out_shape=jax.ShapeDtypeStruct((t, n), BF16),
        compiler_params=_cparams(("parallel",)),
        name="ssd_out_proj",
    )(y, w, g)


LOG2E = 1.4426950408889634


def _split_bf16(v):
    hi = v.astype(BF16)
    return hi, (v - hi.astype(F32)).astype(BF16)


def _dtprep_kernel(raw_ref, bias_ref, alog_ref, acs2_ref, rowt_ref, ehi_ref, elo_ref, whi_ref, wlo_ref, dec_ref):
    rows = raw_ref.shape[0]
    a_head = -jnp.exp(alog_ref[...])
    row = lax.broadcasted_iota(jnp.int32, (SSD_CHUNK, LANES), 0)
    lane = lax.broadcasted_iota(jnp.int32, (SSD_CHUNK, LANES), 1)
    is_bwd = (lane % (2 * HEADS_PER_GROUP)) >= HEADS_PER_GROUP
    for c in range(rows // SSD_CHUNK):
        sl = pl.ds(c * SSD_CHUNK, SSD_CHUNK)
        dt = _softplus(raw_ref[sl, :] + bias_ref[...])
        a = dt * a_head
        fwd = a
        bwd = a
        sh = 1
        while sh < SSD_CHUNK:
            fwd = fwd + jnp.where(row >= sh, pltpu.roll(fwd, sh, 0), 0.0)
            bwd = bwd + jnp.where(row < SSD_CHUNK - sh, pltpu.roll(bwd, SSD_CHUNK - sh, 0), 0.0)
            sh *= 2
        acs = jnp.where(is_bwd, bwd, fwd)
        a_end = jnp.where(is_bwd[0:1, :], bwd[0:1, :], fwd[SSD_CHUNK - 1:SSD_CHUNK, :])
        acs2 = acs * LOG2E
        acs2_ref[sl, :] = acs2
        rowt_ref[sl, :] = acs2 - jnp.log(dt) * LOG2E
        ehi_ref[sl, :], elo_ref[sl, :] = _split_bf16(jnp.exp(acs))
        whi_ref[sl, :], wlo_ref[sl, :] = _split_bf16(jnp.exp(a_end - acs) * dt)
        dec_ref[c:c + 1, :] = jnp.exp(a_end)


def _dtprep(raw, bias, alog, tm):
    t = raw.shape[0]
    spec = pl.BlockSpec((tm, LANES), lambda i: (i, 0))
    vec = pl.BlockSpec((1, LANES), lambda i: (0, 0))
    return pl.pallas_call(
        _dtprep_kernel,
        grid=(t // tm,),
        in_specs=[spec, vec, vec],
        out_specs=[spec] * 6 + [pl.BlockSpec((tm // SSD_CHUNK, LANES), lambda i: (i, 0))],
        out_shape=[jax.ShapeDtypeStruct((t, LANES), F32)] * 2 + [jax.ShapeDtypeStruct((t, LANES), BF16)] * 4
        + [jax.ShapeDtypeStruct((t // SSD_CHUNK, LANES), F32)],
        compiler_params=_cparams(("parallel",)),
        name="dt_prep",
    )(raw, bias, alog)


SSD_PAD = 8
XBC_W = GROUP_CH + 2 * SSD_STATE


EW_COLS = 4 * 2 * HEADS_PER_GROUP


def _head_expand_matrix():
    col = jnp.arange(4 * GROUP_CH)
    part = col // GROUP_CH
    is_w = (part == 1) | (part == 2)
    is_bwd = part >= 2
    head = is_bwd * HEADS_PER_GROUP + (col % GROUP_CH) // SSD_HEAD_DIM
    hi_row = is_w * 16 + head
    row = jnp.arange(EW_COLS)[:, None]
    return ((row == hi_row[None, :]) | (row == hi_row[None, :] + 8)).astype(BF16)


def _ssd_kernel(xs_ref, b_ref, c_ref, h_ref, wz_ref, cw_ref, cb_ref, rowt_ref, acs_ref, ew_ref, sel_ref, dec_ref,
                dskip_ref, nw_ref, o_ref, raw_scr, act_scr, y_scr, upd_scr, st_scr):
    s = xs_ref.shape[1]
    nc = s // SSD_CHUNK
    L = SSD_CHUNK

    n_slabs = XBC_W // LANES
    for j in range(n_slabs):
        raw_scr[j, pl.ds(0, SSD_PAD), :] = jnp.zeros((SSD_PAD, LANES), F32)
        raw_scr[j, pl.ds(SSD_PAD + s, L + SSD_PAD), :] = jnp.zeros((L + SSD_PAD, LANES), F32)
    raw_scr[0, pl.ds(SSD_PAD, s), :] = xs_ref[0, :, :LANES].astype(F32)
    raw_scr[1, pl.ds(SSD_PAD, s), :] = xs_ref[0, :, LANES:].astype(F32)
    raw_scr[2, pl.ds(SSD_PAD, s), :] = b_ref[0].astype(F32)
    raw_scr[3, pl.ds(SSD_PAD, s), :] = c_ref[0].astype(F32)

    def conv_chunk(c):
        for j in range(n_slabs):
            lanes = pl.ds(j * LANES, LANES)
            taps = [jnp.broadcast_to(cw_ref[0, k:k + 1, lanes], (8, LANES)) for k in range(SSD_CONV_WIDTH)]
            bias = jnp.broadcast_to(cb_ref[0, :, lanes], (8, LANES))
            _conv_rows(raw_scr, act_scr, j, taps, bias, c * L, L, SSD_CONV_WIDTH, SSD_PAD, _silu)

    conv_chunk(0)

    li = lax.broadcasted_iota(jnp.int32, (L, L), 0)
    si = lax.broadcasted_iota(jnp.int32, (L, L), 1)
    head_of_ch = lax.broadcasted_iota(jnp.int32, (L, GROUP_CH), 1) // SSD_HEAD_DIM

    keep = (li >= si, si >= li)
    neg_inf = jnp.float32(-jnp.inf)
    n_heads = 2 * HEADS_PER_GROUP
    st_scr[...] = jnp.zeros_like(st_scr)

    def local_body(c, carry):
        r0 = pl.multiple_of(c * L, L)
        x = jnp.concatenate([act_scr[0, pl.ds(r0, L), :], act_scr[1, pl.ds(r0, L), :]], axis=1)
        xb = x.astype(BF16)
        bk = act_scr[2, pl.ds(r0, L), :].astype(BF16)
        ck = act_scr[3, pl.ds(r0, L), :].astype(BF16)
        acs = acs_ref[0, pl.ds(r0, L), :]
        rowt = rowt_ref[0, 0, c]
        ew = jnp.dot(ew_ref[0, pl.ds(r0, L), :], sel_ref[:, :3 * GROUP_CH],
                     preferred_element_type=F32)

        cb = lax.dot_general(ck, bk, (((1,), (1,)), ((), ())), preferred_element_type=F32)
        ms = []
        for j in range(n_heads):
            diff = acs[:, j:j + 1] - rowt[j:j + 1, :]
            decay_dt = jnp.exp2(jnp.where(keep[j // HEADS_PER_GROUP], diff, neg_inf))
            ms.append((cb * decay_dt).astype(BF16))
        xbd = jnp.concatenate([jnp.where(head_of_ch == h, xb, jnp.zeros_like(xb))
                               for h in range(HEADS_PER_GROUP)], axis=0)
        y = jnp.dot(jnp.concatenate(ms, axis=1), jnp.concatenate([xbd, xbd], axis=0),
                    preferred_element_type=F32) + x * dskip_ref[0]

        xw = jnp.concatenate([x * ew[:, GROUP_CH:2 * GROUP_CH], x * ew[:, 2 * GROUP_CH:]], axis=1)
        upd = lax.dot_general(bk, xw.astype(BF16), (((0,), (0,)), ((), ())), preferred_element_type=F32)

        state = st_scr[0]
        y = y + jnp.dot(ck, state.astype(BF16), preferred_element_type=F32) * ew[:, :GROUP_CH]
        st_scr[0] = state * dec_ref[0, 0, c, 0:1, :] + upd[:, :GROUP_CH]
        y_scr[pl.ds(r0, L), :] = y
        upd_scr[c] = upd[:, GROUP_CH:]
        conv_chunk(c + 1)
        return carry

    lax.fori_loop(0, nc, local_body, 0, unroll=2)

    def bwd_body(i, carry):
        c = nc - 1 - i
        r0 = pl.multiple_of(c * L, L)
        ck = act_scr[3, pl.ds(r0, L), :].astype(BF16)
        e_bwd = jnp.dot(ew_ref[0, pl.ds(r0, L), :], sel_ref[:, 3 * GROUP_CH:], preferred_element_type=F32)
        state = st_scr[1]
        y = y_scr[pl.ds(r0, L), :] + jnp.dot(ck, state.astype(BF16), preferred_element_type=F32) * e_bwd
        st_scr[1] = state * dec_ref[0, 0, c, 1:2, :] + upd_scr[c]
        z = jnp.dot(h_ref[0, pl.ds(r0, L), :], wz_ref[...], preferred_element_type=F32)
        yz = y * _silu(z)
        ms = jnp.mean(yz * yz, axis=-1, keepdims=True)
        o_ref[0, pl.ds(r0, L), :] = (yz * lax.rsqrt(ms + RMS_EPS) * nw_ref[0]).astype(o_ref.dtype)
        return carry

    lax.fori_loop(0, nc, bwd_body, 0, unroll=4)


def _ssd(xbc, h, wz, cw, cb, rowt_hm, acs_tm, ew_tm, dec, dskip, nw):
    nb, s, _ = xbc.shape
    d = h.shape[2]
    nc = s // SSD_CHUNK
    return pl.pallas_call(
        _ssd_kernel,
        grid=(nb, SSD_GROUPS),
        in_specs=[
            pl.BlockSpec((1, s, GROUP_CH), lambda b, g: (b, 0, g)),
            pl.BlockSpec((1, s, SSD_STATE), lambda b, g: (b, 0, SSD_D_INNER // SSD_STATE + g)),
            pl.BlockSpec((1, s, SSD_STATE), lambda b, g: (b, 0, SSD_D_INNER // SSD_STATE + SSD_GROUPS + g)),
            pl.BlockSpec((1, s, d), lambda b, g: (b, 0, 0)),
            pl.BlockSpec((d, GROUP_CH), lambda b, g: (0, g)),
            pl.BlockSpec((1, 8, XBC_W), lambda b, g: (g, 0, 0)),
            pl.BlockSpec((1, 1, XBC_W), lambda b, g: (g, 0, 0)),
            pl.BlockSpec((1, 1, nc, 8, SSD_CHUNK), lambda b, g: (b, g, 0, 0, 0)),
            pl.BlockSpec((1, s, 8), lambda b, g: (b * SSD_GROUPS + g, 0, 0)),
            pl.BlockSpec((1, s, EW_COLS), lambda b, g: (b * SSD_GROUPS + g, 0, 0)),
            pl.BlockSpec((EW_COLS, 4 * GROUP_CH), lambda b, g: (0, 0)),
            pl.BlockSpec((1, 1, nc, 2, GROUP_CH), lambda b, g: (b, g, 0, 0, 0)),
            pl.BlockSpec((1, 1, GROUP_CH), lambda b, g: (g, 0, 0)),
            pl.BlockSpec((1, 1, GROUP_CH), lambda b, g: (g, 0, 0)),
        ],
        out_specs=pl.BlockSpec((1, s, GROUP_CH), lambda b, g: (b, 0, g)),
        out_shape=jax.ShapeDtypeStruct((nb, s, SSD_D_INNER), BF16),
        scratch_shapes=[
            pltpu.VMEM((XBC_W // LANES, s + SSD_CHUNK + 2 * SSD_PAD, LANES), F32),
            pltpu.VMEM((XBC_W // LANES, s + SSD_CHUNK, LANES), F32),
            pltpu.VMEM((s, GROUP_CH), F32),
            pltpu.VMEM((nc, SSD_STATE, GROUP_CH), F32),
            pltpu.VMEM((2, SSD_STATE, GROUP_CH), F32),
        ],
        compiler_params=_cparams(("parallel", "parallel")),
        name="ssd_scan",
    )(xbc, xbc, xbc, h, wz, cw, cb, rowt_hm, acs_tm, ew_tm, _head_expand_matrix(), dec, dskip, nw)


D_SLABS = D_MODEL // LANES


def _conformer_kernel(n_first, u_ref, up_ref, un_ref, mb_ref, g_ref, xa_ref, xb_ref, dww_ref, dwb_ref, lnw_ref,
                      lnb_ref, pw_ref, wo_ref, o_ref, u_scr, c_scr):
    i = pl.program_id(1)
    n = pl.num_programs(1)
    tq = u_ref.shape[1]
    x = jnp.where(pl.program_id(0) * n + i < n_first, xa_ref[...], xb_ref[...])
    for j in range(D_SLABS):
        cs = slice(j * LANES, (j + 1) * LANES)
        u_scr[j, pl.ds(0, HALO), :] = jnp.where(i > 0, up_ref[0, :, cs].astype(F32), 0.0)
        u_scr[j, pl.ds(HALO, tq), :] = u_ref[0, :, cs].astype(F32)
        u_scr[j, pl.ds(HALO + tq, HALO), :] = jnp.where(i < n - 1, un_ref[0, :, cs].astype(F32), 0.0)
    for j in range(D_SLABS):
        _depthwise_conv(u_scr, c_scr, j, dww_ref, dwb_ref, tq, CONV_WIDTH, HALO, lambda a: a)

    cv = jnp.concatenate([c_scr[j] for j in range(D_SLABS)], axis=1)
    mu = jnp.mean(cv, axis=-1, keepdims=True)
    xc = cv - mu
    var = jnp.mean(xc * xc, axis=-1, keepdims=True)
    y = xc * lax.rsqrt(var + LN_EPS) * lnw_ref[...] + lnb_ref[...]
    a = jnp.dot(_silu(y).astype(BF16), pw_ref[...], preferred_element_type=F32)
    merged = g_ref[0].astype(F32) * a + mb_ref[0].astype(F32)
    o_ref[0] = x + jnp.dot(merged.astype(BF16), wo_ref[...], preferred_element_type=F32)


def _conformer(u, mb, gates, xa, xb, dww, dwb, lnw, lnb, pw, wo, tq):
    nb, s, d = u.shape
    hb = tq // HALO
    nh = s // HALO
    nq = s // tq
    na = xa.shape[0] // tq
    wspec = pl.BlockSpec((d, d), lambda b, i: (0, 0))
    vspec = pl.BlockSpec((1, d), lambda b, i: (0, 0))
    return pl.pallas_call(
        functools.partial(_conformer_kernel, na),
        grid=(nb, nq),
        in_specs=[
            pl.BlockSpec((1, tq, d), lambda b, i: (b, i, 0)),
            pl.BlockSpec((1, HALO, d), lambda b, i: (b, jnp.maximum(i * hb - 1, 0), 0)),
            pl.BlockSpec((1, HALO, d), lambda b, i: (b, jnp.minimum((i + 1) * hb, nh - 1), 0)),
            pl.BlockSpec((1, tq, d), lambda b, i: (b, i, 0)),
            pl.BlockSpec((1, tq, d), lambda b, i: (b, i, 0)),
            pl.BlockSpec((tq, d), lambda b, i: (_first_or_second(b * nq + i, na)[0], 0)),
            pl.BlockSpec((tq, d), lambda b, i: (_first_or_second(b * nq + i, na)[1], 0)),
            pl.BlockSpec((32, d), lambda b, i: (0, 0)),
            vspec, vspec, vspec, wspec, wspec,
        ],
        out_specs=pl.BlockSpec((1, tq, d), lambda b, i: (b, i, 0)),
        out_shape=jax.ShapeDtypeStruct((nb, s, d), F32),
        scratch_shapes=[pltpu.VMEM((D_SLABS, tq + 2 * HALO, LANES), F32), pltpu.VMEM((D_SLABS, tq, LANES), F32)],
        compiler_params=_cparams(("arbitrary", "arbitrary")),
        name="conformer_merge",
    )(u, u, u, mb, gates, xa, xb, dww, dwb, lnw, lnb, pw, wo)


TILE_ROWS = D_MODEL // LANES


def _store_token_tiles(ref, val):
    n = val.shape[0]
    for j in range(TILE_ROWS):
        ref[pl.ds(j, n, stride=TILE_ROWS), :] = val[:, j * LANES:(j + 1) * LANES]


def _load_token_tiles(ref, first, n):
    return jnp.concatenate([ref[pl.ds(first * TILE_ROWS + j, n, stride=TILE_ROWS), :] for j in range(TILE_ROWS)],
                           axis=1)


def _attn_kernel(x_ref, k_ref, v_ref, nw_ref, wq_ref, wo_ref, nf_ref, wr_ref, br_ref, x2_ref, h3_ref, lg_ref):
    x = x_ref[0]
    h = _rms(x, nw_ref[...]).astype(BF16)
    q = jnp.dot(h, wq_ref[...], preferred_element_type=F32)
    scale = 1.0 / math.sqrt(XATTN_HEAD_DIM)
    outs = []
    for hd in range(XATTN_HEADS):
        cs = slice(hd * XATTN_HEAD_DIM, (hd + 1) * XATTN_HEAD_DIM)
        sc = lax.dot_general(q[:, cs].astype(BF16), k_ref[0, :, cs], (((1,), (1,)), ((), ())),
                             preferred_element_type=F32) * scale
        m = jnp.max(sc, axis=-1, keepdims=True)
        e = jnp.exp(sc - m)
        p = e / jnp.sum(e, axis=-1, keepdims=True)
        outs.append(jnp.dot(p.astype(BF16), v_ref[0, :, cs], preferred_element_type=F32))
    o = jnp.concatenate(outs, axis=1).astype(BF16)
    x2 = x + jnp.dot(o, wo_ref[...], preferred_element_type=F32)
    x2_ref[0] = x2
    h3 = _rms(x2, nf_ref[...])
    _store_token_tiles(h3_ref, h3)
    lg_ref[0] = jnp.dot(h3.astype(BF16), wr_ref[...], preferred_element_type=F32) + br_ref[...]


def _attention(x, kv, nw, wq, wo, nf, wr, br, tq):
    nb, s, d = x.shape
    m = kv.shape[1]
    wspec = pl.BlockSpec((d, d), lambda b, i: (0, 0))
    vspec = pl.BlockSpec((1, d), lambda b, i: (0, 0))
    return pl.pallas_call(
        _attn_kernel,
        grid=(nb, s // tq),
        in_specs=[
            pl.BlockSpec((1, tq, d), lambda b, i: (b, i, 0)),
            pl.BlockSpec((1, m, d), lambda b, i: (b, 0, 0)),
            pl.BlockSpec((1, m, d), lambda b, i: (b, 0, 1)),
            vspec, wspec, wspec, vspec,
            pl.BlockSpec((d, LANES), lambda b, i: (0, 0)),
            pl.BlockSpec((1, LANES), lambda b, i: (0, 0)),
        ],
        out_specs=[
            pl.BlockSpec((1, tq, d), lambda b, i: (b, i, 0)),
            pl.BlockSpec((tq * TILE_ROWS, LANES), lambda b, i: (b * (s // tq) + i, 0)),
            pl.BlockSpec((1, tq, LANES), lambda b, i: (b, i, 0)),
        ],
        out_shape=[
            jax.ShapeDtypeStruct((nb, s, d), F32),
            jax.ShapeDtypeStruct((nb * s * TILE_ROWS, LANES), F32),
            jax.ShapeDtypeStruct((nb, s, LANES), F32),
        ],
        compiler_params=_cparams(("parallel", "parallel")),
        name="cross_attn_router",
    )(x, kv, kv, nw, wq, wo, nf, wr, br)


MOE_ROWS = 512


def _route_math(lg, lane):
    lane_f = lane.astype(F32)
    big = float(LANES)
    neg = -jnp.inf

    is_g = lane < N_GROUPS
    gl = jnp.where(is_g, lg, neg)
    gmax = jnp.max(gl, axis=-1, keepdims=True)
    g_idx = jnp.min(jnp.where(gl == gmax, lane_f, big), axis=-1, keepdims=True)
    g_sum = jnp.sum(jnp.where(is_g, jnp.exp(gl - gmax), 0.0), axis=-1, keepdims=True)
    g_w = 1.0 / g_sum

    e_grp = lax.shift_right_arithmetic(lane - N_GROUPS, 3).astype(F32)
    is_e = (lane >= N_GROUPS) & (lane < N_GROUPS + N_EXPERTS) & (e_grp == g_idx)
    el = jnp.where(is_e, lg, neg)
    emax = jnp.max(el, axis=-1, keepdims=True)
    ee = jnp.where(is_e, jnp.exp(el - emax), 0.0)
    ep = ee / jnp.sum(ee, axis=-1, keepdims=True)
    ep = jnp.where(is_e, ep, -1.0)
    v1 = jnp.max(ep, axis=-1, keepdims=True)
    i1 = jnp.min(jnp.where(ep == v1, lane_f, big), axis=-1, keepdims=True)
    ep2 = jnp.where(lane_f == i1, -1.0, ep)
    v2 = jnp.max(ep2, axis=-1, keepdims=True)
    i2 = jnp.min(jnp.where(ep2 == v2, lane_f, big), axis=-1, keepdims=True)
    tot = v1 + v2
    gate1 = g_w * (v1 / tot)
    gate2 = g_w * (v2 / tot)
    return i1 - N_GROUPS, i2 - N_GROUPS, gate1, gate2


def _plan_kernel(lg_ref, gate_ref, pos_ref, cnt_ref, carry_scr, base_scr):
    p = pl.program_id(0)
    i = pl.program_id(1)
    tm = lg_ref.shape[0]
    lane = lax.broadcasted_iota(jnp.int32, (tm, LANES), 1)
    lane_f = lane.astype(F32)
    e1, e2, gate1, gate2 = _route_math(lg_ref[...], lane)
    oh1 = lane_f == e1
    oh2 = lane_f == e2
    m = jnp.where(oh1 | oh2, 1.0, 0.0)
    colsum = jnp.sum(m, axis=0, keepdims=True)

    @pl.when((p == 0) & (i == 0))
    def _():
        carry_scr[...] = jnp.zeros_like(carry_scr)

    @pl.when(p == 0)
    def _():
        carry_scr[...] += colsum

    @pl.when((p == 1) & (i == 0))
    def _():
        counts = carry_scr[...]
        cnt_ref[...] = counts
        shift = MOE_ROWS.bit_length() - 1
        blocks = lax.shift_right_logical(counts.astype(jnp.int32) + (MOE_ROWS - 1), shift)
        padded = (blocks * MOE_ROWS).astype(F32)
        lane8 = lax.broadcasted_iota(jnp.int32, padded.shape, 1)
        inc = padded
        sh = 1
        while sh < LANES:
            inc = inc + jnp.where(lane8 >= sh, pltpu.roll(inc, sh, 1), 0.0)
            sh *= 2
        base_scr[...] = inc - padded
        carry_scr[...] = jnp.zeros_like(carry_scr)

    @pl.when(p == 1)
    def _():
        r = lax.broadcasted_iota(jnp.int32, (tm, tm), 0)
        c = lax.broadcasted_iota(jnp.int32, (tm, tm), 1)
        earlier = jnp.where(r > c, 1.0, 0.0).astype(BF16)
        tot = (jnp.dot(earlier, m.astype(BF16), preferred_element_type=F32)
               + base_scr[0:1, :] + carry_scr[0:1, :])
        pos1 = jnp.sum(jnp.where(oh1, tot, 0.0), axis=1, keepdims=True)
        pos2 = jnp.sum(jnp.where(oh2, tot, 0.0), axis=1, keepdims=True)
        pos_ref[...] = jnp.where(lane == 0, pos1, jnp.where(lane == 1, pos2, 0.0)).astype(jnp.int32)
        gate_ref[...] = jnp.where(lane == 0, gate1, jnp.where(lane == 1, gate2, 0.0))
        carry_scr[...] += colsum


def _plan(lg, tm):
    t = lg.shape[0]
    out_spec = pl.BlockSpec((tm, LANES), lambda p, i: (i * p, 0))
    return pl.pallas_call(
        _plan_kernel,
        grid=(2, t // tm),
        in_specs=[pl.BlockSpec((tm, LANES), lambda p, i: (i, 0))],
        out_specs=[out_spec, out_spec, pl.BlockSpec((8, LANES), lambda p, i: (0, 0))],
        out_shape=[jax.ShapeDtypeStruct((t, LANES), F32), jax.ShapeDtypeStruct((t, LANES), jnp.int32),
                   jax.ShapeDtypeStruct((8, LANES), F32)],
        scratch_shapes=[pltpu.VMEM((8, LANES), F32), pltpu.VMEM((8, LANES), F32)],
        compiler_params=_cparams(("arbitrary", "arbitrary")),
        name="moe_plan",
    )(lg)


def _dispatch_kernel(last_ref, pos_ref, h_ref, xs_hbm, zero_scr, sem):
    td = pos_ref.shape[2] // 2

    @pl.when(pl.program_id(0) == 0)
    def _():
        zero_scr[...] = jnp.zeros_like(zero_scr)

        def block_copy(e):
            first = pl.multiple_of(last_ref[e] * (MOE_ROWS * TILE_ROWS), MOE_ROWS * TILE_ROWS)
            return pltpu.make_async_copy(zero_scr, xs_hbm.at[pl.ds(first, MOE_ROWS * TILE_ROWS)], sem)

        def fill(e, carry):
            @pl.when(last_ref[e] >= 0)
            def _():
                block_copy(e).start()
            return carry

        def drain(e, carry):
            @pl.when(last_ref[e] >= 0)
            def _():
                block_copy(e).wait()
            return carry

        lax.fori_loop(0, last_ref.shape[0], fill, 0)
        lax.fori_loop(0, last_ref.shape[0], drain, 0)

    def tile(ref, tok):
        return ref.at[pl.ds(pl.multiple_of(tok * TILE_ROWS, TILE_ROWS), TILE_ROWS)]

    def body(r, carry):
        src = tile(h_ref, r)
        pltpu.make_async_copy(src, tile(xs_hbm, pos_ref[0, 0, r]), sem).start(priority=0)
        pltpu.make_async_copy(src, tile(xs_hbm, pos_ref[0, 0, td + r]), sem).start(priority=1)
        return carry

    lax.fori_loop(0, td, body, 0, unroll=8)
    for _ in range(2):
        pltpu.make_async_copy(h_ref, xs_hbm.at[pl.ds(0, td * TILE_ROWS)], sem).wait()


def _dispatch(last_blk, pos_blk, h3_tiles, n_blocks):
    n_steps = pos_blk.shape[0]
    td = pos_blk.shape[2] // 2
    grid_spec = pltpu.PrefetchScalarGridSpec(
        num_scalar_prefetch=1,
        grid=(n_steps,),
        in_specs=[
            pl.BlockSpec((1, 1, 2 * td), lambda i, last: (i, 0, 0), memory_space=pltpu.SMEM),
            pl.BlockSpec((td * TILE_ROWS, LANES), lambda i, last: (i, 0)),
        ],
        out_specs=pl.BlockSpec(memory_space=pl.ANY),
        scratch_shapes=[pltpu.VMEM((MOE_ROWS * TILE_ROWS, LANES), F32), pltpu.SemaphoreType.DMA(())],
    )
    return pl.pallas_call(
        _dispatch_kernel,
        grid_spec=grid_spec,
        out_shape=jax.ShapeDtypeStruct((n_blocks * MOE_ROWS * TILE_ROWS, LANES), F32),
        compiler_params=_cparams(("arbitrary",)),
        name="moe_dispatch",
    )(last_blk, pos_blk, h3_tiles)


def _expert_kernel(be_ref, na_ref, x_ref, wg_ref, wu_ref, wd_ref, o_ref, wg_scr, wu_scr, wd_scr):
    j = pl.program_id(0)
    prev = be_ref[jnp.maximum(j - 1, 0)]

    @pl.when((j == 0) | (be_ref[j] != prev))
    def _():
        wg_scr[...] = wg_ref[0].astype(BF16)
        wu_scr[...] = wu_ref[0].astype(BF16)
        wd_scr[...] = wd_ref[0].astype(BF16)

    @pl.when(j < na_ref[0])
    def _():
        x = _load_token_tiles(x_ref, 0, MOE_ROWS).astype(BF16)
        g = jnp.dot(x, wg_scr[...], preferred_element_type=F32)
        u = jnp.dot(x, wu_scr[...], preferred_element_type=F32)
        hid = (_silu(g) * u).astype(BF16)
        _store_token_tiles(o_ref, jnp.dot(hid, wd_scr[...], preferred_element_type=F32))

    @pl.when(j >= na_ref[0])
    def _():
        o_ref[...] = jnp.zeros_like(o_ref)


def _experts(blk_exp, n_active, xs_tiles, wg, wu, wd):
    n_blocks = blk_exp.shape[0]
    d, ff = wg.shape[1], wg.shape[2]
    blk = (MOE_ROWS * TILE_ROWS, LANES)
    grid_spec = pltpu.PrefetchScalarGridSpec(
        num_scalar_prefetch=2,
        grid=(n_blocks,),
        in_specs=[
            pl.BlockSpec(blk, lambda j, be, na: (jnp.minimum(j, na[0] - 1), 0)),
            pl.BlockSpec((1, d, ff), lambda j, be, na: (be[j], 0, 0)),
            pl.BlockSpec((1, d, ff), lambda j, be, na: (be[j], 0, 0)),
            pl.BlockSpec((1, ff, d), lambda j, be, na: (be[j], 0, 0)),
        ],
        out_specs=pl.BlockSpec(blk, lambda j, be, na: (j, 0)),
        scratch_shapes=[pltpu.VMEM((d, ff), BF16), pltpu.VMEM((d, ff), BF16), pltpu.VMEM((ff, d), BF16)],
    )
    return pl.pallas_call(
        _expert_kernel,
        grid_spec=grid_spec,
        out_shape=jax.ShapeDtypeStruct(xs_tiles.shape, F32),
        compiler_params=_cparams(("arbitrary",)),
        name="moe_experts",
    )(blk_exp, n_active, xs_tiles, wg, wu, wd)


def _combine_kernel(n_first, pos_ref, x_ref, gate_ref, nw_ref, ys_hbm, oa_ref, ob_ref, y_scr, sem):
    tm = x_ref.shape[0]

    def tile(ref, tok):
        return ref.at[pl.ds(pl.multiple_of(tok * TILE_ROWS, TILE_ROWS), TILE_ROWS)]

    def body(r, carry):
        pltpu.make_async_copy(tile(ys_hbm, pos_ref[0, 0, r]), tile(y_scr, r), sem).start(priority=0)
        pltpu.make_async_copy(tile(ys_hbm, pos_ref[0, 0, tm + r]), tile(y_scr, tm + r), sem).start(priority=1)
        return carry

    lax.fori_loop(0, tm, body, 0, unroll=8)
    pltpu.make_async_copy(ys_hbm.at[pl.ds(0, 2 * tm * TILE_ROWS)], y_scr, sem).wait()
    gate = gate_ref[...]
    y = x_ref[...] + (gate[:, 0:1] * _load_token_tiles(y_scr, 0, tm) + gate[:, 1:2] * _load_token_tiles(y_scr, tm, tm))
    out = _rms(y, nw_ref[...])

    @pl.when(pl.program_id(0) < n_first)
    def _():
        oa_ref[...] = out

    @pl.when(pl.program_id(0) >= n_first)
    def _():
        ob_ref[...] = out


def _combine(pos_blk, x2, gates, nw, ys, tm, t_first):
    t, d = x2.shape
    na = t_first // tm
    return pl.pallas_call(
        functools.partial(_combine_kernel, na),
        grid=(t // tm,),
        in_specs=[
            pl.BlockSpec((1, 1, 2 * tm), lambda i: (i, 0, 0), memory_space=pltpu.SMEM),
            pl.BlockSpec((tm, d), lambda i: (i, 0)),
            pl.BlockSpec((tm, LANES), lambda i: (i, 0)),
            pl.BlockSpec((1, d), lambda i: (0, 0)),
            pl.BlockSpec(memory_space=pl.ANY),
        ],
        out_specs=[
            pl.BlockSpec((tm, d), lambda i: (_first_or_second(i, na)[0], 0)),
            pl.BlockSpec((tm, d), lambda i: (_first_or_second(i, na)[1], 0)),
        ],
        out_shape=[jax.ShapeDtypeStruct((t_first, d), F32), jax.ShapeDtypeStruct((t - t_first, d), F32)],
        scratch_shapes=[pltpu.VMEM((2 * tm * TILE_ROWS, LANES), F32), pltpu.SemaphoreType.DMA(())],
        compiler_params=_cparams(("arbitrary",)),
        name="moe_combine",
    )(pos_blk, x2, gates, nw, ys)


def _block_experts(counts, n_blocks):
    per_expert = (counts + MOE_ROWS - 1) // MOE_ROWS
    ends = jnp.cumsum(per_expert)
    blk = jnp.arange(n_blocks, dtype=jnp.int32)
    blk_exp = jnp.minimum(jnp.sum(blk[:, None] >= ends[None, :], axis=1), N_EXPERTS - 1).astype(jnp.int32)
    last_blk = jnp.where(per_expert > 0, ends - 1, -1)
    tail = n_blocks - 1 - jnp.arange(N_EXPERTS)
    to_zero = jnp.concatenate([last_blk, jnp.where(tail >= ends[-1], tail, -1)]).astype(jnp.int32)
    return blk_exp, ends[-1:].astype(jnp.int32), to_zero


def _pick(n, pref):
    t = pref
    while n % t:
        t //= 2
    return t


def _encoder(xa, xb, mem, p):
    (nba, s, d), nbb = xa.shape, xb.shape[0]
    nb = nba + nbb
    t = nb * s
    t_first = nba * s
    nc = s // SSD_CHUNK
    tm = _pick(math.gcd(t_first, t - t_first), 1024)
    tq = _pick(s, 512)
    xa = xa.reshape(t_first, d)
    xb = xb.reshape(t - t_first, d)

    perm = jnp.arange(SSD_HEADS).reshape(SSD_GROUPS, HEADS_PER_GROUP)
    perm = jnp.concatenate([perm, perm + SSD_HEADS], axis=1).reshape(-1)

    w_in = p['w_in']
    o_z = 2 * D_MODEL
    o_x = o_z + SSD_D_INNER
    o_dt = o_x + SSD_D_INNER + 2 * SSD_GROUPS * SSD_STATE
    o_g = o_dt + 2 * SSD_HEADS
    w_val = w_in[:, :D_MODEL].astype(BF16)
    w_gate = w_in[:, D_MODEL:o_z].astype(BF16)
    w_z = w_in[:, o_z:o_x].astype(BF16)
    w_xbc = w_in[:, o_x:o_dt].astype(BF16)
    w_dt = jnp.pad(w_in[:, o_dt:o_g][:, perm], ((0, 0), (0, LANES - 2 * SSD_HEADS))).astype(BF16)
    w_g = w_in[:, o_g:].astype(BF16)
    nmix = p['norm_mix_w'].reshape(1, d)

    h = _prenorm(xa, xb, nmix, tm)
    u = _glu_proj(h, w_val, w_gate, tm, PROJ_TN // 2)
    xbc = _proj(h, w_xbc, lambda a: a, BF16, tm, PROJ_TN, "xbc_proj")
    dt_raw = _proj(h, w_dt, lambda a: a, F32, tm, LANES, "dt_proj")
    gates = _proj(h, w_g, _sigmoid, BF16, tm, PROJ_TN, "gate_proj")

    pad = LANES - 2 * SSD_HEADS
    dt_bias = jnp.pad(jnp.concatenate([p['ssd_dt_bias_f'], p['ssd_dt_bias_b']])[perm], (0, pad)).reshape(1, LANES)
    a_log = jnp.pad(jnp.concatenate([p['ssd_a_log_f'], p['ssd_a_log_b']])[perm], (0, pad)).reshape(1, LANES)
    acs2, rowt, e_hi, e_lo, w_hi, w_lo, dec = _dtprep(dt_raw, dt_bias, a_log, tm)

    def per_group_tm(a):
        return a[:, :2 * SSD_HEADS].reshape(nb, s, SSD_GROUPS, 8).transpose(0, 2, 1, 3)

    def per_group_hm(a):
        return a[:, :2 * SSD_HEADS].reshape(nb, nc, SSD_CHUNK, SSD_GROUPS, 8).transpose(0, 3, 1, 4, 2)

    acs_tm = per_group_tm(acs2).reshape(nb * SSD_GROUPS, s, 8)
    ew_tm = jnp.concatenate([per_group_tm(v) for v in (e_hi, e_lo, w_hi, w_lo)], axis=-1)
    ew_tm = ew_tm.reshape(nb * SSD_GROUPS, s, EW_COLS)
    rowt_hm = per_group_hm(rowt)
    dec = dec[:, :2 * SSD_HEADS].reshape(nb, nc, SSD_GROUPS, 2, HEADS_PER_GROUP).transpose(0, 2, 1, 3, 4)
    dec = jnp.repeat(dec, SSD_HEAD_DIM, axis=-1)

    cwx = p['ssd_conv_w']
    cbx = p['ssd_conv_b']

    def per_group_conv(a):
        xs_ = a[:, :SSD_D_INNER].reshape(-1, SSD_GROUPS, GROUP_CH)
        b_ = a[:, SSD_D_INNER:SSD_D_INNER + SSD_GROUPS * SSD_STATE].reshape(-1, SSD_GROUPS, SSD_STATE)
        c_ = a[:, SSD_D_INNER + SSD_GROUPS * SSD_STATE:].reshape(-1, SSD_GROUPS, SSD_STATE)
        return jnp.concatenate([xs_, b_, c_], axis=-1).transpose(1, 0, 2)

    cw = jnp.pad(per_group_conv(cwx), ((0, 0), (0, 8 - SSD_CONV_WIDTH), (0, 0)))
    cb = per_group_conv(cbx.reshape(1, -1))
    dskip = jnp.repeat(p['ssd_d'], SSD_HEAD_DIM).reshape(SSD_GROUPS, 1, GROUP_CH)
    nw_ssd = p['ssd_norm_w'].reshape(SSD_GROUPS, 1, GROUP_CH)

    yn = _ssd(xbc.reshape(nb, s, -1), h.reshape(nb, s, d), w_z, cw, cb, rowt_hm, acs_tm, ew_tm, dec, dskip, nw_ssd)
    mb = _gated_proj(yn.reshape(t, SSD_D_INNER), p['ssd_out'].astype(BF16), gates, 1, tm)

    dww = jnp.pad(p['conv_dw_w'], ((0, 32 - CONV_WIDTH), (0, 0)))
    x1 = _conformer(u.reshape(nb, s, d), mb.reshape(nb, s, d), gates.reshape(nb, s, 2 * d), xa, xb, dww,
                    p['conv_dw_b'].reshape(1, d), p['conv_ln_w'].reshape(1, d), p['conv_ln_b'].reshape(1, d),
                    p['conv_pw_out'].astype(BF16), p['w_out'].astype(BF16), tq)

    m = mem.shape[1]
    kv = _norm_proj(mem.reshape(nb * m, d), p['norm_mem_w'].reshape(1, d), p['xattn_wkv'].astype(BF16),
                    lambda a: a, BF16, _pick(nb * m, 512), 1024, "kv_proj")
    n_r = N_GROUPS + N_EXPERTS
    wr = jnp.pad(jnp.concatenate([p['router_group_w'], p['router_expert_w']], axis=1),
                 ((0, 0), (0, LANES - n_r))).astype(BF16)
    br = jnp.pad(jnp.concatenate([p['router_group_b'], p['router_expert_b']]), (0, LANES - n_r)).reshape(1, LANES)
    x2, h3, logits = _attention(x1, kv.reshape(nb, m, 2 * d), p['norm_xattn_w'].reshape(1, d),
                                p['xattn_wq'].astype(BF16), p['xattn_wo'].astype(BF16),
                                p['norm_ffn_w'].reshape(1, d), wr, br, tq)

    tc = _pick(math.gcd(t_first, t - t_first), 512)
    gate, pos, cnt = _plan(logits.reshape(t, LANES), tm)
    n_blocks = (2 * t) // MOE_ROWS + N_EXPERTS
    blk_exp, n_active, last_blk = _block_experts(cnt[0, :N_EXPERTS].astype(jnp.int32), n_blocks)
    pos_blk = pos[:, :2].reshape(t // tc, tc, 2).transpose(0, 2, 1).reshape(t // tc, 1, 2 * tc)
    xs = _dispatch(last_blk, pos_blk, h3, n_blocks)
    ys = _experts(blk_exp, n_active, xs, p['expert_w_gate'], p['expert_w_up'], p['expert_w_down'])
    return _combine(pos_blk, x2.reshape(t, d), gate, p['norm_final_w'].reshape(1, d), ys, tc, t_first)


def kernel(x_prompt, x_sample, mem_prompt, mem_sample, norm_mix_w, w_in, conv_dw_w, conv_dw_b, conv_ln_w, conv_ln_b, conv_pw_out, ssd_conv_w, ssd_conv_b, ssd_dt_bias_f, ssd_dt_bias_b, ssd_a_log_f, ssd_a_log_b, ssd_d, ssd_norm_w, ssd_out, w_out, norm_xattn_w, norm_mem_w, xattn_wq, xattn_wkv, xattn_wo, norm_ffn_w, router_group_w, router_group_b, router_expert_w, router_expert_b, expert_w_gate, expert_w_up, expert_w_down, norm_final_w):
    p = {
        'norm_mix_w': norm_mix_w[0], 'w_in': w_in[0], 'conv_dw_w': conv_dw_w[0], 'conv_dw_b': conv_dw_b[0],
        'conv_ln_w': conv_ln_w[0], 'conv_ln_b': conv_ln_b[0], 'conv_pw_out': conv_pw_out[0],
        'ssd_conv_w': ssd_conv_w[0], 'ssd_conv_b': ssd_conv_b[0], 'ssd_dt_bias_f': ssd_dt_bias_f[0],
        'ssd_dt_bias_b': ssd_dt_bias_b[0], 'ssd_a_log_f': ssd_a_log_f[0], 'ssd_a_log_b': ssd_a_log_b[0],
        'ssd_d': ssd_d[0], 'ssd_norm_w': ssd_norm_w[0], 'ssd_out': ssd_out[0], 'w_out': w_out[0],
        'norm_xattn_w': norm_xattn_w[0], 'norm_mem_w': norm_mem_w[0], 'xattn_wq': xattn_wq[0],
        'xattn_wkv': xattn_wkv[0], 'xattn_wo': xattn_wo[0], 'norm_ffn_w': norm_ffn_w[0],
        'router_group_w': router_group_w[0], 'router_group_b': router_group_b[0],
        'router_expert_w': router_expert_w[0], 'router_expert_b': router_expert_b[0],
        'expert_w_gate': expert_w_gate[0], 'expert_w_up': expert_w_up[0], 'expert_w_down': expert_w_down[0],
        'norm_final_w': norm_final_w,
    }
    mem = jnp.concatenate([mem_prompt, mem_sample], axis=0)
    y_prompt, y_sample = _encoder(x_prompt, x_sample, mem, p)
    return (y_prompt.reshape(x_prompt.shape), y_sample.reshape(x_sample.shape))
```

```python
import functools
import math

import jax
import jax.numpy as jnp
from jax import lax
from jax.experimental import pallas as pl
from jax.experimental.pallas import tpu as pltpu

F32 = jnp.float32
BF16 = jnp.bfloat16

D_MODEL = 1024
CONV_WIDTH = 31
SSD_D_INNER = 2048
SSD_HEAD_DIM = 64
SSD_HEADS = 32
SSD_GROUPS = 8
SSD_STATE = 128
SSD_CONV_WIDTH = 5
SSD_CHUNK = 128
GROUP_CH = SSD_D_INNER // SSD_GROUPS
HEADS_PER_GROUP = SSD_HEADS // SSD_GROUPS
XATTN_HEADS = 4
XATTN_HEAD_DIM = D_MODEL // XATTN_HEADS
N_GROUPS = 8
EXPERTS_PER_GROUP = 8
N_EXPERTS = 64
EXPERT_FF = 512
MOE_BLOCK = 128
RMS_EPS = 1e-6
LN_EPS = 1e-5

LANES = 128
HALO = 16
VMEM_LIMIT = 56 * 1024 * 1024
PROJ_TN = 2048


def _cparams(sem):
    return pltpu.CompilerParams(dimension_semantics=sem, vmem_limit_bytes=VMEM_LIMIT)


def _rms(x, w):
    ms = jnp.mean(x * x, axis=-1, keepdims=True)
    return x * lax.rsqrt(ms + RMS_EPS) * w


def _sigmoid(x):
    return 1.0 / (1.0 + jnp.exp2(x * -1.4426950408889634))


def _silu(x):
    return x * _sigmoid(x)


def _softplus(x):
    return jnp.maximum(x, 0.0) + jnp.log1p(jnp.exp(-jnp.abs(x)))


CONV_STRIDE = 4
CONV_ROWS = 8 * CONV_STRIDE
CONV_UNROLL = 4


def _depthwise_conv(src, dst, slab, w_ref, b_ref, n_rows, width, first_row, epilogue):
    lanes = pl.ds(slab * LANES, LANES)
    taps = [jnp.broadcast_to(w_ref[k:k + 1, lanes], (8, LANES)) for k in range(width)]
    bias = jnp.broadcast_to(b_ref[:, lanes], (8, LANES))
    step = CONV_ROWS * CONV_UNROLL

    def body(r, carry):
        _conv_rows(src, dst, slab, taps, bias, r * step, step, width, first_row, epilogue)
        return carry

    lax.fori_loop(0, n_rows // step, body, 0)


def _conv_rows(src, dst, slab, taps, bias, base, n_rows, width, first_row, epilogue):
    for t0 in range(n_rows // 8):
        row = base + (t0 // CONV_STRIDE) * CONV_ROWS + t0 % CONV_STRIDE
        acc = bias
        for k in range(width):
            acc = acc + src[slab, pl.ds(row + first_row - width // 2 + k, 8, stride=CONV_STRIDE), :] * taps[k]
        dst[slab, pl.ds(row, 8, stride=CONV_STRIDE), :] = epilogue(acc)


def _first_or_second(i, n_first):
    return jnp.minimum(i, n_first - 1), jnp.maximum(i - n_first, 0)


def _prenorm_kernel(n_first, xa_ref, xb_ref, nw_ref, o_ref):
    x = jnp.where(pl.program_id(0) < n_first, xa_ref[...], xb_ref[...])
    o_ref[...] = _rms(x, nw_ref[...]).astype(o_ref.dtype)


def _prenorm(xa, xb, nw, tm):
    (ta, d), tb = xa.shape, xb.shape[0]
    na = ta // tm
    return pl.pallas_call(
        functools.partial(_prenorm_kernel, na),
        grid=((ta + tb) // tm,),
        in_specs=[
            pl.BlockSpec((tm, d), lambda i: (_first_or_second(i, na)[0], 0)),
            pl.BlockSpec((tm, d), lambda i: (_first_or_second(i, na)[1], 0)),
            pl.BlockSpec((1, d), lambda i: (0, 0)),
        ],
        out_specs=pl.BlockSpec((tm, d), lambda i: (i, 0)),
        out_shape=jax.ShapeDtypeStruct((ta + tb, d), BF16),
        compiler_params=_cparams(("arbitrary",)),
        name="mix_norm",
    )(xa, xb, nw)


def _glu_kernel(h_ref, wv_ref, wg_ref, o_ref):
    h = h_ref[...]
    v = jnp.dot(h, wv_ref[...], preferred_element_type=F32)
    g = jnp.dot(h, wg_ref[...], preferred_element_type=F32)
    o_ref[...] = (v * _sigmoid(g)).astype(o_ref.dtype)


def _glu_proj(h, wv, wg, tm, tn):
    t, d = h.shape
    n = wv.shape[1]
    return pl.pallas_call(
        _glu_kernel,
        grid=(t // tm, n // tn),
        in_specs=[
            pl.BlockSpec((tm, d), lambda i, j: (i, 0)),
            pl.BlockSpec((d, tn), lambda i, j: (0, j)),
            pl.BlockSpec((d, tn), lambda i, j: (0, j)),
        ],
        out_specs=pl.BlockSpec((tm, tn), lambda i, j: (i, j)),
        out_shape=jax.ShapeDtypeStruct((t, n), BF16),
        compiler_params=_cparams(("parallel", "arbitrary")),
        name="glu_proj",
    )(h, wv, wg)


def _proj_kernel(epi, h_ref, w_ref, o_ref):
    o_ref[...] = epi(jnp.dot(h_ref[...], w_ref[...], preferred_element_type=F32)).astype(o_ref.dtype)


def _proj(h, w, epi, out_dtype, tm, tn, name):
    t, d = h.shape
    n = w.shape[1]
    return pl.pallas_call(
        functools.partial(_proj_kernel, epi),
        grid=(t // tm, n // tn),
        in_specs=[
            pl.BlockSpec((tm, d), lambda i, j: (i, 0)),
            pl.BlockSpec((d, tn), lambda i, j: (0, j)),
        ],
        out_specs=pl.BlockSpec((tm, tn), lambda i, j: (i, j)),
        out_shape=jax.ShapeDtypeStruct((t, n), out_dtype),
        compiler_params=_cparams(("parallel", "arbitrary")),
        name=name,
    )(h, w)


def _norm_proj_kernel(epi, x_ref, nw_ref, w_ref, o_ref, h_scr):
    @pl.when(pl.program_id(1) == 0)
    def _():
        h_scr[...] = _rms(x_ref[...], nw_ref[...]).astype(BF16)

    acc = jnp.dot(h_scr[...], w_ref[...], preferred_element_type=F32)
    o_ref[...] = epi(acc).astype(o_ref.dtype)


def _norm_proj(x, nw, w, epi, out_dtype, tm, tn, name):
    t, d = x.shape
    n = w.shape[1]
    return pl.pallas_call(
        functools.partial(_norm_proj_kernel, epi),
        grid=(t // tm, n // tn),
        in_specs=[
            pl.BlockSpec((tm, d), lambda i, j: (i, 0)),
            pl.BlockSpec((1, d), lambda i, j: (0, 0)),
            pl.BlockSpec((d, tn), lambda i, j: (0, j)),
        ],
        out_specs=pl.BlockSpec((tm, tn), lambda i, j: (i, j)),
        out_shape=jax.ShapeDtypeStruct((t, n), out_dtype),
        scratch_shapes=[pltpu.VMEM((tm, d), BF16)],
        compiler_params=_cparams(("parallel", "arbitrary")),
        name=name,
    )(x, nw, w)


def _gated_proj_kernel(y_ref, w_ref, g_ref, o_ref):
    acc = jnp.dot(y_ref[...], w_ref[...], preferred_element_type=F32)
    o_ref[...] = (acc * g_ref[...].astype(F32)).astype(o_ref.dtype)


def _gated_proj(y, w, g, g_col_block, tm):
    t, k = y.shape
    n = w.shape[1]
    return pl.pallas_call(
        _gated_proj_kernel,
        grid=(t // tm,),
        in_specs=[
            pl.BlockSpec((tm, k), lambda i: (i, 0)),
            pl.BlockSpec((k, n), lambda i: (0, 0)),
            pl.BlockSpec((tm, n), lambda i: (i, g_col_block)),
        ],
        out_specs=pl.BlockSpec((tm, n), lambda i: (i, 0)),
        out_shape=jax.ShapeDtypeStruct((t, n), BF16),
        compiler_params=_cparams(("parallel",)),
        name="ssd_out_proj",
    )(y, w, g)


LOG2E = 1.4426950408889634


def _split_bf16(v):
    hi = v.astype(BF16)
    return hi, (v - hi.astype(F32)).astype(BF16)


def _dtprep_kernel(raw_ref, bias_ref, alog_ref, acs2_ref, rowt_ref, ehi_ref, elo_ref, whi_ref, wlo_ref, dec_ref):
    rows = raw_ref.shape[0]
    a_head = -jnp.exp(alog_ref[...])
    row = lax.broadcasted_iota(jnp.int32, (SSD_CHUNK, LANES), 0)
    lane = lax.broadcasted_iota(jnp.int32, (SSD_CHUNK, LANES), 1)
    is_bwd = (lane % (2 * HEADS_PER_GROUP)) >= HEADS_PER_GROUP
    for c in range(rows // SSD_CHUNK):
        sl = pl.ds(c * SSD_CHUNK, SSD_CHUNK)
        dt = _softplus(raw_ref[sl, :] + bias_ref[...])
        a = dt * a_head
        fwd = a
        bwd = a
        sh = 1
        while sh < SSD_CHUNK:
            fwd = fwd + jnp.where(row >= sh, pltpu.roll(fwd, sh, 0), 0.0)
            bwd = bwd + jnp.where(row < SSD_CHUNK - sh, pltpu.roll(bwd, SSD_CHUNK - sh, 0), 0.0)
            sh *= 2
        acs = jnp.where(is_bwd, bwd, fwd)
        a_end = jnp.where(is_bwd[0:1, :], bwd[0:1, :], fwd[SSD_CHUNK - 1:SSD_CHUNK, :])
        acs2 = acs * LOG2E
        acs2_ref[sl, :] = acs2
        rowt_ref[sl, :] = acs2 - jnp.log(dt) * LOG2E
        ehi_ref[sl, :], elo_ref[sl, :] = _split_bf16(jnp.exp(acs))
        whi_ref[sl, :], wlo_ref[sl, :] = _split_bf16(jnp.exp(a_end - acs) * dt)
        dec_ref[c:c + 1, :] = jnp.exp(a_end)


def _dtprep(raw, bias, alog, tm):
    t = raw.shape[0]
    spec = pl.BlockSpec((tm, LANES), lambda i: (i, 0))
    vec = pl.BlockSpec((1, LANES), lambda i: (0, 0))
    return pl.pallas_call(
        _dtprep_kernel,
        grid=(t // tm,),
        in_specs=[spec, vec, vec],
        out_specs=[spec] * 6 + [pl.BlockSpec((tm // SSD_CHUNK, LANES), lambda i: (i, 0))],
        out_shape=[jax.ShapeDtypeStruct((t, LANES), F32)] * 2 + [jax.ShapeDtypeStruct((t, LANES), BF16)] * 4
        + [jax.ShapeDtypeStruct((t // SSD_CHUNK, LANES), F32)],
        compiler_params=_cparams(("parallel",)),
        name="dt_prep",
    )(raw, bias, alog)


SSD_PAD = 8
XBC_W = GROUP_CH + 2 * SSD_STATE


EW_COLS = 4 * 2 * HEADS_PER_GROUP


def _head_expand_matrix():
    col = jnp.arange(4 * GROUP_CH)
    part = col // GROUP_CH
    is_w = (part == 1) | (part == 2)
    is_bwd = part >= 2
    head = is_bwd * HEADS_PER_GROUP + (col % GROUP_CH) // SSD_HEAD_DIM
    hi_row = is_w * 16 + head
    row = jnp.arange(EW_COLS)[:, None]
    return ((row == hi_row[None, :]) | (row == hi_row[None, :] + 8)).astype(BF16)


def _ssd_kernel(xs_ref, b_ref, c_ref, h_ref, wz_ref, cw_ref, cb_ref, rowt_ref, acs_ref, ew_ref, sel_ref, dec_ref,
                dskip_ref, nw_ref, o_ref, raw_scr, act_scr, y_scr, upd_scr, st_scr):
    s = xs_ref.shape[1]
    nc = s // SSD_CHUNK
    L = SSD_CHUNK

    n_slabs = XBC_W // LANES
    for j in range(n_slabs):
        raw_scr[j, pl.ds(0, SSD_PAD), :] = jnp.zeros((SSD_PAD, LANES), F32)
        raw_scr[j, pl.ds(SSD_PAD + s, L + SSD_PAD), :] = jnp.zeros((L + SSD_PAD, LANES), F32)
    raw_scr[0, pl.ds(SSD_PAD, s), :] = xs_ref[0, :, :LANES].astype(F32)
    raw_scr[1, pl.ds(SSD_PAD, s), :] = xs_ref[0, :, LANES:].astype(F32)
    raw_scr[2, pl.ds(SSD_PAD, s), :] = b_ref[0].astype(F32)
    raw_scr[3, pl.ds(SSD_PAD, s), :] = c_ref[0].astype(F32)

    def conv_chunk(c):
        for j in range(n_slabs):
            lanes = pl.ds(j * LANES, LANES)
            taps = [jnp.broadcast_to(cw_ref[0, k:k + 1, lanes], (8, LANES)) for k in range(SSD_CONV_WIDTH)]
            bias = jnp.broadcast_to(cb_ref[0, :, lanes], (8, LANES))
            _conv_rows(raw_scr, act_scr, j, taps, bias, c * L, L, SSD_CONV_WIDTH, SSD_PAD, _silu)

    conv_chunk(0)

    li = lax.broadcasted_iota(jnp.int32, (L, L), 0)
    si = lax.broadcasted_iota(jnp.int32, (L, L), 1)
    head_of_ch = lax.broadcasted_iota(jnp.int32, (L, GROUP_CH), 1) // SSD_HEAD_DIM

    keep = (li >= si, si >= li)
    neg_inf = jnp.float32(-jnp.inf)
    n_heads = 2 * HEADS_PER_GROUP
    st_scr[...] = jnp.zeros_like(st_scr)

    def local_body(c, carry):
        r0 = pl.multiple_of(c * L, L)
        x = jnp.concatenate([act_scr[0, pl.ds(r0, L), :], act_scr[1, pl.ds(r0, L), :]], axis=1)
        xb = x.astype(BF16)
        bk = act_scr[2, pl.ds(r0, L), :].astype(BF16)
        ck = act_scr[3, pl.ds(r0, L), :].astype(BF16)
        acs = acs_ref[0, pl.ds(r0, L), :]
        rowt = rowt_ref[0, 0, c]
        ew = jnp.dot(ew_ref[0, pl.ds(r0, L), :], sel_ref[:, :3 * GROUP_CH],
                     preferred_element_type=F32)

        cb = lax.dot_general(ck, bk, (((1,), (1,)), ((), ())), preferred_element_type=F32)
        ms = []
        for j in range(n_heads):
            diff = acs[:, j:j + 1] - rowt[j:j + 1, :]
            decay_dt = jnp.exp2(jnp.where(keep[j // HEADS_PER_GROUP], diff, neg_inf))
            ms.append((cb * decay_dt).astype(BF16))
        xbd = jnp.concatenate([jnp.where(head_of_ch == h, xb, jnp.zeros_like(xb))
                               for h in range(HEADS_PER_GROUP)], axis=0)
        y = jnp.dot(jnp.concatenate(ms, axis=1), jnp.concatenate([xbd, xbd], axis=0),
                    preferred_element_type=F32) + x * dskip_ref[0]

        xw = jnp.concatenate([x * ew[:, GROUP_CH:2 * GROUP_CH], x * ew[:, 2 * GROUP_CH:]], axis=1)
        upd = lax.dot_general(bk, xw.astype(BF16), (((0,), (0,)), ((), ())), preferred_element_type=F32)

        state = st_scr[0]
        y = y + jnp.dot(ck, state.astype(BF16), preferred_element_type=F32) * ew[:, :GROUP_CH]
        st_scr[0] = state * dec_ref[0, 0, c, 0:1, :] + upd[:, :GROUP_CH]
        y_scr[pl.ds(r0, L), :] = y
        upd_scr[c] = upd[:, GROUP_CH:]
        conv_chunk(c + 1)
        return carry

    lax.fori_loop(0, nc, local_body, 0, unroll=2)

    def bwd_body(i, carry):
        c = nc - 1 - i
        r0 = pl.multiple_of(c * L, L)
        ck = act_scr[3, pl.ds(r0, L), :].astype(BF16)
        e_bwd = jnp.dot(ew_ref[0, pl.ds(r0, L), :], sel_ref[:, 3 * GROUP_CH:], preferred_element_type=F32)
        state = st_scr[1]
        y = y_scr[pl.ds(r0, L), :] + jnp.dot(ck, state.astype(BF16), preferred_element_type=F32) * e_bwd
        st_scr[1] = state * dec_ref[0, 0, c, 1:2, :] + upd_scr[c]
        z = jnp.dot(h_ref[0, pl.ds(r0, L), :], wz_ref[...], preferred_element_type=F32)
        yz = y * _silu(z)
        ms = jnp.mean(yz * yz, axis=-1, keepdims=True)
        o_ref[0, pl.ds(r0, L), :] = (yz * lax.rsqrt(ms + RMS_EPS) * nw_ref[0]).astype(o_ref.dtype)
        return carry

    lax.fori_loop(0, nc, bwd_body, 0, unroll=4)


def _ssd(xbc, h, wz, cw, cb, rowt_hm, acs_tm, ew_tm, dec, dskip, nw):
    nb, s, _ = xbc.shape
    d = h.shape[2]
    nc = s // SSD_CHUNK
    return pl.pallas_call(
        _ssd_kernel,
        grid=(nb, SSD_GROUPS),
        in_specs=[
            pl.BlockSpec((1, s, GROUP_CH), lambda b, g: (b, 0, g)),
            pl.BlockSpec((1, s, SSD_STATE), lambda b, g: (b, 0, SSD_D_INNER // SSD_STATE + g)),
            pl.BlockSpec((1, s, SSD_STATE), lambda b, g: (b, 0, SSD_D_INNER // SSD_STATE + SSD_GROUPS + g)),
            pl.BlockSpec((1, s, d), lambda b, g: (b, 0, 0)),
            pl.BlockSpec((d, GROUP_CH), lambda b, g: (0, g)),
            pl.BlockSpec((1, 8, XBC_W), lambda b, g: (g, 0, 0)),
            pl.BlockSpec((1, 1, XBC_W), lambda b, g: (g, 0, 0)),
            pl.BlockSpec((1, 1, nc, 8, SSD_CHUNK), lambda b, g: (b, g, 0, 0, 0)),
            pl.BlockSpec((1, s, 8), lambda b, g: (b * SSD_GROUPS + g, 0, 0)),
            pl.BlockSpec((1, s, EW_COLS), lambda b, g: (b * SSD_GROUPS + g, 0, 0)),
            pl.BlockSpec((EW_COLS, 4 * GROUP_CH), lambda b, g: (0, 0)),
            pl.BlockSpec((1, 1, nc, 2, GROUP_CH), lambda b, g: (b, g, 0, 0, 0)),
            pl.BlockSpec((1, 1, GROUP_CH), lambda b, g: (g, 0, 0)),
            pl.BlockSpec((1, 1, GROUP_CH), lambda b, g: (g, 0, 0)),
        ],
        out_specs=pl.BlockSpec((1, s, GROUP_CH), lambda b, g: (b, 0, g)),
        out_shape=jax.ShapeDtypeStruct((nb, s, SSD_D_INNER), BF16),
        scratch_shapes=[
            pltpu.VMEM((XBC_W // LANES, s + SSD_CHUNK + 2 * SSD_PAD, LANES), F32),
            pltpu.VMEM((XBC_W // LANES, s + SSD_CHUNK, LANES), F32),
            pltpu.VMEM((s, GROUP_CH), F32),
            pltpu.VMEM((nc, SSD_STATE, GROUP_CH), F32),
            pltpu.VMEM((2, SSD_STATE, GROUP_CH), F32),
        ],
        compiler_params=_cparams(("parallel", "parallel")),
        name="ssd_scan",
    )(xbc, xbc, xbc, h, wz, cw, cb, rowt_hm, acs_tm, ew_tm, _head_expand_matrix(), dec, dskip, nw)


D_SLABS = D_MODEL // LANES


def _conformer_kernel(n_first, u_ref, up_ref, un_ref, mb_ref, g_ref, xa_ref, xb_ref, dww_ref, dwb_ref, lnw_ref,
                      lnb_ref, pw_ref, wo_ref, o_ref, u_scr, c_scr):
    i = pl.program_id(1)
    n = pl.num_programs(1)
    tq = u_ref.shape[1]
    x = jnp.where(pl.program_id(0) * n + i < n_first, xa_ref[...], xb_ref[...])
    for j in range(D_SLABS):
        cs = slice(j * LANES, (j + 1) * LANES)
        u_scr[j, pl.ds(0, HALO), :] = jnp.where(i > 0, up_ref[0, :, cs].astype(F32), 0.0)
        u_scr[j, pl.ds(HALO, tq), :] = u_ref[0, :, cs].astype(F32)
        u_scr[j, pl.ds(HALO + tq, HALO), :] = jnp.where(i < n - 1, un_ref[0, :, cs].astype(F32), 0.0)
    for j in range(D_SLABS):
        _depthwise_conv(u_scr, c_scr, j, dww_ref, dwb_ref, tq, CONV_WIDTH, HALO, lambda a: a)

    cv = jnp.concatenate([c_scr[j] for j in range(D_SLABS)], axis=1)
    mu = jnp.mean(cv, axis=-1, keepdims=True)
    xc = cv - mu
    var = jnp.mean(xc * xc, axis=-1, keepdims=True)
    y = xc * lax.rsqrt(var + LN_EPS) * lnw_ref[...] + lnb_ref[...]
    a = jnp.dot(_silu(y).astype(BF16), pw_ref[...], preferred_element_type=F32)
    merged = g_ref[0].astype(F32) * a + mb_ref[0].astype(F32)
    o_ref[0] = x + jnp.dot(merged.astype(BF16), wo_ref[...], preferred_element_type=F32)


def _conformer(u, mb, gates, xa, xb, dww, dwb, lnw, lnb, pw, wo, tq):
    nb, s, d = u.shape
    hb = tq // HALO
    nh = s // HALO
    nq = s // tq
    na = xa.shape[0] // tq
    wspec = pl.BlockSpec((d, d), lambda b, i: (0, 0))
    vspec = pl.BlockSpec((1, d), lambda b, i: (0, 0))
    return pl.pallas_call(
        functools.partial(_conformer_kernel, na),
        grid=(nb, nq),
        in_specs=[
            pl.BlockSpec((1, tq, d), lambda b, i: (b, i, 0)),
            pl.BlockSpec((1, HALO, d), lambda b, i: (b, jnp.maximum(i * hb - 1, 0), 0)),
            pl.BlockSpec((1, HALO, d), lambda b, i: (b, jnp.minimum((i + 1) * hb, nh - 1), 0)),
            pl.BlockSpec((1, tq, d), lambda b, i: (b, i, 0)),
            pl.BlockSpec((1, tq, d), lambda b, i: (b, i, 0)),
            pl.BlockSpec((tq, d), lambda b, i: (_first_or_second(b * nq + i, na)[0], 0)),
            pl.BlockSpec((tq, d), lambda b, i: (_first_or_second(b * nq + i, na)[1], 0)),
            pl.BlockSpec((32, d), lambda b, i: (0, 0)),
            vspec, vspec, vspec, wspec, wspec,
        ],
        out_specs=pl.BlockSpec((1, tq, d), lambda b, i: (b, i, 0)),
        out_shape=jax.ShapeDtypeStruct((nb, s, d), F32),
        scratch_shapes=[pltpu.VMEM((D_SLABS, tq + 2 * HALO, LANES), F32), pltpu.VMEM((D_SLABS, tq, LANES), F32)],
        compiler_params=_cparams(("arbitrary", "arbitrary")),
        name="conformer_merge",
    )(u, u, u, mb, gates, xa, xb, dww, dwb, lnw, lnb, pw, wo)


TILE_ROWS = D_MODEL // LANES


def _store_token_tiles(ref, val):
    n = val.shape[0]
    for j in range(TILE_ROWS):
        ref[pl.ds(j, n, stride=TILE_ROWS), :] = val[:, j * LANES:(j + 1) * LANES]


def _load_token_tiles(ref, first, n):
    return jnp.concatenate([ref[pl.ds(first * TILE_ROWS + j, n, stride=TILE_ROWS), :] for j in range(TILE_ROWS)],
                           axis=1)


def _attn_kernel(x_ref, k_ref, v_ref, nw_ref, wq_ref, wo_ref, nf_ref, wr_ref, br_ref, x2_ref, h3_ref, lg_ref):
    x = x_ref[0]
    h = _rms(x, nw_ref[...]).astype(BF16)
    q = jnp.dot(h, wq_ref[...], preferred_element_type=F32)
    scale = 1.0 / math.sqrt(XATTN_HEAD_DIM)
    outs = []
    for hd in range(XATTN_HEADS):
        cs = slice(hd * XATTN_HEAD_DIM, (hd + 1) * XATTN_HEAD_DIM)
        sc = lax.dot_general(q[:, cs].astype(BF16), k_ref[0, :, cs], (((1,), (1,)), ((), ())),
                             preferred_element_type=F32) * scale
        m = jnp.max(sc, axis=-1, keepdims=True)
        e = jnp.exp(sc - m)
        p = e / jnp.sum(e, axis=-1, keepdims=True)
        outs.append(jnp.dot(p.astype(BF16), v_ref[0, :, cs], preferred_element_type=F32))
    o = jnp.concatenate(outs, axis=1).astype(BF16)
    x2 = x + jnp.dot(o, wo_ref[...], preferred_element_type=F32)
    x2_ref[0] = x2
    h3 = _rms(x2, nf_ref[...])
    _store_token_tiles(h3_ref, h3)
    lg_ref[0] = jnp.dot(h3.astype(BF16), wr_ref[...], preferred_element_type=F32) + br_ref[...]


def _attention(x, kv, nw, wq, wo, nf, wr, br, tq):
    nb, s, d = x.shape
    m = kv.shape[1]
    wspec = pl.BlockSpec((d, d), lambda b, i: (0, 0))
    vspec = pl.BlockSpec((1, d), lambda b, i: (0, 0))
    return pl.pallas_call(
        _attn_kernel,
        grid=(nb, s // tq),
        in_specs=[
            pl.BlockSpec((1, tq, d), lambda b, i: (b, i, 0)),
            pl.BlockSpec((1, m, d), lambda b, i: (b, 0, 0)),
            pl.BlockSpec((1, m, d), lambda b, i: (b, 0, 1)),
            vspec, wspec, wspec, vspec,
            pl.BlockSpec((d, LANES), lambda b, i: (0, 0)),
            pl.BlockSpec((1, LANES), lambda b, i: (0, 0)),
        ],
        out_specs=[
            pl.BlockSpec((1, tq, d), lambda b, i: (b, i, 0)),
            pl.BlockSpec((tq * TILE_ROWS, LANES), lambda b, i: (b * (s // tq) + i, 0)),
            pl.BlockSpec((1, tq, LANES), lambda b, i: (b, i, 0)),
        ],
        out_shape=[
            jax.ShapeDtypeStruct((nb, s, d), F32),
            jax.ShapeDtypeStruct((nb * s * TILE_ROWS, LANES), F32),
            jax.ShapeDtypeStruct((nb, s, LANES), F32),
        ],
        compiler_params=_cparams(("parallel", "parallel")),
        name="cross_attn_router",
    )(x, kv, kv, nw, wq, wo, nf, wr, br)


MOE_ROWS = 512


def _route_math(lg, lane):
    lane_f = lane.astype(F32)
    big = float(LANES)
    neg = -jnp.inf

    is_g = lane < N_GROUPS
    gl = jnp.where(is_g, lg, neg)
    gmax = jnp.max(gl, axis=-1, keepdims=True)
    g_idx = jnp.min(jnp.where(gl == gmax, lane_f, big), axis=-1, keepdims=True)
    g_sum = jnp.sum(jnp.where(is_g, jnp.exp(gl - gmax), 0.0), axis=-1, keepdims=True)
    g_w = 1.0 / g_sum

    e_grp = lax.shift_right_arithmetic(lane - N_GROUPS, 3).astype(F32)
    is_e = (lane >= N_GROUPS) & (lane < N_GROUPS + N_EXPERTS) & (e_grp == g_idx)
    el = jnp.where(is_e, lg, neg)
    emax = jnp.max(el, axis=-1, keepdims=True)
    ee = jnp.where(is_e, jnp.exp(el - emax), 0.0)
    ep = ee / jnp.sum(ee, axis=-1, keepdims=True)
    ep = jnp.where(is_e, ep, -1.0)
    v1 = jnp.max(ep, axis=-1, keepdims=True)
    i1 = jnp.min(jnp.where(ep == v1, lane_f, big), axis=-1, keepdims=True)
    ep2 = jnp.where(lane_f == i1, -1.0, ep)
    v2 = jnp.max(ep2, axis=-1, keepdims=True)
    i2 = jnp.min(jnp.where(ep2 == v2, lane_f, big), axis=-1, keepdims=True)
    tot = v1 + v2
    gate1 = g_w * (v1 / tot)
    gate2 = g_w * (v2 / tot)
    return i1 - N_GROUPS, i2 - N_GROUPS, gate1, gate2


def _plan_kernel(lg_ref, gate_ref, pos_ref, cnt_ref, carry_scr, base_scr):
    p = pl.program_id(0)
    i = pl.program_id(1)
    tm = lg_ref.shape[0]
    lane = lax.broadcasted_iota(jnp.int32, (tm, LANES), 1)
    lane_f = lane.astype(F32)
    e1, e2, gate1, gate2 = _route_math(lg_ref[...], lane)
    oh1 = lane_f == e1
    oh2 = lane_f == e2
    m = jnp.where(oh1 | oh2, 1.0, 0.0)
    colsum = jnp.sum(m, axis=0, keepdims=True)

    @pl.when((p == 0) & (i == 0))
    def _():
        carry_scr[...] = jnp.zeros_like(carry_scr)

    @pl.when(p == 0)
    def _():
        carry_scr[...] += colsum

    @pl.when((p == 1) & (i == 0))
    def _():
        counts = carry_scr[...]
        cnt_ref[...] = counts
        shift = MOE_ROWS.bit_length() - 1
        blocks = lax.shift_right_logical(counts.astype(jnp.int32) + (MOE_ROWS - 1), shift)
        padded = (blocks * MOE_ROWS).astype(F32)
        lane8 = lax.broadcasted_iota(jnp.int32, padded.shape, 1)
        inc = padded
        sh = 1
        while sh < LANES:
            inc = inc + jnp.where(lane8 >= sh, pltpu.roll(inc, sh, 1), 0.0)
            sh *= 2
        base_scr[...] = inc - padded
        carry_scr[...] = jnp.zeros_like(carry_scr)

    @pl.when(p == 1)
    def _():
        r = lax.broadcasted_iota(jnp.int32, (tm, tm), 0)
        c = lax.broadcasted_iota(jnp.int32, (tm, tm), 1)
        earlier = jnp.where(r > c, 1.0, 0.0).astype(BF16)
        tot = (jnp.dot(earlier, m.astype(BF16), preferred_element_type=F32)
               + base_scr[0:1, :] + carry_scr[0:1, :])
        pos1 = jnp.sum(jnp.where(oh1, tot, 0.0), axis=1, keepdims=True)
        pos2 = jnp.sum(jnp.where(oh2, tot, 0.0), axis=1, keepdims=True)
        pos_ref[...] = jnp.where(lane == 0, pos1, jnp.where(lane == 1, pos2, 0.0)).astype(jnp.int32)
        gate_ref[...] = jnp.where(lane == 0, gate1, jnp.where(lane == 1, gate2, 0.0))
        carry_scr[...] += colsum


def _plan(lg, tm):
    t = lg.shape[0]
    out_spec = pl.BlockSpec((tm, LANES), lambda p, i: (i * p, 0))
    return pl.pallas_call(
        _plan_kernel,
        grid=(2, t // tm),
        in_specs=[pl.BlockSpec((tm, LANES), lambda p, i: (i, 0))],
        out_specs=[out_spec, out_spec, pl.BlockSpec((8, LANES), lambda p, i: (0, 0))],
        out_shape=[jax.ShapeDtypeStruct((t, LANES), F32), jax.ShapeDtypeStruct((t, LANES), jnp.int32),
                   jax.ShapeDtypeStruct((8, LANES), F32)],
        scratch_shapes=[pltpu.VMEM((8, LANES), F32), pltpu.VMEM((8, LANES), F32)],
        compiler_params=_cparams(("arbitrary", "arbitrary")),
        name="moe_plan",
    )(lg)


def _dispatch_kernel(last_ref, pos_ref, h_ref, xs_hbm, zero_scr, sem):
    td = pos_ref.shape[2] // 2

    @pl.when(pl.program_id(0) == 0)
    def _():
        zero_scr[...] = jnp.zeros_like(zero_scr)

        def block_copy(e):
            first = pl.multiple_of(last_ref[e] * (MOE_ROWS * TILE_ROWS), MOE_ROWS * TILE_ROWS)
            return pltpu.make_async_copy(zero_scr, xs_hbm.at[pl.ds(first, MOE_ROWS * TILE_ROWS)], sem)

        def fill(e, carry):
            @pl.when(last_ref[e] >= 0)
            def _():
                block_copy(e).start()
            return carry

        def drain(e, carry):
            @pl.when(last_ref[e] >= 0)
            def _():
                block_copy(e).wait()
            return carry

        lax.fori_loop(0, last_ref.shape[0], fill, 0)
        lax.fori_loop(0, last_ref.shape[0], drain, 0)

    def tile(ref, tok):
        return ref.at[pl.ds(pl.multiple_of(tok * TILE_ROWS, TILE_ROWS), TILE_ROWS)]

    def body(r, carry):
        src = tile(h_ref, r)
        pltpu.make_async_copy(src, tile(xs_hbm, pos_ref[0, 0, r]), sem).start(priority=0)
        pltpu.make_async_copy(src, tile(xs_hbm, pos_ref[0, 0, td + r]), sem).start(priority=1)
        return carry

    lax.fori_loop(0, td, body, 0, unroll=8)
    for _ in range(2):
        pltpu.make_async_copy(h_ref, xs_hbm.at[pl.ds(0, td * TILE_ROWS)], sem).wait()


def _dispatch(last_blk, pos_blk, h3_tiles, n_blocks):
    n_steps = pos_blk.shape[0]
    td = pos_blk.shape[2] // 2
    grid_spec = pltpu.PrefetchScalarGridSpec(
        num_scalar_prefetch=1,
        grid=(n_steps,),
        in_specs=[
            pl.BlockSpec((1, 1, 2 * td), lambda i, last: (i, 0, 0), memory_space=pltpu.SMEM),
            pl.BlockSpec((td * TILE_ROWS, LANES), lambda i, last: (i, 0)),
        ],
        out_specs=pl.BlockSpec(memory_space=pl.ANY),
        scratch_shapes=[pltpu.VMEM((MOE_ROWS * TILE_ROWS, LANES), F32), pltpu.SemaphoreType.DMA(())],
    )
    return pl.pallas_call(
        _dispatch_kernel,
        grid_spec=grid_spec,
        out_shape=jax.ShapeDtypeStruct((n_blocks * MOE_ROWS * TILE_ROWS, LANES), F32),
        compiler_params=_cparams(("arbitrary",)),
        name="moe_dispatch",
    )(last_blk, pos_blk, h3_tiles)


def _expert_kernel(be_ref, na_ref, x_ref, wg_ref, wu_ref, wd_ref, o_ref, wg_scr, wu_scr, wd_scr):
    j = pl.program_id(0)
    prev = be_ref[jnp.maximum(j - 1, 0)]

    @pl.when((j == 0) | (be_ref[j] != prev))
    def _():
        wg_scr[...] = wg_ref[0].astype(BF16)
        wu_scr[...] = wu_ref[0].astype(BF16)
        wd_scr[...] = wd_ref[0].astype(BF16)

    @pl.when(j < na_ref[0])
    def _():
        x = _load_token_tiles(x_ref, 0, MOE_ROWS).astype(BF16)
        g = jnp.dot(x, wg_scr[...], preferred_element_type=F32)
        u = jnp.dot(x, wu_scr[...], preferred_element_type=F32)
        hid = (_silu(g) * u).astype(BF16)
        _store_token_tiles(o_ref, jnp.dot(hid, wd_scr[...], preferred_element_type=F32))

    @pl.when(j >= na_ref[0])
    def _():
        o_ref[...] = jnp.zeros_like(o_ref)


def _experts(blk_exp, n_active, xs_tiles, wg, wu, wd):
    n_blocks = blk_exp.shape[0]
    d, ff = wg.shape[1], wg.shape[2]
    blk = (MOE_ROWS * TILE_ROWS, LANES)
    grid_spec = pltpu.PrefetchScalarGridSpec(
        num_scalar_prefetch=2,
        grid=(n_blocks,),
        in_specs=[
            pl.BlockSpec(blk, lambda j, be, na: (jnp.minimum(j, na[0] - 1), 0)),
            pl.BlockSpec((1, d, ff), lambda j, be, na: (be[j], 0, 0)),
            pl.BlockSpec((1, d, ff), lambda j, be, na: (be[j], 0, 0)),
            pl.BlockSpec((1, ff, d), lambda j, be, na: (be[j], 0, 0)),
        ],
        out_specs=pl.BlockSpec(blk, lambda j, be, na: (j, 0)),
        scratch_shapes=[pltpu.VMEM((d, ff), BF16), pltpu.VMEM((d, ff), BF16), pltpu.VMEM((ff, d), BF16)],
    )
    return pl.pallas_call(
        _expert_kernel,
        grid_spec=grid_spec,
        out_shape=jax.ShapeDtypeStruct(xs_tiles.shape, F32),
        compiler_params=_cparams(("arbitrary",)),
        name="moe_experts",
    )(blk_exp, n_active, xs_tiles, wg, wu, wd)


COMBINE_ROWS = 32


def _combine_kernel(n_first, pos_ref, pos_next_ref, x_ref, gate_ref, nw_ref, ys_hbm, oa_ref, ob_ref, y_scr, sem):
    i = pl.program_id(0)
    tm = x_ref.shape[0]
    slot = i % 2

    def tile(ref, tok):
        return ref.at[pl.ds(pl.multiple_of(tok * TILE_ROWS, TILE_ROWS), TILE_ROWS)]

    def gather(p_ref, to):
        buf = y_scr.at[to]

        def body(r, carry):
            pltpu.make_async_copy(tile(ys_hbm, p_ref[0, 0, r]), tile(buf, r), sem.at[to]).start(priority=0)
            pltpu.make_async_copy(tile(ys_hbm, p_ref[0, 0, tm + r]), tile(buf, tm + r), sem.at[to]).start(priority=1)
            return carry

        lax.fori_loop(0, tm, body, 0, unroll=8)

    @pl.when(i == 0)
    def _():
        gather(pos_ref, 0)

    @pl.when(i + 1 < pl.num_programs(0))
    def _():
        gather(pos_next_ref, 1 - slot)

    buf = y_scr.at[slot]
    pltpu.make_async_copy(ys_hbm.at[pl.ds(0, 2 * tm * TILE_ROWS)], buf, sem.at[slot]).wait()
    def finish(o_ref):
        def rows(rb, carry):
            r0 = pl.multiple_of(rb * COMBINE_ROWS, COMBINE_ROWS)
            sl = pl.ds(r0, COMBINE_ROWS)
            gate = gate_ref[sl, :]
            y = x_ref[sl, :] + (gate[:, 0:1] * _load_token_tiles(buf, r0, COMBINE_ROWS)
                                + gate[:, 1:2] * _load_token_tiles(buf, tm + r0, COMBINE_ROWS))
            o_ref[sl, :] = _rms(y, nw_ref[...])
            return carry

        lax.fori_loop(0, tm // COMBINE_ROWS, rows, 0, unroll=4)

    @pl.when(i < n_first)
    def _():
        finish(oa_ref)

    @pl.when(i >= n_first)
    def _():
        finish(ob_ref)


def _combine(pos_blk, x2, gates, nw, ys, tm, t_first):
    t, d = x2.shape
    na = t_first // tm
    n = t // tm
    return pl.pallas_call(
        functools.partial(_combine_kernel, na),
        grid=(n,),
        in_specs=[
            pl.BlockSpec((1, 1, 2 * tm), lambda i: (i, 0, 0), memory_space=pltpu.SMEM),
            pl.BlockSpec((1, 1, 2 * tm), lambda i: (jnp.minimum(i + 1, n - 1), 0, 0), memory_space=pltpu.SMEM),
            pl.BlockSpec((tm, d), lambda i: (i, 0)),
            pl.BlockSpec((tm, LANES), lambda i: (i, 0)),
            pl.BlockSpec((1, d), lambda i: (0, 0)),
            pl.BlockSpec(memory_space=pl.ANY),
        ],
        out_specs=[
            pl.BlockSpec((tm, d), lambda i: (_first_or_second(i, na)[0], 0)),
            pl.BlockSpec((tm, d), lambda i: (_first_or_second(i, na)[1], 0)),
        ],
        out_shape=[jax.ShapeDtypeStruct((t_first, d), F32), jax.ShapeDtypeStruct((t - t_first, d), F32)],
        scratch_shapes=[pltpu.VMEM((2, 2 * tm * TILE_ROWS, LANES), F32), pltpu.SemaphoreType.DMA((2,))],
        compiler_params=_cparams(("arbitrary",)),
        name="moe_combine",
    )(pos_blk, pos_blk, x2, gates, nw, ys)


def _block_experts(counts, n_blocks):
    per_expert = (counts + MOE_ROWS - 1) // MOE_ROWS
    ends = jnp.cumsum(per_expert)
    blk = jnp.arange(n_blocks, dtype=jnp.int32)
    blk_exp = jnp.minimum(jnp.sum(blk[:, None] >= ends[None, :], axis=1), N_EXPERTS - 1).astype(jnp.int32)
    last_blk = jnp.where(per_expert > 0, ends - 1, -1)
    tail = n_blocks - 1 - jnp.arange(N_EXPERTS)
    to_zero = jnp.concatenate([last_blk, jnp.where(tail >= ends[-1], tail, -1)]).astype(jnp.int32)
    return blk_exp, ends[-1:].astype(jnp.int32), to_zero


def _pick(n, pref):
    t = pref
    while n % t:
        t //= 2
    return t


def _encoder(xa, xb, mem, p):
    (nba, s, d), nbb = xa.shape, xb.shape[0]
    nb = nba + nbb
    t = nb * s
    t_first = nba * s
    nc = s // SSD_CHUNK
    tm = _pick(math.gcd(t_first, t - t_first), 1024)
    tq = _pick(s, 512)
    xa = xa.reshape(t_first, d)
    xb = xb.reshape(t - t_first, d)

    perm = jnp.arange(SSD_HEADS).reshape(SSD_GROUPS, HEADS_PER_GROUP)
    perm = jnp.concatenate([perm, perm + SSD_HEADS], axis=1).reshape(-1)

    w_in = p['w_in']
    o_z = 2 * D_MODEL
    o_x = o_z + SSD_D_INNER
    o_dt = o_x + SSD_D_INNER + 2 * SSD_GROUPS * SSD_STATE
    o_g = o_dt + 2 * SSD_HEADS
    w_val = w_in[:, :D_MODEL].astype(BF16)
    w_gate = w_in[:, D_MODEL:o_z].astype(BF16)
    w_z = w_in[:, o_z:o_x].astype(BF16)
    w_xbc = w_in[:, o_x:o_dt].astype(BF16)
    w_dt = jnp.pad(w_in[:, o_dt:o_g][:, perm], ((0, 0), (0, LANES - 2 * SSD_HEADS))).astype(BF16)
    w_g = w_in[:, o_g:].astype(BF16)
    nmix = p['norm_mix_w'].reshape(1, d)

    h = _prenorm(xa, xb, nmix, tm)
    u = _glu_proj(h, w_val, w_gate, tm, PROJ_TN // 2)
    xbc = _proj(h, w_xbc, lambda a: a, BF16, tm, PROJ_TN, "xbc_proj")
    dt_raw = _proj(h, w_dt, lambda a: a, F32, tm, LANES, "dt_proj")
    gates = _proj(h, w_g, _sigmoid, BF16, tm, PROJ_TN, "gate_proj")

    pad = LANES - 2 * SSD_HEADS
    dt_bias = jnp.pad(jnp.concatenate([p['ssd_dt_bias_f'], p['ssd_dt_bias_b']])[perm], (0, pad)).reshape(1, LANES)
    a_log = jnp.pad(jnp.concatenate([p['ssd_a_log_f'], p['ssd_a_log_b']])[perm], (0, pad)).reshape(1, LANES)
    acs2, rowt, e_hi, e_lo, w_hi, w_lo, dec = _dtprep(dt_raw, dt_bias, a_log, tm)

    def per_group_tm(a):
        return a[:, :2 * SSD_HEADS].reshape(nb, s, SSD_GROUPS, 8).transpose(0, 2, 1, 3)

    def per_group_hm(a):
        return a[:, :2 * SSD_HEADS].reshape(nb, nc, SSD_CHUNK, SSD_GROUPS, 8).transpose(0, 3, 1, 4, 2)

    acs_tm = per_group_tm(acs2).reshape(nb * SSD_GROUPS, s, 8)
    ew_tm = jnp.concatenate([per_group_tm(v) for v in (e_hi, e_lo, w_hi, w_lo)], axis=-1)
    ew_tm = ew_tm.reshape(nb * SSD_GROUPS, s, EW_COLS)
    rowt_hm = per_group_hm(rowt)
    dec = dec[:, :2 * SSD_HEADS].reshape(nb, nc, SSD_GROUPS, 2, HEADS_PER_GROUP).transpose(0, 2, 1, 3, 4)
    dec = jnp.repeat(dec, SSD_HEAD_DIM, axis=-1)

    cwx = p['ssd_conv_w']
    cbx = p['ssd_conv_b']

    def per_group_conv(a):
        xs_ = a[:, :SSD_D_INNER].reshape(-1, SSD_GROUPS, GROUP_CH)
        b_ = a[:, SSD_D_INNER:SSD_D_INNER + SSD_GROUPS * SSD_STATE].reshape(-1, SSD_GROUPS, SSD_STATE)
        c_ = a[:, SSD_D_INNER + SSD_GROUPS * SSD_STATE:].reshape(-1, SSD_GROUPS, SSD_STATE)
        return jnp.concatenate([xs_, b_, c_], axis=-1).transpose(1, 0, 2)

    cw = jnp.pad(per_group_conv(cwx), ((0, 0), (0, 8 - SSD_CONV_WIDTH), (0, 0)))
    cb = per_group_conv(cbx.reshape(1, -1))
    dskip = jnp.repeat(p['ssd_d'], SSD_HEAD_DIM).reshape(SSD_GROUPS, 1, GROUP_CH)
    nw_ssd = p['ssd_norm_w'].reshape(SSD_GROUPS, 1, GROUP_CH)

    yn = _ssd(xbc.reshape(nb, s, -1), h.reshape(nb, s, d), w_z, cw, cb, rowt_hm, acs_tm, ew_tm, dec, dskip, nw_ssd)
    mb = _gated_proj(yn.reshape(t, SSD_D_INNER), p['ssd_out'].astype(BF16), gates, 1, tm)

    dww = jnp.pad(p['conv_dw_w'], ((0, 32 - CONV_WIDTH), (0, 0)))
    x1 = _conformer(u.reshape(nb, s, d), mb.reshape(nb, s, d), gates.reshape(nb, s, 2 * d), xa, xb, dww,
                    p['conv_dw_b'].reshape(1, d), p['conv_ln_w'].reshape(1, d), p['conv_ln_b'].reshape(1, d),
                    p['conv_pw_out'].astype(BF16), p['w_out'].astype(BF16), tq)

    m = mem.shape[1]
    kv = _norm_proj(mem.reshape(nb * m, d), p['norm_mem_w'].reshape(1, d), p['xattn_wkv'].astype(BF16),
                    lambda a: a, BF16, _pick(nb * m, 512), 1024, "kv_proj")
    n_r = N_GROUPS + N_EXPERTS
    wr = jnp.pad(jnp.concatenate([p['router_group_w'], p['router_expert_w']], axis=1),
                 ((0, 0), (0, LANES - n_r))).astype(BF16)
    br = jnp.pad(jnp.concatenate([p['router_group_b'], p['router_expert_b']]), (0, LANES - n_r)).reshape(1, LANES)
    x2, h3, logits = _attention(x1, kv.reshape(nb, m, 2 * d), p['norm_xattn_w'].reshape(1, d),
                                p['xattn_wq'].astype(BF16), p['xattn_wo'].astype(BF16),
                                p['norm_ffn_w'].reshape(1, d), wr, br, tq)

    tc = _pick(math.gcd(t_first, t - t_first), 512)
    gate, pos, cnt = _plan(logits.reshape(t, LANES), tm)
    n_blocks = (2 * t) // MOE_ROWS + N_EXPERTS
    blk_exp, n_active, last_blk = _block_experts(cnt[0, :N_EXPERTS].astype(jnp.int32), n_blocks)
    pos_blk = pos[:, :2].reshape(t // tc, tc, 2).transpose(0, 2, 1).reshape(t // tc, 1, 2 * tc)
    xs = _dispatch(last_blk, pos_blk, h3, n_blocks)
    ys = _experts(blk_exp, n_active, xs, p['expert_w_gate'], p['expert_w_up'], p['expert_w_down'])
    return _combine(pos_blk, x2.reshape(t, d), gate, p['norm_final_w'].reshape(1, d), ys, tc, t_first)


def kernel(x_prompt, x_sample, mem_prompt, mem_sample, norm_mix_w, w_in, conv_dw_w, conv_dw_b, conv_ln_w, conv_ln_b, conv_pw_out, ssd_conv_w, ssd_conv_b, ssd_dt_bias_f, ssd_dt_bias_b, ssd_a_log_f, ssd_a_log_b, ssd_d, ssd_norm_w, ssd_out, w_out, norm_xattn_w, norm_mem_w, xattn_wq, xattn_wkv, xattn_wo, norm_ffn_w, router_group_w, router_group_b, router_expert_w, router_expert_b, expert_w_gate, expert_w_up, expert_w_down, norm_final_w):
    p = {
        'norm_mix_w': norm_mix_w[0], 'w_in': w_in[0], 'conv_dw_w': conv_dw_w[0], 'conv_dw_b': conv_dw_b[0],
        'conv_ln_w': conv_ln_w[0], 'conv_ln_b': conv_ln_b[0], 'conv_pw_out': conv_pw_out[0],
        'ssd_conv_w': ssd_conv_w[0], 'ssd_conv_b': ssd_conv_b[0], 'ssd_dt_bias_f': ssd_dt_bias_f[0],
        'ssd_dt_bias_b': ssd_dt_bias_b[0], 'ssd_a_log_f': ssd_a_log_f[0], 'ssd_a_log_b': ssd_a_log_b[0],
        'ssd_d': ssd_d[0], 'ssd_norm_w': ssd_norm_w[0], 'ssd_out': ssd_out[0], 'w_out': w_out[0],
        'norm_xattn_w': norm_xattn_w[0], 'norm_mem_w': norm_mem_w[0], 'xattn_wq': xattn_wq[0],
        'xattn_wkv': xattn_wkv[0], 'xattn_wo': xattn_wo[0], 'norm_ffn_w': norm_ffn_w[0],
        'router_group_w': router_group_w[0], 'router_group_b': router_group_b[0],
        'router_expert_w': router_expert_w[0], 'router_expert_b': router_expert_b[0],
        'expert_w_gate': expert_w_gate[0], 'expert_w_up': expert_w_up[0], 'expert_w_down': expert_w_down[0],
        'norm_final_w': norm_final_w,
    }
    mem = jnp.concatenate([mem_prompt, mem_sample], axis=0)
    y_prompt, y_sample = _encoder(x_prompt, x_sample, mem, p)
    return (y_prompt.reshape(x_prompt.shape), y_sample.reshape(x_sample.shape))
```

```python
import functools
import math

import jax
import jax.numpy as jnp
from jax import lax
from jax.experimental import pallas as pl
from jax.experimental.pallas import tpu as pltpu

F32 = jnp.float32
BF16 = jnp.bfloat16

D_MODEL = 1024
CONV_WIDTH = 31
SSD_D_INNER = 2048
SSD_HEAD_DIM = 64
SSD_HEADS = 32
SSD_GROUPS = 8
SSD_STATE = 128
SSD_CONV_WIDTH = 5
SSD_CHUNK = 128
GROUP_CH = SSD_D_INNER // SSD_GROUPS
HEADS_PER_GROUP = SSD_HEADS // SSD_GROUPS
XATTN_HEADS = 4
XATTN_HEAD_DIM = D_MODEL // XATTN_HEADS
N_GROUPS = 8
EXPERTS_PER_GROUP = 8
N_EXPERTS = 64
EXPERT_FF = 512
MOE_BLOCK = 128
RMS_EPS = 1e-6
LN_EPS = 1e-5

LANES = 128
HALO = 16
VMEM_LIMIT = 56 * 1024 * 1024
PROJ_TN = 2048


def _cparams(sem):
    return pltpu.CompilerParams(dimension_semantics=sem, vmem_limit_bytes=VMEM_LIMIT)


def _rms(x, w):
    ms = jnp.mean(x * x, axis=-1, keepdims=True)
    return x * lax.rsqrt(ms + RMS_EPS) * w


def _sigmoid(x):
    return 1.0 / (1.0 + jnp.exp2(x * -1.4426950408889634))


def _silu(x):
    return x * _sigmoid(x)


def _softplus(x):
    return jnp.maximum(x, 0.0) + jnp.log1p(jnp.exp(-jnp.abs(x)))


CONV_STRIDE = 4
CONV_ROWS = 8 * CONV_STRIDE
CONV_UNROLL = 4


def _depthwise_conv(src, dst, slab, w_ref, b_ref, n_rows, width, first_row, epilogue):
    lanes = pl.ds(slab * LANES, LANES)
    taps = [jnp.broadcast_to(w_ref[k:k + 1, lanes], (8, LANES)) for k in range(width)]
    bias = jnp.broadcast_to(b_ref[:, lanes], (8, LANES))
    step = CONV_ROWS * CONV_UNROLL

    def body(r, carry):
        _conv_rows(src, dst, slab, taps, bias, r * step, step, width, first_row, epilogue)
        return carry

    lax.fori_loop(0, n_rows // step, body, 0)


def _conv_rows(src, dst, slab, taps, bias, base, n_rows, width, first_row, epilogue):
    for t0 in range(n_rows // 8):
        row = base + (t0 // CONV_STRIDE) * CONV_ROWS + t0 % CONV_STRIDE
        acc = bias
        for k in range(width):
            acc = acc + src[slab, pl.ds(row + first_row - width // 2 + k, 8, stride=CONV_STRIDE), :] * taps[k]
        dst[slab, pl.ds(row, 8, stride=CONV_STRIDE), :] = epilogue(acc)


def _first_or_second(i, n_first):
    return jnp.minimum(i, n_first - 1), jnp.maximum(i - n_first, 0)


def _prenorm_kernel(n_first, xa_ref, xb_ref, nw_ref, o_ref):
    x = jnp.where(pl.program_id(0) < n_first, xa_ref[...], xb_ref[...])
    o_ref[...] = _rms(x, nw_ref[...]).astype(o_ref.dtype)


def _prenorm(xa, xb, nw, tm):
    (ta, d), tb = xa.shape, xb.shape[0]
    na = ta // tm
    return pl.pallas_call(
        functools.partial(_prenorm_kernel, na),
        grid=((ta + tb) // tm,),
        in_specs=[
            pl.BlockSpec((tm, d), lambda i: (_first_or_second(i, na)[0], 0)),
            pl.BlockSpec((tm, d), lambda i: (_first_or_second(i, na)[1], 0)),
            pl.BlockSpec((1, d), lambda i: (0, 0)),
        ],
        out_specs=pl.BlockSpec((tm, d), lambda i: (i, 0)),
        out_shape=jax.ShapeDtypeStruct((ta + tb, d), BF16),
        compiler_params=_cparams(("arbitrary",)),
        name="mix_norm",
    )(xa, xb, nw)


def _glu_kernel(h_ref, wv_ref, wg_ref, o_ref):
    h = h_ref[...]
    v = jnp.dot(h, wv_ref[...], preferred_element_type=F32)
    g = jnp.dot(h, wg_ref[...], preferred_element_type=F32)
    o_ref[...] = (v * _sigmoid(g)).astype(o_ref.dtype)


def _glu_proj(h, wv, wg, tm, tn):
    t, d = h.shape
    n = wv.shape[1]
    return pl.pallas_call(
        _glu_kernel,
        grid=(t // tm, n // tn),
        in_specs=[
            pl.BlockSpec((tm, d), lambda i, j: (i, 0)),
            pl.BlockSpec((d, tn), lambda i, j: (0, j)),
            pl.BlockSpec((d, tn), lambda i, j: (0, j)),
        ],
        out_specs=pl.BlockSpec((tm, tn), lambda i, j: (i, j)),
        out_shape=jax.ShapeDtypeStruct((t, n), BF16),
        compiler_params=_cparams(("parallel", "arbitrary")),
        name="glu_proj",
    )(h, wv, wg)


def _proj_kernel(epi, h_ref, w_ref, o_ref):
    o_ref[...] = epi(jnp.dot(h_ref[...], w_ref[...], preferred_element_type=F32)).astype(o_ref.dtype)


def _proj(h, w, epi, out_dtype, tm, tn, name):
    t, d = h.shape
    n = w.shape[1]
    return pl.pallas_call(
        functools.partial(_proj_kernel, epi),
        grid=(t // tm, n // tn),
        in_specs=[
            pl.BlockSpec((tm, d), lambda i, j: (i, 0)),
            pl.BlockSpec((d, tn), lambda i, j: (0, j)),
        ],
        out_specs=pl.BlockSpec((tm, tn), lambda i, j: (i, j)),
        out_shape=jax.ShapeDtypeStruct((t, n), out_dtype),
        compiler_params=_cparams(("parallel", "arbitrary")),
        name=name,
    )(h, w)


def _norm_proj_kernel(epi, x_ref, nw_ref, w_ref, o_ref, h_scr):
    @pl.when(pl.program_id(1) == 0)
    def _():
        h_scr[...] = _rms(x_ref[...], nw_ref[...]).astype(BF16)

    acc = jnp.dot(h_scr[...], w_ref[...], preferred_element_type=F32)
    o_ref[...] = epi(acc).astype(o_ref.dtype)


def _norm_proj(x, nw, w, epi, out_dtype, tm, tn, name):
    t, d = x.shape
    n = w.shape[1]
    return pl.pallas_call(
        functools.partial(_norm_proj_kernel, epi),
        grid=(t // tm, n // tn),
        in_specs=[
            pl.BlockSpec((tm, d), lambda i, j: (i, 0)),
            pl.BlockSpec((1, d), lambda i, j: (0, 0)),
            pl.BlockSpec((d, tn), lambda i, j: (0, j)),
        ],
        out_specs=pl.BlockSpec((tm, tn), lambda i, j: (i, j)),
        out_shape=jax.ShapeDtypeStruct((t, n), out_dtype),
        scratch_shapes=[pltpu.VMEM((tm, d), BF16)],
        compiler_params=_cparams(("parallel", "arbitrary")),
        name=name,
    )(x, nw, w)


def _gated_proj_kernel(y_ref, w_ref, g_ref, o_ref):
    acc = jnp.dot(y_ref[...], w_ref[...], preferred_element_type=F32)
    o_ref[...] = (acc * g_ref[...].astype(F32)).astype(o_ref.dtype)


def _gated_proj(y, w, g, g_col_block, tm):
    t, k = y.shape
    n = w.shape[1]
    return pl.pallas_call(
        _gated_proj_kernel,
        grid=(t // tm,),
        in_specs=[
            pl.BlockSpec((tm, k), lambda i: (i, 0)),
            pl.BlockSpec((k, n), lambda i: (0, 0)),
            pl.BlockSpec((tm, n), lambda i: (i, g_col_block)),
        ],
        out_specs=pl.BlockSpec((tm, n), lambda i: (i, 0)),
        out_shape=jax.ShapeDtypeStruct((t, n), BF16),
        compiler_params=_cparams(("parallel",)),
        name="ssd_out_proj",
    )(y, w, g)


LOG2E = 1.4426950408889634


def _split_bf16(v):
    hi = v.astype(BF16)
    return hi, (v - hi.astype(F32)).astype(BF16)


def _dtprep_kernel(raw_ref, bias_ref, alog_ref, acs2_ref, rowt_ref, ehi_ref, elo_ref, whi_ref, wlo_ref, dec_ref):
    rows = raw_ref.shape[0]
    a_head = -jnp.exp(alog_ref[...])
    row = lax.broadcasted_iota(jnp.int32, (SSD_CHUNK, LANES), 0)
    lane = lax.broadcasted_iota(jnp.int32, (SSD_CHUNK, LANES), 1)
    is_bwd = (lane % (2 * HEADS_PER_GROUP)) >= HEADS_PER_GROUP
    for c in range(rows // SSD_CHUNK):
        sl = pl.ds(c * SSD_CHUNK, SSD_CHUNK)
        dt = _softplus(raw_ref[sl, :] + bias_ref[...])
        a = dt * a_head
        fwd = a
        bwd = a
        sh = 1
        while sh < SSD_CHUNK:
            fwd = fwd + jnp.where(row >= sh, pltpu.roll(fwd, sh, 0), 0.0)
            bwd = bwd + jnp.where(row < SSD_CHUNK - sh, pltpu.roll(bwd, SSD_CHUNK - sh, 0), 0.0)
            sh *= 2
        acs = jnp.where(is_bwd, bwd, fwd)
        a_end = jnp.where(is_bwd[0:1, :], bwd[0:1, :], fwd[SSD_CHUNK - 1:SSD_CHUNK, :])
        acs2 = acs * LOG2E
        acs2_ref[sl, :] = acs2
        rowt_ref[sl, :] = acs2 - jnp.log(dt) * LOG2E
        ehi_ref[sl, :], elo_ref[sl, :] = _split_bf16(jnp.exp(acs))
        whi_ref[sl, :], wlo_ref[sl, :] = _split_bf16(jnp.exp(a_end - acs) * dt)
        dec_ref[c:c + 1, :] = jnp.exp(a_end)


def _dtprep(raw, bias, alog, tm):
    t = raw.shape[0]
    spec = pl.BlockSpec((tm, LANES), lambda i: (i, 0))
    vec = pl.BlockSpec((1, LANES), lambda i: (0, 0))
    return pl.pallas_call(
        _dtprep_kernel,
        grid=(t // tm,),
        in_specs=[spec, vec, vec],
        out_specs=[spec] * 6 + [pl.BlockSpec((tm // SSD_CHUNK, LANES), lambda i: (i, 0))],
        out_shape=[jax.ShapeDtypeStruct((t, LANES), F32)] * 2 + [jax.ShapeDtypeStruct((t, LANES), BF16)] * 4
        + [jax.ShapeDtypeStruct((t // SSD_CHUNK, LANES), F32)],
        compiler_params=_cparams(("parallel",)),
        name="dt_prep",
    )(raw, bias, alog)


SSD_PAD = 8
XBC_W = GROUP_CH + 2 * SSD_STATE


EW_COLS = 4 * 2 * HEADS_PER_GROUP


def _head_expand_matrix():
    col = jnp.arange(4 * GROUP_CH)
    part = col // GROUP_CH
    is_w = (part == 1) | (part == 2)
    is_bwd = part >= 2
    head = is_bwd * HEADS_PER_GROUP + (col % GROUP_CH) // SSD_HEAD_DIM
    hi_row = is_w * 16 + head
    row = jnp.arange(EW_COLS)[:, None]
    return ((row == hi_row[None, :]) | (row == hi_row[None, :] + 8)).astype(BF16)


def _ssd_kernel(xs_ref, b_ref, c_ref, h_ref, wz_ref, cw_ref, cb_ref, rowt_ref, acs_ref, ew_ref, sel_ref, dec_ref,
                dskip_ref, nw_ref, o_ref, raw_scr, act_scr, y_scr, upd_scr, st_scr):
    s = xs_ref.shape[1]
    nc = s // SSD_CHUNK
    L = SSD_CHUNK

    n_slabs = XBC_W // LANES
    for j in range(n_slabs):
        raw_scr[j, pl.ds(0, SSD_PAD), :] = jnp.zeros((SSD_PAD, LANES), F32)
        raw_scr[j, pl.ds(SSD_PAD + s, L + SSD_PAD), :] = jnp.zeros((L + SSD_PAD, LANES), F32)
    raw_scr[0, pl.ds(SSD_PAD, s), :] = xs_ref[0, :, :LANES].astype(F32)
    raw_scr[1, pl.ds(SSD_PAD, s), :] = xs_ref[0, :, LANES:].astype(F32)
    raw_scr[2, pl.ds(SSD_PAD, s), :] = b_ref[0].astype(F32)
    raw_scr[3, pl.ds(SSD_PAD, s), :] = c_ref[0].astype(F32)

    def conv_chunk(c):
        for j in range(n_slabs):
            lanes = pl.ds(j * LANES, LANES)
            taps = [jnp.broadcast_to(cw_ref[0, k:k + 1, lanes], (8, LANES)) for k in range(SSD_CONV_WIDTH)]
            bias = jnp.broadcast_to(cb_ref[0, :, lanes], (8, LANES))
            _conv_rows(raw_scr, act_scr, j, taps, bias, c * L, L, SSD_CONV_WIDTH, SSD_PAD, _silu)

    conv_chunk(0)

    li = lax.broadcasted_iota(jnp.int32, (L, L), 0)
    si = lax.broadcasted_iota(jnp.int32, (L, L), 1)
    head_of_ch = lax.broadcasted_iota(jnp.int32, (L, GROUP_CH), 1) // SSD_HEAD_DIM

    keep = (li >= si, si >= li)
    neg_inf = jnp.float32(-jnp.inf)
    n_heads = 2 * HEADS_PER_GROUP
    st_scr[...] = jnp.zeros_like(st_scr)

    def local_body(c, carry):
        r0 = pl.multiple_of(c * L, L)
        x = jnp.concatenate([act_scr[0, pl.ds(r0, L), :], act_scr[1, pl.ds(r0, L), :]], axis=1)
        xb = x.astype(BF16)
        bk = act_scr[2, pl.ds(r0, L), :].astype(BF16)
        ck = act_scr[3, pl.ds(r0, L), :].astype(BF16)
        acs = acs_ref[0, pl.ds(r0, L), :]
        rowt = rowt_ref[0, 0, c]
        ew = jnp.dot(ew_ref[0, pl.ds(r0, L), :], sel_ref[:, :3 * GROUP_CH],
                     preferred_element_type=F32)

        cb = lax.dot_general(ck, bk, (((1,), (1,)), ((), ())), preferred_element_type=F32)
        ms = []
        for j in range(n_heads):
            diff = acs[:, j:j + 1] - rowt[j:j + 1, :]
            decay_dt = jnp.exp2(jnp.where(keep[j // HEADS_PER_GROUP], diff, neg_inf))
            ms.append((cb * decay_dt).astype(BF16))
        xbd = jnp.concatenate([jnp.where(head_of_ch == h, xb, jnp.zeros_like(xb))
                               for h in range(HEADS_PER_GROUP)], axis=0)
        y = jnp.dot(jnp.concatenate(ms, axis=1), jnp.concatenate([xbd, xbd], axis=0),
                    preferred_element_type=F32) + x * dskip_ref[0]

        xw = jnp.concatenate([x * ew[:, GROUP_CH:2 * GROUP_CH], x * ew[:, 2 * GROUP_CH:]], axis=1)
        upd = lax.dot_general(bk, xw.astype(BF16), (((0,), (0,)), ((), ())), preferred_element_type=F32)

        state = st_scr[0]
        y = y + jnp.dot(ck, state.astype(BF16), preferred_element_type=F32) * ew[:, :GROUP_CH]
        st_scr[0] = state * dec_ref[0, 0, c, 0:1, :] + upd[:, :GROUP_CH]
        y_scr[pl.ds(r0, L), :] = y
        upd_scr[c] = upd[:, GROUP_CH:]
        conv_chunk(c + 1)
        return carry

    lax.fori_loop(0, nc, local_body, 0, unroll=2)

    def bwd_body(i, carry):
        c = nc - 1 - i
        r0 = pl.multiple_of(c * L, L)
        ck = act_scr[3, pl.ds(r0, L), :].astype(BF16)
        e_bwd = jnp.dot(ew_ref[0, pl.ds(r0, L), :], sel_ref[:, 3 * GROUP_CH:], preferred_element_type=F32)
        state = st_scr[1]
        y = y_scr[pl.ds(r0, L), :] + jnp.dot(ck, state.astype(BF16), preferred_element_type=F32) * e_bwd
        st_scr[1] = state * dec_ref[0, 0, c, 1:2, :] + upd_scr[c]
        z = jnp.dot(h_ref[0, pl.ds(r0, L), :], wz_ref[...], preferred_element_type=F32)
        yz = y * _silu(z)
        ms = jnp.mean(yz * yz, axis=-1, keepdims=True)
        o_ref[0, pl.ds(r0, L), :] = (yz * lax.rsqrt(ms + RMS_EPS) * nw_ref[0]).astype(o_ref.dtype)
        return carry

    lax.fori_loop(0, nc, bwd_body, 0, unroll=4)


def _ssd(xbc, h, wz, cw, cb, rowt_hm, acs_tm, ew_tm, dec, dskip, nw):
    nb, s, _ = xbc.shape
    d = h.shape[2]
    nc = s // SSD_CHUNK
    return pl.pallas_call(
        _ssd_kernel,
        grid=(nb, SSD_GROUPS),
        in_specs=[
            pl.BlockSpec((1, s, GROUP_CH), lambda b, g: (b, 0, g)),
            pl.BlockSpec((1, s, SSD_STATE), lambda b, g: (b, 0, SSD_D_INNER // SSD_STATE + g)),
            pl.BlockSpec((1, s, SSD_STATE), lambda b, g: (b, 0, SSD_D_INNER // SSD_STATE + SSD_GROUPS + g)),
            pl.BlockSpec((1, s, d), lambda b, g: (b, 0, 0)),
            pl.BlockSpec((d, GROUP_CH), lambda b, g: (0, g)),
            pl.BlockSpec((1, 8, XBC_W), lambda b, g: (g, 0, 0)),
            pl.BlockSpec((1, 1, XBC_W), lambda b, g: (g, 0, 0)),
            pl.BlockSpec((1, 1, nc, 8, SSD_CHUNK), lambda b, g: (b, g, 0, 0, 0)),
            pl.BlockSpec((1, s, 8), lambda b, g: (b * SSD_GROUPS + g, 0, 0)),
            pl.BlockSpec((1, s, EW_COLS), lambda b, g: (b * SSD_GROUPS + g, 0, 0)),
            pl.BlockSpec((EW_COLS, 4 * GROUP_CH), lambda b, g: (0, 0)),
            pl.BlockSpec((1, 1, nc, 2, GROUP_CH), lambda b, g: (b, g, 0, 0, 0)),
            pl.BlockSpec((1, 1, GROUP_CH), lambda b, g: (g, 0, 0)),
            pl.BlockSpec((1, 1, GROUP_CH), lambda b, g: (g, 0, 0)),
        ],
        out_specs=pl.BlockSpec((1, s, GROUP_CH), lambda b, g: (b, 0, g)),
        out_shape=jax.ShapeDtypeStruct((nb, s, SSD_D_INNER), BF16),
        scratch_shapes=[
            pltpu.VMEM((XBC_W // LANES, s + SSD_CHUNK + 2 * SSD_PAD, LANES), F32),
            pltpu.VMEM((XBC_W // LANES, s + SSD_CHUNK, LANES), F32),
            pltpu.VMEM((s, GROUP_CH), F32),
            pltpu.VMEM((nc, SSD_STATE, GROUP_CH), F32),
            pltpu.VMEM((2, SSD_STATE, GROUP_CH), F32),
        ],
        compiler_params=_cparams(("parallel", "parallel")),
        name="ssd_scan",
    )(xbc, xbc, xbc, h, wz, cw, cb, rowt_hm, acs_tm, ew_tm, _head_expand_matrix(), dec, dskip, nw)


D_SLABS = D_MODEL // LANES


def _conformer_kernel(n_first, u_ref, up_ref, un_ref, mb_ref, g_ref, xa_ref, xb_ref, dww_ref, dwb_ref, lnw_ref,
                      lnb_ref, pw_ref, wo_ref, o_ref, u_scr, c_scr):
    i = pl.program_id(1)
    n = pl.num_programs(1)
    tq = u_ref.shape[1]
    x = jnp.where(pl.program_id(0) * n + i < n_first, xa_ref[...], xb_ref[...])
    for j in range(D_SLABS):
        cs = slice(j * LANES, (j + 1) * LANES)
        u_scr[j, pl.ds(0, HALO), :] = jnp.where(i > 0, up_ref[0, :, cs].astype(F32), 0.0)
        u_scr[j, pl.ds(HALO, tq), :] = u_ref[0, :, cs].astype(F32)
        u_scr[j, pl.ds(HALO + tq, HALO), :] = jnp.where(i < n - 1, un_ref[0, :, cs].astype(F32), 0.0)
    for j in range(D_SLABS):
        _depthwise_conv(u_scr, c_scr, j, dww_ref, dwb_ref, tq, CONV_WIDTH, HALO, lambda a: a)

    cv = jnp.concatenate([c_scr[j] for j in range(D_SLABS)], axis=1)
    mu = jnp.mean(cv, axis=-1, keepdims=True)
    xc = cv - mu
    var = jnp.mean(xc * xc, axis=-1, keepdims=True)
    y = xc * lax.rsqrt(var + LN_EPS) * lnw_ref[...] + lnb_ref[...]
    a = jnp.dot(_silu(y).astype(BF16), pw_ref[...], preferred_element_type=F32)
    merged = g_ref[0].astype(F32) * a + mb_ref[0].astype(F32)
    o_ref[0] = x + jnp.dot(merged.astype(BF16), wo_ref[...], preferred_element_type=F32)


def _conformer(u, mb, gates, xa, xb, dww, dwb, lnw, lnb, pw, wo, tq):
    nb, s, d = u.shape
    hb = tq // HALO
    nh = s // HALO
    nq = s // tq
    na = xa.shape[0] // tq
    wspec = pl.BlockSpec((d, d), lambda b, i: (0, 0))
    vspec = pl.BlockSpec((1, d), lambda b, i: (0, 0))
    return pl.pallas_call(
        functools.partial(_conformer_kernel, na),
        grid=(nb, nq),
        in_specs=[
            pl.BlockSpec((1, tq, d), lambda b, i: (b, i, 0)),
            pl.BlockSpec((1, HALO, d), lambda b, i: (b, jnp.maximum(i * hb - 1, 0), 0)),
            pl.BlockSpec((1, HALO, d), lambda b, i: (b, jnp.minimum((i + 1) * hb, nh - 1), 0)),
            pl.BlockSpec((1, tq, d), lambda b, i: (b, i, 0)),
            pl.BlockSpec((1, tq, d), lambda b, i: (b, i, 0)),
            pl.BlockSpec((tq, d), lambda b, i: (_first_or_second(b * nq + i, na)[0], 0)),
            pl.BlockSpec((tq, d), lambda b, i: (_first_or_second(b * nq + i, na)[1], 0)),
            pl.BlockSpec((32, d), lambda b, i: (0, 0)),
            vspec, vspec, vspec, wspec, wspec,
        ],
        out_specs=pl.BlockSpec((1, tq, d), lambda b, i: (b, i, 0)),
        out_shape=jax.ShapeDtypeStruct((nb, s, d), F32),
        scratch_shapes=[pltpu.VMEM((D_SLABS, tq + 2 * HALO, LANES), F32), pltpu.VMEM((D_SLABS, tq, LANES), F32)],
        compiler_params=_cparams(("arbitrary", "arbitrary")),
        name="conformer_merge",
    )(u, u, u, mb, gates, xa, xb, dww, dwb, lnw, lnb, pw, wo)


TILE_ROWS = D_MODEL // (2 * LANES)
HIGH_HALF = 0xFFFF0000


def _bf16_bits(v):
    return lax.bitcast_convert_type(v.astype(BF16).astype(F32), jnp.uint32)


def _store_token_tiles(ref, val):
    n, d = val.shape
    packed = (_bf16_bits(val[:, :d // 2]) >> 16) | (_bf16_bits(val[:, d // 2:]) & jnp.uint32(HIGH_HALF))
    for j in range(TILE_ROWS):
        ref[pl.ds(j, n, stride=TILE_ROWS), :] = packed[:, j * LANES:(j + 1) * LANES]


def _load_token_tiles(ref, first, n):
    packed = jnp.concatenate([ref[pl.ds(first * TILE_ROWS + j, n, stride=TILE_ROWS), :] for j in range(TILE_ROWS)],
                             axis=1)
    low = lax.bitcast_convert_type(packed << 16, F32)
    high = lax.bitcast_convert_type(packed & jnp.uint32(HIGH_HALF), F32)
    return jnp.concatenate([low, high], axis=1)


def _attn_kernel(x_ref, k_ref, v_ref, nw_ref, wq_ref, wo_ref, nf_ref, wr_ref, br_ref, x2_ref, h3_ref, lg_ref):
    x = x_ref[0]
    h = _rms(x, nw_ref[...]).astype(BF16)
    q = jnp.dot(h, wq_ref[...], preferred_element_type=F32)
    scale = 1.0 / math.sqrt(XATTN_HEAD_DIM)
    outs = []
    for hd in range(XATTN_HEADS):
        cs = slice(hd * XATTN_HEAD_DIM, (hd + 1) * XATTN_HEAD_DIM)
        sc = lax.dot_general(q[:, cs].astype(BF16), k_ref[0, :, cs], (((1,), (1,)), ((), ())),
                             preferred_element_type=F32) * scale
        m = jnp.max(sc, axis=-1, keepdims=True)
        e = jnp.exp(sc - m)
        p = e / jnp.sum(e, axis=-1, keepdims=True)
        outs.append(jnp.dot(p.astype(BF16), v_ref[0, :, cs], preferred_element_type=F32))
    o = jnp.concatenate(outs, axis=1).astype(BF16)
    x2 = x + jnp.dot(o, wo_ref[...], preferred_element_type=F32)
    x2_ref[0] = x2
    h3 = _rms(x2, nf_ref[...])
    _store_token_tiles(h3_ref, h3)
    lg_ref[0] = jnp.dot(h3.astype(BF16), wr_ref[...], preferred_element_type=F32) + br_ref[...]


def _attention(x, kv, nw, wq, wo, nf, wr, br, tq):
    nb, s, d = x.shape
    m = kv.shape[1]
    wspec = pl.BlockSpec((d, d), lambda b, i: (0, 0))
    vspec = pl.BlockSpec((1, d), lambda b, i: (0, 0))
    return pl.pallas_call(
        _attn_kernel,
        grid=(nb, s // tq),
        in_specs=[
            pl.BlockSpec((1, tq, d), lambda b, i: (b, i, 0)),
            pl.BlockSpec((1, m, d), lambda b, i: (b, 0, 0)),
            pl.BlockSpec((1, m, d), lambda b, i: (b, 0, 1)),
            vspec, wspec, wspec, vspec,
            pl.BlockSpec((d, LANES), lambda b, i: (0, 0)),
            pl.BlockSpec((1, LANES), lambda b, i: (0, 0)),
        ],
        out_specs=[
            pl.BlockSpec((1, tq, d), lambda b, i: (b, i, 0)),
            pl.BlockSpec((tq * TILE_ROWS, LANES), lambda b, i: (b * (s // tq) + i, 0)),
            pl.BlockSpec((1, tq, LANES), lambda b, i: (b, i, 0)),
        ],
        out_shape=[
            jax.ShapeDtypeStruct((nb, s, d), F32),
            jax.ShapeDtypeStruct((nb * s * TILE_ROWS, LANES), jnp.uint32),
            jax.ShapeDtypeStruct((nb, s, LANES), F32),
        ],
        compiler_params=_cparams(("parallel", "parallel")),
        name="cross_attn_router",
    )(x, kv, kv, nw, wq, wo, nf, wr, br)


MOE_ROWS = 512


def _route_math(lg, lane):
    lane_f = lane.astype(F32)
    big = float(LANES)
    neg = -jnp.inf

    is_g = lane < N_GROUPS
    gl = jnp.where(is_g, lg, neg)
    gmax = jnp.max(gl, axis=-1, keepdims=True)
    g_idx = jnp.min(jnp.where(gl == gmax, lane_f, big), axis=-1, keepdims=True)
    g_sum = jnp.sum(jnp.where(is_g, jnp.exp(gl - gmax), 0.0), axis=-1, keepdims=True)
    g_w = 1.0 / g_sum

    e_grp = lax.shift_right_arithmetic(lane - N_GROUPS, 3).astype(F32)
    is_e = (lane >= N_GROUPS) & (lane < N_GROUPS + N_EXPERTS) & (e_grp == g_idx)
    el = jnp.where(is_e, lg, neg)
    emax = jnp.max(el, axis=-1, keepdims=True)
    ee = jnp.where(is_e, jnp.exp(el - emax), 0.0)
    ep = ee / jnp.sum(ee, axis=-1, keepdims=True)
    ep = jnp.where(is_e, ep, -1.0)
    v1 = jnp.max(ep, axis=-1, keepdims=True)
    i1 = jnp.min(jnp.where(ep == v1, lane_f, big), axis=-1, keepdims=True)
    ep2 = jnp.where(lane_f == i1, -1.0, ep)
    v2 = jnp.max(ep2, axis=-1, keepdims=True)
    i2 = jnp.min(jnp.where(ep2 == v2, lane_f, big), axis=-1, keepdims=True)
    tot = v1 + v2
    gate1 = g_w * (v1 / tot)
    gate2 = g_w * (v2 / tot)
    return i1 - N_GROUPS, i2 - N_GROUPS, gate1, gate2


def _plan_kernel(lg_ref, gate_ref, pos_ref, cnt_ref, carry_scr, base_scr):
    p = pl.program_id(0)
    i = pl.program_id(1)
    tm = lg_ref.shape[0]
    lane = lax.broadcasted_iota(jnp.int32, (tm, LANES), 1)
    lane_f = lane.astype(F32)
    e1, e2, gate1, gate2 = _route_math(lg_ref[...], lane)
    oh1 = lane_f == e1
    oh2 = lane_f == e2
    m = jnp.where(oh1 | oh2, 1.0, 0.0)
    colsum = jnp.sum(m, axis=0, keepdims=True)

    @pl.when((p == 0) & (i == 0))
    def _():
        carry_scr[...] = jnp.zeros_like(carry_scr)

    @pl.when(p == 0)
    def _():
        carry_scr[...] += colsum

    @pl.when((p == 1) & (i == 0))
    def _():
        counts = carry_scr[...]
        cnt_ref[...] = counts
        shift = MOE_ROWS.bit_length() - 1
        blocks = lax.shift_right_logical(counts.astype(jnp.int32) + (MOE_ROWS - 1), shift)
        padded = (blocks * MOE_ROWS).astype(F32)
        lane8 = lax.broadcasted_iota(jnp.int32, padded.shape, 1)
        inc = padded
        sh = 1
        while sh < LANES:
            inc = inc + jnp.where(lane8 >= sh, pltpu.roll(inc, sh, 1), 0.0)
            sh *= 2
        base_scr[...] = inc - padded
        carry_scr[...] = jnp.zeros_like(carry_scr)

    @pl.when(p == 1)
    def _():
        r = lax.broadcasted_iota(jnp.int32, (tm, tm), 0)
        c = lax.broadcasted_iota(jnp.int32, (tm, tm), 1)
        earlier = jnp.where(r > c, 1.0, 0.0).astype(BF16)
        tot = (jnp.dot(earlier, m.astype(BF16), preferred_element_type=F32)
               + base_scr[0:1, :] + carry_scr[0:1, :])
        pos1 = jnp.sum(jnp.where(oh1, tot, 0.0), axis=1, keepdims=True)
        pos2 = jnp.sum(jnp.where(oh2, tot, 0.0), axis=1, keepdims=True)
        pos_ref[...] = jnp.where(lane == 0, pos1, jnp.where(lane == 1, pos2, 0.0)).astype(jnp.int32)
        gate_ref[...] = jnp.where(lane == 0, gate1, jnp.where(lane == 1, gate2, 0.0))
        carry_scr[...] += colsum


def _plan(lg, tm):
    t = lg.shape[0]
    out_spec = pl.BlockSpec((tm, LANES), lambda p, i: (i * p, 0))
    return pl.pallas_call(
        _plan_kernel,
        grid=(2, t // tm),
        in_specs=[pl.BlockSpec((tm, LANES), lambda p, i: (i, 0))],
        out_specs=[out_spec, out_spec, pl.BlockSpec((8, LANES), lambda p, i: (0, 0))],
        out_shape=[jax.ShapeDtypeStruct((t, LANES), F32), jax.ShapeDtypeStruct((t, LANES), jnp.int32),
                   jax.ShapeDtypeStruct((8, LANES), F32)],
        scratch_shapes=[pltpu.VMEM((8, LANES), F32), pltpu.VMEM((8, LANES), F32)],
        compiler_params=_cparams(("arbitrary", "arbitrary")),
        name="moe_plan",
    )(lg)


def _dispatch_kernel(last_ref, pos_ref, h_ref, xs_hbm, zero_scr, sem):
    td = pos_ref.shape[2] // 2

    @pl.when(pl.program_id(0) == 0)
    def _():
        zero_scr[...] = jnp.zeros_like(zero_scr)

        def block_copy(e):
            first = pl.multiple_of(last_ref[e] * (MOE_ROWS * TILE_ROWS), MOE_ROWS * TILE_ROWS)
            return pltpu.make_async_copy(zero_scr, xs_hbm.at[pl.ds(first, MOE_ROWS * TILE_ROWS)], sem)

        def fill(e, carry):
            @pl.when(last_ref[e] >= 0)
            def _():
                block_copy(e).start()
            return carry

        def drain(e, carry):
            @pl.when(last_ref[e] >= 0)
            def _():
                block_copy(e).wait()
            return carry

        lax.fori_loop(0, last_ref.shape[0], fill, 0)
        lax.fori_loop(0, last_ref.shape[0], drain, 0)

    def tile(ref, tok):
        return ref.at[pl.ds(pl.multiple_of(tok * TILE_ROWS, TILE_ROWS), TILE_ROWS)]

    def body(r, carry):
        src = tile(h_ref, r)
        pltpu.make_async_copy(src, tile(xs_hbm, pos_ref[0, 0, r]), sem).start(priority=0)
        pltpu.make_async_copy(src, tile(xs_hbm, pos_ref[0, 0, td + r]), sem).start(priority=1)
        return carry

    lax.fori_loop(0, td, body, 0, unroll=8)
    for _ in range(2):
        pltpu.make_async_copy(h_ref, xs_hbm.at[pl.ds(0, td * TILE_ROWS)], sem).wait()


def _dispatch(last_blk, pos_blk, h3_tiles, n_blocks):
    n_steps = pos_blk.shape[0]
    td = pos_blk.shape[2] // 2
    grid_spec = pltpu.PrefetchScalarGridSpec(
        num_scalar_prefetch=1,
        grid=(n_steps,),
        in_specs=[
            pl.BlockSpec((1, 1, 2 * td), lambda i, last: (i, 0, 0), memory_space=pltpu.SMEM),
            pl.BlockSpec((td * TILE_ROWS, LANES), lambda i, last: (i, 0)),
        ],
        out_specs=pl.BlockSpec(memory_space=pl.ANY),
        scratch_shapes=[pltpu.VMEM((MOE_ROWS * TILE_ROWS, LANES), jnp.uint32), pltpu.SemaphoreType.DMA(())],
    )
    return pl.pallas_call(
        _dispatch_kernel,
        grid_spec=grid_spec,
        out_shape=jax.ShapeDtypeStruct((n_blocks * MOE_ROWS * TILE_ROWS, LANES), jnp.uint32),
        compiler_params=_cparams(("arbitrary",)),
        name="moe_dispatch",
    )(last_blk, pos_blk, h3_tiles)


def _expert_kernel(be_ref, na_ref, x_ref, wg_ref, wu_ref, wd_ref, o_ref, wg_scr, wu_scr, wd_scr):
    j = pl.program_id(0)
    prev = be_ref[jnp.maximum(j - 1, 0)]

    @pl.when((j == 0) | (be_ref[j] != prev))
    def _():
        wg_scr[...] = wg_ref[0].astype(BF16)
        wu_scr[...] = wu_ref[0].astype(BF16)
        wd_scr[...] = wd_ref[0].astype(BF16)

    @pl.when(j < na_ref[0])
    def _():
        x = _load_token_tiles(x_ref, 0, MOE_ROWS).astype(BF16)
        g = jnp.dot(x, wg_scr[...], preferred_element_type=F32)
        u = jnp.dot(x, wu_scr[...], preferred_element_type=F32)
        hid = (_silu(g) * u).astype(BF16)
        _store_token_tiles(o_ref, jnp.dot(hid, wd_scr[...], preferred_element_type=F32))

    @pl.when(j >= na_ref[0])
    def _():
        o_ref[...] = jnp.zeros_like(o_ref)


def _experts(blk_exp, n_active, xs_tiles, wg, wu, wd):
    n_blocks = blk_exp.shape[0]
    d, ff = wg.shape[1], wg.shape[2]
    blk = (MOE_ROWS * TILE_ROWS, LANES)
    grid_spec = pltpu.PrefetchScalarGridSpec(
        num_scalar_prefetch=2,
        grid=(n_blocks,),
        in_specs=[
            pl.BlockSpec(blk, lambda j, be, na: (jnp.minimum(j, na[0] - 1), 0)),
            pl.BlockSpec((1, d, ff), lambda j, be, na: (be[j], 0, 0)),
            pl.BlockSpec((1, d, ff), lambda j, be, na: (be[j], 0, 0)),
            pl.BlockSpec((1, ff, d), lambda j, be, na: (be[j], 0, 0)),
        ],
        out_specs=pl.BlockSpec(blk, lambda j, be, na: (j, 0)),
        scratch_shapes=[pltpu.VMEM((d, ff), BF16), pltpu.VMEM((d, ff), BF16), pltpu.VMEM((ff, d), BF16)],
    )
    return pl.pallas_call(
        _expert_kernel,
        grid_spec=grid_spec,
        out_shape=jax.ShapeDtypeStruct(xs_tiles.shape, jnp.uint32),
        compiler_params=_cparams(("arbitrary",)),
        name="moe_experts",
    )(blk_exp, n_active, xs_tiles, wg, wu, wd)


COMBINE_ROWS = 32


def _combine_kernel(n_first, pos_ref, pos_next_ref, x_ref, gate_ref, nw_ref, ys_hbm, oa_ref, ob_ref, y_scr, sem):
    i = pl.program_id(0)
    tm = x_ref.shape[0]
    slot = i % 2

    def tile(ref, tok):
        return ref.at[pl.ds(pl.multiple_of(tok * TILE_ROWS, TILE_ROWS), TILE_ROWS)]

    def gather(p_ref, to):
        buf = y_scr.at[to]

        def body(r, carry):
            pltpu.make_async_copy(tile(ys_hbm, p_ref[0, 0, r]), tile(buf, r), sem.at[to]).start(priority=0)
            pltpu.make_async_copy(tile(ys_hbm, p_ref[0, 0, tm + r]), tile(buf, tm + r), sem.at[to]).start(priority=1)
            return carry

        lax.fori_loop(0, tm, body, 0, unroll=8)

    @pl.when(i == 0)
    def _():
        gather(pos_ref, 0)

    @pl.when(i + 1 < pl.num_programs(0))
    def _():
        gather(pos_next_ref, 1 - slot)

    buf = y_scr.at[slot]
    pltpu.make_async_copy(ys_hbm.at[pl.ds(0, 2 * tm * TILE_ROWS)], buf, sem.at[slot]).wait()
    def finish(o_ref):
        def rows(rb, carry):
            r0 = pl.multiple_of(rb * COMBINE_ROWS, COMBINE_ROWS)
            sl = pl.ds(r0, COMBINE_ROWS)
            gate = gate_ref[sl, :]
            y = x_ref[sl, :] + (gate[:, 0:1] * _load_token_tiles(buf, r0, COMBINE_ROWS)
                                + gate[:, 1:2] * _load_token_tiles(buf, tm + r0, COMBINE_ROWS))
            o_ref[sl, :] = _rms(y, nw_ref[...])
            return carry

        lax.fori_loop(0, tm // COMBINE_ROWS, rows, 0, unroll=4)

    @pl.when(i < n_first)
    def _():
        finish(oa_ref)

    @pl.when(i >= n_first)
    def _():
        finish(ob_ref)


def _combine(pos_blk, x2, gates, nw, ys, tm, t_first):
    t, d = x2.shape
    na = t_first // tm
    n = t // tm
    return pl.pallas_call(
        functools.partial(_combine_kernel, na),
        grid=(n,),
        in_specs=[
            pl.BlockSpec((1, 1, 2 * tm), lambda i: (i, 0, 0), memory_space=pltpu.SMEM),
            pl.BlockSpec((1, 1, 2 * tm), lambda i: (jnp.minimum(i + 1, n - 1), 0, 0), memory_space=pltpu.SMEM),
            pl.BlockSpec((tm, d), lambda i: (i, 0)),
            pl.BlockSpec((tm, LANES), lambda i: (i, 0)),
            pl.BlockSpec((1, d), lambda i: (0, 0)),
            pl.BlockSpec(memory_space=pl.ANY),
        ],
        out_specs=[
            pl.BlockSpec((tm, d), lambda i: (_first_or_second(i, na)[0], 0)),
            pl.BlockSpec((tm, d), lambda i: (_first_or_second(i, na)[1], 0)),
        ],
        out_shape=[jax.ShapeDtypeStruct((t_first, d), F32), jax.ShapeDtypeStruct((t - t_first, d), F32)],
        scratch_shapes=[pltpu.VMEM((2, 2 * tm * TILE_ROWS, LANES), jnp.uint32), pltpu.SemaphoreType.DMA((2,))],
        compiler_params=_cparams(("arbitrary",)),
        name="moe_combine",
    )(pos_blk, pos_blk, x2, gates, nw, ys)


def _block_experts(counts, n_blocks):
    per_expert = (counts + MOE_ROWS - 1) // MOE_ROWS
    ends = jnp.cumsum(per_expert)
    blk = jnp.arange(n_blocks, dtype=jnp.int32)
    blk_exp = jnp.minimum(jnp.sum(blk[:, None] >= ends[None, :], axis=1), N_EXPERTS - 1).astype(jnp.int32)
    last_blk = jnp.where(per_expert > 0, ends - 1, -1)
    tail = n_blocks - 1 - jnp.arange(N_EXPERTS)
    to_zero = jnp.concatenate([last_blk, jnp.where(tail >= ends[-1], tail, -1)]).astype(jnp.int32)
    return blk_exp, ends[-1:].astype(jnp.int32), to_zero


def _pick(n, pref):
    t = pref
    while n % t:
        t //= 2
    return t


def _encoder(xa, xb, mem, p):
    (nba, s, d), nbb = xa.shape, xb.shape[0]
    nb = nba + nbb
    t = nb * s
    t_first = nba * s
    nc = s // SSD_CHUNK
    tm = _pick(math.gcd(t_first, t - t_first), 1024)
    tq = _pick(s, 512)
    xa = xa.reshape(t_first, d)
    xb = xb.reshape(t - t_first, d)

    perm = jnp.arange(SSD_HEADS).reshape(SSD_GROUPS, HEADS_PER_GROUP)
    perm = jnp.concatenate([perm, perm + SSD_HEADS], axis=1).reshape(-1)

    w_in = p['w_in']
    o_z = 2 * D_MODEL
    o_x = o_z + SSD_D_INNER
    o_dt = o_x + SSD_D_INNER + 2 * SSD_GROUPS * SSD_STATE
    o_g = o_dt + 2 * SSD_HEADS
    w_val = w_in[:, :D_MODEL].astype(BF16)
    w_gate = w_in[:, D_MODEL:o_z].astype(BF16)
    w_z = w_in[:, o_z:o_x].astype(BF16)
    w_xbc = w_in[:, o_x:o_dt].astype(BF16)
    w_dt = jnp.pad(w_in[:, o_dt:o_g][:, perm], ((0, 0), (0, LANES - 2 * SSD_HEADS))).astype(BF16)
    w_g = w_in[:, o_g:].astype(BF16)
    nmix = p['norm_mix_w'].reshape(1, d)

    h = _prenorm(xa, xb, nmix, tm)
    u = _glu_proj(h, w_val, w_gate, tm, PROJ_TN // 2)
    xbc = _proj(h, w_xbc, lambda a: a, BF16, tm, PROJ_TN, "xbc_proj")
    dt_raw = _proj(h, w_dt, lambda a: a, F32, tm, LANES, "dt_proj")
    gates = _proj(h, w_g, _sigmoid, BF16, tm, PROJ_TN, "gate_proj")

    pad = LANES - 2 * SSD_HEADS
    dt_bias = jnp.pad(jnp.concatenate([p['ssd_dt_bias_f'], p['ssd_dt_bias_b']])[perm], (0, pad)).reshape(1, LANES)
    a_log = jnp.pad(jnp.concatenate([p['ssd_a_log_f'], p['ssd_a_log_b']])[perm], (0, pad)).reshape(1, LANES)
    acs2, rowt, e_hi, e_lo, w_hi, w_lo, dec = _dtprep(dt_raw, dt_bias, a_log, tm)

    def per_group_tm(a):
        return a[:, :2 * SSD_HEADS].reshape(nb, s, SSD_GROUPS, 8).transpose(0, 2, 1, 3)

    def per_group_hm(a):
        return a[:, :2 * SSD_HEADS].reshape(nb, nc, SSD_CHUNK, SSD_GROUPS, 8).transpose(0, 3, 1, 4, 2)

    acs_tm = per_group_tm(acs2).reshape(nb * SSD_GROUPS, s, 8)
    ew_tm = jnp.concatenate([per_group_tm(v) for v in (e_hi, e_lo, w_hi, w_lo)], axis=-1)
    ew_tm = ew_tm.reshape(nb * SSD_GROUPS, s, EW_COLS)
    rowt_hm = per_group_hm(rowt)
    dec = dec[:, :2 * SSD_HEADS].reshape(nb, nc, SSD_GROUPS, 2, HEADS_PER_GROUP).transpose(0, 2, 1, 3, 4)
    dec = jnp.repeat(dec, SSD_HEAD_DIM, axis=-1)

    cwx = p['ssd_conv_w']
    cbx = p['ssd_conv_b']

    def per_group_conv(a):
        xs_ = a[:, :SSD_D_INNER].reshape(-1, SSD_GROUPS, GROUP_CH)
        b_ = a[:, SSD_D_INNER:SSD_D_INNER + SSD_GROUPS * SSD_STATE].reshape(-1, SSD_GROUPS, SSD_STATE)
        c_ = a[:, SSD_D_INNER + SSD_GROUPS * SSD_STATE:].reshape(-1, SSD_GROUPS, SSD_STATE)
        return jnp.concatenate([xs_, b_, c_], axis=-1).transpose(1, 0, 2)

    cw = jnp.pad(per_group_conv(cwx), ((0, 0), (0, 8 - SSD_CONV_WIDTH), (0, 0)))
    cb = per_group_conv(cbx.reshape(1, -1))
    dskip = jnp.repeat(p['ssd_d'], SSD_HEAD_DIM).reshape(SSD_GROUPS, 1, GROUP_CH)
    nw_ssd = p['ssd_norm_w'].reshape(SSD_GROUPS, 1, GROUP_CH)

    yn = _ssd(xbc.reshape(nb, s, -1), h.reshape(nb, s, d), w_z, cw, cb, rowt_hm, acs_tm, ew_tm, dec, dskip, nw_ssd)
    mb = _gated_proj(yn.reshape(t, SSD_D_INNER), p['ssd_out'].astype(BF16), gates, 1, tm)

    dww = jnp.pad(p['conv_dw_w'], ((0, 32 - CONV_WIDTH), (0, 0)))
    x1 = _conformer(u.reshape(nb, s, d), mb.reshape(nb, s, d), gates.reshape(nb, s, 2 * d), xa, xb, dww,
                    p['conv_dw_b'].reshape(1, d), p['conv_ln_w'].reshape(1, d), p['conv_ln_b'].reshape(1, d),
                    p['conv_pw_out'].astype(BF16), p['w_out'].astype(BF16), tq)

    m = mem.shape[1]
    kv = _norm_proj(mem.reshape(nb * m, d), p['norm_mem_w'].reshape(1, d), p['xattn_wkv'].astype(BF16),
                    lambda a: a, BF16, _pick(nb * m, 512), 1024, "kv_proj")
    n_r = N_GROUPS + N_EXPERTS
    wr = jnp.pad(jnp.concatenate([p['router_group_w'], p['router_expert_w']], axis=1),
                 ((0, 0), (0, LANES - n_r))).astype(BF16)
    br = jnp.pad(jnp.concatenate([p['router_group_b'], p['router_expert_b']]), (0, LANES - n_r)).reshape(1, LANES)
    x2, h3, logits = _attention(x1, kv.reshape(nb, m, 2 * d), p['norm_xattn_w'].reshape(1, d),
                                p['xattn_wq'].astype(BF16), p['xattn_wo'].astype(BF16),
                                p['norm_ffn_w'].reshape(1, d), wr, br, tq)

    tc = _pick(math.gcd(t_first, t - t_first), 512)
    gate, pos, cnt = _plan(logits.reshape(t, LANES), tm)
    n_blocks = (2 * t) // MOE_ROWS + N_EXPERTS
    blk_exp, n_active, last_blk = _block_experts(cnt[0, :N_EXPERTS].astype(jnp.int32), n_blocks)
    pos_blk = pos[:, :2].reshape(t // tc, tc, 2).transpose(0, 2, 1).reshape(t // tc, 1, 2 * tc)
    xs = _dispatch(last_blk, pos_blk, h3, n_blocks)
    ys = _experts(blk_exp, n_active, xs, p['expert_w_gate'], p['expert_w_up'], p['expert_w_down'])
    return _combine(pos_blk, x2.reshape(t, d), gate, p['norm_final_w'].reshape(1, d), ys, tc, t_first)


def kernel(x_prompt, x_sample, mem_prompt, mem_sample, norm_mix_w, w_in, conv_dw_w, conv_dw_b, conv_ln_w, conv_ln_b, conv_pw_out, ssd_conv_w, ssd_conv_b, ssd_dt_bias_f, ssd_dt_bias_b, ssd_a_log_f, ssd_a_log_b, ssd_d, ssd_norm_w, ssd_out, w_out, norm_xattn_w, norm_mem_w, xattn_wq, xattn_wkv, xattn_wo, norm_ffn_w, router_group_w, router_group_b, router_expert_w, router_expert_b, expert_w_gate, expert_w_up, expert_w_down, norm_final_w):
    p = {
        'norm_mix_w': norm_mix_w[0], 'w_in': w_in[0], 'conv_dw_w': conv_dw_w[0], 'conv_dw_b': conv_dw_b[0],
        'conv_ln_w': conv_ln_w[0], 'conv_ln_b': conv_ln_b[0], 'conv_pw_out': conv_pw_out[0],
        'ssd_conv_w': ssd_conv_w[0], 'ssd_conv_b': ssd_conv_b[0], 'ssd_dt_bias_f': ssd_dt_bias_f[0],
        'ssd_dt_bias_b': ssd_dt_bias_b[0], 'ssd_a_log_f': ssd_a_log_f[0], 'ssd_a_log_b': ssd_a_log_b[0],
        'ssd_d': ssd_d[0], 'ssd_norm_w': ssd_norm_w[0], 'ssd_out': ssd_out[0], 'w_out': w_out[0],
        'norm_xattn_w': norm_xattn_w[0], 'norm_mem_w': norm_mem_w[0], 'xattn_wq': xattn_wq[0],
        'xattn_wkv': xattn_wkv[0], 'xattn_wo': xattn_wo[0], 'norm_ffn_w': norm_ffn_w[0],
        'router_group_w': router_group_w[0], 'router_group_b': router_group_b[0],
        'router_expert_w': router_expert_w[0], 'router_expert_b': router_expert_b[0],
        'expert_w_gate': expert_w_gate[0], 'expert_w_up': expert_w_up[0], 'expert_w_down': expert_w_down[0],
        'norm_final_w': norm_final_w,
    }
    mem = jnp.concatenate([mem_prompt, mem_sample], axis=0)
    y_prompt, y_sample = _encoder(x_prompt, x_sample, mem, p)
    return (y_prompt.reshape(x_prompt.shape), y_sample.reshape(x_sample.shape))
```

```python
import functools
import math

import jax
import jax.numpy as jnp
from jax import lax
from jax.experimental import pallas as pl
from jax.experimental.pallas import tpu as pltpu

F32 = jnp.float32
BF16 = jnp.bfloat16

D_MODEL = 1024
CONV_WIDTH = 31
SSD_D_INNER = 2048
SSD_HEAD_DIM = 64
SSD_HEADS = 32
SSD_GROUPS = 8
SSD_STATE = 128
SSD_CONV_WIDTH = 5
SSD_CHUNK = 128
GROUP_CH = SSD_D_INNER // SSD_GROUPS
HEADS_PER_GROUP = SSD_HEADS // SSD_GROUPS
XATTN_HEADS = 4
XATTN_HEAD_DIM = D_MODEL // XATTN_HEADS
N_GROUPS = 8
EXPERTS_PER_GROUP = 8
N_EXPERTS = 64
EXPERT_FF = 512
MOE_BLOCK = 128
RMS_EPS = 1e-6
LN_EPS = 1e-5

LANES = 128
HALO = 16
VMEM_LIMIT = 56 * 1024 * 1024
PROJ_TN = 2048


def _cparams(sem):
    return pltpu.CompilerParams(dimension_semantics=sem, vmem_limit_bytes=VMEM_LIMIT)


def _rms(x, w):
    ms = jnp.mean(x * x, axis=-1, keepdims=True)
    return x * lax.rsqrt(ms + RMS_EPS) * w


def _sigmoid(x):
    return 1.0 / (1.0 + jnp.exp2(x * -1.4426950408889634))


def _silu(x):
    return x * _sigmoid(x)


def _softplus(x):
    return jnp.maximum(x, 0.0) + jnp.log1p(jnp.exp(-jnp.abs(x)))


CONV_STRIDE = 4
CONV_ROWS = 8 * CONV_STRIDE
CONV_UNROLL = 4


def _depthwise_conv(src, dst, slab, w_ref, b_ref, n_rows, width, first_row, epilogue):
    lanes = pl.ds(slab * LANES, LANES)
    taps = [jnp.broadcast_to(w_ref[k:k + 1, lanes], (8, LANES)) for k in range(width)]
    bias = jnp.broadcast_to(b_ref[:, lanes], (8, LANES))
    step = CONV_ROWS * CONV_UNROLL

    def body(r, carry):
        _conv_rows(src, dst, slab, taps, bias, r * step, step, width, first_row, epilogue)
        return carry

    lax.fori_loop(0, n_rows // step, body, 0)


def _conv_rows(src, dst, slab, taps, bias, base, n_rows, width, first_row, epilogue):
    for t0 in range(n_rows // 8):
        row = base + (t0 // CONV_STRIDE) * CONV_ROWS + t0 % CONV_STRIDE
        acc = bias
        for k in range(width):
            acc = acc + src[slab, pl.ds(row + first_row - width // 2 + k, 8, stride=CONV_STRIDE), :] * taps[k]
        dst[slab, pl.ds(row, 8, stride=CONV_STRIDE), :] = epilogue(acc)


def _first_or_second(i, n_first):
    return jnp.minimum(i, n_first - 1), jnp.maximum(i - n_first, 0)


def _prenorm_kernel(n_first, xa_ref, xb_ref, nw_ref, o_ref):
    x = jnp.where(pl.program_id(0) < n_first, xa_ref[...], xb_ref[...])
    o_ref[...] = _rms(x, nw_ref[...]).astype(o_ref.dtype)


def _prenorm(xa, xb, nw, tm):
    (ta, d), tb = xa.shape, xb.shape[0]
    na = ta // tm
    return pl.pallas_call(
        functools.partial(_prenorm_kernel, na),
        grid=((ta + tb) // tm,),
        in_specs=[
            pl.BlockSpec((tm, d), lambda i: (_first_or_second(i, na)[0], 0)),
            pl.BlockSpec((tm, d), lambda i: (_first_or_second(i, na)[1], 0)),
            pl.BlockSpec((1, d), lambda i: (0, 0)),
        ],
        out_specs=pl.BlockSpec((tm, d), lambda i: (i, 0)),
        out_shape=jax.ShapeDtypeStruct((ta + tb, d), BF16),
        compiler_params=_cparams(("arbitrary",)),
        name="mix_norm",
    )(xa, xb, nw)


def _glu_kernel(h_ref, wv_ref, wg_ref, o_ref):
    h = h_ref[...]
    v = jnp.dot(h, wv_ref[...], preferred_element_type=F32)
    g = jnp.dot(h, wg_ref[...], preferred_element_type=F32)
    o_ref[...] = (v * _sigmoid(g)).astype(o_ref.dtype)


def _glu_proj(h, wv, wg, tm, tn):
    t, d = h.shape
    n = wv.shape[1]
    return pl.pallas_call(
        _glu_kernel,
        grid=(t // tm, n // tn),
        in_specs=[
            pl.BlockSpec((tm, d), lambda i, j: (i, 0)),
            pl.BlockSpec((d, tn), lambda i, j: (0, j)),
            pl.BlockSpec((d, tn), lambda i, j: (0, j)),
        ],
        out_specs=pl.BlockSpec((tm, tn), lambda i, j: (i, j)),
        out_shape=jax.ShapeDtypeStruct((t, n), BF16),
        compiler_params=_cparams(("parallel", "arbitrary")),
        name="glu_proj",
    )(h, wv, wg)


def _proj_kernel(epi, h_ref, w_ref, o_ref):
    o_ref[...] = epi(jnp.dot(h_ref[...], w_ref[...], preferred_element_type=F32)).astype(o_ref.dtype)


def _proj(h, w, epi, out_dtype, tm, tn, name):
    t, d = h.shape
    n = w.shape[1]
    return pl.pallas_call(
        functools.partial(_proj_kernel, epi),
        grid=(t // tm, n // tn),
        in_specs=[
            pl.BlockSpec((tm, d), lambda i, j: (i, 0)),
            pl.BlockSpec((d, tn), lambda i, j: (0, j)),
        ],
        out_specs=pl.BlockSpec((tm, tn), lambda i, j: (i, j)),
        out_shape=jax.ShapeDtypeStruct((t, n), out_dtype),
        compiler_params=_cparams(("parallel", "arbitrary")),
        name=name,
    )(h, w)


def _norm_proj_kernel(epi, x_ref, nw_ref, w_ref, o_ref, h_scr):
    @pl.when(pl.program_id(1) == 0)
    def _():
        h_scr[...] = _rms(x_ref[...], nw_ref[...]).astype(BF16)

    acc = jnp.dot(h_scr[...], w_ref[...], preferred_element_type=F32)
    o_ref[...] = epi(acc).astype(o_ref.dtype)


def _norm_proj(x, nw, w, epi, out_dtype, tm, tn, name):
    t, d = x.shape
    n = w.shape[1]
    return pl.pallas_call(
        functools.partial(_norm_proj_kernel, epi),
        grid=(t // tm, n // tn),
        in_specs=[
            pl.BlockSpec((tm, d), lambda i, j: (i, 0)),
            pl.BlockSpec((1, d), lambda i, j: (0, 0)),
            pl.BlockSpec((d, tn), lambda i, j: (0, j)),
        ],
        out_specs=pl.BlockSpec((tm, tn), lambda i, j: (i, j)),
        out_shape=jax.ShapeDtypeStruct((t, n), out_dtype),
        scratch_shapes=[pltpu.VMEM((tm, d), BF16)],
        compiler_params=_cparams(("parallel", "arbitrary")),
        name=name,
    )(x, nw, w)


def _gated_proj_kernel(y_ref, w_ref, g_ref, o_ref):
    acc = jnp.dot(y_ref[...], w_ref[...], preferred_element_type=F32)
    o_ref[...] = (acc * g_ref[...].astype(F32)).astype(o_ref.dtype)


def _gated_proj(y, w, g, g_col_block, tm):
    t, k = y.shape
    n = w.shape[1]
    return pl.pallas_call(
        _gated_proj_kernel,
        grid=(t // tm,),
        in_specs=[
            pl.BlockSpec((tm, k), lambda i: (i, 0)),
            pl.BlockSpec((k, n), lambda i: (0, 0)),
            pl.BlockSpec((tm, n), lambda i: (i, g_col_block)),
        ],
        out_specs=pl.BlockSpec((tm, n), lambda i: (i, 0)),
        out_shape=jax.ShapeDtypeStruct((t, n), BF16),
        compiler_params=_cparams(("parallel",)),
        name="ssd_out_proj",
    )(y, w, g)


LOG2E = 1.4426950408889634


def _split_bf16(v):
    hi = v.astype(BF16)
    return hi, (v - hi.astype(F32)).astype(BF16)


def _dtprep_kernel(raw_ref, bias_ref, alog_ref, acs2_ref, rowt_ref, ehi_ref, elo_ref, whi_ref, wlo_ref, dec_ref):
    rows = raw_ref.shape[0]
    a_head = -jnp.exp(alog_ref[...])
    row = lax.broadcasted_iota(jnp.int32, (SSD_CHUNK, LANES), 0)
    lane = lax.broadcasted_iota(jnp.int32, (SSD_CHUNK, LANES), 1)
    is_bwd = (lane % (2 * HEADS_PER_GROUP)) >= HEADS_PER_GROUP
    for c in range(rows // SSD_CHUNK):
        sl = pl.ds(c * SSD_CHUNK, SSD_CHUNK)
        dt = _softplus(raw_ref[sl, :] + bias_ref[...])
        a = dt * a_head
        fwd = a
        bwd = a
        sh = 1
        while sh < SSD_CHUNK:
            fwd = fwd + jnp.where(row >= sh, pltpu.roll(fwd, sh, 0), 0.0)
            bwd = bwd + jnp.where(row < SSD_CHUNK - sh, pltpu.roll(bwd, SSD_CHUNK - sh, 0), 0.0)
            sh *= 2
        acs = jnp.where(is_bwd, bwd, fwd)
        a_end = jnp.where(is_bwd[0:1, :], bwd[0:1, :], fwd[SSD_CHUNK - 1:SSD_CHUNK, :])
        acs2 = acs * LOG2E
        acs2_ref[sl, :] = acs2
        rowt_ref[sl, :] = acs2 - jnp.log(dt) * LOG2E
        ehi_ref[sl, :], elo_ref[sl, :] = _split_bf16(jnp.exp(acs))
        whi_ref[sl, :], wlo_ref[sl, :] = _split_bf16(jnp.exp(a_end - acs) * dt)
        dec_ref[c:c + 1, :] = jnp.exp(a_end)


def _dtprep(raw, bias, alog, tm):
    t = raw.shape[0]
    spec = pl.BlockSpec((tm, LANES), lambda i: (i, 0))
    vec = pl.BlockSpec((1, LANES), lambda i: (0, 0))
    return pl.pallas_call(
        _dtprep_kernel,
        grid=(t // tm,),
        in_specs=[spec, vec, vec],
        out_specs=[spec] * 6 + [pl.BlockSpec((tm // SSD_CHUNK, LANES), lambda i: (i, 0))],
        out_shape=[jax.ShapeDtypeStruct((t, LANES), F32)] * 2 + [jax.ShapeDtypeStruct((t, LANES), BF16)] * 4
        + [jax.ShapeDtypeStruct((t // SSD_CHUNK, LANES), F32)],
        compiler_params=_cparams(("parallel",)),
        name="dt_prep",
    )(raw, bias, alog)


SSD_PAD = 8
XBC_W = GROUP_CH + 2 * SSD_STATE


EW_COLS = 4 * 2 * HEADS_PER_GROUP


def _head_expand_matrix():
    col = jnp.arange(4 * GROUP_CH)
    part = col // GROUP_CH
    is_w = (part == 1) | (part == 2)
    is_bwd = part >= 2
    head = is_bwd * HEADS_PER_GROUP + (col % GROUP_CH) // SSD_HEAD_DIM
    hi_row = is_w * 16 + head
    row = jnp.arange(EW_COLS)[:, None]
    return ((row == hi_row[None, :]) | (row == hi_row[None, :] + 8)).astype(BF16)


def _ssd_kernel(xs_ref, b_ref, c_ref, h_ref, wz_ref, cw_ref, cb_ref, rowt_ref, acs_ref, ew_ref, sel_ref, dec_ref,
                dskip_ref, nw_ref, o_ref, raw_scr, act_scr, y_scr, upd_scr, st_scr):
    s = xs_ref.shape[1]
    nc = s // SSD_CHUNK
    L = SSD_CHUNK

    n_slabs = XBC_W // LANES
    for j in range(n_slabs):
        raw_scr[j, pl.ds(0, SSD_PAD), :] = jnp.zeros((SSD_PAD, LANES), F32)
        raw_scr[j, pl.ds(SSD_PAD + s, L + SSD_PAD), :] = jnp.zeros((L + SSD_PAD, LANES), F32)
    raw_scr[0, pl.ds(SSD_PAD, s), :] = xs_ref[0, :, :LANES].astype(F32)
    raw_scr[1, pl.ds(SSD_PAD, s), :] = xs_ref[0, :, LANES:].astype(F32)
    raw_scr[2, pl.ds(SSD_PAD, s), :] = b_ref[0].astype(F32)
    raw_scr[3, pl.ds(SSD_PAD, s), :] = c_ref[0].astype(F32)

    def conv_chunk(c):
        for j in range(n_slabs):
            lanes = pl.ds(j * LANES, LANES)
            taps = [jnp.broadcast_to(cw_ref[0, k:k + 1, lanes], (8, LANES)) for k in range(SSD_CONV_WIDTH)]
            bias = jnp.broadcast_to(cb_ref[0, :, lanes], (8, LANES))
            _conv_rows(raw_scr, act_scr, j, taps, bias, c * L, L, SSD_CONV_WIDTH, SSD_PAD, _silu)

    conv_chunk(0)

    li = lax.broadcasted_iota(jnp.int32, (L, L), 0)
    si = lax.broadcasted_iota(jnp.int32, (L, L), 1)
    head_of_ch = lax.broadcasted_iota(jnp.int32, (L, GROUP_CH), 1) // SSD_HEAD_DIM

    keep = (li >= si, si >= li)
    neg_inf = jnp.float32(-jnp.inf)
    n_heads = 2 * HEADS_PER_GROUP
    st_scr[...] = jnp.zeros_like(st_scr)

    def local_body(c, carry):
        r0 = pl.multiple_of(c * L, L)
        x = jnp.concatenate([act_scr[0, pl.ds(r0, L), :], act_scr[1, pl.ds(r0, L), :]], axis=1)
        xb = x.astype(BF16)
        bk = act_scr[2, pl.ds(r0, L), :].astype(BF16)
        ck = act_scr[3, pl.ds(r0, L), :].astype(BF16)
        acs = acs_ref[0, pl.ds(r0, L), :]
        rowt = rowt_ref[0, 0, c]
        ew = jnp.dot(ew_ref[0, pl.ds(r0, L), :], sel_ref[:, :3 * GROUP_CH],
                     preferred_element_type=F32)

        cb = lax.dot_general(ck, bk, (((1,), (1,)), ((), ())), preferred_element_type=F32)
        ms = []
        for j in range(n_heads):
            diff = acs[:, j:j + 1] - rowt[j:j + 1, :]
            decay_dt = jnp.exp2(jnp.where(keep[j // HEADS_PER_GROUP], diff, neg_inf))
            ms.append((cb * decay_dt).astype(BF16))
        xbd = jnp.concatenate([jnp.where(head_of_ch == h, xb, jnp.zeros_like(xb))
                               for h in range(HEADS_PER_GROUP)], axis=0)
        y = jnp.dot(jnp.concatenate(ms, axis=1), jnp.concatenate([xbd, xbd], axis=0),
                    preferred_element_type=F32) + x * dskip_ref[0]

        xw = jnp.concatenate([x * ew[:, GROUP_CH:2 * GROUP_CH], x * ew[:, 2 * GROUP_CH:]], axis=1)
        upd = lax.dot_general(bk, xw.astype(BF16), (((0,), (0,)), ((), ())), preferred_element_type=F32)

        state = st_scr[0]
        y = y + jnp.dot(ck, state.astype(BF16), preferred_element_type=F32) * ew[:, :GROUP_CH]
        st_scr[0] = state * dec_ref[0, 0, c, 0:1, :] + upd[:, :GROUP_CH]
        y_scr[pl.ds(r0, L), :] = y
        upd_scr[c] = upd[:, GROUP_CH:]
        conv_chunk(c + 1)
        return carry

    lax.fori_loop(0, nc, local_body, 0, unroll=4)

    def bwd_body(i, carry):
        c = nc - 1 - i
        r0 = pl.multiple_of(c * L, L)
        ck = act_scr[3, pl.ds(r0, L), :].astype(BF16)
        e_bwd = jnp.dot(ew_ref[0, pl.ds(r0, L), :], sel_ref[:, 3 * GROUP_CH:], preferred_element_type=F32)
        state = st_scr[1]
        y = y_scr[pl.ds(r0, L), :] + jnp.dot(ck, state.astype(BF16), preferred_element_type=F32) * e_bwd
        st_scr[1] = state * dec_ref[0, 0, c, 1:2, :] + upd_scr[c]
        z = jnp.dot(h_ref[0, pl.ds(r0, L), :], wz_ref[...], preferred_element_type=F32)
        yz = y * _silu(z)
        ms = jnp.mean(yz * yz, axis=-1, keepdims=True)
        o_ref[0, pl.ds(r0, L), :] = (yz * lax.rsqrt(ms + RMS_EPS) * nw_ref[0]).astype(o_ref.dtype)
        return carry

    lax.fori_loop(0, nc, bwd_body, 0, unroll=4)


def _ssd(xbc, h, wz, cw, cb, rowt_hm, acs_tm, ew_tm, dec, dskip, nw):
    nb, s, _ = xbc.shape
    d = h.shape[2]
    nc = s // SSD_CHUNK
    return pl.pallas_call(
        _ssd_kernel,
        grid=(nb, SSD_GROUPS),
        in_specs=[
            pl.BlockSpec((1, s, GROUP_CH), lambda b, g: (b, 0, g)),
            pl.BlockSpec((1, s, SSD_STATE), lambda b, g: (b, 0, SSD_D_INNER // SSD_STATE + g)),
            pl.BlockSpec((1, s, SSD_STATE), lambda b, g: (b, 0, SSD_D_INNER // SSD_STATE + SSD_GROUPS + g)),
            pl.BlockSpec((1, s, d), lambda b, g: (b, 0, 0)),
            pl.BlockSpec((d, GROUP_CH), lambda b, g: (0, g)),
            pl.BlockSpec((1, 8, XBC_W), lambda b, g: (g, 0, 0)),
            pl.BlockSpec((1, 1, XBC_W), lambda b, g: (g, 0, 0)),
            pl.BlockSpec((1, 1, nc, 8, SSD_CHUNK), lambda b, g: (b, g, 0, 0, 0)),
            pl.BlockSpec((1, s, 8), lambda b, g: (b * SSD_GROUPS + g, 0, 0)),
            pl.BlockSpec((1, s, EW_COLS), lambda b, g: (b * SSD_GROUPS + g, 0, 0)),
            pl.BlockSpec((EW_COLS, 4 * GROUP_CH), lambda b, g: (0, 0)),
            pl.BlockSpec((1, 1, nc, 2, GROUP_CH), lambda b, g: (b, g, 0, 0, 0)),
            pl.BlockSpec((1, 1, GROUP_CH), lambda b, g: (g, 0, 0)),
            pl.BlockSpec((1, 1, GROUP_CH), lambda b, g: (g, 0, 0)),
        ],
        out_specs=pl.BlockSpec((1, s, GROUP_CH), lambda b, g: (b, 0, g)),
        out_shape=jax.ShapeDtypeStruct((nb, s, SSD_D_INNER), BF16),
        scratch_shapes=[
            pltpu.VMEM((XBC_W // LANES, s + SSD_CHUNK + 2 * SSD_PAD, LANES), F32),
            pltpu.VMEM((XBC_W // LANES, s + SSD_CHUNK, LANES), F32),
            pltpu.VMEM((s, GROUP_CH), F32),
            pltpu.VMEM((nc, SSD_STATE, GROUP_CH), F32),
            pltpu.VMEM((2, SSD_STATE, GROUP_CH), F32),
        ],
        compiler_params=_cparams(("parallel", "parallel")),
        name="ssd_scan",
    )(xbc, xbc, xbc, h, wz, cw, cb, rowt_hm, acs_tm, ew_tm, _head_expand_matrix(), dec, dskip, nw)


D_SLABS = D_MODEL // LANES


def _conformer_kernel(n_first, u_ref, up_ref, un_ref, mb_ref, g_ref, xa_ref, xb_ref, dww_ref, dwb_ref, lnw_ref,
                      lnb_ref, pw_ref, wo_ref, o_ref, u_scr, c_scr):
    i = pl.program_id(1)
    n = pl.num_programs(1)
    tq = u_ref.shape[1]
    x = jnp.where(pl.program_id(0) * n + i < n_first, xa_ref[...], xb_ref[...])
    for j in range(D_SLABS):
        cs = slice(j * LANES, (j + 1) * LANES)
        u_scr[j, pl.ds(0, HALO), :] = jnp.where(i > 0, up_ref[0, :, cs].astype(F32), 0.0)
        u_scr[j, pl.ds(HALO, tq), :] = u_ref[0, :, cs].astype(F32)
        u_scr[j, pl.ds(HALO + tq, HALO), :] = jnp.where(i < n - 1, un_ref[0, :, cs].astype(F32), 0.0)
    for j in range(D_SLABS):
        _depthwise_conv(u_scr, c_scr, j, dww_ref, dwb_ref, tq, CONV_WIDTH, HALO, lambda a: a)

    cv = jnp.concatenate([c_scr[j] for j in range(D_SLABS)], axis=1)
    mu = jnp.mean(cv, axis=-1, keepdims=True)
    xc = cv - mu
    var = jnp.mean(xc * xc, axis=-1, keepdims=True)
    y = xc * lax.rsqrt(var + LN_EPS) * lnw_ref[...] + lnb_ref[...]
    a = jnp.dot(_silu(y).astype(BF16), pw_ref[...], preferred_element_type=F32)
    merged = g_ref[0].astype(F32) * a + mb_ref[0].astype(F32)
    o_ref[0] = x + jnp.dot(merged.astype(BF16), wo_ref[...], preferred_element_type=F32)


def _conformer(u, mb, gates, xa, xb, dww, dwb, lnw, lnb, pw, wo, tq):
    nb, s, d = u.shape
    hb = tq // HALO
    nh = s // HALO
    nq = s // tq
    na = xa.shape[0] // tq
    wspec = pl.BlockSpec((d, d), lambda b, i: (0, 0))
    vspec = pl.BlockSpec((1, d), lambda b, i: (0, 0))
    return pl.pallas_call(
        functools.partial(_conformer_kernel, na),
        grid=(nb, nq),
        in_specs=[
            pl.BlockSpec((1, tq, d), lambda b, i: (b, i, 0)),
            pl.BlockSpec((1, HALO, d), lambda b, i: (b, jnp.maximum(i * hb - 1, 0), 0)),
            pl.BlockSpec((1, HALO, d), lambda b, i: (b, jnp.minimum((i + 1) * hb, nh - 1), 0)),
            pl.BlockSpec((1, tq, d), lambda b, i: (b, i, 0)),
            pl.BlockSpec((1, tq, d), lambda b, i: (b, i, 0)),
            pl.BlockSpec((tq, d), lambda b, i: (_first_or_second(b * nq + i, na)[0], 0)),
            pl.BlockSpec((tq, d), lambda b, i: (_first_or_second(b * nq + i, na)[1], 0)),
            pl.BlockSpec((32, d), lambda b, i: (0, 0)),
            vspec, vspec, vspec, wspec, wspec,
        ],
        out_specs=pl.BlockSpec((1, tq, d), lambda b, i: (b, i, 0)),
        out_shape=jax.ShapeDtypeStruct((nb, s, d), F32),
        scratch_shapes=[pltpu.VMEM((D_SLABS, tq + 2 * HALO, LANES), F32), pltpu.VMEM((D_SLABS, tq, LANES), F32)],
        compiler_params=_cparams(("arbitrary", "arbitrary")),
        name="conformer_merge",
    )(u, u, u, mb, gates, xa, xb, dww, dwb, lnw, lnb, pw, wo)


TILE_ROWS = D_MODEL // (2 * LANES)
HIGH_HALF = 0xFFFF0000


def _bf16_bits(v):
    return lax.bitcast_convert_type(v.astype(BF16).astype(F32), jnp.uint32)


def _store_token_tiles(ref, val):
    n, d = val.shape
    packed = (_bf16_bits(val[:, :d // 2]) >> 16) | (_bf16_bits(val[:, d // 2:]) & jnp.uint32(HIGH_HALF))
    for j in range(TILE_ROWS):
        ref[pl.ds(j, n, stride=TILE_ROWS), :] = packed[:, j * LANES:(j + 1) * LANES]


def _load_token_tiles(ref, first, n):
    packed = jnp.concatenate([ref[pl.ds(first * TILE_ROWS + j, n, stride=TILE_ROWS), :] for j in range(TILE_ROWS)],
                             axis=1)
    low = lax.bitcast_convert_type(packed << 16, F32)
    high = lax.bitcast_convert_type(packed & jnp.uint32(HIGH_HALF), F32)
    return jnp.concatenate([low, high], axis=1)


def _attn_kernel(x_ref, k_ref, v_ref, nw_ref, wq_ref, wo_ref, nf_ref, wr_ref, br_ref, x2_ref, h3_ref, lg_ref):
    x = x_ref[0]
    h = _rms(x, nw_ref[...]).astype(BF16)
    q = jnp.dot(h, wq_ref[...], preferred_element_type=F32)
    scale = 1.0 / math.sqrt(XATTN_HEAD_DIM)
    outs = []
    for hd in range(XATTN_HEADS):
        cs = slice(hd * XATTN_HEAD_DIM, (hd + 1) * XATTN_HEAD_DIM)
        sc = lax.dot_general(q[:, cs].astype(BF16), k_ref[0, :, cs], (((1,), (1,)), ((), ())),
                             preferred_element_type=F32) * scale
        m = jnp.max(sc, axis=-1, keepdims=True)
        e = jnp.exp(sc - m)
        p = e / jnp.sum(e, axis=-1, keepdims=True)
        outs.append(jnp.dot(p.astype(BF16), v_ref[0, :, cs], preferred_element_type=F32))
    o = jnp.concatenate(outs, axis=1).astype(BF16)
    x2 = x + jnp.dot(o, wo_ref[...], preferred_element_type=F32)
    x2_ref[0] = x2
    h3 = _rms(x2, nf_ref[...])
    _store_token_tiles(h3_ref, h3)
    lg_ref[0] = jnp.dot(h3.astype(BF16), wr_ref[...], preferred_element_type=F32) + br_ref[...]


def _attention(x, kv, nw, wq, wo, nf, wr, br, tq):
    nb, s, d = x.shape
    m = kv.shape[1]
    wspec = pl.BlockSpec((d, d), lambda b, i: (0, 0))
    vspec = pl.BlockSpec((1, d), lambda b, i: (0, 0))
    return pl.pallas_call(
        _attn_kernel,
        grid=(nb, s // tq),
        in_specs=[
            pl.BlockSpec((1, tq, d), lambda b, i: (b, i, 0)),
            pl.BlockSpec((1, m, d), lambda b, i: (b, 0, 0)),
            pl.BlockSpec((1, m, d), lambda b, i: (b, 0, 1)),
            vspec, wspec, wspec, vspec,
            pl.BlockSpec((d, LANES), lambda b, i: (0, 0)),
            pl.BlockSpec((1, LANES), lambda b, i: (0, 0)),
        ],
        out_specs=[
            pl.BlockSpec((1, tq, d), lambda b, i: (b, i, 0)),
            pl.BlockSpec((tq * TILE_ROWS, LANES), lambda b, i: (b * (s // tq) + i, 0)),
            pl.BlockSpec((1, tq, LANES), lambda b, i: (b, i, 0)),
        ],
        out_shape=[
            jax.ShapeDtypeStruct((nb, s, d), F32),
            jax.ShapeDtypeStruct((nb * s * TILE_ROWS, LANES), jnp.uint32),
            jax.ShapeDtypeStruct((nb, s, LANES), F32),
        ],
        compiler_params=_cparams(("parallel", "parallel")),
        name="cross_attn_router",
    )(x, kv, kv, nw, wq, wo, nf, wr, br)


MOE_ROWS = 512


def _route_math(lg, lane):
    lane_f = lane.astype(F32)
    big = float(LANES)
    neg = -jnp.inf

    is_g = lane < N_GROUPS
    gl = jnp.where(is_g, lg, neg)
    gmax = jnp.max(gl, axis=-1, keepdims=True)
    g_idx = jnp.min(jnp.where(gl == gmax, lane_f, big), axis=-1, keepdims=True)
    g_sum = jnp.sum(jnp.where(is_g, jnp.exp(gl - gmax), 0.0), axis=-1, keepdims=True)
    g_w = 1.0 / g_sum

    e_grp = lax.shift_right_arithmetic(lane - N_GROUPS, 3).astype(F32)
    is_e = (lane >= N_GROUPS) & (lane < N_GROUPS + N_EXPERTS) & (e_grp == g_idx)
    el = jnp.where(is_e, lg, neg)
    emax = jnp.max(el, axis=-1, keepdims=True)
    ee = jnp.where(is_e, jnp.exp(el - emax), 0.0)
    ep = ee / jnp.sum(ee, axis=-1, keepdims=True)
    ep = jnp.where(is_e, ep, -1.0)
    v1 = jnp.max(ep, axis=-1, keepdims=True)
    i1 = jnp.min(jnp.where(ep == v1, lane_f, big), axis=-1, keepdims=True)
    ep2 = jnp.where(lane_f == i1, -1.0, ep)
    v2 = jnp.max(ep2, axis=-1, keepdims=True)
    i2 = jnp.min(jnp.where(ep2 == v2, lane_f, big), axis=-1, keepdims=True)
    tot = v1 + v2
    gate1 = g_w * (v1 / tot)
    gate2 = g_w * (v2 / tot)
    return i1 - N_GROUPS, i2 - N_GROUPS, gate1, gate2


def _plan_kernel(lg_ref, gate_ref, pos_ref, cnt_ref, carry_scr, base_scr):
    p = pl.program_id(0)
    i = pl.program_id(1)
    tm = lg_ref.shape[0]
    lane = lax.broadcasted_iota(jnp.int32, (tm, LANES), 1)
    lane_f = lane.astype(F32)
    e1, e2, gate1, gate2 = _route_math(lg_ref[...], lane)
    oh1 = lane_f == e1
    oh2 = lane_f == e2
    m = jnp.where(oh1 | oh2, 1.0, 0.0)
    colsum = jnp.sum(m, axis=0, keepdims=True)

    @pl.when((p == 0) & (i == 0))
    def _():
        carry_scr[...] = jnp.zeros_like(carry_scr)

    @pl.when(p == 0)
    def _():
        carry_scr[...] += colsum

    @pl.when((p == 1) & (i == 0))
    def _():
        counts = carry_scr[...]
        cnt_ref[...] = counts
        shift = MOE_ROWS.bit_length() - 1
        blocks = lax.shift_right_logical(counts.astype(jnp.int32) + (MOE_ROWS - 1), shift)
        padded = (blocks * MOE_ROWS).astype(F32)
        lane8 = lax.broadcasted_iota(jnp.int32, padded.shape, 1)
        inc = padded
        sh = 1
        while sh < LANES:
            inc = inc + jnp.where(lane8 >= sh, pltpu.roll(inc, sh, 1), 0.0)
            sh *= 2
        base_scr[...] = inc - padded
        carry_scr[...] = jnp.zeros_like(carry_scr)

    @pl.when(p == 1)
    def _():
        r = lax.broadcasted_iota(jnp.int32, (tm, tm), 0)
        c = lax.broadcasted_iota(jnp.int32, (tm, tm), 1)
        earlier = jnp.where(r > c, 1.0, 0.0).astype(BF16)
        tot = (jnp.dot(earlier, m.astype(BF16), preferred_element_type=F32)
               + base_scr[0:1, :] + carry_scr[0:1, :])
        pos1 = jnp.sum(jnp.where(oh1, tot, 0.0), axis=1, keepdims=True)
        pos2 = jnp.sum(jnp.where(oh2, tot, 0.0), axis=1, keepdims=True)
        pos_ref[...] = jnp.where(lane == 0, pos1, jnp.where(lane == 1, pos2, 0.0)).astype(jnp.int32)
        gate_ref[...] = jnp.where(lane == 0, gate1, jnp.where(lane == 1, gate2, 0.0))
        carry_scr[...] += colsum


def _plan(lg, tm):
    t = lg.shape[0]
    out_spec = pl.BlockSpec((tm, LANES), lambda p, i: (i * p, 0))
    return pl.pallas_call(
        _plan_kernel,
        grid=(2, t // tm),
        in_specs=[pl.BlockSpec((tm, LANES), lambda p, i: (i, 0))],
        out_specs=[out_spec, out_spec, pl.BlockSpec((8, LANES), lambda p, i: (0, 0))],
        out_shape=[jax.ShapeDtypeStruct((t, LANES), F32), jax.ShapeDtypeStruct((t, LANES), jnp.int32),
                   jax.ShapeDtypeStruct((8, LANES), F32)],
        scratch_shapes=[pltpu.VMEM((8, LANES), F32), pltpu.VMEM((8, LANES), F32)],
        compiler_params=_cparams(("arbitrary", "arbitrary")),
        name="moe_plan",
    )(lg)


def _dispatch_kernel(last_ref, pos_ref, h_ref, xs_hbm, zero_scr, sem):
    td = pos_ref.shape[2] // 2

    @pl.when(pl.program_id(0) == 0)
    def _():
        zero_scr[...] = jnp.zeros_like(zero_scr)

        def block_copy(e):
            first = pl.multiple_of(last_ref[e] * (MOE_ROWS * TILE_ROWS), MOE_ROWS * TILE_ROWS)
            return pltpu.make_async_copy(zero_scr, xs_hbm.at[pl.ds(first, MOE_ROWS * TILE_ROWS)], sem)

        def fill(e, carry):
            @pl.when(last_ref[e] >= 0)
            def _():
                block_copy(e).start()
            return carry

        def drain(e, carry):
            @pl.when(last_ref[e] >= 0)
            def _():
                block_copy(e).wait()
            return carry

        lax.fori_loop(0, last_ref.shape[0], fill, 0)
        lax.fori_loop(0, last_ref.shape[0], drain, 0)

    def tile(ref, tok):
        return ref.at[pl.ds(pl.multiple_of(tok * TILE_ROWS, TILE_ROWS), TILE_ROWS)]

    def body(r, carry):
        src = tile(h_ref, r)
        pltpu.make_async_copy(src, tile(xs_hbm, pos_ref[0, 0, r]), sem).start(priority=0)
        pltpu.make_async_copy(src, tile(xs_hbm, pos_ref[0, 0, td + r]), sem).start(priority=1)
        return carry

    lax.fori_loop(0, td, body, 0, unroll=8)
    for _ in range(2):
        pltpu.make_async_copy(h_ref, xs_hbm.at[pl.ds(0, td * TILE_ROWS)], sem).wait()


def _dispatch(last_blk, pos_blk, h3_tiles, n_blocks):
    n_steps = pos_blk.shape[0]
    td = pos_blk.shape[2] // 2
    grid_spec = pltpu.PrefetchScalarGridSpec(
        num_scalar_prefetch=1,
        grid=(n_steps,),
        in_specs=[
            pl.BlockSpec((1, 1, 2 * td), lambda i, last: (i, 0, 0), memory_space=pltpu.SMEM),
            pl.BlockSpec((td * TILE_ROWS, LANES), lambda i, last: (i, 0)),
        ],
        out_specs=pl.BlockSpec(memory_space=pl.ANY),
        scratch_shapes=[pltpu.VMEM((MOE_ROWS * TILE_ROWS, LANES), jnp.uint32), pltpu.SemaphoreType.DMA(())],
    )
    return pl.pallas_call(
        _dispatch_kernel,
        grid_spec=grid_spec,
        out_shape=jax.ShapeDtypeStruct((n_blocks * MOE_ROWS * TILE_ROWS, LANES), jnp.uint32),
        compiler_params=_cparams(("arbitrary",)),
        name="moe_dispatch",
    )(last_blk, pos_blk, h3_tiles)


def _expert_kernel(be_ref, na_ref, x_ref, wg_ref, wu_ref, wd_ref, o_ref, wg_scr, wu_scr, wd_scr):
    j = pl.program_id(0)
    prev = be_ref[jnp.maximum(j - 1, 0)]

    @pl.when((j == 0) | (be_ref[j] != prev))
    def _():
        wg_scr[...] = wg_ref[0].astype(BF16)
        wu_scr[...] = wu_ref[0].astype(BF16)
        wd_scr[...] = wd_ref[0].astype(BF16)

    @pl.when(j < na_ref[0])
    def _():
        x = _load_token_tiles(x_ref, 0, MOE_ROWS).astype(BF16)
        g = jnp.dot(x, wg_scr[...], preferred_element_type=F32)
        u = jnp.dot(x, wu_scr[...], preferred_element_type=F32)
        hid = (_silu(g) * u).astype(BF16)
        _store_token_tiles(o_ref, jnp.dot(hid, wd_scr[...], preferred_element_type=F32))

    @pl.when(j >= na_ref[0])
    def _():
        o_ref[...] = jnp.zeros_like(o_ref)


def _experts(blk_exp, n_active, xs_tiles, wg, wu, wd):
    n_blocks = blk_exp.shape[0]
    d, ff = wg.shape[1], wg.shape[2]
    blk = (MOE_ROWS * TILE_ROWS, LANES)
    grid_spec = pltpu.PrefetchScalarGridSpec(
        num_scalar_prefetch=2,
        grid=(n_blocks,),
        in_specs=[
            pl.BlockSpec(blk, lambda j, be, na: (jnp.minimum(j, na[0] - 1), 0)),
            pl.BlockSpec((1, d, ff), lambda j, be, na: (be[j], 0, 0)),
            pl.BlockSpec((1, d, ff), lambda j, be, na: (be[j], 0, 0)),
            pl.BlockSpec((1, ff, d), lambda j, be, na: (be[j], 0, 0)),
        ],
        out_specs=pl.BlockSpec(blk, lambda j, be, na: (j, 0)),
        scratch_shapes=[pltpu.VMEM((d, ff), BF16), pltpu.VMEM((d, ff), BF16), pltpu.VMEM((ff, d), BF16)],
    )
    return pl.pallas_call(
        _expert_kernel,
        grid_spec=grid_spec,
        out_shape=jax.ShapeDtypeStruct(xs_tiles.shape, jnp.uint32),
        compiler_params=_cparams(("arbitrary",)),
        name="moe_experts",
    )(blk_exp, n_active, xs_tiles, wg, wu, wd)


COMBINE_ROWS = 32


def _combine_kernel(n_first, pos_ref, pos_next_ref, x_ref, gate_ref, nw_ref, ys_hbm, oa_ref, ob_ref, y_scr, sem):
    i = pl.program_id(0)
    tm = x_ref.shape[0]
    slot = i % 2

    def tile(ref, tok):
        return ref.at[pl.ds(pl.multiple_of(tok * TILE_ROWS, TILE_ROWS), TILE_ROWS)]

    def gather(p_ref, to):
        buf = y_scr.at[to]

        def body(r, carry):
            pltpu.make_async_copy(tile(ys_hbm, p_ref[0, 0, r]), tile(buf, r), sem.at[to]).start(priority=0)
            pltpu.make_async_copy(tile(ys_hbm, p_ref[0, 0, tm + r]), tile(buf, tm + r), sem.at[to]).start(priority=1)
            return carry

        lax.fori_loop(0, tm, body, 0, unroll=8)

    @pl.when(i == 0)
    def _():
        gather(pos_ref, 0)

    @pl.when(i + 1 < pl.num_programs(0))
    def _():
        gather(pos_next_ref, 1 - slot)

    buf = y_scr.at[slot]
    pltpu.make_async_copy(ys_hbm.at[pl.ds(0, 2 * tm * TILE_ROWS)], buf, sem.at[slot]).wait()
    def finish(o_ref):
        def rows(rb, carry):
            r0 = pl.multiple_of(rb * COMBINE_ROWS, COMBINE_ROWS)
            sl = pl.ds(r0, COMBINE_ROWS)
            gate = gate_ref[sl, :]
            y = x_ref[sl, :] + (gate[:, 0:1] * _load_token_tiles(buf, r0, COMBINE_ROWS)
                                + gate[:, 1:2] * _load_token_tiles(buf, tm + r0, COMBINE_ROWS))
            o_ref[sl, :] = _rms(y, nw_ref[...])
            return carry

        lax.fori_loop(0, tm // COMBINE_ROWS, rows, 0, unroll=4)

    @pl.when(i < n_first)
    def _():
        finish(oa_ref)

    @pl.when(i >= n_first)
    def _():
        finish(ob_ref)


def _combine(pos_blk, x2, gates, nw, ys, tm, t_first):
    t, d = x2.shape
    na = t_first // tm
    n = t // tm
    return pl.pallas_call(
        functools.partial(_combine_kernel, na),
        grid=(n,),
        in_specs=[
            pl.BlockSpec((1, 1, 2 * tm), lambda i: (i, 0, 0), memory_space=pltpu.SMEM),
            pl.BlockSpec((1, 1, 2 * tm), lambda i: (jnp.minimum(i + 1, n - 1), 0, 0), memory_space=pltpu.SMEM),
            pl.BlockSpec((tm, d), lambda i: (i, 0)),
            pl.BlockSpec((tm, LANES), lambda i: (i, 0)),
            pl.BlockSpec((1, d), lambda i: (0, 0)),
            pl.BlockSpec(memory_space=pl.ANY),
        ],
        out_specs=[
            pl.BlockSpec((tm, d), lambda i: (_first_or_second(i, na)[0], 0)),
            pl.BlockSpec((tm, d), lambda i: (_first_or_second(i, na)[1], 0)),
        ],
        out_shape=[jax.ShapeDtypeStruct((t_first, d), F32), jax.ShapeDtypeStruct((t - t_first, d), F32)],
        scratch_shapes=[pltpu.VMEM((2, 2 * tm * TILE_ROWS, LANES), jnp.uint32), pltpu.SemaphoreType.DMA((2,))],
        compiler_params=_cparams(("arbitrary",)),
        name="moe_combine",
    )(pos_blk, pos_blk, x2, gates, nw, ys)


def _block_experts(counts, n_blocks):
    per_expert = (counts + MOE_ROWS - 1) // MOE_ROWS
    ends = jnp.cumsum(per_expert)
    blk = jnp.arange(n_blocks, dtype=jnp.int32)
    blk_exp = jnp.minimum(jnp.sum(blk[:, None] >= ends[None, :], axis=1), N_EXPERTS - 1).astype(jnp.int32)
    last_blk = jnp.where(per_expert > 0, ends - 1, -1)
    tail = n_blocks - 1 - jnp.arange(N_EXPERTS)
    to_zero = jnp.concatenate([last_blk, jnp.where(tail >= ends[-1], tail, -1)]).astype(jnp.int32)
    return blk_exp, ends[-1:].astype(jnp.int32), to_zero


def _pick(n, pref):
    t = pref
    while n % t:
        t //= 2
    return t


def _encoder(xa, xb, mem, p):
    (nba, s, d), nbb = xa.shape, xb.shape[0]
    nb = nba + nbb
    t = nb * s
    t_first = nba * s
    nc = s // SSD_CHUNK
    tm = _pick(math.gcd(t_first, t - t_first), 1024)
    tq = _pick(s, 512)
    xa = xa.reshape(t_first, d)
    xb = xb.reshape(t - t_first, d)

    perm = jnp.arange(SSD_HEADS).reshape(SSD_GROUPS, HEADS_PER_GROUP)
    perm = jnp.concatenate([perm, perm + SSD_HEADS], axis=1).reshape(-1)

    w_in = p['w_in']
    o_z = 2 * D_MODEL
    o_x = o_z + SSD_D_INNER
    o_dt = o_x + SSD_D_INNER + 2 * SSD_GROUPS * SSD_STATE
    o_g = o_dt + 2 * SSD_HEADS
    w_val = w_in[:, :D_MODEL].astype(BF16)
    w_gate = w_in[:, D_MODEL:o_z].astype(BF16)
    w_z = w_in[:, o_z:o_x].astype(BF16)
    w_xbc = w_in[:, o_x:o_dt].astype(BF16)
    w_dt = jnp.pad(w_in[:, o_dt:o_g][:, perm], ((0, 0), (0, LANES - 2 * SSD_HEADS))).astype(BF16)
    w_g = w_in[:, o_g:].astype(BF16)
    nmix = p['norm_mix_w'].reshape(1, d)

    h = _prenorm(xa, xb, nmix, tm)
    u = _glu_proj(h, w_val, w_gate, tm, PROJ_TN // 2)
    xbc = _proj(h, w_xbc, lambda a: a, BF16, tm, PROJ_TN, "xbc_proj")
    dt_raw = _proj(h, w_dt, lambda a: a, F32, tm, LANES, "dt_proj")
    gates = _proj(h, w_g, _sigmoid, BF16, tm, PROJ_TN, "gate_proj")

    pad = LANES - 2 * SSD_HEADS
    dt_bias = jnp.pad(jnp.concatenate([p['ssd_dt_bias_f'], p['ssd_dt_bias_b']])[perm], (0, pad)).reshape(1, LANES)
    a_log = jnp.pad(jnp.concatenate([p['ssd_a_log_f'], p['ssd_a_log_b']])[perm], (0, pad)).reshape(1, LANES)
    acs2, rowt, e_hi, e_lo, w_hi, w_lo, dec = _dtprep(dt_raw, dt_bias, a_log, tm)

    def per_group_tm(a):
        return a[:, :2 * SSD_HEADS].reshape(nb, s, SSD_GROUPS, 8).transpose(0, 2, 1, 3)

    def per_group_hm(a):
        return a[:, :2 * SSD_HEADS].reshape(nb, nc, SSD_CHUNK, SSD_GROUPS, 8).transpose(0, 3, 1, 4, 2)

    acs_tm = per_group_tm(acs2).reshape(nb * SSD_GROUPS, s, 8)
    ew_tm = jnp.concatenate([per_group_tm(v) for v in (e_hi, e_lo, w_hi, w_lo)], axis=-1)
    ew_tm = ew_tm.reshape(nb * SSD_GROUPS, s, EW_COLS)
    rowt_hm = per_group_hm(rowt)
    dec = dec[:, :2 * SSD_HEADS].reshape(nb, nc, SSD_GROUPS, 2, HEADS_PER_GROUP).transpose(0, 2, 1, 3, 4)
    dec = jnp.repeat(dec, SSD_HEAD_DIM, axis=-1)

    cwx = p['ssd_conv_w']
    cbx = p['ssd_conv_b']

    def per_group_conv(a):
        xs_ = a[:, :SSD_D_INNER].reshape(-1, SSD_GROUPS, GROUP_CH)
        b_ = a[:, SSD_D_INNER:SSD_D_INNER + SSD_GROUPS * SSD_STATE].reshape(-1, SSD_GROUPS, SSD_STATE)
        c_ = a[:, SSD_D_INNER + SSD_GROUPS * SSD_STATE:].reshape(-1, SSD_GROUPS, SSD_STATE)
        return jnp.concatenate([xs_, b_, c_], axis=-1).transpose(1, 0, 2)

    cw = jnp.pad(per_group_conv(cwx), ((0, 0), (0, 8 - SSD_CONV_WIDTH), (0, 0)))
    cb = per_group_conv(cbx.reshape(1, -1))
    dskip = jnp.repeat(p['ssd_d'], SSD_HEAD_DIM).reshape(SSD_GROUPS, 1, GROUP_CH)
    nw_ssd = p['ssd_norm_w'].reshape(SSD_GROUPS, 1, GROUP_CH)

    yn = _ssd(xbc.reshape(nb, s, -1), h.reshape(nb, s, d), w_z, cw, cb, rowt_hm, acs_tm, ew_tm, dec, dskip, nw_ssd)
    mb = _gated_proj(yn.reshape(t, SSD_D_INNER), p['ssd_out'].astype(BF16), gates, 1, tm)

    dww = jnp.pad(p['conv_dw_w'], ((0, 32 - CONV_WIDTH), (0, 0)))
    x1 = _conformer(u.reshape(nb, s, d), mb.reshape(nb, s, d), gates.reshape(nb, s, 2 * d), xa, xb, dww,
                    p['conv_dw_b'].reshape(1, d), p['conv_ln_w'].reshape(1, d), p['conv_ln_b'].reshape(1, d),
                    p['conv_pw_out'].astype(BF16), p['w_out'].astype(BF16), tq)

    m = mem.shape[1]
    kv = _norm_proj(mem.reshape(nb * m, d), p['norm_mem_w'].reshape(1, d), p['xattn_wkv'].astype(BF16),
                    lambda a: a, BF16, _pick(nb * m, 512), 1024, "kv_proj")
    n_r = N_GROUPS + N_EXPERTS
    wr = jnp.pad(jnp.concatenate([p['router_group_w'], p['router_expert_w']], axis=1),
                 ((0, 0), (0, LANES - n_r))).astype(BF16)
    br = jnp.pad(jnp.concatenate([p['router_group_b'], p['router_expert_b']]), (0, LANES - n_r)).reshape(1, LANES)
    x2, h3, logits = _attention(x1, kv.reshape(nb, m, 2 * d), p['norm_xattn_w'].reshape(1, d),
                                p['xattn_wq'].astype(BF16), p['xattn_wo'].astype(BF16),
                                p['norm_ffn_w'].reshape(1, d), wr, br, tq)

    tc = _pick(math.gcd(t_first, t - t_first), 512)
    gate, pos, cnt = _plan(logits.reshape(t, LANES), tm)
    n_blocks = (2 * t) // MOE_ROWS + N_EXPERTS
    blk_exp, n_active, last_blk = _block_experts(cnt[0, :N_EXPERTS].astype(jnp.int32), n_blocks)
    pos_blk = pos[:, :2].reshape(t // tc, tc, 2).transpose(0, 2, 1).reshape(t // tc, 1, 2 * tc)
    pos_blk_d = pos[:, :2].reshape(t // tm, tm, 2).transpose(0, 2, 1).reshape(t // tm, 1, 2 * tm)
    xs = _dispatch(last_blk, pos_blk_d, h3, n_blocks)
    ys = _experts(blk_exp, n_active, xs, p['expert_w_gate'], p['expert_w_up'], p['expert_w_down'])
    return _combine(pos_blk, x2.reshape(t, d), gate, p['norm_final_w'].reshape(1, d), ys, tc, t_first)


def kernel(x_prompt, x_sample, mem_prompt, mem_sample, norm_mix_w, w_in, conv_dw_w, conv_dw_b, conv_ln_w, conv_ln_b, conv_pw_out, ssd_conv_w, ssd_conv_b, ssd_dt_bias_f, ssd_dt_bias_b, ssd_a_log_f, ssd_a_log_b, ssd_d, ssd_norm_w, ssd_out, w_out, norm_xattn_w, norm_mem_w, xattn_wq, xattn_wkv, xattn_wo, norm_ffn_w, router_group_w, router_group_b, router_expert_w, router_expert_b, expert_w_gate, expert_w_up, expert_w_down, norm_final_w):
    p = {
        'norm_mix_w': norm_mix_w[0], 'w_in': w_in[0], 'conv_dw_w': conv_dw_w[0], 'conv_dw_b': conv_dw_b[0],
        'conv_ln_w': conv_ln_w[0], 'conv_ln_b': conv_ln_b[0], 'conv_pw_out': conv_pw_out[0],
        'ssd_conv_w': ssd_conv_w[0], 'ssd_conv_b': ssd_conv_b[0], 'ssd_dt_bias_f': ssd_dt_bias_f[0],
        'ssd_dt_bias_b': ssd_dt_bias_b[0], 'ssd_a_log_f': ssd_a_log_f[0], 'ssd_a_log_b': ssd_a_log_b[0],
        'ssd_d': ssd_d[0], 'ssd_norm_w': ssd_norm_w[0], 'ssd_out': ssd_out[0], 'w_out': w_out[0],
        'norm_xattn_w': norm_xattn_w[0], 'norm_mem_w': norm_mem_w[0], 'xattn_wq': xattn_wq[0],
        'xattn_wkv': xattn_wkv[0], 'xattn_wo': xattn_wo[0], 'norm_ffn_w': norm_ffn_w[0],
        'router_group_w': router_group_w[0], 'router_group_b': router_group_b[0],
        'router_expert_w': router_expert_w[0], 'router_expert_b': router_expert_b[0],
        'expert_w_gate': expert_w_gate[0], 'expert_w_up': expert_w_up[0], 'expert_w_down': expert_w_down[0],
        'norm_final_w': norm_final_w,
    }
    mem = jnp.concatenate([mem_prompt, mem_sample], axis=0)
    y_prompt, y_sample = _encoder(x_prompt, x_sample, mem, p)
    return (y_prompt.reshape(x_prompt.shape), y_sample.reshape(x_sample.shape))
```

```python
import functools
import math

import jax
import jax.numpy as jnp
from jax import lax
from jax.experimental import pallas as pl
from jax.experimental.pallas import tpu as pltpu

F32 = jnp.float32
BF16 = jnp.bfloat16

D_MODEL = 1024
CONV_WIDTH = 31
SSD_D_INNER = 2048
SSD_HEAD_DIM = 64
SSD_HEADS = 32
SSD_GROUPS = 8
SSD_STATE = 128
SSD_CONV_WIDTH = 5
SSD_CHUNK = 128
GROUP_CH = SSD_D_INNER // SSD_GROUPS
HEADS_PER_GROUP = SSD_HEADS // SSD_GROUPS
XATTN_HEADS = 4
XATTN_HEAD_DIM = D_MODEL // XATTN_HEADS
N_GROUPS = 8
EXPERTS_PER_GROUP = 8
N_EXPERTS = 64
EXPERT_FF = 512
MOE_BLOCK = 128
RMS_EPS = 1e-6
LN_EPS = 1e-5

LANES = 128
HALO = 16
VMEM_LIMIT = 56 * 1024 * 1024
PROJ_TN = 2048


def _cparams(sem):
    return pltpu.CompilerParams(dimension_semantics=sem, vmem_limit_bytes=VMEM_LIMIT)


def _rms(x, w):
    ms = jnp.mean(x * x, axis=-1, keepdims=True)
    return x * lax.rsqrt(ms + RMS_EPS) * w


def _sigmoid(x):
    return 1.0 / (1.0 + jnp.exp2(x * -1.4426950408889634))


def _silu(x):
    return x * _sigmoid(x)


def _softplus(x):
    return jnp.maximum(x, 0.0) + jnp.log1p(jnp.exp(-jnp.abs(x)))


CONV_STRIDE = 4
CONV_ROWS = 8 * CONV_STRIDE
CONV_UNROLL = 4


def _depthwise_conv(src, dst, slab, w_ref, b_ref, n_rows, width, first_row, epilogue):
    lanes = pl.ds(slab * LANES, LANES)
    taps = [jnp.broadcast_to(w_ref[k:k + 1, lanes], (8, LANES)) for k in range(width)]
    bias = jnp.broadcast_to(b_ref[:, lanes], (8, LANES))
    step = CONV_ROWS * CONV_UNROLL

    def body(r, carry):
        _conv_rows(src, dst, slab, taps, bias, r * step, step, width, first_row, epilogue)
        return carry

    lax.fori_loop(0, n_rows // step, body, 0)


def _conv_rows(src, dst, slab, taps, bias, base, n_rows, width, first_row, epilogue):
    for t0 in range(n_rows // 8):
        row = base + (t0 // CONV_STRIDE) * CONV_ROWS + t0 % CONV_STRIDE
        acc = bias
        for k in range(width):
            acc = acc + src[slab, pl.ds(row + first_row - width // 2 + k, 8, stride=CONV_STRIDE), :] * taps[k]
        dst[slab, pl.ds(row, 8, stride=CONV_STRIDE), :] = epilogue(acc)


def _first_or_second(i, n_first):
    return jnp.minimum(i, n_first - 1), jnp.maximum(i - n_first, 0)


def _prenorm_kernel(n_first, xa_ref, xb_ref, nw_ref, o_ref):
    x = jnp.where(pl.program_id(0) < n_first, xa_ref[...], xb_ref[...])
    o_ref[...] = _rms(x, nw_ref[...]).astype(o_ref.dtype)


def _prenorm(xa, xb, nw, tm):
    (ta, d), tb = xa.shape, xb.shape[0]
    na = ta // tm
    return pl.pallas_call(
        functools.partial(_prenorm_kernel, na),
        grid=((ta + tb) // tm,),
        in_specs=[
            pl.BlockSpec((tm, d), lambda i: (_first_or_second(i, na)[0], 0)),
            pl.BlockSpec((tm, d), lambda i: (_first_or_second(i, na)[1], 0)),
            pl.BlockSpec((1, d), lambda i: (0, 0)),
        ],
        out_specs=pl.BlockSpec((tm, d), lambda i: (i, 0)),
        out_shape=jax.ShapeDtypeStruct((ta + tb, d), BF16),
        compiler_params=_cparams(("arbitrary",)),
        name="mix_norm",
    )(xa, xb, nw)


def _glu_kernel(h_ref, wv_ref, wg_ref, o_ref):
    h = h_ref[...]
    v = jnp.dot(h, wv_ref[...], preferred_element_type=F32)
    g = jnp.dot(h, wg_ref[...], preferred_element_type=F32)
    o_ref[...] = (v * _sigmoid(g)).astype(o_ref.dtype)


def _glu_proj(h, wv, wg, tm, tn):
    t, d = h.shape
    n = wv.shape[1]
    return pl.pallas_call(
        _glu_kernel,
        grid=(t // tm, n // tn),
        in_specs=[
            pl.BlockSpec((tm, d), lambda i, j: (i, 0)),
            pl.BlockSpec((d, tn), lambda i, j: (0, j)),
            pl.BlockSpec((d, tn), lambda i, j: (0, j)),
        ],
        out_specs=pl.BlockSpec((tm, tn), lambda i, j: (i, j)),
        out_shape=jax.ShapeDtypeStruct((t, n), BF16),
        compiler_params=_cparams(("parallel", "arbitrary")),
        name="glu_proj",
    )(h, wv, wg)


def _proj_kernel(epi, h_ref, w_ref, o_ref):
    o_ref[...] = epi(jnp.dot(h_ref[...], w_ref[...], preferred_element_type=F32)).astype(o_ref.dtype)


def _proj(h, w, epi, out_dtype, tm, tn, name):
    t, d = h.shape
    n = w.shape[1]
    return pl.pallas_call(
        functools.partial(_proj_kernel, epi),
        grid=(t // tm, n // tn),
        in_specs=[
            pl.BlockSpec((tm, d), lambda i, j: (i, 0)),
            pl.BlockSpec((d, tn), lambda i, j: (0, j)),
        ],
        out_specs=pl.BlockSpec((tm, tn), lambda i, j: (i, j)),
        out_shape=jax.ShapeDtypeStruct((t, n), out_dtype),
        compiler_params=_cparams(("parallel", "arbitrary")),
        name=name,
    )(h, w)


def _norm_proj_kernel(epi, x_ref, nw_ref, w_ref, o_ref, h_scr):
    @pl.when(pl.program_id(1) == 0)
    def _():
        h_scr[...] = _rms(x_ref[...], nw_ref[...]).astype(BF16)

    acc = jnp.dot(h_scr[...], w_ref[...], preferred_element_type=F32)
    o_ref[...] = epi(acc).astype(o_ref.dtype)


def _norm_proj(x, nw, w, epi, out_dtype, tm, tn, name):
    t, d = x.shape
    n = w.shape[1]
    return pl.pallas_call(
        functools.partial(_norm_proj_kernel, epi),
        grid=(t // tm, n // tn),
        in_specs=[
            pl.BlockSpec((tm, d), lambda i, j: (i, 0)),
            pl.BlockSpec((1, d), lambda i, j: (0, 0)),
            pl.BlockSpec((d, tn), lambda i, j: (0, j)),
        ],
        out_specs=pl.BlockSpec((tm, tn), lambda i, j: (i, j)),
        out_shape=jax.ShapeDtypeStruct((t, n), out_dtype),
        scratch_shapes=[pltpu.VMEM((tm, d), BF16)],
        compiler_params=_cparams(("parallel", "arbitrary")),
        name=name,
    )(x, nw, w)


def _gated_proj_kernel(y_ref, w_ref, g_ref, o_ref):
    acc = jnp.dot(y_ref[...], w_ref[...], preferred_element_type=F32)
    o_ref[...] = (acc * g_ref[...].astype(F32)).astype(o_ref.dtype)


def _gated_proj(y, w, g, g_col_block, tm):
    t, k = y.shape
    n = w.shape[1]
    return pl.pallas_call(
        _gated_proj_kernel,
        grid=(t // tm,),
        in_specs=[
            pl.BlockSpec((tm, k), lambda i: (i, 0)),
            pl.BlockSpec((k, n), lambda i: (0, 0)),
            pl.BlockSpec((tm, n), lambda i: (i, g_col_block)),
        ],
        out_specs=pl.BlockSpec((tm, n), lambda i: (i, 0)),
        out_shape=jax.ShapeDtypeStruct((t, n), BF16),
        compiler_params=_cparams(("parallel",)),
        name="ssd_out_proj",
    )(y, w, g)


LOG2E = 1.4426950408889634


def _split_bf16(v):
    hi = v.astype(BF16)
    return hi, (v - hi.astype(F32)).astype(BF16)


def _dtprep_kernel(raw_ref, bias_ref, alog_ref, acs2_ref, rowt_ref, ehi_ref, elo_ref, whi_ref, wlo_ref, dec_ref):
    rows = raw_ref.shape[0]
    a_head = -jnp.exp(alog_ref[...])
    row = lax.broadcasted_iota(jnp.int32, (SSD_CHUNK, LANES), 0)
    lane = lax.broadcasted_iota(jnp.int32, (SSD_CHUNK, LANES), 1)
    is_bwd = (lane % (2 * HEADS_PER_GROUP)) >= HEADS_PER_GROUP
    for c in range(rows // SSD_CHUNK):
        sl = pl.ds(c * SSD_CHUNK, SSD_CHUNK)
        dt = _softplus(raw_ref[sl, :] + bias_ref[...])
        a = dt * a_head
        fwd = a
        bwd = a
        sh = 1
        while sh < SSD_CHUNK:
            fwd = fwd + jnp.where(row >= sh, pltpu.roll(fwd, sh, 0), 0.0)
            bwd = bwd + jnp.where(row < SSD_CHUNK - sh, pltpu.roll(bwd, SSD_CHUNK - sh, 0), 0.0)
            sh *= 2
        acs = jnp.where(is_bwd, bwd, fwd)
        a_end = jnp.where(is_bwd[0:1, :], bwd[0:1, :], fwd[SSD_CHUNK - 1:SSD_CHUNK, :])
        acs2 = acs * LOG2E
        acs2_ref[sl, :] = acs2
        rowt_ref[sl, :] = acs2 - jnp.log(dt) * LOG2E
        ehi_ref[sl, :], elo_ref[sl, :] = _split_bf16(jnp.exp(acs))
        whi_ref[sl, :], wlo_ref[sl, :] = _split_bf16(jnp.exp(a_end - acs) * dt)
        dec_ref[c:c + 1, :] = jnp.exp(a_end)


def _dtprep(raw, bias, alog, tm):
    t = raw.shape[0]
    spec = pl.BlockSpec((tm, LANES), lambda i: (i, 0))
    vec = pl.BlockSpec((1, LANES), lambda i: (0, 0))
    return pl.pallas_call(
        _dtprep_kernel,
        grid=(t // tm,),
        in_specs=[spec, vec, vec],
        out_specs=[spec] * 6 + [pl.BlockSpec((tm // SSD_CHUNK, LANES), lambda i: (i, 0))],
        out_shape=[jax.ShapeDtypeStruct((t, LANES), F32)] * 2 + [jax.ShapeDtypeStruct((t, LANES), BF16)] * 4
        + [jax.ShapeDtypeStruct((t // SSD_CHUNK, LANES), F32)],
        compiler_params=_cparams(("parallel",)),
        name="dt_prep",
    )(raw, bias, alog)


SSD_PAD = 8
XBC_W = GROUP_CH + 2 * SSD_STATE


EW_COLS = 4 * 2 * HEADS_PER_GROUP


def _head_expand_matrix():
    col = jnp.arange(4 * GROUP_CH)
    part = col // GROUP_CH
    is_w = (part == 1) | (part == 2)
    is_bwd = part >= 2
    head = is_bwd * HEADS_PER_GROUP + (col % GROUP_CH) // SSD_HEAD_DIM
    hi_row = is_w * 16 + head
    row = jnp.arange(EW_COLS)[:, None]
    return ((row == hi_row[None, :]) | (row == hi_row[None, :] + 8)).astype(BF16)


def _ssd_kernel(xs_ref, b_ref, c_ref, h_ref, wz_ref, cw_ref, cb_ref, rowt_ref, acs_ref, ew_ref, sel_ref, dec_ref,
                dskip_ref, nw_ref, o_ref, raw_scr, act_scr, y_scr, upd_scr, st_scr):
    s = xs_ref.shape[1]
    nc = s // SSD_CHUNK
    L = SSD_CHUNK

    n_slabs = XBC_W // LANES
    for j in range(n_slabs):
        raw_scr[j, pl.ds(0, SSD_PAD), :] = jnp.zeros((SSD_PAD, LANES), F32)
        raw_scr[j, pl.ds(SSD_PAD + s, L + SSD_PAD), :] = jnp.zeros((L + SSD_PAD, LANES), F32)
    raw_scr[0, pl.ds(SSD_PAD, s), :] = xs_ref[0, :, :LANES].astype(F32)
    raw_scr[1, pl.ds(SSD_PAD, s), :] = xs_ref[0, :, LANES:].astype(F32)
    raw_scr[2, pl.ds(SSD_PAD, s), :] = b_ref[0].astype(F32)
    raw_scr[3, pl.ds(SSD_PAD, s), :] = c_ref[0].astype(F32)

    def conv_chunk(c):
        for j in range(n_slabs):
            lanes = pl.ds(j * LANES, LANES)
            taps = [jnp.broadcast_to(cw_ref[0, k:k + 1, lanes], (8, LANES)) for k in range(SSD_CONV_WIDTH)]
            bias = jnp.broadcast_to(cb_ref[0, :, lanes], (8, LANES))
            _conv_rows(raw_scr, act_scr, j, taps, bias, c * L, L, SSD_CONV_WIDTH, SSD_PAD, _silu)

    conv_chunk(0)

    li = lax.broadcasted_iota(jnp.int32, (L, L), 0)
    si = lax.broadcasted_iota(jnp.int32, (L, L), 1)
    head_of_ch = lax.broadcasted_iota(jnp.int32, (L, GROUP_CH), 1) // SSD_HEAD_DIM

    keep = (li >= si, si >= li)
    neg_inf = jnp.float32(-jnp.inf)
    n_heads = 2 * HEADS_PER_GROUP
    st_scr[...] = jnp.zeros_like(st_scr)

    def local_body(c, carry):
        r0 = pl.multiple_of(c * L, L)
        x = jnp.concatenate([act_scr[0, pl.ds(r0, L), :], act_scr[1, pl.ds(r0, L), :]], axis=1)
        xb = x.astype(BF16)
        bk = act_scr[2, pl.ds(r0, L), :].astype(BF16)
        ck = act_scr[3, pl.ds(r0, L), :].astype(BF16)
        acs = acs_ref[0, pl.ds(r0, L), :]
        rowt = rowt_ref[0, 0, c]
        ew = jnp.dot(ew_ref[0, pl.ds(r0, L), :], sel_ref[:, :3 * GROUP_CH],
                     preferred_element_type=F32)

        cb = lax.dot_general(ck, bk, (((1,), (1,)), ((), ())), preferred_element_type=F32)
        ms = []
        for j in range(n_heads):
            diff = acs[:, j:j + 1] - rowt[j:j + 1, :]
            decay_dt = jnp.exp2(jnp.where(keep[j // HEADS_PER_GROUP], diff, neg_inf))
            ms.append((cb * decay_dt).astype(BF16))
        xbd = jnp.concatenate([jnp.where(head_of_ch == h, xb, jnp.zeros_like(xb))
                               for h in range(HEADS_PER_GROUP)], axis=0)
        y = jnp.dot(jnp.concatenate(ms, axis=1), jnp.concatenate([xbd, xbd], axis=0),
                    preferred_element_type=F32) + x * dskip_ref[0]

        xw = jnp.concatenate([x * ew[:, GROUP_CH:2 * GROUP_CH], x * ew[:, 2 * GROUP_CH:]], axis=1)
        upd = lax.dot_general(bk, xw.astype(BF16), (((0,), (0,)), ((), ())), preferred_element_type=F32)

        state = st_scr[0]
        y = y + jnp.dot(ck, state.astype(BF16), preferred_element_type=F32) * ew[:, :GROUP_CH]
        st_scr[0] = state * dec_ref[0, 0, c, 0:1, :] + upd[:, :GROUP_CH]
        y_scr[pl.ds(r0, L), :] = y
        upd_scr[c] = upd[:, GROUP_CH:]
        conv_chunk(c + 1)
        return carry

    lax.fori_loop(0, nc, local_body, 0, unroll=4)

    def bwd_body(i, carry):
        c = nc - 1 - i
        r0 = pl.multiple_of(c * L, L)
        ck = act_scr[3, pl.ds(r0, L), :].astype(BF16)
        e_bwd = jnp.dot(ew_ref[0, pl.ds(r0, L), :], sel_ref[:, 3 * GROUP_CH:], preferred_element_type=F32)
        state = st_scr[1]
        y = y_scr[pl.ds(r0, L), :] + jnp.dot(ck, state.astype(BF16), preferred_element_type=F32) * e_bwd
        st_scr[1] = state * dec_ref[0, 0, c, 1:2, :] + upd_scr[c]
        z = jnp.dot(h_ref[0, pl.ds(r0, L), :], wz_ref[...], preferred_element_type=F32)
        yz = y * _silu(z)
        ms = jnp.mean(yz * yz, axis=-1, keepdims=True)
        o_ref[0, pl.ds(r0, L), :] = (yz * lax.rsqrt(ms + RMS_EPS) * nw_ref[0]).astype(o_ref.dtype)
        return carry

    lax.fori_loop(0, nc, bwd_body, 0, unroll=8)


def _ssd(xbc, h, wz, cw, cb, rowt_hm, acs_tm, ew_tm, dec, dskip, nw):
    nb, s, _ = xbc.shape
    d = h.shape[2]
    nc = s // SSD_CHUNK
    return pl.pallas_call(
        _ssd_kernel,
        grid=(nb, SSD_GROUPS),
        in_specs=[
            pl.BlockSpec((1, s, GROUP_CH), lambda b, g: (b, 0, g)),
            pl.BlockSpec((1, s, SSD_STATE), lambda b, g: (b, 0, SSD_D_INNER // SSD_STATE + g)),
            pl.BlockSpec((1, s, SSD_STATE), lambda b, g: (b, 0, SSD_D_INNER // SSD_STATE + SSD_GROUPS + g)),
            pl.BlockSpec((1, s, d), lambda b, g: (b, 0, 0)),
            pl.BlockSpec((d, GROUP_CH), lambda b, g: (0, g)),
            pl.BlockSpec((1, 8, XBC_W), lambda b, g: (g, 0, 0)),
            pl.BlockSpec((1, 1, XBC_W), lambda b, g: (g, 0, 0)),
            pl.BlockSpec((1, 1, nc, 8, SSD_CHUNK), lambda b, g: (b, g, 0, 0, 0)),
            pl.BlockSpec((1, s, 8), lambda b, g: (b * SSD_GROUPS + g, 0, 0)),
            pl.BlockSpec((1, s, EW_COLS), lambda b, g: (b * SSD_GROUPS + g, 0, 0)),
            pl.BlockSpec((EW_COLS, 4 * GROUP_CH), lambda b, g: (0, 0)),
            pl.BlockSpec((1, 1, nc, 2, GROUP_CH), lambda b, g: (b, g, 0, 0, 0)),
            pl.BlockSpec((1, 1, GROUP_CH), lambda b, g: (g, 0, 0)),
            pl.BlockSpec((1, 1, GROUP_CH), lambda b, g: (g, 0, 0)),
        ],
        out_specs=pl.BlockSpec((1, s, GROUP_CH), lambda b, g: (b, 0, g)),
        out_shape=jax.ShapeDtypeStruct((nb, s, SSD_D_INNER), BF16),
        scratch_shapes=[
            pltpu.VMEM((XBC_W // LANES, s + SSD_CHUNK + 2 * SSD_PAD, LANES), F32),
            pltpu.VMEM((XBC_W // LANES, s + SSD_CHUNK, LANES), F32),
            pltpu.VMEM((s, GROUP_CH), F32),
            pltpu.VMEM((nc, SSD_STATE, GROUP_CH), F32),
            pltpu.VMEM((2, SSD_STATE, GROUP_CH), F32),
        ],
        compiler_params=_cparams(("parallel", "parallel")),
        name="ssd_scan",
    )(xbc, xbc, xbc, h, wz, cw, cb, rowt_hm, acs_tm, ew_tm, _head_expand_matrix(), dec, dskip, nw)


D_SLABS = D_MODEL // LANES


def _conformer_kernel(n_first, u_ref, up_ref, un_ref, mb_ref, g_ref, xa_ref, xb_ref, dww_ref, dwb_ref, lnw_ref,
                      lnb_ref, pw_ref, wo_ref, o_ref, u_scr, c_scr):
    i = pl.program_id(1)
    n = pl.num_programs(1)
    tq = u_ref.shape[1]
    x = jnp.where(pl.program_id(0) * n + i < n_first, xa_ref[...], xb_ref[...])
    for j in range(D_SLABS):
        cs = slice(j * LANES, (j + 1) * LANES)
        u_scr[j, pl.ds(0, HALO), :] = jnp.where(i > 0, up_ref[0, :, cs].astype(F32), 0.0)
        u_scr[j, pl.ds(HALO, tq), :] = u_ref[0, :, cs].astype(F32)
        u_scr[j, pl.ds(HALO + tq, HALO), :] = jnp.where(i < n - 1, un_ref[0, :, cs].astype(F32), 0.0)
    for j in range(D_SLABS):
        _depthwise_conv(u_scr, c_scr, j, dww_ref, dwb_ref, tq, CONV_WIDTH, HALO, lambda a: a)

    cv = jnp.concatenate([c_scr[j] for j in range(D_SLABS)], axis=1)
    mu = jnp.mean(cv, axis=-1, keepdims=True)
    xc = cv - mu
    var = jnp.mean(xc * xc, axis=-1, keepdims=True)
    y = xc * lax.rsqrt(var + LN_EPS) * lnw_ref[...] + lnb_ref[...]
    a = jnp.dot(_silu(y).astype(BF16), pw_ref[...], preferred_element_type=F32)
    merged = g_ref[0].astype(F32) * a + mb_ref[0].astype(F32)
    o_ref[0] = x + jnp.dot(merged.astype(BF16), wo_ref[...], preferred_element_type=F32)


def _conformer(u, mb, gates, xa, xb, dww, dwb, lnw, lnb, pw, wo, tq):
    nb, s, d = u.shape
    hb = tq // HALO
    nh = s // HALO
    nq = s // tq
    na = xa.shape[0] // tq
    wspec = pl.BlockSpec((d, d), lambda b, i: (0, 0))
    vspec = pl.BlockSpec((1, d), lambda b, i: (0, 0))
    return pl.pallas_call(
        functools.partial(_conformer_kernel, na),
        grid=(nb, nq),
        in_specs=[
            pl.BlockSpec((1, tq, d), lambda b, i: (b, i, 0)),
            pl.BlockSpec((1, HALO, d), lambda b, i: (b, jnp.maximum(i * hb - 1, 0), 0)),
            pl.BlockSpec((1, HALO, d), lambda b, i: (b, jnp.minimum((i + 1) * hb, nh - 1), 0)),
            pl.BlockSpec((1, tq, d), lambda b, i: (b, i, 0)),
            pl.BlockSpec((1, tq, d), lambda b, i: (b, i, 0)),
            pl.BlockSpec((tq, d), lambda b, i: (_first_or_second(b * nq + i, na)[0], 0)),
            pl.BlockSpec((tq, d), lambda b, i: (_first_or_second(b * nq + i, na)[1], 0)),
            pl.BlockSpec((32, d), lambda b, i: (0, 0)),
            vspec, vspec, vspec, wspec, wspec,
        ],
        out_specs=pl.BlockSpec((1, tq, d), lambda b, i: (b, i, 0)),
        out_shape=jax.ShapeDtypeStruct((nb, s, d), F32),
        scratch_shapes=[pltpu.VMEM((D_SLABS, tq + 2 * HALO, LANES), F32), pltpu.VMEM((D_SLABS, tq, LANES), F32)],
        compiler_params=_cparams(("arbitrary", "arbitrary")),
        name="conformer_merge",
    )(u, u, u, mb, gates, xa, xb, dww, dwb, lnw, lnb, pw, wo)


TILE_ROWS = D_MODEL // (2 * LANES)
HIGH_HALF = 0xFFFF0000


def _bf16_bits(v):
    return lax.bitcast_convert_type(v.astype(BF16).astype(F32), jnp.uint32)


def _store_token_tiles(ref, val):
    n, d = val.shape
    packed = (_bf16_bits(val[:, :d // 2]) >> 16) | (_bf16_bits(val[:, d // 2:]) & jnp.uint32(HIGH_HALF))
    for j in range(TILE_ROWS):
        ref[pl.ds(j, n, stride=TILE_ROWS), :] = packed[:, j * LANES:(j + 1) * LANES]


def _load_token_tiles(ref, first, n):
    packed = jnp.concatenate([ref[pl.ds(first * TILE_ROWS + j, n, stride=TILE_ROWS), :] for j in range(TILE_ROWS)],
                             axis=1)
    low = lax.bitcast_convert_type(packed << 16, F32)
    high = lax.bitcast_convert_type(packed & jnp.uint32(HIGH_HALF), F32)
    return jnp.concatenate([low, high], axis=1)


def _attn_kernel(x_ref, k_ref, v_ref, nw_ref, wq_ref, wo_ref, nf_ref, wr_ref, br_ref, x2_ref, h3_ref, lg_ref):
    x = x_ref[0]
    h = _rms(x, nw_ref[...]).astype(BF16)
    q = jnp.dot(h, wq_ref[...], preferred_element_type=F32)
    scale = 1.0 / math.sqrt(XATTN_HEAD_DIM)
    outs = []
    for hd in range(XATTN_HEADS):
        cs = slice(hd * XATTN_HEAD_DIM, (hd + 1) * XATTN_HEAD_DIM)
        sc = lax.dot_general(q[:, cs].astype(BF16), k_ref[0, :, cs], (((1,), (1,)), ((), ())),
                             preferred_element_type=F32) * scale
        m = jnp.max(sc, axis=-1, keepdims=True)
        e = jnp.exp(sc - m)
        p = e / jnp.sum(e, axis=-1, keepdims=True)
        outs.append(jnp.dot(p.astype(BF16), v_ref[0, :, cs], preferred_element_type=F32))
    o = jnp.concatenate(outs, axis=1).astype(BF16)
    x2 = x + jnp.dot(o, wo_ref[...], preferred_element_type=F32)
    x2_ref[0] = x2
    h3 = _rms(x2, nf_ref[...])
    _store_token_tiles(h3_ref, h3)
    lg_ref[0] = jnp.dot(h3.astype(BF16), wr_ref[...], preferred_element_type=F32) + br_ref[...]


def _attention(x, kv, nw, wq, wo, nf, wr, br, tq):
    nb, s, d = x.shape
    m = kv.shape[1]
    wspec = pl.BlockSpec((d, d), lambda b, i: (0, 0))
    vspec = pl.BlockSpec((1, d), lambda b, i: (0, 0))
    return pl.pallas_call(
        _attn_kernel,
        grid=(nb, s // tq),
        in_specs=[
            pl.BlockSpec((1, tq, d), lambda b, i: (b, i, 0)),
            pl.BlockSpec((1, m, d), lambda b, i: (b, 0, 0)),
            pl.BlockSpec((1, m, d), lambda b, i: (b, 0, 1)),
            vspec, wspec, wspec, vspec,
            pl.BlockSpec((d, LANES), lambda b, i: (0, 0)),
            pl.BlockSpec((1, LANES), lambda b, i: (0, 0)),
        ],
        out_specs=[
            pl.BlockSpec((1, tq, d), lambda b, i: (b, i, 0)),
            pl.BlockSpec((tq * TILE_ROWS, LANES), lambda b, i: (b * (s // tq) + i, 0)),
            pl.BlockSpec((1, tq, LANES), lambda b, i: (b, i, 0)),
        ],
        out_shape=[
            jax.ShapeDtypeStruct((nb, s, d), F32),
            jax.ShapeDtypeStruct((nb * s * TILE_ROWS, LANES), jnp.uint32),
            jax.ShapeDtypeStruct((nb, s, LANES), F32),
        ],
        compiler_params=_cparams(("parallel", "parallel")),
        name="cross_attn_router",
    )(x, kv, kv, nw, wq, wo, nf, wr, br)


MOE_ROWS = 512


def _route_math(lg, lane):
    lane_f = lane.astype(F32)
    big = float(LANES)
    neg = -jnp.inf

    is_g = lane < N_GROUPS
    gl = jnp.where(is_g, lg, neg)
    gmax = jnp.max(gl, axis=-1, keepdims=True)
    g_idx = jnp.min(jnp.where(gl == gmax, lane_f, big), axis=-1, keepdims=True)
    g_sum = jnp.sum(jnp.where(is_g, jnp.exp(gl - gmax), 0.0), axis=-1, keepdims=True)
    g_w = 1.0 / g_sum

    e_grp = lax.shift_right_arithmetic(lane - N_GROUPS, 3).astype(F32)
    is_e = (lane >= N_GROUPS) & (lane < N_GROUPS + N_EXPERTS) & (e_grp == g_idx)
    el = jnp.where(is_e, lg, neg)
    emax = jnp.max(el, axis=-1, keepdims=True)
    ee = jnp.where(is_e, jnp.exp(el - emax), 0.0)
    ep = ee / jnp.sum(ee, axis=-1, keepdims=True)
    ep = jnp.where(is_e, ep, -1.0)
    v1 = jnp.max(ep, axis=-1, keepdims=True)
    i1 = jnp.min(jnp.where(ep == v1, lane_f, big), axis=-1, keepdims=True)
    ep2 = jnp.where(lane_f == i1, -1.0, ep)
    v2 = jnp.max(ep2, axis=-1, keepdims=True)
    i2 = jnp.min(jnp.where(ep2 == v2, lane_f, big), axis=-1, keepdims=True)
    tot = v1 + v2
    gate1 = g_w * (v1 / tot)
    gate2 = g_w * (v2 / tot)
    return i1 - N_GROUPS, i2 - N_GROUPS, gate1, gate2


def _plan_kernel(lg_ref, gate_ref, pos_ref, cnt_ref, carry_scr, base_scr):
    p = pl.program_id(0)
    i = pl.program_id(1)
    tm = lg_ref.shape[0]
    lane = lax.broadcasted_iota(jnp.int32, (tm, LANES), 1)
    lane_f = lane.astype(F32)
    e1, e2, gate1, gate2 = _route_math(lg_ref[...], lane)
    oh1 = lane_f == e1
    oh2 = lane_f == e2
    m = jnp.where(oh1 | oh2, 1.0, 0.0)
    colsum = jnp.sum(m, axis=0, keepdims=True)

    @pl.when((p == 0) & (i == 0))
    def _():
        carry_scr[...] = jnp.zeros_like(carry_scr)

    @pl.when(p == 0)
    def _():
        carry_scr[...] += colsum

    @pl.when((p == 1) & (i == 0))
    def _():
        counts = carry_scr[...]
        cnt_ref[...] = counts
        shift = MOE_ROWS.bit_length() - 1
        blocks = lax.shift_right_logical(counts.astype(jnp.int32) + (MOE_ROWS - 1), shift)
        padded = (blocks * MOE_ROWS).astype(F32)
        lane8 = lax.broadcasted_iota(jnp.int32, padded.shape, 1)
        inc = padded
        sh = 1
        while sh < LANES:
            inc = inc + jnp.where(lane8 >= sh, pltpu.roll(inc, sh, 1), 0.0)
            sh *= 2
        base_scr[...] = inc - padded
        carry_scr[...] = jnp.zeros_like(carry_scr)

    @pl.when(p == 1)
    def _():
        r = lax.broadcasted_iota(jnp.int32, (tm, tm), 0)
        c = lax.broadcasted_iota(jnp.int32, (tm, tm), 1)
        earlier = jnp.where(r > c, 1.0, 0.0).astype(BF16)
        tot = (jnp.dot(earlier, m.astype(BF16), preferred_element_type=F32)
               + base_scr[0:1, :] + carry_scr[0:1, :])
        pos1 = jnp.sum(jnp.where(oh1, tot, 0.0), axis=1, keepdims=True)
        pos2 = jnp.sum(jnp.where(oh2, tot, 0.0), axis=1, keepdims=True)
        pos_ref[...] = jnp.where(lane == 0, pos1, jnp.where(lane == 1, pos2, 0.0)).astype(jnp.int32)
        gate_ref[...] = jnp.where(lane == 0, gate1, jnp.where(lane == 1, gate2, 0.0))
        carry_scr[...] += colsum


def _plan(lg, tm):
    t = lg.shape[0]
    out_spec = pl.BlockSpec((tm, LANES), lambda p, i: (i * p, 0))
    return pl.pallas_call(
        _plan_kernel,
        grid=(2, t // tm),
        in_specs=[pl.BlockSpec((tm, LANES), lambda p, i: (i, 0))],
        out_specs=[out_spec, out_spec, pl.BlockSpec((8, LANES), lambda p, i: (0, 0))],
        out_shape=[jax.ShapeDtypeStruct((t, LANES), F32), jax.ShapeDtypeStruct((t, LANES), jnp.int32),
                   jax.ShapeDtypeStruct((8, LANES), F32)],
        scratch_shapes=[pltpu.VMEM((8, LANES), F32), pltpu.VMEM((8, LANES), F32)],
        compiler_params=_cparams(("arbitrary", "arbitrary")),
        name="moe_plan",
    )(lg)


def _dispatch_kernel(last_ref, pos_ref, h_ref, xs_hbm, zero_scr, sem):
    td = pos_ref.shape[2] // 2

    @pl.when(pl.program_id(0) == 0)
    def _():
        zero_scr[...] = jnp.zeros_like(zero_scr)

        def block_copy(e):
            first = pl.multiple_of(last_ref[e] * (MOE_ROWS * TILE_ROWS), MOE_ROWS * TILE_ROWS)
            return pltpu.make_async_copy(zero_scr, xs_hbm.at[pl.ds(first, MOE_ROWS * TILE_ROWS)], sem)

        def fill(e, carry):
            @pl.when(last_ref[e] >= 0)
            def _():
                block_copy(e).start()
            return carry

        def drain(e, carry):
            @pl.when(last_ref[e] >= 0)
            def _():
                block_copy(e).wait()
            return carry

        lax.fori_loop(0, last_ref.shape[0], fill, 0)
        lax.fori_loop(0, last_ref.shape[0], drain, 0)

    def tile(ref, tok):
        return ref.at[pl.ds(pl.multiple_of(tok * TILE_ROWS, TILE_ROWS), TILE_ROWS)]

    def body(r, carry):
        src = tile(h_ref, r)
        pltpu.make_async_copy(src, tile(xs_hbm, pos_ref[0, 0, r]), sem).start(priority=0)
        pltpu.make_async_copy(src, tile(xs_hbm, pos_ref[0, 0, td + r]), sem).start(priority=1)
        return carry

    lax.fori_loop(0, td, body, 0, unroll=8)
    for _ in range(2):
        pltpu.make_async_copy(h_ref, xs_hbm.at[pl.ds(0, td * TILE_ROWS)], sem).wait()


def _dispatch(last_blk, pos_blk, h3_tiles, n_blocks):
    n_steps = pos_blk.shape[0]
    td = pos_blk.shape[2] // 2
    grid_spec = pltpu.PrefetchScalarGridSpec(
        num_scalar_prefetch=1,
        grid=(n_steps,),
        in_specs=[
            pl.BlockSpec((1, 1, 2 * td), lambda i, last: (i, 0, 0), memory_space=pltpu.SMEM),
            pl.BlockSpec((td * TILE_ROWS, LANES), lambda i, last: (i, 0)),
        ],
        out_specs=pl.BlockSpec(memory_space=pl.ANY),
        scratch_shapes=[pltpu.VMEM((MOE_ROWS * TILE_ROWS, LANES), jnp.uint32), pltpu.SemaphoreType.DMA(())],
    )
    return pl.pallas_call(
        _dispatch_kernel,
        grid_spec=grid_spec,
        out_shape=jax.ShapeDtypeStruct((n_blocks * MOE_ROWS * TILE_ROWS, LANES), jnp.uint32),
        compiler_params=_cparams(("arbitrary",)),
        name="moe_dispatch",
    )(last_blk, pos_blk, h3_tiles)


def _expert_kernel(be_ref, na_ref, x_ref, wg_ref, wu_ref, wd_ref, o_ref, wg_scr, wu_scr, wd_scr):
    j = pl.program_id(0)
    prev = be_ref[jnp.maximum(j - 1, 0)]

    @pl.when((j == 0) | (be_ref[j] != prev))
    def _():
        wg_scr[...] = wg_ref[0].astype(BF16)
        wu_scr[...] = wu_ref[0].astype(BF16)
        wd_scr[...] = wd_ref[0].astype(BF16)

    @pl.when(j < na_ref[0])
    def _():
        x = _load_token_tiles(x_ref, 0, MOE_ROWS).astype(BF16)
        g = jnp.dot(x, wg_scr[...], preferred_element_type=F32)
        u = jnp.dot(x, wu_scr[...], preferred_element_type=F32)
        hid = (_silu(g) * u).astype(BF16)
        _store_token_tiles(o_ref, jnp.dot(hid, wd_scr[...], preferred_element_type=F32))

    @pl.when(j >= na_ref[0])
    def _():
        o_ref[...] = jnp.zeros_like(o_ref)


def _experts(blk_exp, n_active, xs_tiles, wg, wu, wd):
    n_blocks = blk_exp.shape[0]
    d, ff = wg.shape[1], wg.shape[2]
    blk = (MOE_ROWS * TILE_ROWS, LANES)
    grid_spec = pltpu.PrefetchScalarGridSpec(
        num_scalar_prefetch=2,
        grid=(n_blocks,),
        in_specs=[
            pl.BlockSpec(blk, lambda j, be, na: (jnp.minimum(j, na[0] - 1), 0)),
            pl.BlockSpec((1, d, ff), lambda j, be, na: (be[j], 0, 0)),
            pl.BlockSpec((1, d, ff), lambda j, be, na: (be[j], 0, 0)),
            pl.BlockSpec((1, ff, d), lambda j, be, na: (be[j], 0, 0)),
        ],
        out_specs=pl.BlockSpec(blk, lambda j, be, na: (j, 0)),
        scratch_shapes=[pltpu.VMEM((d, ff), BF16), pltpu.VMEM((d, ff), BF16), pltpu.VMEM((ff, d), BF16)],
    )
    return pl.pallas_call(
        _expert_kernel,
        grid_spec=grid_spec,
        out_shape=jax.ShapeDtypeStruct(xs_tiles.shape, jnp.uint32),
        compiler_params=_cparams(("arbitrary",)),
        name="moe_experts",
    )(blk_exp, n_active, xs_tiles, wg, wu, wd)


COMBINE_ROWS = 32


def _combine_kernel(n_first, pos_ref, pos_next_ref, x_ref, gate_ref, nw_ref, ys_hbm, oa_ref, ob_ref, y_scr, sem):
    i = pl.program_id(0)
    tm = x_ref.shape[0]
    slot = i % 2

    def tile(ref, tok):
        return ref.at[pl.ds(pl.multiple_of(tok * TILE_ROWS, TILE_ROWS), TILE_ROWS)]

    def gather(p_ref, to):
        buf = y_scr.at[to]

        def body(r, carry):
            pltpu.make_async_copy(tile(ys_hbm, p_ref[0, 0, r]), tile(buf, r), sem.at[to]).start(priority=0)
            pltpu.make_async_copy(tile(ys_hbm, p_ref[0, 0, tm + r]), tile(buf, tm + r), sem.at[to]).start(priority=1)
            return carry

        lax.fori_loop(0, tm, body, 0, unroll=8)

    @pl.when(i == 0)
    def _():
        gather(pos_ref, 0)

    @pl.when(i + 1 < pl.num_programs(0))
    def _():
        gather(pos_next_ref, 1 - slot)

    buf = y_scr.at[slot]
    pltpu.make_async_copy(ys_hbm.at[pl.ds(0, 2 * tm * TILE_ROWS)], buf, sem.at[slot]).wait()
    def finish(o_ref):
        def rows(rb, carry):
            r0 = pl.multiple_of(rb * COMBINE_ROWS, COMBINE_ROWS)
            sl = pl.ds(r0, COMBINE_ROWS)
            gate = gate_ref[sl, :]
            y = x_ref[sl, :] + (gate[:, 0:1] * _load_token_tiles(buf, r0, COMBINE_ROWS)
                                + gate[:, 1:2] * _load_token_tiles(buf, tm + r0, COMBINE_ROWS))
            o_ref[sl, :] = _rms(y, nw_ref[...])
            return carry

        lax.fori_loop(0, tm // COMBINE_ROWS, rows, 0, unroll=4)

    @pl.when(i < n_first)
    def _():
        finish(oa_ref)

    @pl.when(i >= n_first)
    def _():
        finish(ob_ref)


def _combine(pos_blk, x2, gates, nw, ys, tm, t_first):
    t, d = x2.shape
    na = t_first // tm
    n = t // tm
    return pl.pallas_call(
        functools.partial(_combine_kernel, na),
        grid=(n,),
        in_specs=[
            pl.BlockSpec((1, 1, 2 * tm), lambda i: (i, 0, 0), memory_space=pltpu.SMEM),
            pl.BlockSpec((1, 1, 2 * tm), lambda i: (jnp.minimum(i + 1, n - 1), 0, 0), memory_space=pltpu.SMEM),
            pl.BlockSpec((tm, d), lambda i: (i, 0)),
            pl.BlockSpec((tm, LANES), lambda i: (i, 0)),
            pl.BlockSpec((1, d), lambda i: (0, 0)),
            pl.BlockSpec(memory_space=pl.ANY),
        ],
        out_specs=[
            pl.BlockSpec((tm, d), lambda i: (_first_or_second(i, na)[0], 0)),
            pl.BlockSpec((tm, d), lambda i: (_first_or_second(i, na)[1], 0)),
        ],
        out_shape=[jax.ShapeDtypeStruct((t_first, d), F32), jax.ShapeDtypeStruct((t - t_first, d), F32)],
        scratch_shapes=[pltpu.VMEM((2, 2 * tm * TILE_ROWS, LANES), jnp.uint32), pltpu.SemaphoreType.DMA((2,))],
        compiler_params=_cparams(("arbitrary",)),
        name="moe_combine",
    )(pos_blk, pos_blk, x2, gates, nw, ys)


def _block_experts(counts, n_blocks):
    per_expert = (counts + MOE_ROWS - 1) // MOE_ROWS
    ends = jnp.cumsum(per_expert)
    blk = jnp.arange(n_blocks, dtype=jnp.int32)
    blk_exp = jnp.minimum(jnp.sum(blk[:, None] >= ends[None, :], axis=1), N_EXPERTS - 1).astype(jnp.int32)
    last_blk = jnp.where(per_expert > 0, ends - 1, -1)
    tail = n_blocks - 1 - jnp.arange(N_EXPERTS)
    to_zero = jnp.concatenate([last_blk, jnp.where(tail >= ends[-1], tail, -1)]).astype(jnp.int32)
    return blk_exp, ends[-1:].astype(jnp.int32), to_zero


def _pick(n, pref):
    t = pref
    while n % t:
        t //= 2
    return t


def _encoder(xa, xb, mem, p):
    (nba, s, d), nbb = xa.shape, xb.shape[0]
    nb = nba + nbb
    t = nb * s
    t_first = nba * s
    nc = s // SSD_CHUNK
    tm = _pick(math.gcd(t_first, t - t_first), 1024)
    tq = _pick(s, 512)
    xa = xa.reshape(t_first, d)
    xb = xb.reshape(t - t_first, d)

    perm = jnp.arange(SSD_HEADS).reshape(SSD_GROUPS, HEADS_PER_GROUP)
    perm = jnp.concatenate([perm, perm + SSD_HEADS], axis=1).reshape(-1)

    w_in = p['w_in']
    o_z = 2 * D_MODEL
    o_x = o_z + SSD_D_INNER
    o_dt = o_x + SSD_D_INNER + 2 * SSD_GROUPS * SSD_STATE
    o_g = o_dt + 2 * SSD_HEADS
    w_val = w_in[:, :D_MODEL].astype(BF16)
    w_gate = w_in[:, D_MODEL:o_z].astype(BF16)
    w_z = w_in[:, o_z:o_x].astype(BF16)
    w_xbc = w_in[:, o_x:o_dt].astype(BF16)
    w_dt = jnp.pad(w_in[:, o_dt:o_g][:, perm], ((0, 0), (0, LANES - 2 * SSD_HEADS))).astype(BF16)
    w_g = w_in[:, o_g:].astype(BF16)
    nmix = p['norm_mix_w'].reshape(1, d)

    h = _prenorm(xa, xb, nmix, tm)
    u = _glu_proj(h, w_val, w_gate, tm, PROJ_TN // 2)
    xbc = _proj(h, w_xbc, lambda a: a, BF16, tm, PROJ_TN, "xbc_proj")
    dt_raw = _proj(h, w_dt, lambda a: a, F32, tm, LANES, "dt_proj")
    gates = _proj(h, w_g, _sigmoid, BF16, tm, PROJ_TN, "gate_proj")

    pad = LANES - 2 * SSD_HEADS
    dt_bias = jnp.pad(jnp.concatenate([p['ssd_dt_bias_f'], p['ssd_dt_bias_b']])[perm], (0, pad)).reshape(1, LANES)
    a_log = jnp.pad(jnp.concatenate([p['ssd_a_log_f'], p['ssd_a_log_b']])[perm], (0, pad)).reshape(1, LANES)
    acs2, rowt, e_hi, e_lo, w_hi, w_lo, dec = _dtprep(dt_raw, dt_bias, a_log, tm)

    def per_group_tm(a):
        return a[:, :2 * SSD_HEADS].reshape(nb, s, SSD_GROUPS, 8).transpose(0, 2, 1, 3)

    def per_group_hm(a):
        return a[:, :2 * SSD_HEADS].reshape(nb, nc, SSD_CHUNK, SSD_GROUPS, 8).transpose(0, 3, 1, 4, 2)

    acs_tm = per_group_tm(acs2).reshape(nb * SSD_GROUPS, s, 8)
    ew_tm = jnp.concatenate([per_group_tm(v) for v in (e_hi, e_lo, w_hi, w_lo)], axis=-1)
    ew_tm = ew_tm.reshape(nb * SSD_GROUPS, s, EW_COLS)
    rowt_hm = per_group_hm(rowt)
    dec = dec[:, :2 * SSD_HEADS].reshape(nb, nc, SSD_GROUPS, 2, HEADS_PER_GROUP).transpose(0, 2, 1, 3, 4)
    dec = jnp.repeat(dec, SSD_HEAD_DIM, axis=-1)

    cwx = p['ssd_conv_w']
    cbx = p['ssd_conv_b']

    def per_group_conv(a):
        xs_ = a[:, :SSD_D_INNER].reshape(-1, SSD_GROUPS, GROUP_CH)
        b_ = a[:, SSD_D_INNER:SSD_D_INNER + SSD_GROUPS * SSD_STATE].reshape(-1, SSD_GROUPS, SSD_STATE)
        c_ = a[:, SSD_D_INNER + SSD_GROUPS * SSD_STATE:].reshape(-1, SSD_GROUPS, SSD_STATE)
        return jnp.concatenate([xs_, b_, c_], axis=-1).transpose(1, 0, 2)

    cw = jnp.pad(per_group_conv(cwx), ((0, 0), (0, 8 - SSD_CONV_WIDTH), (0, 0)))
    cb = per_group_conv(cbx.reshape(1, -1))
    dskip = jnp.repeat(p['ssd_d'], SSD_HEAD_DIM).reshape(SSD_GROUPS, 1, GROUP_CH)
    nw_ssd = p['ssd_norm_w'].reshape(SSD_GROUPS, 1, GROUP_CH)

    yn = _ssd(xbc.reshape(nb, s, -1), h.reshape(nb, s, d), w_z, cw, cb, rowt_hm, acs_tm, ew_tm, dec, dskip, nw_ssd)
    mb = _gated_proj(yn.reshape(t, SSD_D_INNER), p['ssd_out'].astype(BF16), gates, 1, tm)

    dww = jnp.pad(p['conv_dw_w'], ((0, 32 - CONV_WIDTH), (0, 0)))
    x1 = _conformer(u.reshape(nb, s, d), mb.reshape(nb, s, d), gates.reshape(nb, s, 2 * d), xa, xb, dww,
                    p['conv_dw_b'].reshape(1, d), p['conv_ln_w'].reshape(1, d), p['conv_ln_b'].reshape(1, d),
                    p['conv_pw_out'].astype(BF16), p['w_out'].astype(BF16), tq)

    m = mem.shape[1]
    kv = _norm_proj(mem.reshape(nb * m, d), p['norm_mem_w'].reshape(1, d), p['xattn_wkv'].astype(BF16),
                    lambda a: a, BF16, _pick(nb * m, 512), 1024, "kv_proj")
    n_r = N_GROUPS + N_EXPERTS
    wr = jnp.pad(jnp.concatenate([p['router_group_w'], p['router_expert_w']], axis=1),
                 ((0, 0), (0, LANES - n_r))).astype(BF16)
    br = jnp.pad(jnp.concatenate([p['router_group_b'], p['router_expert_b']]), (0, LANES - n_r)).reshape(1, LANES)
    x2, h3, logits = _attention(x1, kv.reshape(nb, m, 2 * d), p['norm_xattn_w'].reshape(1, d),
                                p['xattn_wq'].astype(BF16), p['xattn_wo'].astype(BF16),
                                p['norm_ffn_w'].reshape(1, d), wr, br, tq)

    gate, pos, cnt = _plan(logits.reshape(t, LANES), tm)
    n_blocks = (2 * t) // MOE_ROWS + N_EXPERTS
    blk_exp, n_active, last_blk = _block_experts(cnt[0, :N_EXPERTS].astype(jnp.int32), n_blocks)
    pos_blk = pos[:, :2].reshape(t // tm, tm, 2).transpose(0, 2, 1).reshape(t // tm, 1, 2 * tm)
    xs = _dispatch(last_blk, pos_blk, h3, n_blocks)
    ys = _experts(blk_exp, n_active, xs, p['expert_w_gate'], p['expert_w_up'], p['expert_w_down'])
    return _combine(pos_blk, x2.reshape(t, d), gate, p['norm_final_w'].reshape(1, d), ys, tm, t_first)


def kernel(x_prompt, x_sample, mem_prompt, mem_sample, norm_mix_w, w_in, conv_dw_w, conv_dw_b, conv_ln_w, conv_ln_b, conv_pw_out, ssd_conv_w, ssd_conv_b, ssd_dt_bias_f, ssd_dt_bias_b, ssd_a_log_f, ssd_a_log_b, ssd_d, ssd_norm_w, ssd_out, w_out, norm_xattn_w, norm_mem_w, xattn_wq, xattn_wkv, xattn_wo, norm_ffn_w, router_group_w, router_group_b, router_expert_w, router_expert_b, expert_w_gate, expert_w_up, expert_w_down, norm_final_w):
    p = {
        'norm_mix_w': norm_mix_w[0], 'w_in': w_in[0], 'conv_dw_w': conv_dw_w[0], 'conv_dw_b': conv_dw_b[0],
        'conv_ln_w': conv_ln_w[0], 'conv_ln_b': conv_ln_b[0], 'conv_pw_out': conv_pw_out[0],
        'ssd_conv_w': ssd_conv_w[0], 'ssd_conv_b': ssd_conv_b[0], 'ssd_dt_bias_f': ssd_dt_bias_f[0],
        'ssd_dt_bias_b': ssd_dt_bias_b[0], 'ssd_a_log_f': ssd_a_log_f[0], 'ssd_a_log_b': ssd_a_log_b[0],
        'ssd_d': ssd_d[0], 'ssd_norm_w': ssd_norm_w[0], 'ssd_out': ssd_out[0], 'w_out': w_out[0],
        'norm_xattn_w': norm_xattn_w[0], 'norm_mem_w': norm_mem_w[0], 'xattn_wq': xattn_wq[0],
        'xattn_wkv': xattn_wkv[0], 'xattn_wo': xattn_wo[0], 'norm_ffn_w': norm_ffn_w[0],
        'router_group_w': router_group_w[0], 'router_group_b': router_group_b[0],
        'router_expert_w': router_expert_w[0], 'router_expert_b': router_expert_b[0],
        'expert_w_gate': expert_w_gate[0], 'expert_w_up': expert_w_up[0], 'expert_w_down': expert_w_down[0],
        'norm_final_w': norm_final_w,
    }
    mem = jnp.concatenate([mem_prompt, mem_sample], axis=0)
    y_prompt, y_sample = _encoder(x_prompt, x_sample, mem, p)
    return (y_prompt.reshape(x_prompt.shape), y_sample.reshape(x_sample.shape))
```

```python
import functools
import math

import jax
import jax.numpy as jnp
from jax import lax
from jax.experimental import pallas as pl
from jax.experimental.pallas import tpu as pltpu

F32 = jnp.float32
BF16 = jnp.bfloat16

D_MODEL = 1024
CONV_WIDTH = 31
SSD_D_INNER = 2048
SSD_HEAD_DIM = 64
SSD_HEADS = 32
SSD_GROUPS = 8
SSD_STATE = 128
SSD_CONV_WIDTH = 5
SSD_CHUNK = 128
GROUP_CH = SSD_D_INNER // SSD_GROUPS
HEADS_PER_GROUP = SSD_HEADS // SSD_GROUPS
XATTN_HEADS = 4
XATTN_HEAD_DIM = D_MODEL // XATTN_HEADS
N_GROUPS = 8
EXPERTS_PER_GROUP = 8
N_EXPERTS = 64
EXPERT_FF = 512
MOE_BLOCK = 128
RMS_EPS = 1e-6
LN_EPS = 1e-5

LANES = 128
HALO = 16
VMEM_LIMIT = 56 * 1024 * 1024
PROJ_TN = 2048


def _cparams(sem):
    return pltpu.CompilerParams(dimension_semantics=sem, vmem_limit_bytes=VMEM_LIMIT)


def _rms(x, w):
    ms = jnp.mean(x * x, axis=-1, keepdims=True)
    return x * lax.rsqrt(ms + RMS_EPS) * w


def _sigmoid(x):
    return 1.0 / (1.0 + jnp.exp2(x * -1.4426950408889634))


def _silu(x):
    return x * _sigmoid(x)


def _softplus(x):
    return jnp.maximum(x, 0.0) + jnp.log1p(jnp.exp(-jnp.abs(x)))


CONV_STRIDE = 4
CONV_ROWS = 8 * CONV_STRIDE
CONV_UNROLL = 4


def _depthwise_conv(src, dst, slab, w_ref, b_ref, n_rows, width, first_row, epilogue):
    lanes = pl.ds(slab * LANES, LANES)
    taps = [jnp.broadcast_to(w_ref[k:k + 1, lanes], (8, LANES)) for k in range(width)]
    bias = jnp.broadcast_to(b_ref[:, lanes], (8, LANES))
    step = CONV_ROWS * CONV_UNROLL

    def body(r, carry):
        _conv_rows(src, dst, slab, taps, bias, r * step, step, width, first_row, epilogue)
        return carry

    lax.fori_loop(0, n_rows // step, body, 0)


def _conv_rows(src, dst, slab, taps, bias, base, n_rows, width, first_row, epilogue):
    for t0 in range(n_rows // 8):
        row = base + (t0 // CONV_STRIDE) * CONV_ROWS + t0 % CONV_STRIDE
        acc = bias
        for k in range(width):
            acc = acc + src[slab, pl.ds(row + first_row - width // 2 + k, 8, stride=CONV_STRIDE), :] * taps[k]
        dst[slab, pl.ds(row, 8, stride=CONV_STRIDE), :] = epilogue(acc)


def _first_or_second(i, n_first):
    return jnp.minimum(i, n_first - 1), jnp.maximum(i - n_first, 0)


def _prenorm_kernel(n_first, xa_ref, xb_ref, nw_ref, o_ref):
    x = jnp.where(pl.program_id(0) < n_first, xa_ref[...], xb_ref[...])
    o_ref[...] = _rms(x, nw_ref[...]).astype(o_ref.dtype)


def _prenorm(xa, xb, nw, tm):
    (ta, d), tb = xa.shape, xb.shape[0]
    na = ta // tm
    return pl.pallas_call(
        functools.partial(_prenorm_kernel, na),
        grid=((ta + tb) // tm,),
        in_specs=[
            pl.BlockSpec((tm, d), lambda i: (_first_or_second(i, na)[0], 0)),
            pl.BlockSpec((tm, d), lambda i: (_first_or_second(i, na)[1], 0)),
            pl.BlockSpec((1, d), lambda i: (0, 0)),
        ],
        out_specs=pl.BlockSpec((tm, d), lambda i: (i, 0)),
        out_shape=jax.ShapeDtypeStruct((ta + tb, d), BF16),
        compiler_params=_cparams(("arbitrary",)),
        name="mix_norm",
    )(xa, xb, nw)


def _glu_kernel(h_ref, wv_ref, wg_ref, o_ref):
    h = h_ref[...]
    v = jnp.dot(h, wv_ref[...], preferred_element_type=F32)
    g = jnp.dot(h, wg_ref[...], preferred_element_type=F32)
    o_ref[...] = (v * _sigmoid(g)).astype(o_ref.dtype)


def _glu_proj(h, wv, wg, tm, tn):
    t, d = h.shape
    n = wv.shape[1]
    return pl.pallas_call(
        _glu_kernel,
        grid=(t // tm, n // tn),
        in_specs=[
            pl.BlockSpec((tm, d), lambda i, j: (i, 0)),
            pl.BlockSpec((d, tn), lambda i, j: (0, j)),
            pl.BlockSpec((d, tn), lambda i, j: (0, j)),
        ],
        out_specs=pl.BlockSpec((tm, tn), lambda i, j: (i, j)),
        out_shape=jax.ShapeDtypeStruct((t, n), BF16),
        compiler_params=_cparams(("parallel", "arbitrary")),
        name="glu_proj",
    )(h, wv, wg)


def _proj_kernel(epi, h_ref, w_ref, o_ref):
    o_ref[...] = epi(jnp.dot(h_ref[...], w_ref[...], preferred_element_type=F32)).astype(o_ref.dtype)


def _proj(h, w, epi, out_dtype, tm, tn, name):
    t, d = h.shape
    n = w.shape[1]
    return pl.pallas_call(
        functools.partial(_proj_kernel, epi),
        grid=(t // tm, n // tn),
        in_specs=[
            pl.BlockSpec((tm, d), lambda i, j: (i, 0)),
            pl.BlockSpec((d, tn), lambda i, j: (0, j)),
        ],
        out_specs=pl.BlockSpec((tm, tn), lambda i, j: (i, j)),
        out_shape=jax.ShapeDtypeStruct((t, n), out_dtype),
        compiler_params=_cparams(("parallel", "arbitrary")),
        name=name,
    )(h, w)


def _norm_proj_kernel(epi, x_ref, nw_ref, w_ref, o_ref, h_scr):
    @pl.when(pl.program_id(1) == 0)
    def _():
        h_scr[...] = _rms(x_ref[...], nw_ref[...]).astype(BF16)

    acc = jnp.dot(h_scr[...], w_ref[...], preferred_element_type=F32)
    o_ref[...] = epi(acc).astype(o_ref.dtype)


def _norm_proj(x, nw, w, epi, out_dtype, tm, tn, name):
    t, d = x.shape
    n = w.shape[1]
    return pl.pallas_call(
        functools.partial(_norm_proj_kernel, epi),
        grid=(t // tm, n // tn),
        in_specs=[
            pl.BlockSpec((tm, d), lambda i, j: (i, 0)),
            pl.BlockSpec((1, d), lambda i, j: (0, 0)),
            pl.BlockSpec((d, tn), lambda i, j: (0, j)),
        ],
        out_specs=pl.BlockSpec((tm, tn), lambda i, j: (i, j)),
        out_shape=jax.ShapeDtypeStruct((t, n), out_dtype),
        scratch_shapes=[pltpu.VMEM((tm, d), BF16)],
        compiler_params=_cparams(("parallel", "arbitrary")),
        name=name,
    )(x, nw, w)


def _gated_proj_kernel(y_ref, w_ref, g_ref, o_ref):
    acc = jnp.dot(y_ref[...], w_ref[...], preferred_element_type=F32)
    o_ref[...] = (acc * g_ref[...].astype(F32)).astype(o_ref.dtype)


def _gated_proj(y, w, g, g_col_block, tm):
    t, k = y.shape
    n = w.shape[1]
    return pl.pallas_call(
        _gated_proj_kernel,
        grid=(t // tm,),
        in_specs=[
            pl.BlockSpec((tm, k), lambda i: (i, 0)),
            pl.BlockSpec((k, n), lambda i: (0, 0)),
            pl.BlockSpec((tm, n), lambda i: (i, g_col_block)),
        ],
        out_specs=pl.BlockSpec((tm, n), lambda i: (i, 0)),
        out_shape=jax.ShapeDtypeStruct((t, n), BF16),
        compiler_params=_cparams(("parallel",)),
        name="ssd_out_proj",
    )(y, w, g)


LOG2E = 1.4426950408889634


def _split_bf16(v):
    hi = v.astype(BF16)
    return hi, (v - hi.astype(F32)).astype(BF16)


def _dtprep_kernel(raw_ref, bias_ref, alog_ref, acs2_ref, rowt_ref, ehi_ref, elo_ref, whi_ref, wlo_ref, dec_ref):
    rows = raw_ref.shape[0]
    a_head = -jnp.exp(alog_ref[...])
    row = lax.broadcasted_iota(jnp.int32, (SSD_CHUNK, LANES), 0)
    lane = lax.broadcasted_iota(jnp.int32, (SSD_CHUNK, LANES), 1)
    is_bwd = (lane % (2 * HEADS_PER_GROUP)) >= HEADS_PER_GROUP
    for c in range(rows // SSD_CHUNK):
        sl = pl.ds(c * SSD_CHUNK, SSD_CHUNK)
        dt = _softplus(raw_ref[sl, :] + bias_ref[...])
        a = dt * a_head
        fwd = a
        bwd = a
        sh = 1
        while sh < SSD_CHUNK:
            fwd = fwd + jnp.where(row >= sh, pltpu.roll(fwd, sh, 0), 0.0)
            bwd = bwd + jnp.where(row < SSD_CHUNK - sh, pltpu.roll(bwd, SSD_CHUNK - sh, 0), 0.0)
            sh *= 2
        acs = jnp.where(is_bwd, bwd, fwd)
        a_end = jnp.where(is_bwd[0:1, :], bwd[0:1, :], fwd[SSD_CHUNK - 1:SSD_CHUNK, :])
        acs2 = acs * LOG2E
        acs2_ref[sl, :] = acs2
        rowt_ref[sl, :] = acs2 - jnp.log(dt) * LOG2E
        ehi_ref[sl, :], elo_ref[sl, :] = _split_bf16(jnp.exp(acs))
        whi_ref[sl, :], wlo_ref[sl, :] = _split_bf16(jnp.exp(a_end - acs) * dt)
        dec_ref[c:c + 1, :] = jnp.exp(a_end)


def _dtprep(raw, bias, alog, tm):
    t = raw.shape[0]
    spec = pl.BlockSpec((tm, LANES), lambda i: (i, 0))
    vec = pl.BlockSpec((1, LANES), lambda i: (0, 0))
    return pl.pallas_call(
        _dtprep_kernel,
        grid=(t // tm,),
        in_specs=[spec, vec, vec],
        out_specs=[spec] * 6 + [pl.BlockSpec((tm // SSD_CHUNK, LANES), lambda i: (i, 0))],
        out_shape=[jax.ShapeDtypeStruct((t, LANES), F32)] * 2 + [jax.ShapeDtypeStruct((t, LANES), BF16)] * 4
        + [jax.ShapeDtypeStruct((t // SSD_CHUNK, LANES), F32)],
        compiler_params=_cparams(("parallel",)),
        name="dt_prep",
    )(raw, bias, alog)


SSD_PAD = 8
XBC_W = GROUP_CH + 2 * SSD_STATE


EW_COLS = 4 * 2 * HEADS_PER_GROUP


def _head_expand_matrix():
    col = jnp.arange(4 * GROUP_CH)
    part = col // GROUP_CH
    is_w = (part == 1) | (part == 2)
    is_bwd = part >= 2
    head = is_bwd * HEADS_PER_GROUP + (col % GROUP_CH) // SSD_HEAD_DIM
    hi_row = is_w * 16 + head
    row = jnp.arange(EW_COLS)[:, None]
    return ((row == hi_row[None, :]) | (row == hi_row[None, :] + 8)).astype(BF16)


def _ssd_kernel(xs_ref, b_ref, c_ref, h_ref, wz_ref, cw_ref, cb_ref, rowt_ref, acs_ref, ew_ref, sel_ref, dec_ref,
                dskip_ref, nw_ref, o_ref, raw_scr, act_scr, y_scr, upd_scr, st_scr):
    s = xs_ref.shape[1]
    nc = s // SSD_CHUNK
    L = SSD_CHUNK

    n_slabs = XBC_W // LANES
    for j in range(n_slabs):
        raw_scr[j, pl.ds(0, SSD_PAD), :] = jnp.zeros((SSD_PAD, LANES), F32)
        raw_scr[j, pl.ds(SSD_PAD + s, L + SSD_PAD), :] = jnp.zeros((L + SSD_PAD, LANES), F32)
    raw_scr[0, pl.ds(SSD_PAD, s), :] = xs_ref[0, :, :LANES].astype(F32)
    raw_scr[1, pl.ds(SSD_PAD, s), :] = xs_ref[0, :, LANES:].astype(F32)
    raw_scr[2, pl.ds(SSD_PAD, s), :] = b_ref[0].astype(F32)
    raw_scr[3, pl.ds(SSD_PAD, s), :] = c_ref[0].astype(F32)

    def conv_chunk(c):
        for j in range(n_slabs):
            lanes = pl.ds(j * LANES, LANES)
            taps = [jnp.broadcast_to(cw_ref[0, k:k + 1, lanes], (8, LANES)) for k in range(SSD_CONV_WIDTH)]
            bias = jnp.broadcast_to(cb_ref[0, :, lanes], (8, LANES))
            _conv_rows(raw_scr, act_scr, j, taps, bias, c * L, L, SSD_CONV_WIDTH, SSD_PAD, _silu)

    conv_chunk(0)

    li = lax.broadcasted_iota(jnp.int32, (L, L), 0)
    si = lax.broadcasted_iota(jnp.int32, (L, L), 1)
    head_of_ch = lax.broadcasted_iota(jnp.int32, (L, GROUP_CH), 1) // SSD_HEAD_DIM

    keep = (li >= si, si >= li)
    neg_inf = jnp.float32(-jnp.inf)
    n_heads = 2 * HEADS_PER_GROUP
    st_scr[...] = jnp.zeros_like(st_scr)

    def local_body(c, carry):
        r0 = pl.multiple_of(c * L, L)
        x = jnp.concatenate([act_scr[0, pl.ds(r0, L), :], act_scr[1, pl.ds(r0, L), :]], axis=1)
        xb = x.astype(BF16)
        bk = act_scr[2, pl.ds(r0, L), :].astype(BF16)
        ck = act_scr[3, pl.ds(r0, L), :].astype(BF16)
        acs = acs_ref[0, pl.ds(r0, L), :]
        rowt = rowt_ref[0, 0, c]
        ew = jnp.dot(ew_ref[0, pl.ds(r0, L), :], sel_ref[:, :3 * GROUP_CH],
                     preferred_element_type=F32)

        cb = lax.dot_general(ck, bk, (((1,), (1,)), ((), ())), preferred_element_type=F32)
        ms = []
        for j in range(n_heads):
            diff = acs[:, j:j + 1] - rowt[j:j + 1, :]
            decay_dt = jnp.exp2(jnp.where(keep[j // HEADS_PER_GROUP], diff, neg_inf))
            ms.append((cb * decay_dt).astype(BF16))
        xbd = jnp.concatenate([jnp.where(head_of_ch == h, xb, jnp.zeros_like(xb))
                               for h in range(HEADS_PER_GROUP)], axis=0)
        y = jnp.dot(jnp.concatenate(ms, axis=1), jnp.concatenate([xbd, xbd], axis=0),
                    preferred_element_type=F32) + x * dskip_ref[0]

        xw = jnp.concatenate([x * ew[:, GROUP_CH:2 * GROUP_CH], x * ew[:, 2 * GROUP_CH:]], axis=1)
        upd = lax.dot_general(bk, xw.astype(BF16), (((0,), (0,)), ((), ())), preferred_element_type=F32)

        state = st_scr[0]
        y = y + jnp.dot(ck, state.astype(BF16), preferred_element_type=F32) * ew[:, :GROUP_CH]
        st_scr[0] = state * dec_ref[0, 0, c, 0:1, :] + upd[:, :GROUP_CH]
        y_scr[pl.ds(r0, L), :] = y
        upd_scr[c] = upd[:, GROUP_CH:]
        conv_chunk(c + 1)
        return carry

    lax.fori_loop(0, nc, local_body, 0, unroll=4)

    def bwd_body(i, carry):
        c = nc - 1 - i
        r0 = pl.multiple_of(c * L, L)
        ck = act_scr[3, pl.ds(r0, L), :].astype(BF16)
        e_bwd = jnp.dot(ew_ref[0, pl.ds(r0, L), :], sel_ref[:, 3 * GROUP_CH:], preferred_element_type=F32)
        state = st_scr[1]
        y = y_scr[pl.ds(r0, L), :] + jnp.dot(ck, state.astype(BF16), preferred_element_type=F32) * e_bwd
        st_scr[1] = state * dec_ref[0, 0, c, 1:2, :] + upd_scr[c]
        z = jnp.dot(h_ref[0, pl.ds(r0, L), :], wz_ref[...], preferred_element_type=F32)
        yz = y * _silu(z)
        ms = jnp.mean(yz * yz, axis=-1, keepdims=True)
        o_ref[0, pl.ds(r0, L), :] = (yz * lax.rsqrt(ms + RMS_EPS) * nw_ref[0]).astype(o_ref.dtype)
        return carry

    lax.fori_loop(0, nc, bwd_body, 0, unroll=8)


def _ssd(xbc, h, wz, cw, cb, rowt_hm, acs_tm, ew_tm, dec, dskip, nw):
    nb, s, _ = xbc.shape
    d = h.shape[2]
    nc = s // SSD_CHUNK
    return pl.pallas_call(
        _ssd_kernel,
        grid=(nb, SSD_GROUPS),
        in_specs=[
            pl.BlockSpec((1, s, GROUP_CH), lambda b, g: (b, 0, g)),
            pl.BlockSpec((1, s, SSD_STATE), lambda b, g: (b, 0, SSD_D_INNER // SSD_STATE + g)),
            pl.BlockSpec((1, s, SSD_STATE), lambda b, g: (b, 0, SSD_D_INNER // SSD_STATE + SSD_GROUPS + g)),
            pl.BlockSpec((1, s, d), lambda b, g: (b, 0, 0)),
            pl.BlockSpec((d, GROUP_CH), lambda b, g: (0, g)),
            pl.BlockSpec((1, 8, XBC_W), lambda b, g: (g, 0, 0)),
            pl.BlockSpec((1, 1, XBC_W), lambda b, g: (g, 0, 0)),
            pl.BlockSpec((1, 1, nc, 8, SSD_CHUNK), lambda b, g: (b, g, 0, 0, 0)),
            pl.BlockSpec((1, s, 8), lambda b, g: (b * SSD_GROUPS + g, 0, 0)),
            pl.BlockSpec((1, s, EW_COLS), lambda b, g: (b * SSD_GROUPS + g, 0, 0)),
            pl.BlockSpec((EW_COLS, 4 * GROUP_CH), lambda b, g: (0, 0)),
            pl.BlockSpec((1, 1, nc, 2, GROUP_CH), lambda b, g: (b, g, 0, 0, 0)),
            pl.BlockSpec((1, 1, GROUP_CH), lambda b, g: (g, 0, 0)),
            pl.BlockSpec((1, 1, GROUP_CH), lambda b, g: (g, 0, 0)),
        ],
        out_specs=pl.BlockSpec((1, s, GROUP_CH), lambda b, g: (b, 0, g)),
        out_shape=jax.ShapeDtypeStruct((nb, s, SSD_D_INNER), BF16),
        scratch_shapes=[
            pltpu.VMEM((XBC_W // LANES, s + SSD_CHUNK + 2 * SSD_PAD, LANES), F32),
            pltpu.VMEM((XBC_W // LANES, s + SSD_CHUNK, LANES), F32),
            pltpu.VMEM((s, GROUP_CH), F32),
            pltpu.VMEM((nc, SSD_STATE, GROUP_CH), F32),
            pltpu.VMEM((2, SSD_STATE, GROUP_CH), F32),
        ],
        compiler_params=_cparams(("parallel", "parallel")),
        name="ssd_scan",
    )(xbc, xbc, xbc, h, wz, cw, cb, rowt_hm, acs_tm, ew_tm, _head_expand_matrix(), dec, dskip, nw)


D_SLABS = D_MODEL // LANES


def _conformer_kernel(n_first, u_ref, up_ref, un_ref, mb_ref, g_ref, xa_ref, xb_ref, dww_ref, dwb_ref, lnw_ref,
                      lnb_ref, pw_ref, wo_ref, o_ref, u_scr, c_scr):
    i = pl.program_id(1)
    n = pl.num_programs(1)
    tq = u_ref.shape[1]
    x = jnp.where(pl.program_id(0) * n + i < n_first, xa_ref[...], xb_ref[...])
    for j in range(D_SLABS):
        cs = slice(j * LANES, (j + 1) * LANES)
        u_scr[j, pl.ds(0, HALO), :] = jnp.where(i > 0, up_ref[0, :, cs].astype(F32), 0.0)
        u_scr[j, pl.ds(HALO, tq), :] = u_ref[0, :, cs].astype(F32)
        u_scr[j, pl.ds(HALO + tq, HALO), :] = jnp.where(i < n - 1, un_ref[0, :, cs].astype(F32), 0.0)
    for j in range(D_SLABS):
        _depthwise_conv(u_scr, c_scr, j, dww_ref, dwb_ref, tq, CONV_WIDTH, HALO, lambda a: a)

    cv = jnp.concatenate([c_scr[j] for j in range(D_SLABS)], axis=1)
    mu = jnp.mean(cv, axis=-1, keepdims=True)
    xc = cv - mu
    var = jnp.mean(xc * xc, axis=-1, keepdims=True)
    y = xc * lax.rsqrt(var + LN_EPS) * lnw_ref[...] + lnb_ref[...]
    a = jnp.dot(_silu(y).astype(BF16), pw_ref[...], preferred_element_type=F32)
    merged = g_ref[0].astype(F32) * a + mb_ref[0].astype(F32)
    o_ref[0] = x + jnp.dot(merged.astype(BF16), wo_ref[...], preferred_element_type=F32)


def _conformer(u, mb, gates, xa, xb, dww, dwb, lnw, lnb, pw, wo, tq):
    nb, s, d = u.shape
    hb = tq // HALO
    nh = s // HALO
    nq = s // tq
    na = xa.shape[0] // tq
    wspec = pl.BlockSpec((d, d), lambda b, i: (0, 0))
    vspec = pl.BlockSpec((1, d), lambda b, i: (0, 0))
    return pl.pallas_call(
        functools.partial(_conformer_kernel, na),
        grid=(nb, nq),
        in_specs=[
            pl.BlockSpec((1, tq, d), lambda b, i: (b, i, 0)),
            pl.BlockSpec((1, HALO, d), lambda b, i: (b, jnp.maximum(i * hb - 1, 0), 0)),
            pl.BlockSpec((1, HALO, d), lambda b, i: (b, jnp.minimum((i + 1) * hb, nh - 1), 0)),
            pl.BlockSpec((1, tq, d), lambda b, i: (b, i, 0)),
            pl.BlockSpec((1, tq, d), lambda b, i: (b, i, 0)),
            pl.BlockSpec((tq, d), lambda b, i: (_first_or_second(b * nq + i, na)[0], 0)),
            pl.BlockSpec((tq, d), lambda b, i: (_first_or_second(b * nq + i, na)[1], 0)),
            pl.BlockSpec((32, d), lambda b, i: (0, 0)),
            vspec, vspec, vspec, wspec, wspec,
        ],
        out_specs=pl.BlockSpec((1, tq, d), lambda b, i: (b, i, 0)),
        out_shape=jax.ShapeDtypeStruct((nb, s, d), F32),
        scratch_shapes=[pltpu.VMEM((D_SLABS, tq + 2 * HALO, LANES), F32), pltpu.VMEM((D_SLABS, tq, LANES), F32)],
        compiler_params=_cparams(("arbitrary", "arbitrary")),
        name="conformer_merge",
    )(u, u, u, mb, gates, xa, xb, dww, dwb, lnw, lnb, pw, wo)


TILE_ROWS = D_MODEL // (2 * LANES)
HIGH_HALF = 0xFFFF0000


def _bf16_bits(v):
    return lax.bitcast_convert_type(v.astype(BF16).astype(F32), jnp.uint32)


def _store_token_tiles(ref, val):
    n, d = val.shape
    packed = (_bf16_bits(val[:, :d // 2]) >> 16) | (_bf16_bits(val[:, d // 2:]) & jnp.uint32(HIGH_HALF))
    for j in range(TILE_ROWS):
        ref[pl.ds(j, n, stride=TILE_ROWS), :] = packed[:, j * LANES:(j + 1) * LANES]


def _load_token_tiles(ref, first, n):
    packed = jnp.concatenate([ref[pl.ds(first * TILE_ROWS + j, n, stride=TILE_ROWS), :] for j in range(TILE_ROWS)],
                             axis=1)
    low = lax.bitcast_convert_type(packed << 16, F32)
    high = lax.bitcast_convert_type(packed & jnp.uint32(HIGH_HALF), F32)
    return jnp.concatenate([low, high], axis=1)


def _attn_kernel(x_ref, k_ref, v_ref, nw_ref, wq_ref, wo_ref, nf_ref, wr_ref, br_ref, x2_ref, h3_ref, lg_ref):
    x = x_ref[0]
    h = _rms(x, nw_ref[...]).astype(BF16)
    q = jnp.dot(h, wq_ref[...], preferred_element_type=F32)
    scale = 1.0 / math.sqrt(XATTN_HEAD_DIM)
    outs = []
    for hd in range(XATTN_HEADS):
        cs = slice(hd * XATTN_HEAD_DIM, (hd + 1) * XATTN_HEAD_DIM)
        sc = lax.dot_general(q[:, cs].astype(BF16), k_ref[0, :, cs], (((1,), (1,)), ((), ())),
                             preferred_element_type=F32) * scale
        m = jnp.max(sc, axis=-1, keepdims=True)
        e = jnp.exp(sc - m)
        p = e / jnp.sum(e, axis=-1, keepdims=True)
        outs.append(jnp.dot(p.astype(BF16), v_ref[0, :, cs], preferred_element_type=F32))
    o = jnp.concatenate(outs, axis=1).astype(BF16)
    x2 = x + jnp.dot(o, wo_ref[...], preferred_element_type=F32)
    x2_ref[0] = x2
    h3 = _rms(x2, nf_ref[...])
    _store_token_tiles(h3_ref, h3)
    lg_ref[0] = jnp.dot(h3.astype(BF16), wr_ref[...], preferred_element_type=F32) + br_ref[...]


def _attention(x, kv, nw, wq, wo, nf, wr, br, tq):
    nb, s, d = x.shape
    m = kv.shape[1]
    wspec = pl.BlockSpec((d, d), lambda b, i: (0, 0))
    vspec = pl.BlockSpec((1, d), lambda b, i: (0, 0))
    return pl.pallas_call(
        _attn_kernel,
        grid=(nb, s // tq),
        in_specs=[
            pl.BlockSpec((1, tq, d), lambda b, i: (b, i, 0)),
            pl.BlockSpec((1, m, d), lambda b, i: (b, 0, 0)),
            pl.BlockSpec((1, m, d), lambda b, i: (b, 0, 1)),
            vspec, wspec, wspec, vspec,
            pl.BlockSpec((d, LANES), lambda b, i: (0, 0)),
            pl.BlockSpec((1, LANES), lambda b, i: (0, 0)),
        ],
        out_specs=[
            pl.BlockSpec((1, tq, d), lambda b, i: (b, i, 0)),
            pl.BlockSpec((tq * TILE_ROWS, LANES), lambda b, i: (b * (s // tq) + i, 0)),
            pl.BlockSpec((1, tq, LANES), lambda b, i: (b, i, 0)),
        ],
        out_shape=[
            jax.ShapeDtypeStruct((nb, s, d), F32),
            jax.ShapeDtypeStruct((nb * s * TILE_ROWS, LANES), jnp.uint32),
            jax.ShapeDtypeStruct((nb, s, LANES), F32),
        ],
        compiler_params=_cparams(("parallel", "parallel")),
        name="cross_attn_router",
    )(x, kv, kv, nw, wq, wo, nf, wr, br)


MOE_ROWS = 512


def _route_math(lg, lane):
    lane_f = lane.astype(F32)
    big = float(LANES)
    neg = -jnp.inf

    is_g = lane < N_GROUPS
    gl = jnp.where(is_g, lg, neg)
    gmax = jnp.max(gl, axis=-1, keepdims=True)
    g_idx = jnp.min(jnp.where(gl == gmax, lane_f, big), axis=-1, keepdims=True)
    g_sum = jnp.sum(jnp.where(is_g, jnp.exp(gl - gmax), 0.0), axis=-1, keepdims=True)
    g_w = 1.0 / g_sum

    e_grp = lax.shift_right_arithmetic(lane - N_GROUPS, 3).astype(F32)
    is_e = (lane >= N_GROUPS) & (lane < N_GROUPS + N_EXPERTS) & (e_grp == g_idx)
    el = jnp.where(is_e, lg, neg)
    emax = jnp.max(el, axis=-1, keepdims=True)
    ee = jnp.where(is_e, jnp.exp(el - emax), 0.0)
    ep = ee / jnp.sum(ee, axis=-1, keepdims=True)
    ep = jnp.where(is_e, ep, -1.0)
    v1 = jnp.max(ep, axis=-1, keepdims=True)
    i1 = jnp.min(jnp.where(ep == v1, lane_f, big), axis=-1, keepdims=True)
    ep2 = jnp.where(lane_f == i1, -1.0, ep)
    v2 = jnp.max(ep2, axis=-1, keepdims=True)
    i2 = jnp.min(jnp.where(ep2 == v2, lane_f, big), axis=-1, keepdims=True)
    tot = v1 + v2
    gate1 = g_w * (v1 / tot)
    gate2 = g_w * (v2 / tot)
    return i1 - N_GROUPS, i2 - N_GROUPS, gate1, gate2


def _plan_kernel(lg_ref, gate_ref, pos_ref, cnt_ref, carry_scr, base_scr):
    p = pl.program_id(0)
    i = pl.program_id(1)
    tm = lg_ref.shape[0]
    lane = lax.broadcasted_iota(jnp.int32, (tm, LANES), 1)
    lane_f = lane.astype(F32)
    e1, e2, gate1, gate2 = _route_math(lg_ref[...], lane)
    oh1 = lane_f == e1
    oh2 = lane_f == e2
    m = jnp.where(oh1 | oh2, 1.0, 0.0)
    colsum = jnp.sum(m, axis=0, keepdims=True)

    @pl.when((p == 0) & (i == 0))
    def _():
        carry_scr[...] = jnp.zeros_like(carry_scr)

    @pl.when(p == 0)
    def _():
        carry_scr[...] += colsum

    @pl.when((p == 1) & (i == 0))
    def _():
        counts = carry_scr[...]
        cnt_ref[...] = counts
        shift = MOE_ROWS.bit_length() - 1
        blocks = lax.shift_right_logical(counts.astype(jnp.int32) + (MOE_ROWS - 1), shift)
        padded = (blocks * MOE_ROWS).astype(F32)
        lane8 = lax.broadcasted_iota(jnp.int32, padded.shape, 1)
        inc = padded
        sh = 1
        while sh < LANES:
            inc = inc + jnp.where(lane8 >= sh, pltpu.roll(inc, sh, 1), 0.0)
            sh *= 2
        base_scr[...] = inc - padded
        carry_scr[...] = jnp.zeros_like(carry_scr)

    @pl.when(p == 1)
    def _():
        r = lax.broadcasted_iota(jnp.int32, (tm, tm), 0)
        c = lax.broadcasted_iota(jnp.int32, (tm, tm), 1)
        earlier = jnp.where(r > c, 1.0, 0.0).astype(BF16)
        tot = (jnp.dot(earlier, m.astype(BF16), preferred_element_type=F32)
               + base_scr[0:1, :] + carry_scr[0:1, :])
        pos1 = jnp.sum(jnp.where(oh1, tot, 0.0), axis=1, keepdims=True)
        pos2 = jnp.sum(jnp.where(oh2, tot, 0.0), axis=1, keepdims=True)
        pos_ref[...] = jnp.where(lane == 0, pos1, jnp.where(lane == 1, pos2, 0.0)).astype(jnp.int32)
        gate_ref[...] = jnp.where(lane == 0, gate1, jnp.where(lane == 1, gate2, 0.0))
        carry_scr[...] += colsum


def _plan(lg, tm):
    t = lg.shape[0]
    out_spec = pl.BlockSpec((tm, LANES), lambda p, i: (i * p, 0))
    return pl.pallas_call(
        _plan_kernel,
        grid=(2, t // tm),
        in_specs=[pl.BlockSpec((tm, LANES), lambda p, i: (i, 0))],
        out_specs=[out_spec, out_spec, pl.BlockSpec((8, LANES), lambda p, i: (0, 0))],
        out_shape=[jax.ShapeDtypeStruct((t, LANES), F32), jax.ShapeDtypeStruct((t, LANES), jnp.int32),
                   jax.ShapeDtypeStruct((8, LANES), F32)],
        scratch_shapes=[pltpu.VMEM((8, LANES), F32), pltpu.VMEM((8, LANES), F32)],
        compiler_params=_cparams(("arbitrary", "arbitrary")),
        name="moe_plan",
    )(lg)


def _dispatch_kernel(last_ref, pos_ref, h_ref, xs_hbm, zero_scr, sem):
    td = pos_ref.shape[2] // 2

    @pl.when(pl.program_id(0) == 0)
    def _():
        zero_scr[...] = jnp.zeros_like(zero_scr)

        def block_copy(e):
            first = pl.multiple_of(last_ref[e] * (MOE_ROWS * TILE_ROWS), MOE_ROWS * TILE_ROWS)
            return pltpu.make_async_copy(zero_scr, xs_hbm.at[pl.ds(first, MOE_ROWS * TILE_ROWS)], sem)

        def fill(e, carry):
            @pl.when(last_ref[e] >= 0)
            def _():
                block_copy(e).start()
            return carry

        def drain(e, carry):
            @pl.when(last_ref[e] >= 0)
            def _():
                block_copy(e).wait()
            return carry

        lax.fori_loop(0, last_ref.shape[0], fill, 0)
        lax.fori_loop(0, last_ref.shape[0], drain, 0)

    def tile(ref, tok):
        return ref.at[pl.ds(pl.multiple_of(tok * TILE_ROWS, TILE_ROWS), TILE_ROWS)]

    def body(r, carry):
        src = tile(h_ref, r)
        pltpu.make_async_copy(src, tile(xs_hbm, pos_ref[0, 0, r]), sem).start(priority=0)
        pltpu.make_async_copy(src, tile(xs_hbm, pos_ref[0, 0, td + r]), sem).start(priority=1)
        return carry

    lax.fori_loop(0, td, body, 0, unroll=8)
    for _ in range(2):
        pltpu.make_async_copy(h_ref, xs_hbm.at[pl.ds(0, td * TILE_ROWS)], sem).wait()


def _dispatch(last_blk, pos_blk, h3_tiles, n_blocks):
    n_steps = pos_blk.shape[0]
    td = pos_blk.shape[2] // 2
    grid_spec = pltpu.PrefetchScalarGridSpec(
        num_scalar_prefetch=1,
        grid=(n_steps,),
        in_specs=[
            pl.BlockSpec((1, 1, 2 * td), lambda i, last: (i, 0, 0), memory_space=pltpu.SMEM),
            pl.BlockSpec((td * TILE_ROWS, LANES), lambda i, last: (i, 0)),
        ],
        out_specs=pl.BlockSpec(memory_space=pl.ANY),
        scratch_shapes=[pltpu.VMEM((MOE_ROWS * TILE_ROWS, LANES), jnp.uint32), pltpu.SemaphoreType.DMA(())],
    )
    return pl.pallas_call(
        _dispatch_kernel,
        grid_spec=grid_spec,
        out_shape=jax.ShapeDtypeStruct((n_blocks * MOE_ROWS * TILE_ROWS, LANES), jnp.uint32),
        compiler_params=_cparams(("arbitrary",)),
        name="moe_dispatch",
    )(last_blk, pos_blk, h3_tiles)


def _expert_kernel(be_ref, na_ref, x_ref, wg_ref, wu_ref, wd_ref, o_ref, wg_scr, wu_scr, wd_scr):
    j = pl.program_id(0)
    prev = be_ref[jnp.maximum(j - 1, 0)]

    @pl.when((j == 0) | (be_ref[j] != prev))
    def _():
        wg_scr[...] = wg_ref[0].astype(BF16)
        wu_scr[...] = wu_ref[0].astype(BF16)
        wd_scr[...] = wd_ref[0].astype(BF16)

    @pl.when(j < na_ref[0])
    def _():
        x = _load_token_tiles(x_ref, 0, MOE_ROWS).astype(BF16)
        g = jnp.dot(x, wg_scr[...], preferred_element_type=F32)
        u = jnp.dot(x, wu_scr[...], preferred_element_type=F32)
        hid = (_silu(g) * u).astype(BF16)
        _store_token_tiles(o_ref, jnp.dot(hid, wd_scr[...], preferred_element_type=F32))

    @pl.when(j >= na_ref[0])
    def _():
        o_ref[...] = jnp.zeros_like(o_ref)


def _experts(blk_exp, n_active, xs_tiles, wg, wu, wd):
    n_blocks = blk_exp.shape[0]
    d, ff = wg.shape[1], wg.shape[2]
    blk = (MOE_ROWS * TILE_ROWS, LANES)
    grid_spec = pltpu.PrefetchScalarGridSpec(
        num_scalar_prefetch=2,
        grid=(n_blocks,),
        in_specs=[
            pl.BlockSpec(blk, lambda j, be, na: (jnp.minimum(j, na[0] - 1), 0)),
            pl.BlockSpec((1, d, ff), lambda j, be, na: (be[j], 0, 0)),
            pl.BlockSpec((1, d, ff), lambda j, be, na: (be[j], 0, 0)),
            pl.BlockSpec((1, ff, d), lambda j, be, na: (be[j], 0, 0)),
        ],
        out_specs=pl.BlockSpec(blk, lambda j, be, na: (j, 0)),
        scratch_shapes=[pltpu.VMEM((d, ff), BF16), pltpu.VMEM((d, ff), BF16), pltpu.VMEM((ff, d), BF16)],
    )
    return pl.pallas_call(
        _expert_kernel,
        grid_spec=grid_spec,
        out_shape=jax.ShapeDtypeStruct(xs_tiles.shape, jnp.uint32),
        compiler_params=_cparams(("arbitrary",)),
        name="moe_experts",
    )(blk_exp, n_active, xs_tiles, wg, wu, wd)


COMBINE_ROWS = 32


def _combine_kernel(n_first, pos_ref, pos_next_ref, x_ref, gate_ref, nw_ref, ys_hbm, oa_ref, ob_ref, y_scr, sem):
    i = pl.program_id(0)
    tm = x_ref.shape[0]
    slot = i % 2

    def tile(ref, tok):
        return ref.at[pl.ds(pl.multiple_of(tok * TILE_ROWS, TILE_ROWS), TILE_ROWS)]

    def gather(p_ref, to):
        buf = y_scr.at[to]

        def body(r, carry):
            pltpu.make_async_copy(tile(ys_hbm, p_ref[0, 0, r]), tile(buf, r), sem.at[to]).start(priority=0)
            pltpu.make_async_copy(tile(ys_hbm, p_ref[0, 0, tm + r]), tile(buf, tm + r), sem.at[to]).start(priority=1)
            return carry

        lax.fori_loop(0, tm, body, 0, unroll=8)

    @pl.when(i == 0)
    def _():
        gather(pos_ref, 0)

    @pl.when(i + 1 < pl.num_programs(0))
    def _():
        gather(pos_next_ref, 1 - slot)

    buf = y_scr.at[slot]
    pltpu.make_async_copy(ys_hbm.at[pl.ds(0, 2 * tm * TILE_ROWS)], buf, sem.at[slot]).wait()
    def finish(o_ref):
        def rows(rb, carry):
            r0 = pl.multiple_of(rb * COMBINE_ROWS, COMBINE_ROWS)
            sl = pl.ds(r0, COMBINE_ROWS)
            gate = gate_ref[sl, :]
            y = x_ref[sl, :] + (gate[:, 0:1] * _load_token_tiles(buf, r0, COMBINE_ROWS)
                                + gate[:, 1:2] * _load_token_tiles(buf, tm + r0, COMBINE_ROWS))
            o_ref[sl, :] = _rms(y, nw_ref[...])
            return carry

        lax.fori_loop(0, tm // COMBINE_ROWS, rows, 0, unroll=4)

    @pl.when(i < n_first)
    def _():
        finish(oa_ref)

    @pl.when(i >= n_first)
    def _():
        finish(ob_ref)


def _combine(pos_blk, x2, gates, nw, ys, tm, t_first):
    t, d = x2.shape
    na = t_first // tm
    n = t // tm
    return pl.pallas_call(
        functools.partial(_combine_kernel, na),
        grid=(n,),
        in_specs=[
            pl.BlockSpec((1, 1, 2 * tm), lambda i: (i, 0, 0), memory_space=pltpu.SMEM),
            pl.BlockSpec((1, 1, 2 * tm), lambda i: (jnp.minimum(i + 1, n - 1), 0, 0), memory_space=pltpu.SMEM),
            pl.BlockSpec((tm, d), lambda i: (i, 0)),
            pl.BlockSpec((tm, LANES), lambda i: (i, 0)),
            pl.BlockSpec((1, d), lambda i: (0, 0)),
            pl.BlockSpec(memory_space=pl.ANY),
        ],
        out_specs=[
            pl.BlockSpec((tm, d), lambda i: (_first_or_second(i, na)[0], 0)),
            pl.BlockSpec((tm, d), lambda i: (_first_or_second(i, na)[1], 0)),
        ],
        out_shape=[jax.ShapeDtypeStruct((t_first, d), F32), jax.ShapeDtypeStruct((t - t_first, d), F32)],
        scratch_shapes=[pltpu.VMEM((2, 2 * tm * TILE_ROWS, LANES), jnp.uint32), pltpu.SemaphoreType.DMA((2,))],
        compiler_params=_cparams(("arbitrary",)),
        name="moe_combine",
    )(pos_blk, pos_blk, x2, gates, nw, ys)


def _block_experts(counts, n_blocks):
    per_expert = (counts + MOE_ROWS - 1) // MOE_ROWS
    ends = jnp.cumsum(per_expert)
    blk = jnp.arange(n_blocks, dtype=jnp.int32)
    blk_exp = jnp.minimum(jnp.sum(blk[:, None] >= ends[None, :], axis=1), N_EXPERTS - 1).astype(jnp.int32)
    last_blk = jnp.where(per_expert > 0, ends - 1, -1)
    tail = n_blocks - 1 - jnp.arange(N_EXPERTS)
    to_zero = jnp.concatenate([last_blk, jnp.where(tail >= ends[-1], tail, -1)]).astype(jnp.int32)
    return blk_exp, ends[-1:].astype(jnp.int32), to_zero


def _pick(n, pref):
    t = pref
    while n % t:
        t //= 2
    return t


def _encoder(xa, xb, mem, p):
    (nba, s, d), nbb = xa.shape, xb.shape[0]
    nb = nba + nbb
    t = nb * s
    t_first = nba * s
    nc = s // SSD_CHUNK
    tm = _pick(math.gcd(t_first, t - t_first), 1024)
    tq = _pick(s, 512)
    xa = xa.reshape(t_first, d)
    xb = xb.reshape(t - t_first, d)

    perm = jnp.arange(SSD_HEADS).reshape(SSD_GROUPS, HEADS_PER_GROUP)
    perm = jnp.concatenate([perm, perm + SSD_HEADS], axis=1).reshape(-1)

    w_in = p['w_in']
    o_z = 2 * D_MODEL
    o_x = o_z + SSD_D_INNER
    o_dt = o_x + SSD_D_INNER + 2 * SSD_GROUPS * SSD_STATE
    o_g = o_dt + 2 * SSD_HEADS
    w_val = w_in[:, :D_MODEL].astype(BF16)
    w_gate = w_in[:, D_MODEL:o_z].astype(BF16)
    w_z = w_in[:, o_z:o_x].astype(BF16)
    w_xbc = w_in[:, o_x:o_dt].astype(BF16)
    w_dt = jnp.pad(w_in[:, o_dt:o_g][:, perm], ((0, 0), (0, LANES - 2 * SSD_HEADS))).astype(BF16)
    w_g = w_in[:, o_g:].astype(BF16)
    nmix = p['norm_mix_w'].reshape(1, d)

    h = _prenorm(xa, xb, nmix, tm)
    u = _glu_proj(h, w_val, w_gate, tm, PROJ_TN // 2)
    xbc = _proj(h, w_xbc, lambda a: a, BF16, tm, PROJ_TN, "xbc_proj")
    dt_raw = _proj(h, w_dt, lambda a: a, F32, tm, LANES, "dt_proj")
    gates = _proj(h, w_g, _sigmoid, BF16, tm, PROJ_TN, "gate_proj")

    pad = LANES - 2 * SSD_HEADS
    dt_bias = jnp.pad(jnp.concatenate([p['ssd_dt_bias_f'], p['ssd_dt_bias_b']])[perm], (0, pad)).reshape(1, LANES)
    a_log = jnp.pad(jnp.concatenate([p['ssd_a_log_f'], p['ssd_a_log_b']])[perm], (0, pad)).reshape(1, LANES)
    acs2, rowt, e_hi, e_lo, w_hi, w_lo, dec = _dtprep(dt_raw, dt_bias, a_log, tm)

    def per_group_tm(a):
        return a[:, :2 * SSD_HEADS].reshape(nb, s, SSD_GROUPS, 8).transpose(0, 2, 1, 3)

    def per_group_hm(a):
        return a[:, :2 * SSD_HEADS].reshape(nb, nc, SSD_CHUNK, SSD_GROUPS, 8).transpose(0, 3, 1, 4, 2)

    acs_tm = per_group_tm(acs2).reshape(nb * SSD_GROUPS, s, 8)
    ew_tm = jnp.concatenate([per_group_tm(v) for v in (e_hi, e_lo, w_hi, w_lo)], axis=-1)
    ew_tm = ew_tm.reshape(nb * SSD_GROUPS, s, EW_COLS)
    rowt_hm = per_group_hm(rowt)
    dec = dec[:, :2 * SSD_HEADS].reshape(nb, nc, SSD_GROUPS, 2, HEADS_PER_GROUP).transpose(0, 2, 1, 3, 4)
    dec = jnp.repeat(dec, SSD_HEAD_DIM, axis=-1)

    cwx = p['ssd_conv_w']
    cbx = p['ssd_conv_b']

    def per_group_conv(a):
        xs_ = a[:, :SSD_D_INNER].reshape(-1, SSD_GROUPS, GROUP_CH)
        b_ = a[:, SSD_D_INNER:SSD_D_INNER + SSD_GROUPS * SSD_STATE].reshape(-1, SSD_GROUPS, SSD_STATE)
        c_ = a[:, SSD_D_INNER + SSD_GROUPS * SSD_STATE:].reshape(-1, SSD_GROUPS, SSD_STATE)
        return jnp.concatenate([xs_, b_, c_], axis=-1).transpose(1, 0, 2)

    cw = jnp.pad(per_group_conv(cwx), ((0, 0), (0, 8 - SSD_CONV_WIDTH), (0, 0)))
    cb = per_group_conv(cbx.reshape(1, -1))
    dskip = jnp.repeat(p['ssd_d'], SSD_HEAD_DIM).reshape(SSD_GROUPS, 1, GROUP_CH)
    nw_ssd = p['ssd_norm_w'].reshape(SSD_GROUPS, 1, GROUP_CH)

    yn = _ssd(xbc.reshape(nb, s, -1), h.reshape(nb, s, d), w_z, cw, cb, rowt_hm, acs_tm, ew_tm, dec, dskip, nw_ssd)
    mb = _gated_proj(yn.reshape(t, SSD_D_INNER), p['ssd_out'].astype(BF16), gates, 1, tm)

    dww = jnp.pad(p['conv_dw_w'], ((0, 32 - CONV_WIDTH), (0, 0)))
    x1 = _conformer(u.reshape(nb, s, d), mb.reshape(nb, s, d), gates.reshape(nb, s, 2 * d), xa, xb, dww,
                    p['conv_dw_b'].reshape(1, d), p['conv_ln_w'].reshape(1, d), p['conv_ln_b'].reshape(1, d),
                    p['conv_pw_out'].astype(BF16), p['w_out'].astype(BF16), tq)

    m = mem.shape[1]
    kv = _norm_proj(mem.reshape(nb * m, d), p['norm_mem_w'].reshape(1, d), p['xattn_wkv'].astype(BF16),
                    lambda a: a, BF16, _pick(nb * m, 512), 1024, "kv_proj")
    n_r = N_GROUPS + N_EXPERTS
    wr = jnp.pad(jnp.concatenate([p['router_group_w'], p['router_expert_w']], axis=1),
                 ((0, 0), (0, LANES - n_r))).astype(BF16)
    br = jnp.pad(jnp.concatenate([p['router_group_b'], p['router_expert_b']]), (0, LANES - n_r)).reshape(1, LANES)
    x2, h3, logits = _attention(x1, kv.reshape(nb, m, 2 * d), p['norm_xattn_w'].reshape(1, d),
                                p['xattn_wq'].astype(BF16), p['xattn_wo'].astype(BF16),
                                p['norm_ffn_w'].reshape(1, d), wr, br, tq)

    tc = _pick(math.gcd(t_first, t - t_first), 512)
    gate, pos, cnt = _plan(logits.reshape(t, LANES), tm)
    n_blocks = (2 * t) // MOE_ROWS + N_EXPERTS
    blk_exp, n_active, last_blk = _block_experts(cnt[0, :N_EXPERTS].astype(jnp.int32), n_blocks)
    pos_blk = pos[:, :2].reshape(t // tc, tc, 2).transpose(0, 2, 1).reshape(t // tc, 1, 2 * tc)
    pos_blk_d = pos[:, :2].reshape(t // tm, tm, 2).transpose(0, 2, 1).reshape(t // tm, 1, 2 * tm)
    xs = _dispatch(last_blk, pos_blk_d, h3, n_blocks)
    ys = _experts(blk_exp, n_active, xs, p['expert_w_gate'], p['expert_w_up'], p['expert_w_down'])
    return _combine(pos_blk, x2.reshape(t, d), gate, p['norm_final_w'].reshape(1, d), ys, tc, t_first)


def kernel(x_prompt, x_sample, mem_prompt, mem_sample, norm_mix_w, w_in, conv_dw_w, conv_dw_b, conv_ln_w, conv_ln_b, conv_pw_out, ssd_conv_w, ssd_conv_b, ssd_dt_bias_f, ssd_dt_bias_b, ssd_a_log_f, ssd_a_log_b, ssd_d, ssd_norm_w, ssd_out, w_out, norm_xattn_w, norm_mem_w, xattn_wq, xattn_wkv, xattn_wo, norm_ffn_w, router_group_w, router_group_b, router_expert_w, router_expert_b, expert_w_gate, expert_w_up, expert_w_down, norm_final_w):
    p = {
        'norm_mix_w': norm_mix_w[0], 'w_in': w_in[0], 'conv_dw_w': conv_dw_w[0], 'conv_dw_b': conv_dw_b[0],
        'conv_ln_w': conv_ln_w[0], 'conv_ln_b': conv_ln_b[0], 'conv_pw_out': conv_pw_out[0],
        'ssd_conv_w': ssd_conv_w[0], 'ssd_conv_b': ssd_conv_b[0], 'ssd_dt_bias_f': ssd_dt_bias_f[0],
        'ssd_dt_bias_b': ssd_dt_bias_b[0], 'ssd_a_log_f': ssd_a_log_f[0], 'ssd_a_log_b': ssd_a_log_b[0],
        'ssd_d': ssd_d[0], 'ssd_norm_w': ssd_norm_w[0], 'ssd_out': ssd_out[0], 'w_out': w_out[0],
        'norm_xattn_w': norm_xattn_w[0], 'norm_mem_w': norm_mem_w[0], 'xattn_wq': xattn_wq[0],
        'xattn_wkv': xattn_wkv[0], 'xattn_wo': xattn_wo[0], 'norm_ffn_w': norm_ffn_w[0],
        'router_group_w': router_group_w[0], 'router_group_b': router_group_b[0],
        'router_expert_w': router_expert_w[0], 'router_expert_b': router_expert_b[0],
        'expert_w_gate': expert_w_gate[0], 'expert_w_up': expert_w_up[0], 'expert_w_down': expert_w_down[0],
        'norm_final_w': norm_final_w,
    }
    mem = jnp.concatenate([mem_prompt, mem_sample], axis=0)
    y_prompt, y_sample = _encoder(x_prompt, x_sample, mem, p)
    return (y_prompt.reshape(x_prompt.shape), y_sample.reshape(x_sample.shape))
```

```python
import functools
import math

import jax
import jax.numpy as jnp
from jax import lax
from jax.experimental import pallas as pl
from jax.experimental.pallas import tpu as pltpu

F32 = jnp.float32
BF16 = jnp.bfloat16

D_MODEL = 1024
CONV_WIDTH = 31
SSD_D_INNER = 2048
SSD_HEAD_DIM = 64
SSD_HEADS = 32
SSD_GROUPS = 8
SSD_STATE = 128
SSD_CONV_WIDTH = 5
SSD_CHUNK = 128
GROUP_CH = SSD_D_INNER // SSD_GROUPS
HEADS_PER_GROUP = SSD_HEADS // SSD_GROUPS
XATTN_HEADS = 4
XATTN_HEAD_DIM = D_MODEL // XATTN_HEADS
N_GROUPS = 8
EXPERTS_PER_GROUP = 8
N_EXPERTS = 64
EXPERT_FF = 512
MOE_BLOCK = 128
RMS_EPS = 1e-6
LN_EPS = 1e-5

LANES = 128
HALO = 16
VMEM_LIMIT = 56 * 1024 * 1024
PROJ_TN = 2048


def _cparams(sem):
    return pltpu.CompilerParams(dimension_semantics=sem, vmem_limit_bytes=VMEM_LIMIT)


def _rms(x, w):
    ms = jnp.mean(x * x, axis=-1, keepdims=True)
    return x * lax.rsqrt(ms + RMS_EPS) * w


def _sigmoid(x):
    return 1.0 / (1.0 + jnp.exp2(x * -1.4426950408889634))


def _silu(x):
    return x * _sigmoid(x)


def _softplus(x):
    return jnp.maximum(x, 0.0) + jnp.log1p(jnp.exp(-jnp.abs(x)))


CONV_STRIDE = 4
CONV_ROWS = 8 * CONV_STRIDE
CONV_UNROLL = 4


def _depthwise_conv(src, dst, slab, w_ref, b_ref, n_rows, width, first_row, epilogue):
    lanes = pl.ds(slab * LANES, LANES)
    taps = [jnp.broadcast_to(w_ref[k:k + 1, lanes], (8, LANES)) for k in range(width)]
    bias = jnp.broadcast_to(b_ref[:, lanes], (8, LANES))
    step = CONV_ROWS * CONV_UNROLL

    def body(r, carry):
        _conv_rows(src, dst, slab, taps, bias, r * step, step, width, first_row, epilogue)
        return carry

    lax.fori_loop(0, n_rows // step, body, 0)


def _conv_rows(src, dst, slab, taps, bias, base, n_rows, width, first_row, epilogue):
    for t0 in range(n_rows // 8):
        row = base + (t0 // CONV_STRIDE) * CONV_ROWS + t0 % CONV_STRIDE
        acc = bias
        for k in range(width):
            acc = acc + src[slab, pl.ds(row + first_row - width // 2 + k, 8, stride=CONV_STRIDE), :] * taps[k]
        dst[slab, pl.ds(row, 8, stride=CONV_STRIDE), :] = epilogue(acc)


def _first_or_second(i, n_first):
    return jnp.minimum(i, n_first - 1), jnp.maximum(i - n_first, 0)


def _prenorm_kernel(n_first, xa_ref, xb_ref, nw_ref, o_ref):
    x = jnp.where(pl.program_id(0) < n_first, xa_ref[...], xb_ref[...])
    o_ref[...] = _rms(x, nw_ref[...]).astype(o_ref.dtype)


def _prenorm(xa, xb, nw, tm):
    (ta, d), tb = xa.shape, xb.shape[0]
    na = ta // tm
    return pl.pallas_call(
        functools.partial(_prenorm_kernel, na),
        grid=((ta + tb) // tm,),
        in_specs=[
            pl.BlockSpec((tm, d), lambda i: (_first_or_second(i, na)[0], 0)),
            pl.BlockSpec((tm, d), lambda i: (_first_or_second(i, na)[1], 0)),
            pl.BlockSpec((1, d), lambda i: (0, 0)),
        ],
        out_specs=pl.BlockSpec((tm, d), lambda i: (i, 0)),
        out_shape=jax.ShapeDtypeStruct((ta + tb, d), BF16),
        compiler_params=_cparams(("arbitrary",)),
        name="mix_norm",
    )(xa, xb, nw)


def _glu_kernel(h_ref, wv_ref, wg_ref, o_ref):
    h = h_ref[...]
    v = jnp.dot(h, wv_ref[...], preferred_element_type=F32)
    g = jnp.dot(h, wg_ref[...], preferred_element_type=F32)
    o_ref[...] = (v * _sigmoid(g)).astype(o_ref.dtype)


def _glu_proj(h, wv, wg, tm, tn):
    t, d = h.shape
    n = wv.shape[1]
    return pl.pallas_call(
        _glu_kernel,
        grid=(t // tm, n // tn),
        in_specs=[
            pl.BlockSpec((tm, d), lambda i, j: (i, 0)),
            pl.BlockSpec((d, tn), lambda i, j: (0, j)),
            pl.BlockSpec((d, tn), lambda i, j: (0, j)),
        ],
        out_specs=pl.BlockSpec((tm, tn), lambda i, j: (i, j)),
        out_shape=jax.ShapeDtypeStruct((t, n), BF16),
        compiler_params=_cparams(("parallel", "arbitrary")),
        name="glu_proj",
    )(h, wv, wg)


def _proj_kernel(epi, h_ref, w_ref, o_ref):
    o_ref[...] = epi(jnp.dot(h_ref[...], w_ref[...], preferred_element_type=F32)).astype(o_ref.dtype)


def _proj(h, w, epi, out_dtype, tm, tn, name):
    t, d = h.shape
    n = w.shape[1]
    return pl.pallas_call(
        functools.partial(_proj_kernel, epi),
        grid=(t // tm, n // tn),
        in_specs=[
            pl.BlockSpec((tm, d), lambda i, j: (i, 0)),
            pl.BlockSpec((d, tn), lambda i, j: (0, j)),
        ],
        out_specs=pl.BlockSpec((tm, tn), lambda i, j: (i, j)),
        out_shape=jax.ShapeDtypeStruct((t, n), out_dtype),
        compiler_params=_cparams(("parallel", "arbitrary")),
        name=name,
    )(h, w)


def _norm_proj_kernel(epi, x_ref, nw_ref, w_ref, o_ref, h_scr):
    @pl.when(pl.program_id(1) == 0)
    def _():
        h_scr[...] = _rms(x_ref[...], nw_ref[...]).astype(BF16)

    acc = jnp.dot(h_scr[...], w_ref[...], preferred_element_type=F32)
    o_ref[...] = epi(acc).astype(o_ref.dtype)


def _norm_proj(x, nw, w, epi, out_dtype, tm, tn, name):
    t, d = x.shape
    n = w.shape[1]
    return pl.pallas_call(
        functools.partial(_norm_proj_kernel, epi),
        grid=(t // tm, n // tn),
        in_specs=[
            pl.BlockSpec((tm, d), lambda i, j: (i, 0)),
            pl.BlockSpec((1, d), lambda i, j: (0, 0)),
            pl.BlockSpec((d, tn), lambda i, j: (0, j)),
        ],
        out_specs=pl.BlockSpec((tm, tn), lambda i, j: (i, j)),
        out_shape=jax.ShapeDtypeStruct((t, n), out_dtype),
        scratch_shapes=[pltpu.VMEM((tm, d), BF16)],
        compiler_params=_cparams(("parallel", "arbitrary")),
        name=name,
    )(x, nw, w)


def _gated_proj_kernel(y_ref, w_ref, g_ref, o_ref):
    acc = jnp.dot(y_ref[...], w_ref[...], preferred_element_type=F32)
    o_ref[...] = (acc * g_ref[...].astype(F32)).astype(o_ref.dtype)


def _gated_proj(y, w, g, g_col_block, tm):
    t, k = y.shape
    n = w.shape[1]
    return pl.pallas_call(
        _gated_proj_kernel,
        grid=(t // tm,),
        in_specs=[
            pl.BlockSpec((tm, k), lambda i: (i, 0)),
            pl.BlockSpec((k, n), lambda i: (0, 0)),
            pl.BlockSpec((tm, n), lambda i: (i, g_col_block)),
        ],
        out_specs=pl.BlockSpec((tm, n), lambda i: (i, 0)),
        out_shape=jax.ShapeDtypeStruct((t, n), BF16),
        compiler_params=_cparams(("parallel",)),
        name="ssd_out_proj",
    )(y, w, g)


LOG2E = 1.4426950408889634


def _split_bf16(v):
    hi = v.astype(BF16)
    return hi, (v - hi.astype(F32)).astype(BF16)


def _dtprep_kernel(raw_ref, bias_ref, alog_ref, acs2_ref, rowt_ref, ehi_ref, elo_ref, whi_ref, wlo_ref, dec_ref):
    rows = raw_ref.shape[0]
    a_head = -jnp.exp(alog_ref[...])
    row = lax.broadcasted_iota(jnp.int32, (SSD_CHUNK, LANES), 0)
    lane = lax.broadcasted_iota(jnp.int32, (SSD_CHUNK, LANES), 1)
    is_bwd = (lane % (2 * HEADS_PER_GROUP)) >= HEADS_PER_GROUP
    for c in range(rows // SSD_CHUNK):
        sl = pl.ds(c * SSD_CHUNK, SSD_CHUNK)
        dt = _softplus(raw_ref[sl, :] + bias_ref[...])
        a = dt * a_head
        fwd = a
        bwd = a
        sh = 1
        while sh < SSD_CHUNK:
            fwd = fwd + jnp.where(row >= sh, pltpu.roll(fwd, sh, 0), 0.0)
            bwd = bwd + jnp.where(row < SSD_CHUNK - sh, pltpu.roll(bwd, SSD_CHUNK - sh, 0), 0.0)
            sh *= 2
        acs = jnp.where(is_bwd, bwd, fwd)
        a_end = jnp.where(is_bwd[0:1, :], bwd[0:1, :], fwd[SSD_CHUNK - 1:SSD_CHUNK, :])
        acs2 = acs * LOG2E
        acs2_ref[sl, :] = acs2
        rowt_ref[sl, :] = acs2 - jnp.log(dt) * LOG2E
        ehi_ref[sl, :], elo_ref[sl, :] = _split_bf16(jnp.exp(acs))
        whi_ref[sl, :], wlo_ref[sl, :] = _split_bf16(jnp.exp(a_end - acs) * dt)
        dec_ref[c:c + 1, :] = jnp.exp(a_end)


def _dtprep(raw, bias, alog, tm):
    t = raw.shape[0]
    spec = pl.BlockSpec((tm, LANES), lambda i: (i, 0))
    vec = pl.BlockSpec((1, LANES), lambda i: (0, 0))
    return pl.pallas_call(
        _dtprep_kernel,
        grid=(t // tm,),
        in_specs=[spec, vec, vec],
        out_specs=[spec] * 6 + [pl.BlockSpec((tm // SSD_CHUNK, LANES), lambda i: (i, 0))],
        out_shape=[jax.ShapeDtypeStruct((t, LANES), F32)] * 2 + [jax.ShapeDtypeStruct((t, LANES), BF16)] * 4
        + [jax.ShapeDtypeStruct((t // SSD_CHUNK, LANES), F32)],
        compiler_params=_cparams(("parallel",)),
        name="dt_prep",
    )(raw, bias, alog)


SSD_PAD = 8
XBC_W = GROUP_CH + 2 * SSD_STATE


EW_COLS = 4 * 2 * HEADS_PER_GROUP


def _head_expand_matrix():
    col = jnp.arange(4 * GROUP_CH)
    part = col // GROUP_CH
    is_w = (part == 1) | (part == 2)
    is_bwd = part >= 2
    head = is_bwd * HEADS_PER_GROUP + (col % GROUP_CH) // SSD_HEAD_DIM
    hi_row = is_w * 16 + head
    row = jnp.arange(EW_COLS)[:, None]
    return ((row == hi_row[None, :]) | (row == hi_row[None, :] + 8)).astype(BF16)


def _ssd_kernel(xs_ref, b_ref, c_ref, h_ref, wz_ref, cw_ref, cb_ref, rowt_ref, acs_ref, ew_ref, sel_ref, dec_ref,
                dskip_ref, nw_ref, o_ref, raw_scr, act_scr, y_scr, upd_scr, st_scr):
    s = xs_ref.shape[1]
    nc = s // SSD_CHUNK
    L = SSD_CHUNK

    n_slabs = XBC_W // LANES
    for j in range(n_slabs):
        raw_scr[j, pl.ds(0, SSD_PAD), :] = jnp.zeros((SSD_PAD, LANES), F32)
        raw_scr[j, pl.ds(SSD_PAD + s, L + SSD_PAD), :] = jnp.zeros((L + SSD_PAD, LANES), F32)
    raw_scr[0, pl.ds(SSD_PAD, s), :] = xs_ref[0, :, :LANES].astype(F32)
    raw_scr[1, pl.ds(SSD_PAD, s), :] = xs_ref[0, :, LANES:].astype(F32)
    raw_scr[2, pl.ds(SSD_PAD, s), :] = b_ref[0].astype(F32)
    raw_scr[3, pl.ds(SSD_PAD, s), :] = c_ref[0].astype(F32)

    def conv_chunk(c):
        for j in range(n_slabs):
            lanes = pl.ds(j * LANES, LANES)
            taps = [jnp.broadcast_to(cw_ref[0, k:k + 1, lanes], (8, LANES)) for k in range(SSD_CONV_WIDTH)]
            bias = jnp.broadcast_to(cb_ref[0, :, lanes], (8, LANES))
            _conv_rows(raw_scr, act_scr, j, taps, bias, c * L, L, SSD_CONV_WIDTH, SSD_PAD, _silu)

    conv_chunk(0)

    li = lax.broadcasted_iota(jnp.int32, (L, L), 0)
    si = lax.broadcasted_iota(jnp.int32, (L, L), 1)
    head_of_ch = lax.broadcasted_iota(jnp.int32, (L, GROUP_CH), 1) // SSD_HEAD_DIM

    keep = (li >= si, si >= li)
    neg_inf = jnp.float32(-jnp.inf)
    n_heads = 2 * HEADS_PER_GROUP
    st_scr[...] = jnp.zeros_like(st_scr)

    def local_body(c, carry):
        r0 = pl.multiple_of(c * L, L)
        x = jnp.concatenate([act_scr[0, pl.ds(r0, L), :], act_scr[1, pl.ds(r0, L), :]], axis=1)
        xb = x.astype(BF16)
        bk = act_scr[2, pl.ds(r0, L), :].astype(BF16)
        ck = act_scr[3, pl.ds(r0, L), :].astype(BF16)
        acs = acs_ref[0, pl.ds(r0, L), :]
        rowt = rowt_ref[0, 0, c]
        ew = jnp.dot(ew_ref[0, pl.ds(r0, L), :], sel_ref[:, :3 * GROUP_CH],
                     preferred_element_type=F32)

        cb = lax.dot_general(ck, bk, (((1,), (1,)), ((), ())), preferred_element_type=F32)
        ms = []
        for j in range(n_heads):
            diff = acs[:, j:j + 1] - rowt[j:j + 1, :]
            decay_dt = jnp.exp2(jnp.where(keep[j // HEADS_PER_GROUP], diff, neg_inf))
            ms.append((cb * decay_dt).astype(BF16))
        xbd = jnp.concatenate([jnp.where(head_of_ch == h, xb, jnp.zeros_like(xb))
                               for h in range(HEADS_PER_GROUP)], axis=0)
        y = jnp.dot(jnp.concatenate(ms, axis=1), jnp.concatenate([xbd, xbd], axis=0),
                    preferred_element_type=F32) + x * dskip_ref[0]

        xw = jnp.concatenate([x * ew[:, GROUP_CH:2 * GROUP_CH], x * ew[:, 2 * GROUP_CH:]], axis=1)
        upd = lax.dot_general(bk, xw.astype(BF16), (((0,), (0,)), ((), ())), preferred_element_type=F32)

        state = st_scr[0]
        y = y + jnp.dot(ck, state.astype(BF16), preferred_element_type=F32) * ew[:, :GROUP_CH]
        st_scr[0] = state * dec_ref[0, 0, c, 0:1, :] + upd[:, :GROUP_CH]
        y_scr[pl.ds(r0, L), :] = y
        upd_scr[c] = upd[:, GROUP_CH:]
        conv_chunk(c + 1)
        return carry

    lax.fori_loop(0, nc, local_body, 0, unroll=4)

    def bwd_body(i, carry):
        c = nc - 1 - i
        r0 = pl.multiple_of(c * L, L)
        ck = act_scr[3, pl.ds(r0, L), :].astype(BF16)
        e_bwd = jnp.dot(ew_ref[0, pl.ds(r0, L), :], sel_ref[:, 3 * GROUP_CH:], preferred_element_type=F32)
        state = st_scr[1]
        y = y_scr[pl.ds(r0, L), :] + jnp.dot(ck, state.astype(BF16), preferred_element_type=F32) * e_bwd
        st_scr[1] = state * dec_ref[0, 0, c, 1:2, :] + upd_scr[c]
        z = jnp.dot(h_ref[0, pl.ds(r0, L), :], wz_ref[...], preferred_element_type=F32)
        yz = y * _silu(z)
        ms = jnp.mean(yz * yz, axis=-1, keepdims=True)
        o_ref[0, pl.ds(r0, L), :] = (yz * lax.rsqrt(ms + RMS_EPS) * nw_ref[0]).astype(o_ref.dtype)
        return carry

    lax.fori_loop(0, nc, bwd_body, 0, unroll=8)


def _ssd(xbc, h, wz, cw, cb, rowt_hm, acs_tm, ew_tm, dec, dskip, nw):
    nb, s, _ = xbc.shape
    d = h.shape[2]
    nc = s // SSD_CHUNK
    return pl.pallas_call(
        _ssd_kernel,
        grid=(nb, SSD_GROUPS),
        in_specs=[
            pl.BlockSpec((1, s, GROUP_CH), lambda b, g: (b, 0, g)),
            pl.BlockSpec((1, s, SSD_STATE), lambda b, g: (b, 0, SSD_D_INNER // SSD_STATE + g)),
            pl.BlockSpec((1, s, SSD_STATE), lambda b, g: (b, 0, SSD_D_INNER // SSD_STATE + SSD_GROUPS + g)),
            pl.BlockSpec((1, s, d), lambda b, g: (b, 0, 0)),
            pl.BlockSpec((d, GROUP_CH), lambda b, g: (0, g)),
            pl.BlockSpec((1, 8, XBC_W), lambda b, g: (g, 0, 0)),
            pl.BlockSpec((1, 1, XBC_W), lambda b, g: (g, 0, 0)),
            pl.BlockSpec((1, 1, nc, 8, SSD_CHUNK), lambda b, g: (b, g, 0, 0, 0)),
            pl.BlockSpec((1, s, 8), lambda b, g: (b * SSD_GROUPS + g, 0, 0)),
            pl.BlockSpec((1, s, EW_COLS), lambda b, g: (b * SSD_GROUPS + g, 0, 0)),
            pl.BlockSpec((EW_COLS, 4 * GROUP_CH), lambda b, g: (0, 0)),
            pl.BlockSpec((1, 1, nc, 2, GROUP_CH), lambda b, g: (b, g, 0, 0, 0)),
            pl.BlockSpec((1, 1, GROUP_CH), lambda b, g: (g, 0, 0)),
            pl.BlockSpec((1, 1, GROUP_CH), lambda b, g: (g, 0, 0)),
        ],
        out_specs=pl.BlockSpec((1, s, GROUP_CH), lambda b, g: (b, 0, g)),
        out_shape=jax.ShapeDtypeStruct((nb, s, SSD_D_INNER), BF16),
        scratch_shapes=[
            pltpu.VMEM((XBC_W // LANES, s + SSD_CHUNK + 2 * SSD_PAD, LANES), F32),
            pltpu.VMEM((XBC_W // LANES, s + SSD_CHUNK, LANES), F32),
            pltpu.VMEM((s, GROUP_CH), F32),
            pltpu.VMEM((nc, SSD_STATE, GROUP_CH), F32),
            pltpu.VMEM((2, SSD_STATE, GROUP_CH), F32),
        ],
        compiler_params=_cparams(("parallel", "parallel")),
        name="ssd_scan",
    )(xbc, xbc, xbc, h, wz, cw, cb, rowt_hm, acs_tm, ew_tm, _head_expand_matrix(), dec, dskip, nw)


D_SLABS = D_MODEL // LANES


def _conformer_kernel(n_first, u_ref, up_ref, un_ref, mb_ref, g_ref, xa_ref, xb_ref, dww_ref, dwb_ref, lnw_ref,
                      lnb_ref, pw_ref, wo_ref, o_ref, u_scr, c_scr):
    i = pl.program_id(1)
    n = pl.num_programs(1)
    tq = u_ref.shape[1]
    x = jnp.where(pl.program_id(0) * n + i < n_first, xa_ref[...], xb_ref[...])
    for j in range(D_SLABS):
        cs = slice(j * LANES, (j + 1) * LANES)
        u_scr[j, pl.ds(0, HALO), :] = jnp.where(i > 0, up_ref[0, :, cs].astype(F32), 0.0)
        u_scr[j, pl.ds(HALO, tq), :] = u_ref[0, :, cs].astype(F32)
        u_scr[j, pl.ds(HALO + tq, HALO), :] = jnp.where(i < n - 1, un_ref[0, :, cs].astype(F32), 0.0)
    for j in range(D_SLABS):
        _depthwise_conv(u_scr, c_scr, j, dww_ref, dwb_ref, tq, CONV_WIDTH, HALO, lambda a: a)

    cv = jnp.concatenate([c_scr[j] for j in range(D_SLABS)], axis=1)
    mu = jnp.mean(cv, axis=-1, keepdims=True)
    xc = cv - mu
    var = jnp.mean(xc * xc, axis=-1, keepdims=True)
    y = xc * lax.rsqrt(var + LN_EPS) * lnw_ref[...] + lnb_ref[...]
    a = jnp.dot(_silu(y).astype(BF16), pw_ref[...], preferred_element_type=F32)
    merged = g_ref[0].astype(F32) * a + mb_ref[0].astype(F32)
    o_ref[0] = x + jnp.dot(merged.astype(BF16), wo_ref[...], preferred_element_type=F32)


def _conformer(u, mb, gates, xa, xb, dww, dwb, lnw, lnb, pw, wo, tq):
    nb, s, d = u.shape
    hb = tq // HALO
    nh = s // HALO
    nq = s // tq
    na = xa.shape[0] // tq
    wspec = pl.BlockSpec((d, d), lambda b, i: (0, 0))
    vspec = pl.BlockSpec((1, d), lambda b, i: (0, 0))
    return pl.pallas_call(
        functools.partial(_conformer_kernel, na),
        grid=(nb, nq),
        in_specs=[
            pl.BlockSpec((1, tq, d), lambda b, i: (b, i, 0)),
            pl.BlockSpec((1, HALO, d), lambda b, i: (b, jnp.maximum(i * hb - 1, 0), 0)),
            pl.BlockSpec((1, HALO, d), lambda b, i: (b, jnp.minimum((i + 1) * hb, nh - 1), 0)),
            pl.BlockSpec((1, tq, d), lambda b, i: (b, i, 0)),
            pl.BlockSpec((1, tq, d), lambda b, i: (b, i, 0)),
            pl.BlockSpec((tq, d), lambda b, i: (_first_or_second(b * nq + i, na)[0], 0)),
            pl.BlockSpec((tq, d), lambda b, i: (_first_or_second(b * nq + i, na)[1], 0)),
            pl.BlockSpec((32, d), lambda b, i: (0, 0)),
            vspec, vspec, vspec, wspec, wspec,
        ],
        out_specs=pl.BlockSpec((1, tq, d), lambda b, i: (b, i, 0)),
        out_shape=jax.ShapeDtypeStruct((nb, s, d), F32),
        scratch_shapes=[pltpu.VMEM((D_SLABS, tq + 2 * HALO, LANES), F32), pltpu.VMEM((D_SLABS, tq, LANES), F32)],
        compiler_params=_cparams(("arbitrary", "arbitrary")),
        name="conformer_merge",
    )(u, u, u, mb, gates, xa, xb, dww, dwb, lnw, lnb, pw, wo)


TILE_ROWS = D_MODEL // (2 * LANES)
HIGH_HALF = 0xFFFF0000


def _bf16_bits(v):
    return lax.bitcast_convert_type(v.astype(BF16).astype(F32), jnp.uint32)


def _store_token_tiles(ref, val):
    n, d = val.shape
    packed = (_bf16_bits(val[:, :d // 2]) >> 16) | (_bf16_bits(val[:, d // 2:]) & jnp.uint32(HIGH_HALF))
    for j in range(TILE_ROWS):
        ref[pl.ds(j, n, stride=TILE_ROWS), :] = packed[:, j * LANES:(j + 1) * LANES]


def _load_token_tiles(ref, first, n):
    packed = jnp.concatenate([ref[pl.ds(first * TILE_ROWS + j, n, stride=TILE_ROWS), :] for j in range(TILE_ROWS)],
                             axis=1)
    low = lax.bitcast_convert_type(packed << 16, F32)
    high = lax.bitcast_convert_type(packed & jnp.uint32(HIGH_HALF), F32)
    return jnp.concatenate([low, high], axis=1)


def _attn_kernel(x_ref, k_ref, v_ref, nw_ref, wq_ref, wo_ref, nf_ref, wr_ref, br_ref, x2_ref, h3_ref, lg_ref):
    x = x_ref[0]
    h = _rms(x, nw_ref[...]).astype(BF16)
    q = jnp.dot(h, wq_ref[...], preferred_element_type=F32)
    scale = 1.0 / math.sqrt(XATTN_HEAD_DIM)
    outs = []
    for hd in range(XATTN_HEADS):
        cs = slice(hd * XATTN_HEAD_DIM, (hd + 1) * XATTN_HEAD_DIM)
        sc = lax.dot_general(q[:, cs].astype(BF16), k_ref[0, :, cs], (((1,), (1,)), ((), ())),
                             preferred_element_type=F32) * scale
        m = jnp.max(sc, axis=-1, keepdims=True)
        e = jnp.exp(sc - m)
        p = e / jnp.sum(e, axis=-1, keepdims=True)
        outs.append(jnp.dot(p.astype(BF16), v_ref[0, :, cs], preferred_element_type=F32))
    o = jnp.concatenate(outs, axis=1).astype(BF16)
    x2 = x + jnp.dot(o, wo_ref[...], preferred_element_type=F32)
    x2_ref[0] = x2
    h3 = _rms(x2, nf_ref[...])
    _store_token_tiles(h3_ref, h3)
    lg_ref[0] = jnp.dot(h3.astype(BF16), wr_ref[...], preferred_element_type=F32) + br_ref[...]


def _attention(x, kv, nw, wq, wo, nf, wr, br, tq):
    nb, s, d = x.shape
    m = kv.shape[1]
    wspec = pl.BlockSpec((d, d), lambda b, i: (0, 0))
    vspec = pl.BlockSpec((1, d), lambda b, i: (0, 0))
    return pl.pallas_call(
        _attn_kernel,
        grid=(nb, s // tq),
        in_specs=[
            pl.BlockSpec((1, tq, d), lambda b, i: (b, i, 0)),
            pl.BlockSpec((1, m, d), lambda b, i: (b, 0, 0)),
            pl.BlockSpec((1, m, d), lambda b, i: (b, 0, 1)),
            vspec, wspec, wspec, vspec,
            pl.BlockSpec((d, LANES), lambda b, i: (0, 0)),
            pl.BlockSpec((1, LANES), lambda b, i: (0, 0)),
        ],
        out_specs=[
            pl.BlockSpec((1, tq, d), lambda b, i: (b, i, 0)),
            pl.BlockSpec((tq * TILE_ROWS, LANES), lambda b, i: (b * (s // tq) + i, 0)),
            pl.BlockSpec((1, tq, LANES), lambda b, i: (b, i, 0)),
        ],
        out_shape=[
            jax.ShapeDtypeStruct((nb, s, d), F32),
            jax.ShapeDtypeStruct((nb * s * TILE_ROWS, LANES), jnp.uint32),
            jax.ShapeDtypeStruct((nb, s, LANES), F32),
        ],
        compiler_params=_cparams(("parallel", "parallel")),
        name="cross_attn_router",
    )(x, kv, kv, nw, wq, wo, nf, wr, br)


MOE_ROWS = 512


def _route_math(lg, lane):
    lane_f = lane.astype(F32)
    big = float(LANES)
    neg = -jnp.inf

    is_g = lane < N_GROUPS
    gl = jnp.where(is_g, lg, neg)
    gmax = jnp.max(gl, axis=-1, keepdims=True)
    g_idx = jnp.min(jnp.where(gl == gmax, lane_f, big), axis=-1, keepdims=True)
    g_sum = jnp.sum(jnp.where(is_g, jnp.exp(gl - gmax), 0.0), axis=-1, keepdims=True)
    g_w = 1.0 / g_sum

    e_grp = lax.shift_right_arithmetic(lane - N_GROUPS, 3).astype(F32)
    is_e = (lane >= N_GROUPS) & (lane < N_GROUPS + N_EXPERTS) & (e_grp == g_idx)
    el = jnp.where(is_e, lg, neg)
    emax = jnp.max(el, axis=-1, keepdims=True)
    ee = jnp.where(is_e, jnp.exp(el - emax), 0.0)
    ep = ee / jnp.sum(ee, axis=-1, keepdims=True)
    ep = jnp.where(is_e, ep, -1.0)
    v1 = jnp.max(ep, axis=-1, keepdims=True)
    i1 = jnp.min(jnp.where(ep == v1, lane_f, big), axis=-1, keepdims=True)
    ep2 = jnp.where(lane_f == i1, -1.0, ep)
    v2 = jnp.max(ep2, axis=-1, keepdims=True)
    i2 = jnp.min(jnp.where(ep2 == v2, lane_f, big), axis=-1, keepdims=True)
    tot = v1 + v2
    gate1 = g_w * (v1 / tot)
    gate2 = g_w * (v2 / tot)
    return i1 - N_GROUPS, i2 - N_GROUPS, gate1, gate2


def _plan_kernel(lg_ref, gate_ref, pos_ref, cnt_ref, carry_scr, base_scr):
    p = pl.program_id(0)
    i = pl.program_id(1)
    tm = lg_ref.shape[0]
    lane = lax.broadcasted_iota(jnp.int32, (tm, LANES), 1)
    lane_f = lane.astype(F32)
    e1, e2, gate1, gate2 = _route_math(lg_ref[...], lane)
    oh1 = lane_f == e1
    oh2 = lane_f == e2
    m = jnp.where(oh1 | oh2, 1.0, 0.0)
    colsum = jnp.sum(m, axis=0, keepdims=True)

    @pl.when((p == 0) & (i == 0))
    def _():
        carry_scr[...] = jnp.zeros_like(carry_scr)

    @pl.when(p == 0)
    def _():
        carry_scr[...] += colsum

    @pl.when((p == 1) & (i == 0))
    def _():
        counts = carry_scr[...]
        cnt_ref[...] = counts
        shift = MOE_ROWS.bit_length() - 1
        blocks = lax.shift_right_logical(counts.astype(jnp.int32) + (MOE_ROWS - 1), shift)
        padded = (blocks * MOE_ROWS).astype(F32)
        lane8 = lax.broadcasted_iota(jnp.int32, padded.shape, 1)
        inc = padded
        sh = 1
        while sh < LANES:
            inc = inc + jnp.where(lane8 >= sh, pltpu.roll(inc, sh, 1), 0.0)
            sh *= 2
        base_scr[...] = inc - padded
        carry_scr[...] = jnp.zeros_like(carry_scr)

    @pl.when(p == 1)
    def _():
        r = lax.broadcasted_iota(jnp.int32, (tm, tm), 0)
        c = lax.broadcasted_iota(jnp.int32, (tm, tm), 1)
        earlier = jnp.where(r > c, 1.0, 0.0).astype(BF16)
        tot = (jnp.dot(earlier, m.astype(BF16), preferred_element_type=F32)
               + base_scr[0:1, :] + carry_scr[0:1, :])
        pos1 = jnp.sum(jnp.where(oh1, tot, 0.0), axis=1, keepdims=True)
        pos2 = jnp.sum(jnp.where(oh2, tot, 0.0), axis=1, keepdims=True)
        pos_ref[...] = jnp.where(lane == 0, pos1, jnp.where(lane == 1, pos2, 0.0)).astype(jnp.int32)
        gate_ref[...] = jnp.where(lane == 0, gate1, jnp.where(lane == 1, gate2, 0.0))
        carry_scr[...] += colsum


def _plan(lg, tm):
    t = lg.shape[0]
    out_spec = pl.BlockSpec((tm, LANES), lambda p, i: (i * p, 0))
    return pl.pallas_call(
        _plan_kernel,
        grid=(2, t // tm),
        in_specs=[pl.BlockSpec((tm, LANES), lambda p, i: (i, 0))],
        out_specs=[out_spec, out_spec, pl.BlockSpec((8, LANES), lambda p, i: (0, 0))],
        out_shape=[jax.ShapeDtypeStruct((t, LANES), F32), jax.ShapeDtypeStruct((t, LANES), jnp.int32),
                   jax.ShapeDtypeStruct((8, LANES), F32)],
        scratch_shapes=[pltpu.VMEM((8, LANES), F32), pltpu.VMEM((8, LANES), F32)],
        compiler_params=_cparams(("arbitrary", "arbitrary")),
        name="moe_plan",
    )(lg)


def _dispatch_kernel(last_ref, pos_ref, h_ref, xs_hbm, zero_scr, sem):
    td = pos_ref.shape[2] // 2

    @pl.when(pl.program_id(0) == 0)
    def _():
        zero_scr[...] = jnp.zeros_like(zero_scr)

        def block_copy(e):
            first = pl.multiple_of(last_ref[e] * (MOE_ROWS * TILE_ROWS), MOE_ROWS * TILE_ROWS)
            return pltpu.make_async_copy(zero_scr, xs_hbm.at[pl.ds(first, MOE_ROWS * TILE_ROWS)], sem)

        def fill(e, carry):
            @pl.when(last_ref[e] >= 0)
            def _():
                block_copy(e).start()
            return carry

        def drain(e, carry):
            @pl.when(last_ref[e] >= 0)
            def _():
                block_copy(e).wait()
            return carry

        lax.fori_loop(0, last_ref.shape[0], fill, 0)
        lax.fori_loop(0, last_ref.shape[0], drain, 0)

    def tile(ref, tok):
        return ref.at[pl.ds(pl.multiple_of(tok * TILE_ROWS, TILE_ROWS), TILE_ROWS)]

    def body(r, carry):
        src = tile(h_ref, r)
        pltpu.make_async_copy(src, tile(xs_hbm, pos_ref[0, 0, r]), sem).start(priority=0)
        pltpu.make_async_copy(src, tile(xs_hbm, pos_ref[0, 0, td + r]), sem).start(priority=1)
        return carry

    lax.fori_loop(0, td, body, 0, unroll=8)
    for _ in range(2):
        pltpu.make_async_copy(h_ref, xs_hbm.at[pl.ds(0, td * TILE_ROWS)], sem).wait()


def _dispatch(last_blk, pos_blk, h3_tiles, n_blocks):
    n_steps = pos_blk.shape[0]
    td = pos_blk.shape[2] // 2
    grid_spec = pltpu.PrefetchScalarGridSpec(
        num_scalar_prefetch=1,
        grid=(n_steps,),
        in_specs=[
            pl.BlockSpec((1, 1, 2 * td), lambda i, last: (i, 0, 0), memory_space=pltpu.SMEM),
            pl.BlockSpec((td * TILE_ROWS, LANES), lambda i, last: (i, 0)),
        ],
        out_specs=pl.BlockSpec(memory_space=pl.ANY),
        scratch_shapes=[pltpu.VMEM((MOE_ROWS * TILE_ROWS, LANES), jnp.uint32), pltpu.SemaphoreType.DMA(())],
    )
    return pl.pallas_call(
        _dispatch_kernel,
        grid_spec=grid_spec,
        out_shape=jax.ShapeDtypeStruct((n_blocks * MOE_ROWS * TILE_ROWS, LANES), jnp.uint32),
        compiler_params=_cparams(("arbitrary",)),
        name="moe_dispatch",
    )(last_blk, pos_blk, h3_tiles)


def _expert_kernel(be_ref, na_ref, slot_ref, next_ref, x_ref, wg_hbm, wu_hbm, wd_hbm, o_ref,
                   wg_buf, wu_buf, wd_buf, wg_scr, wu_scr, wd_scr, sem):
    j = pl.program_id(0)
    prev = be_ref[jnp.maximum(j - 1, 0)]

    def fetch(e, s):
        return (pltpu.make_async_copy(wg_hbm.at[e], wg_buf.at[s], sem.at[s]),
                pltpu.make_async_copy(wu_hbm.at[e], wu_buf.at[s], sem.at[s]),
                pltpu.make_async_copy(wd_hbm.at[e], wd_buf.at[s], sem.at[s]))

    @pl.when((j < na_ref[0]) & ((j == 0) | (be_ref[j] != prev)))
    def _():
        s = slot_ref[j]

        @pl.when(j == 0)
        def _():
            for c in fetch(be_ref[0], s):
                c.start()

        for c in fetch(be_ref[j], s):
            c.wait()

        @pl.when(next_ref[j] >= 0)
        def _():
            for c in fetch(next_ref[j], 1 - s):
                c.start()

        wg_scr[...] = wg_buf[s].astype(BF16)
        wu_scr[...] = wu_buf[s].astype(BF16)
        wd_scr[...] = wd_buf[s].astype(BF16)

    @pl.when(j < na_ref[0])
    def _():
        x = _load_token_tiles(x_ref, 0, MOE_ROWS).astype(BF16)
        g = jnp.dot(x, wg_scr[...], preferred_element_type=F32)
        u = jnp.dot(x, wu_scr[...], preferred_element_type=F32)
        hid = (_silu(g) * u).astype(BF16)
        _store_token_tiles(o_ref, jnp.dot(hid, wd_scr[...], preferred_element_type=F32))

    @pl.when(j >= na_ref[0])
    def _():
        o_ref[...] = jnp.zeros_like(o_ref)


def _experts(blk_exp, n_active, blk_slot, blk_next, xs_tiles, wg, wu, wd):
    n_blocks = blk_exp.shape[0]
    d, ff = wg.shape[1], wg.shape[2]
    blk = (MOE_ROWS * TILE_ROWS, LANES)
    grid_spec = pltpu.PrefetchScalarGridSpec(
        num_scalar_prefetch=4,
        grid=(n_blocks,),
        in_specs=[
            pl.BlockSpec(blk, lambda j, be, na, sl, nx: (jnp.minimum(j, na[0] - 1), 0)),
            pl.BlockSpec(memory_space=pl.ANY),
            pl.BlockSpec(memory_space=pl.ANY),
            pl.BlockSpec(memory_space=pl.ANY),
        ],
        out_specs=pl.BlockSpec(blk, lambda j, be, na, sl, nx: (j, 0)),
        scratch_shapes=[pltpu.VMEM((2, d, ff), F32), pltpu.VMEM((2, d, ff), F32), pltpu.VMEM((2, ff, d), F32),
                        pltpu.VMEM((d, ff), BF16), pltpu.VMEM((d, ff), BF16), pltpu.VMEM((ff, d), BF16),
                        pltpu.SemaphoreType.DMA((2,))],
    )
    return pl.pallas_call(
        _expert_kernel,
        grid_spec=grid_spec,
        out_shape=jax.ShapeDtypeStruct(xs_tiles.shape, jnp.uint32),
        compiler_params=_cparams(("arbitrary",)),
        name="moe_experts",
    )(blk_exp, n_active, blk_slot, blk_next, xs_tiles, wg, wu, wd)


COMBINE_ROWS = 32


def _combine_kernel(n_first, pos_ref, pos_next_ref, x_ref, gate_ref, nw_ref, ys_hbm, oa_ref, ob_ref, y_scr, sem):
    i = pl.program_id(0)
    tm = x_ref.shape[0]
    slot = i % 2

    def tile(ref, tok):
        return ref.at[pl.ds(pl.multiple_of(tok * TILE_ROWS, TILE_ROWS), TILE_ROWS)]

    def gather(p_ref, to):
        buf = y_scr.at[to]

        def body(r, carry):
            pltpu.make_async_copy(tile(ys_hbm, p_ref[0, 0, r]), tile(buf, r), sem.at[to]).start(priority=0)
            pltpu.make_async_copy(tile(ys_hbm, p_ref[0, 0, tm + r]), tile(buf, tm + r), sem.at[to]).start(priority=1)
            return carry

        lax.fori_loop(0, tm, body, 0, unroll=8)

    @pl.when(i == 0)
    def _():
        gather(pos_ref, 0)

    @pl.when(i + 1 < pl.num_programs(0))
    def _():
        gather(pos_next_ref, 1 - slot)

    buf = y_scr.at[slot]
    pltpu.make_async_copy(ys_hbm.at[pl.ds(0, 2 * tm * TILE_ROWS)], buf, sem.at[slot]).wait()
    def finish(o_ref):
        def rows(rb, carry):
            r0 = pl.multiple_of(rb * COMBINE_ROWS, COMBINE_ROWS)
            sl = pl.ds(r0, COMBINE_ROWS)
            gate = gate_ref[sl, :]
            y = x_ref[sl, :] + (gate[:, 0:1] * _load_token_tiles(buf, r0, COMBINE_ROWS)
                                + gate[:, 1:2] * _load_token_tiles(buf, tm + r0, COMBINE_ROWS))
            o_ref[sl, :] = _rms(y, nw_ref[...])
            return carry

        lax.fori_loop(0, tm // COMBINE_ROWS, rows, 0, unroll=4)

    @pl.when(i < n_first)
    def _():
        finish(oa_ref)

    @pl.when(i >= n_first)
    def _():
        finish(ob_ref)


def _combine(pos_blk, x2, gates, nw, ys, tm, t_first):
    t, d = x2.shape
    na = t_first // tm
    n = t // tm
    return pl.pallas_call(
        functools.partial(_combine_kernel, na),
        grid=(n,),
        in_specs=[
            pl.BlockSpec((1, 1, 2 * tm), lambda i: (i, 0, 0), memory_space=pltpu.SMEM),
            pl.BlockSpec((1, 1, 2 * tm), lambda i: (jnp.minimum(i + 1, n - 1), 0, 0), memory_space=pltpu.SMEM),
            pl.BlockSpec((tm, d), lambda i: (i, 0)),
            pl.BlockSpec((tm, LANES), lambda i: (i, 0)),
            pl.BlockSpec((1, d), lambda i: (0, 0)),
            pl.BlockSpec(memory_space=pl.ANY),
        ],
        out_specs=[
            pl.BlockSpec((tm, d), lambda i: (_first_or_second(i, na)[0], 0)),
            pl.BlockSpec((tm, d), lambda i: (_first_or_second(i, na)[1], 0)),
        ],
        out_shape=[jax.ShapeDtypeStruct((t_first, d), F32), jax.ShapeDtypeStruct((t - t_first, d), F32)],
        scratch_shapes=[pltpu.VMEM((2, 2 * tm * TILE_ROWS, LANES), jnp.uint32), pltpu.SemaphoreType.DMA((2,))],
        compiler_params=_cparams(("arbitrary",)),
        name="moe_combine",
    )(pos_blk, pos_blk, x2, gates, nw, ys)


def _block_experts(counts, n_blocks):
    per_expert = (counts + MOE_ROWS - 1) // MOE_ROWS
    ends = jnp.cumsum(per_expert)
    blk = jnp.arange(n_blocks, dtype=jnp.int32)
    blk_exp = jnp.minimum(jnp.sum(blk[:, None] >= ends[None, :], axis=1), N_EXPERTS - 1).astype(jnp.int32)
    last_blk = jnp.where(per_expert > 0, ends - 1, -1)
    tail = n_blocks - 1 - jnp.arange(N_EXPERTS)
    to_zero = jnp.concatenate([last_blk, jnp.where(tail >= ends[-1], tail, -1)]).astype(jnp.int32)
    e = jnp.arange(N_EXPERTS)
    nonempty = per_expert > 0
    slot = (jnp.cumsum(nonempty) - nonempty) % 2
    later = jnp.where(nonempty[None, :] & (e[None, :] > e[:, None]), e[None, :], N_EXPERTS)
    nxt = jnp.min(later, axis=1)
    nxt = jnp.where(nxt < N_EXPERTS, nxt, -1)
    return (blk_exp, ends[-1:].astype(jnp.int32), to_zero, slot[blk_exp].astype(jnp.int32),
            nxt[blk_exp].astype(jnp.int32))


def _pick(n, pref):
    t = pref
    while n % t:
        t //= 2
    return t


def _encoder(xa, xb, mem, p):
    (nba, s, d), nbb = xa.shape, xb.shape[0]
    nb = nba + nbb
    t = nb * s
    t_first = nba * s
    nc = s // SSD_CHUNK
    tm = _pick(math.gcd(t_first, t - t_first), 1024)
    tq = _pick(s, 512)
    xa = xa.reshape(t_first, d)
    xb = xb.reshape(t - t_first, d)

    perm = jnp.arange(SSD_HEADS).reshape(SSD_GROUPS, HEADS_PER_GROUP)
    perm = jnp.concatenate([perm, perm + SSD_HEADS], axis=1).reshape(-1)

    w_in = p['w_in']
    o_z = 2 * D_MODEL
    o_x = o_z + SSD_D_INNER
    o_dt = o_x + SSD_D_INNER + 2 * SSD_GROUPS * SSD_STATE
    o_g = o_dt + 2 * SSD_HEADS
    w_val = w_in[:, :D_MODEL].astype(BF16)
    w_gate = w_in[:, D_MODEL:o_z].astype(BF16)
    w_z = w_in[:, o_z:o_x].astype(BF16)
    w_xbc = w_in[:, o_x:o_dt].astype(BF16)
    w_dt = jnp.pad(w_in[:, o_dt:o_g][:, perm], ((0, 0), (0, LANES - 2 * SSD_HEADS))).astype(BF16)
    w_g = w_in[:, o_g:].astype(BF16)
    nmix = p['norm_mix_w'].reshape(1, d)

    h = _prenorm(xa, xb, nmix, tm)
    u = _glu_proj(h, w_val, w_gate, tm, PROJ_TN // 2)
    xbc = _proj(h, w_xbc, lambda a: a, BF16, tm, PROJ_TN, "xbc_proj")
    dt_raw = _proj(h, w_dt, lambda a: a, F32, tm, LANES, "dt_proj")
    gates = _proj(h, w_g, _sigmoid, BF16, tm, PROJ_TN, "gate_proj")

    pad = LANES - 2 * SSD_HEADS
    dt_bias = jnp.pad(jnp.concatenate([p['ssd_dt_bias_f'], p['ssd_dt_bias_b']])[perm], (0, pad)).reshape(1, LANES)
    a_log = jnp.pad(jnp.concatenate([p['ssd_a_log_f'], p['ssd_a_log_b']])[perm], (0, pad)).reshape(1, LANES)
    acs2, rowt, e_hi, e_lo, w_hi, w_lo, dec = _dtprep(dt_raw, dt_bias, a_log, tm)

    def per_group_tm(a):
        return a[:, :2 * SSD_HEADS].reshape(nb, s, SSD_GROUPS, 8).transpose(0, 2, 1, 3)

    def per_group_hm(a):
        return a[:, :2 * SSD_HEADS].reshape(nb, nc, SSD_CHUNK, SSD_GROUPS, 8).transpose(0, 3, 1, 4, 2)

    acs_tm = per_group_tm(acs2).reshape(nb * SSD_GROUPS, s, 8)
    ew_tm = jnp.concatenate([per_group_tm(v) for v in (e_hi, e_lo, w_hi, w_lo)], axis=-1)
    ew_tm = ew_tm.reshape(nb * SSD_GROUPS, s, EW_COLS)
    rowt_hm = per_group_hm(rowt)
    dec = dec[:, :2 * SSD_HEADS].reshape(nb, nc, SSD_GROUPS, 2, HEADS_PER_GROUP).transpose(0, 2, 1, 3, 4)
    dec = jnp.repeat(dec, SSD_HEAD_DIM, axis=-1)

    cwx = p['ssd_conv_w']
    cbx = p['ssd_conv_b']

    def per_group_conv(a):
        xs_ = a[:, :SSD_D_INNER].reshape(-1, SSD_GROUPS, GROUP_CH)
        b_ = a[:, SSD_D_INNER:SSD_D_INNER + SSD_GROUPS * SSD_STATE].reshape(-1, SSD_GROUPS, SSD_STATE)
        c_ = a[:, SSD_D_INNER + SSD_GROUPS * SSD_STATE:].reshape(-1, SSD_GROUPS, SSD_STATE)
        return jnp.concatenate([xs_, b_, c_], axis=-1).transpose(1, 0, 2)

    cw = jnp.pad(per_group_conv(cwx), ((0, 0), (0, 8 - SSD_CONV_WIDTH), (0, 0)))
    cb = per_group_conv(cbx.reshape(1, -1))
    dskip = jnp.repeat(p['ssd_d'], SSD_HEAD_DIM).reshape(SSD_GROUPS, 1, GROUP_CH)
    nw_ssd = p['ssd_norm_w'].reshape(SSD_GROUPS, 1, GROUP_CH)

    yn = _ssd(xbc.reshape(nb, s, -1), h.reshape(nb, s, d), w_z, cw, cb, rowt_hm, acs_tm, ew_tm, dec, dskip, nw_ssd)
    mb = _gated_proj(yn.reshape(t, SSD_D_INNER), p['ssd_out'].astype(BF16), gates, 1, tm)

    dww = jnp.pad(p['conv_dw_w'], ((0, 32 - CONV_WIDTH), (0, 0)))
    x1 = _conformer(u.reshape(nb, s, d), mb.reshape(nb, s, d), gates.reshape(nb, s, 2 * d), xa, xb, dww,
                    p['conv_dw_b'].reshape(1, d), p['conv_ln_w'].reshape(1, d), p['conv_ln_b'].reshape(1, d),
                    p['conv_pw_out'].astype(BF16), p['w_out'].astype(BF16), tq)

    m = mem.shape[1]
    kv = _norm_proj(mem.reshape(nb * m, d), p['norm_mem_w'].reshape(1, d), p['xattn_wkv'].astype(BF16),
                    lambda a: a, BF16, _pick(nb * m, 512), 1024, "kv_proj")
    n_r = N_GROUPS + N_EXPERTS
    wr = jnp.pad(jnp.concatenate([p['router_group_w'], p['router_expert_w']], axis=1),
                 ((0, 0), (0, LANES - n_r))).astype(BF16)
    br = jnp.pad(jnp.concatenate([p['router_group_b'], p['router_expert_b']]), (0, LANES - n_r)).reshape(1, LANES)
    x2, h3, logits = _attention(x1, kv.reshape(nb, m, 2 * d), p['norm_xattn_w'].reshape(1, d),
                                p['xattn_wq'].astype(BF16), p['xattn_wo'].astype(BF16),
                                p['norm_ffn_w'].reshape(1, d), wr, br, tq)

    tc = _pick(math.gcd(t_first, t - t_first), 512)
    gate, pos, cnt = _plan(logits.reshape(t, LANES), tm)
    n_blocks = (2 * t) // MOE_ROWS + N_EXPERTS
    blk_exp, n_active, last_blk, blk_slot, blk_next = _block_experts(cnt[0, :N_EXPERTS].astype(jnp.int32), n_blocks)
    pos_blk = pos[:, :2].reshape(t // tc, tc, 2).transpose(0, 2, 1).reshape(t // tc, 1, 2 * tc)
    pos_blk_d = pos[:, :2].reshape(t // tm, tm, 2).transpose(0, 2, 1).reshape(t // tm, 1, 2 * tm)
    xs = _dispatch(last_blk, pos_blk_d, h3, n_blocks)
    ys = _experts(blk_exp, n_active, blk_slot, blk_next, xs, p['expert_w_gate'], p['expert_w_up'],
                  p['expert_w_down'])
    return _combine(pos_blk, x2.reshape(t, d), gate, p['norm_final_w'].reshape(1, d), ys, tc, t_first)


def kernel(x_prompt, x_sample, mem_prompt, mem_sample, norm_mix_w, w_in, conv_dw_w, conv_dw_b, conv_ln_w, conv_ln_b, conv_pw_out, ssd_conv_w, ssd_conv_b, ssd_dt_bias_f, ssd_dt_bias_b, ssd_a_log_f, ssd_a_log_b, ssd_d, ssd_norm_w, ssd_out, w_out, norm_xattn_w, norm_mem_w, xattn_wq, xattn_wkv, xattn_wo, norm_ffn_w, router_group_w, router_group_b, router_expert_w, router_expert_b, expert_w_gate, expert_w_up, expert_w_down, norm_final_w):
    p = {
        'norm_mix_w': norm_mix_w[0], 'w_in': w_in[0], 'conv_dw_w': conv_dw_w[0], 'conv_dw_b': conv_dw_b[0],
        'conv_ln_w': conv_ln_w[0], 'conv_ln_b': conv_ln_b[0], 'conv_pw_out': conv_pw_out[0],
        'ssd_conv_w': ssd_conv_w[0], 'ssd_conv_b': ssd_conv_b[0], 'ssd_dt_bias_f': ssd_dt_bias_f[0],
        'ssd_dt_bias_b': ssd_dt_bias_b[0], 'ssd_a_log_f': ssd_a_log_f[0], 'ssd_a_log_b': ssd_a_log_b[0],
        'ssd_d': ssd_d[0], 'ssd_norm_w': ssd_norm_w[0], 'ssd_out': ssd_out[0], 'w_out': w_out[0],
        'norm_xattn_w': norm_xattn_w[0], 'norm_mem_w': norm_mem_w[0], 'xattn_wq': xattn_wq[0],
        'xattn_wkv': xattn_wkv[0], 'xattn_wo': xattn_wo[0], 'norm_ffn_w': norm_ffn_w[0],
        'router_group_w': router_group_w[0], 'router_group_b': router_group_b[0],
        'router_expert_w': router_expert_w[0], 'router_expert_b': router_expert_b[0],
        'expert_w_gate': expert_w_gate[0], 'expert_w_up': expert_w_up[0], 'expert_w_down': expert_w_down[0],
        'norm_final_w': norm_final_w,
    }
    mem = jnp.concatenate([mem_prompt, mem_sample], axis=0)
    y_prompt, y_sample = _encoder(x_prompt, x_sample, mem, p)
    return (y_prompt.reshape(x_prompt.shape), y_sample.reshape(x_sample.shape))
```
